```python
import jax, jax.numpy as jnp
from jax import lax
import numpy as np

D_MODEL = 1024
BATCH = 16
SEQ = 256
DEPTH = 4
DEC_BATCH = 2
DEC_SEQ = 1024
PAST_LEN = 512

GRID_W = 64
HEAD_DIM = 64
H_RWKV = 4
H_NA = 4
H_GQA = 8
H_GQA_KV = 2
GQA_GROUP = H_GQA // H_GQA_KV
W_RWKV = H_RWKV * HEAD_DIM
W_NA = H_NA * HEAD_DIM
W_GQA = H_GQA * HEAD_DIM
W_GQA_KV = H_GQA_KV * HEAD_DIM
LORA_W = 64
LORA_A = 64
LORA_G = 128
SHORT_CONV = 3
RWKV_IN = 3 * W_RWKV + 2 * LORA_W + 2 * LORA_A + LORA_G
NA_IN = 3 * W_NA
GQA_IN = W_GQA + 2 * W_GQA_KV
D_IN = RWKV_IN + NA_IN + GQA_IN
NA_ROWS = 8
NA_COLS = 16
Q_BLOCK = 128
ROPE_BASE = 10000.0
ROPE_FREQ = HEAD_DIM // 4
D_FF = ((8 * D_MODEL + 3 * 256 - 1) // (3 * 256)) * 256
DEEPNORM_ALPHA = (2 * DEPTH) ** 0.25
DEEPNORM_BETA = (8 * DEPTH) ** -0.25
LN_EPS = 1e-5
RMS_EPS = 1e-6
GN_EPS = 64e-5
NEG_INF = -1e30

kernel_name = 'hybrid_rwkv7_natten_gqa_diffusion_step'


def _layer_norm(x, w, b):
    xf = x.astype(jnp.float32)
    mu = jnp.mean(xf, -1, keepdims=True)
    var = jnp.mean(jnp.square(xf - mu), -1, keepdims=True)
    return ((xf - mu) * lax.rsqrt(var + LN_EPS) * w + b).astype(x.dtype)


def _rms_norm(x, w):
    xf = x.astype(jnp.float32)
    return (xf * lax.rsqrt(jnp.mean(xf * xf, -1, keepdims=True) + RMS_EPS) * w).astype(x.dtype)


def _axial_rope(x):
    t = jnp.arange(x.shape[1])
    inv = ROPE_BASE ** (-jnp.arange(ROPE_FREQ, dtype=jnp.float32) / ROPE_FREQ)

    def rotate(xa, pos):
        ang = pos.astype(jnp.float32)[:, None] * inv
        cos, sin = jnp.cos(ang)[None, :, None, :], jnp.sin(ang)[None, :, None, :]
        x1 = xa[..., :ROPE_FREQ].astype(jnp.float32)
        x2 = xa[..., ROPE_FREQ:].astype(jnp.float32)
        return jnp.concatenate([x1 * cos - x2 * sin, x2 * cos + x1 * sin], -1)

    half = HEAD_DIM // 2
    out = jnp.concatenate([rotate(x[..., :half], t // GRID_W), rotate(x[..., half:], t % GRID_W)], -1)
    return out.astype(x.dtype)


def _blocked_attention(q, k, v):
    b, lq, hk, g, d = q.shape
    qb = jnp.moveaxis(q.reshape(b, lq // Q_BLOCK, Q_BLOCK, hk, g, d), 1, 0)

    def block(qi):
        s = jnp.einsum('bqhgd,bkhd->bhgqk', qi, k).astype(jnp.float32) * d ** -0.5
        p = jax.nn.softmax(s, axis=-1).astype(v.dtype)
        return jnp.einsum('bhgqk,bkhd->bqhgd', p, v)

    o = lax.map(block, qb)
    return jnp.moveaxis(o, 0, 1).reshape(b, lq, hk * g * d)


def _neighbourhood_attention(q, k, v, k_ctx, v_ctx, rpb):
    b, s, h, d = q.shape
    rows = s // GRID_W
    kh = min(NA_ROWS, rows)
    qg = q.reshape(b, rows, GRID_W, h, d)
    kg = k.reshape(b, rows, GRID_W, h, d)
    vg = v.reshape(b, rows, GRID_W, h, d)
    r = jnp.arange(rows)
    row_idx = jnp.clip(r - kh // 2, 0, rows - kh)[:, None] + jnp.arange(kh)[None, :]
    kb = jnp.take(kg, row_idx, axis=1)
    vb = jnp.take(vg, row_idx, axis=1)
    col = jnp.arange(GRID_W)
    c0 = jnp.clip(col - NA_COLS // 2, 0, GRID_W - NA_COLS)
    in_win = (col[None, :] >= c0[:, None]) & (col[None, :] < c0[:, None] + NA_COLS)
    dr = row_idx - r[:, None] + NA_ROWS - 1
    dc = jnp.clip(col[None, :] - col[:, None], 1 - NA_COLS, NA_COLS - 1) + NA_COLS - 1
    bias = rpb[:, dr[:, None, :, None], dc[None, :, None, :]].astype(jnp.float32)
    scale = d ** -0.5
    s_nb = jnp.einsum('brqhd,brjchd->bhrqjc', qg, kb).astype(jnp.float32) * scale + bias
    s_nb = jnp.where(in_win[:, None, :], s_nb, NEG_INF)
    s_ctx = jnp.einsum('brqhd,bkhd->bhrqk', qg, k_ctx).astype(jnp.float32) * scale
    n_nb = kh * GRID_W
    p = jax.nn.softmax(jnp.concatenate([s_nb.reshape(b, h, rows, GRID_W, n_nb), s_ctx], -1), axis=-1)
    p = p.astype(v.dtype)
    o = (jnp.einsum('bhrqjc,brjchd->brqhd', p[..., :n_nb].reshape(b, h, rows, GRID_W, kh, GRID_W), vb)
         + jnp.einsum('bhrqk,bkhd->brqhd', p[..., n_nb:], v_ctx))
    return o.reshape(b, s, h * d)


def _short_conv(x, w):
    n = x.shape[1]
    xp = jnp.pad(x, ((0, 0), (SHORT_CONV // 2, SHORT_CONV // 2), (0, 0)))
    return sum(xp[:, j:j + n] * w[j] for j in range(SHORT_CONV))


def _wkv_scan(s0, r, w, kk, a, k, v, reverse):
    def step(s, inp):
        r_t, w_t, kk_t, a_t, k_t, v_t = inp
        sk = jnp.einsum('bhvk,bhk->bhv', s, kk_t)
        s = (s * w_t[:, :, None, :] - sk[..., None] * (a_t * kk_t)[:, :, None, :]
             + v_t[..., None] * k_t[:, :, None, :])
        return s, jnp.einsum('bhvk,bhk->bhv', s, r_t)

    xs = tuple(jnp.moveaxis(t, 1, 0) for t in (r, w, kk, a, k, v))
    s_fin, o = lax.scan(step, s0, xs, reverse=reverse)
    return s_fin, jnp.moveaxis(o, 0, 1)


def _rwkv_mix(feat, s0_f, s0_b, p):
    conv, w0, w2, a0, a2, g2, k_k, k_a, r_k, lnx_w, lnx_b = p
    b, n, _ = feat.shape
    f = _short_conv(feat, conv).astype(jnp.float32)
    o1, o2, o3 = W_RWKV, 2 * W_RWKV, 3 * W_RWKV
    o4 = o3 + 2 * LORA_W
    o5 = o4 + 2 * LORA_A
    r, k, v = f[..., :o1], f[..., o1:o2], f[..., o2:o3]
    wd = f[..., o3:o4].reshape(b, n, 2, LORA_W)
    ad = f[..., o4:o5].reshape(b, n, 2, LORA_A)
    gd = f[..., o5:]
    log_w = -jax.nn.softplus(-(w0 + jnp.einsum('bndr,drc->bndc', jnp.tanh(wd), w2))) - 0.5
    decay = jnp.exp(-jnp.exp(log_w))
    a = jax.nn.sigmoid(a0 + jnp.einsum('bndr,drc->bndc', ad, a2))
    g = jax.nn.sigmoid(gd) @ g2
    heads = lambda t: t.reshape(t.shape[:-1] + (H_RWKV, HEAD_DIM))
    kk = heads(k * k_k)
    kk = kk / jnp.maximum(jnp.sqrt(jnp.sum(kk * kk, -1, keepdims=True)), 1e-12)
    k_dir = heads(k[:, :, None, :] * (1 + (a - 1) * k_a))
    decay, a = heads(decay), heads(a)
    r_h, v_h = heads(r), heads(v)
    s_f, o_f = _wkv_scan(s0_f.astype(jnp.float32), r_h, decay[:, :, 0], kk, a[:, :, 0], k_dir[:, :, 0], v_h, False)
    s_b, o_b = _wkv_scan(s0_b.astype(jnp.float32), r_h, decay[:, :, 1], kk, a[:, :, 1], k_dir[:, :, 1], v_h, True)
    o = o_f + o_b
    mu = jnp.mean(o, -1, keepdims=True)
    var = jnp.mean(jnp.square(o - mu), -1, keepdims=True)
    o = ((o - mu) * lax.rsqrt(var + GN_EPS)).reshape(b, n, W_RWKV) * lnx_w + lnx_b
    bonus = jnp.sum(r_h * (k_dir[:, :, 0] + k_dir[:, :, 1]) * r_k, -1, keepdims=True) * v_h
    y = (o + bonus.reshape(b, n, W_RWKV)) * g
    return y.astype(feat.dtype), s_f, s_b


def _layer(x, mod, shared, cached):
    (w_in, rwkv_p, rpb, q_norm, k_norm, w_out, ln1_w, ln1_b, w_ffn_in, w_ffn_out, ln2_w, ln2_b) = shared
    shift1, scale1, gate1, shift2, scale2, gate2 = jnp.split(mod, 6, axis=-1)
    b, n = x.shape[:2]
    proj = (x * (1 + scale1) + shift1) @ w_in
    f_rwkv = proj[..., :RWKV_IN]
    f_na = proj[..., RWKV_IN:RWKV_IN + NA_IN].reshape(b, n, 3, H_NA, HEAD_DIM)
    na_q, na_k, na_v = f_na[:, :, 0], f_na[:, :, 1], f_na[:, :, 2]
    f_g = proj[..., RWKV_IN + NA_IN:]
    g_q = _rms_norm(f_g[..., :W_GQA].reshape(b, n, H_GQA, HEAD_DIM), q_norm)
    g_k = _rms_norm(f_g[..., W_GQA:W_GQA + W_GQA_KV].reshape(b, n, H_GQA_KV, HEAD_DIM), k_norm)
    g_v = f_g[..., W_GQA + W_GQA_KV:].reshape(b, n, H_GQA_KV, HEAD_DIM)
    if cached is None:
        s0 = jnp.zeros((b, H_RWKV, HEAD_DIM, HEAD_DIM), jnp.float32)
        o_rwkv, s_f, s_b = _rwkv_mix(f_rwkv, s0, s0, rwkv_p)
        o_na = _blocked_attention(na_q[:, :, :, None], na_k, na_v)
        o_g = _blocked_attention(g_q.reshape(b, n, H_GQA_KV, GQA_GROUP, HEAD_DIM), g_k, g_v)
        ctx_tensors = (jnp.stack([s_f, s_b], 1).astype(x.dtype), na_k, na_v, g_k, g_v)
    else:
        s0_f, s0_b, na_k_ctx, na_v_ctx, g_k_ctx, g_v_ctx = cached
        o_rwkv, _, _ = _rwkv_mix(f_rwkv, s0_f, s0_b, rwkv_p)
        o_na = _neighbourhood_attention(na_q, na_k, na_v, na_k_ctx, na_v_ctx, rpb)
        g_q = _axial_rope(g_q)
        g_k = _axial_rope(g_k)
        keys = jnp.concatenate([g_k_ctx.astype(g_k.dtype), g_k], 1)
        vals = jnp.concatenate([g_v_ctx.astype(g_v.dtype), g_v], 1)
        o_g = _blocked_attention(g_q.reshape(b, n, H_GQA_KV, GQA_GROUP, HEAD_DIM), keys, vals)
        ctx_tensors = None
    mix = jnp.concatenate([o_rwkv, o_na, o_g], -1) @ w_out
    x = _layer_norm(DEEPNORM_ALPHA * x + gate1 * mix, ln1_w, ln1_b)
    gate, up = jnp.split((x * (1 + scale2) + shift2) @ w_ffn_in, 2, axis=-1)
    x = _layer_norm(DEEPNORM_ALPHA * x + gate2 * ((jax.nn.silu(gate) * up) @ w_ffn_out), ln2_w, ln2_b)
    return x, ctx_tensors


def setup_inputs(seed: int = 0) -> dict:
    key = jax.random.key(seed)
    ks = iter(jax.random.split(key, 40))
    nrm = lambda shape, s=1.0: s * jax.random.normal(next(ks), shape, jnp.float32)
    D = D_MODEL
    conv_base = jnp.asarray(np.array([0.25, 0.5, 0.25], np.float32))[None, :, None]
    return {
        'x_prompt': nrm((BATCH, SEQ, D)),
        'x_sample': nrm((DEC_BATCH, DEC_SEQ, D)),
        'state_rwkv': nrm((DEC_BATCH, DEPTH, 2, H_RWKV, HEAD_DIM, HEAD_DIM), 0.5),
        'cache_na_k': nrm((DEC_BATCH, DEPTH, PAST_LEN, H_NA, HEAD_DIM)),
        'cache_na_v': nrm((DEC_BATCH, DEPTH, PAST_LEN, H_NA, HEAD_DIM)),
        'cache_gqa_k': nrm((DEC_BATCH, DEPTH, PAST_LEN, H_GQA_KV, HEAD_DIM)),
        'cache_gqa_v': nrm((DEC_BATCH, DEPTH, PAST_LEN, H_GQA_KV, HEAD_DIM)),
        'c': nrm((DEC_BATCH, D)),
        'c_ctx': nrm((D,)),
        'w_mod': nrm((DEPTH, D, 6 * D), 0.5 * D ** -0.5),
        'b_mod': nrm((DEPTH, 6 * D), 0.02),
        'w_in': nrm((DEPTH, D, D_IN), D ** -0.5),
        'rwkv_conv': conv_base + nrm((DEPTH, SHORT_CONV, RWKV_IN), 0.1),
        'rwkv_w0': nrm((DEPTH, 2, W_RWKV), 0.5),
        'rwkv_w2': nrm((DEPTH, 2, LORA_W, W_RWKV), 0.5 * LORA_W ** -0.5),
        'rwkv_a0': nrm((DEPTH, 2, W_RWKV), 0.5),
        'rwkv_a2': nrm((DEPTH, 2, LORA_A, W_RWKV), 0.5 * LORA_A ** -0.5),
        'rwkv_g2': nrm((DEPTH, LORA_G, W_RWKV), LORA_G ** -0.5),
        'rwkv_k_k': 1.0 + nrm((DEPTH, W_RWKV), 0.1),
        'rwkv_k_a': 1.0 + nrm((DEPTH, W_RWKV), 0.1),
        'rwkv_r_k': nrm((DEPTH, H_RWKV, HEAD_DIM), 0.1),
        'rwkv_lnx_w': 1.0 + nrm((DEPTH, W_RWKV), 0.1),
        'rwkv_lnx_b': nrm((DEPTH, W_RWKV), 0.02),
        'na_rpb': nrm((DEPTH, H_NA, 2 * NA_ROWS - 1, 2 * NA_COLS - 1), 0.2),
        'gqa_q_norm': 1.0 + nrm((DEPTH, HEAD_DIM), 0.1),
        'gqa_k_norm': 1.0 + nrm((DEPTH, HEAD_DIM), 0.1),
        'w_out': nrm((DEPTH, D, D), DEEPNORM_BETA * D ** -0.5),
        'ln1_w': 1.0 + nrm((DEPTH, D), 0.1),
        'ln1_b': nrm((DEPTH, D), 0.02),
        'w_ffn_in': nrm((DEPTH, D, 2 * D_FF), D ** -0.5),
        'w_ffn_out': nrm((DEPTH, D_FF, D), DEEPNORM_BETA * D_FF ** -0.5),
        'ln2_w': 1.0 + nrm((DEPTH, D), 0.1),
        'ln2_b': nrm((DEPTH, D), 0.02),
    }


def reference(x_prompt, x_sample, state_rwkv, cache_na_k, cache_na_v, cache_gqa_k, cache_gqa_v, c, c_ctx,
              w_mod, b_mod, w_in, rwkv_conv, rwkv_w0, rwkv_w2, rwkv_a0, rwkv_a2, rwkv_g2, rwkv_k_k, rwkv_k_a,
              rwkv_r_k, rwkv_lnx_w, rwkv_lnx_b, na_rpb, gqa_q_norm, gqa_k_norm, w_out, ln1_w, ln1_b,
              w_ffn_in, w_ffn_out, ln2_w, ln2_b):
    y_prompt = x_prompt
    y_sample = x_sample
    st_rwkv, st_na_k, st_na_v, st_g_k, st_g_v = [], [], [], [], []
    for l in range(DEPTH):
        rwkv_p = (rwkv_conv[l], rwkv_w0[l], rwkv_w2[l], rwkv_a0[l], rwkv_a2[l], rwkv_g2[l],
                  rwkv_k_k[l], rwkv_k_a[l], rwkv_r_k[l], rwkv_lnx_w[l], rwkv_lnx_b[l])
        shared = (w_in[l], rwkv_p, na_rpb[l], gqa_q_norm[l], gqa_k_norm[l], w_out[l],
                  ln1_w[l], ln1_b[l], w_ffn_in[l], w_ffn_out[l], ln2_w[l], ln2_b[l])
        mod_ctx = jax.nn.silu(c_ctx) @ w_mod[l] + b_mod[l]
        y_prompt, ctx_t = _layer(y_prompt, mod_ctx, shared, None)
        st_rwkv.append(ctx_t[0])
        st_na_k.append(ctx_t[1])
        st_na_v.append(ctx_t[2])
        st_g_k.append(ctx_t[3])
        st_g_v.append(ctx_t[4])
        mod_lat = (jax.nn.silu(c) @ w_mod[l] + b_mod[l])[:, None, :]
        cached = (state_rwkv[:, l, 0], state_rwkv[:, l, 1], cache_na_k[:, l], cache_na_v[:, l],
                  cache_gqa_k[:, l], cache_gqa_v[:, l])
        y_sample, _ = _layer(y_sample, mod_lat, shared, cached)
    new_state_rwkv = jnp.stack(st_rwkv, 1)
    new_cache_na_k = jnp.stack(st_na_k, 1)
    new_cache_na_v = jnp.stack(st_na_v, 1)
    new_cache_gqa_k = jnp.stack(st_g_k, 1)
    new_cache_gqa_v = jnp.stack(st_g_v, 1)
    return (y_prompt, y_sample, new_state_rwkv, new_cache_na_k, new_cache_na_v, new_cache_gqa_k, new_cache_gqa_v)
```

```python
import functools

import jax
import jax.numpy as jnp
from jax import lax
from jax.experimental import pallas as pl
from jax.experimental.pallas import tpu as pltpu

F32 = jnp.float32
BF16 = jnp.bfloat16
HIGHEST = lax.Precision.HIGHEST

D_MODEL = 1024
BATCH = 16
SEQ = 256
DEPTH = 4
DEC_BATCH = 2
DEC_SEQ = 1024
PAST_LEN = 512
GRID_W = 64
GRID_ROWS = DEC_SEQ // GRID_W
HEAD_DIM = 64
H_RWKV = 4
H_NA = 4
H_GQA = 8
H_GQA_KV = 2
W_RWKV = H_RWKV * HEAD_DIM
W_NA = H_NA * HEAD_DIM
W_GQA = H_GQA * HEAD_DIM
W_GQA_KV = H_GQA_KV * HEAD_DIM
LORA_W = 64
LORA_A = 64
LORA_G = 128
RWKV_IN = 3 * W_RWKV + 2 * LORA_W + 2 * LORA_A + LORA_G
NA_IN = 3 * W_NA
GQA_IN = W_GQA + 2 * W_GQA_KV
D_IN = RWKV_IN + NA_IN + GQA_IN
NA_ROWS = 8
NA_COLS = 16
ROPE_BASE = 10000.0
ROPE_FREQ = HEAD_DIM // 4
D_FF = ((8 * D_MODEL + 3 * 256 - 1) // (3 * 256)) * 256
DEEPNORM_ALPHA = (2 * DEPTH) ** 0.25
LN_EPS = 1e-5
RMS_EPS = 1e-6
GN_EPS = 64e-5
NEG_INF = -1e30
ATTN_SCALE = HEAD_DIM ** -0.5

LANES = 128
TM = 256
N_CTX = BATCH * SEQ
N_LAT = DEC_BATCH * DEC_SEQ
N_TOK = N_CTX + N_LAT
NBLK = N_TOK // TM
NCB = N_CTX // TM
LAT_BLKS = DEC_SEQ // TM
N_SEQ = BATCH + DEC_BATCH
CHUNK = 64
N_CHUNK = TM // CHUNK
MOD_ROWS = 8
VMEM_LIMIT = 48 * 1024 * 1024


def _cparams(sem):
    return pltpu.CompilerParams(dimension_semantics=sem, vmem_limit_bytes=VMEM_LIMIT)


def _iota(shape, dim):
    return lax.broadcasted_iota(jnp.int32, shape, dim)


def _dot(a, b, precision=None):
    return jnp.dot(a, b, preferred_element_type=F32, precision=precision)


def _dot_nt(a, b, precision=None):
    return lax.dot_general(a, b, (((1,), (1,)), ((), ())), preferred_element_type=F32, precision=precision)


def _dot_tn(a, b, precision=None):
    return lax.dot_general(a, b, (((0,), (0,)), ((), ())), preferred_element_type=F32, precision=precision)


def _sigmoid(x):
    return 1.0 / (1.0 + jnp.exp(-x))


def _softplus(x):
    return jnp.maximum(x, 0.0) + jnp.log(1.0 + jnp.exp(-jnp.abs(x)))


def _seg64_sum(x):
    rows, width = x.shape
    lo = _iota((rows, LANES), 1) < HEAD_DIM
    outs = []
    for c in range(width // LANES):
        blk = x[:, c * LANES:(c + 1) * LANES]
        s_lo = jnp.sum(jnp.where(lo, blk, 0.0), axis=-1, keepdims=True)
        s_hi = jnp.sum(jnp.where(lo, 0.0, blk), axis=-1, keepdims=True)
        outs.append(jnp.where(lo, s_lo, s_hi))
    return outs[0] if len(outs) == 1 else jnp.concatenate(outs, axis=1)


def _layer_norm(x, w, b):
    mu = jnp.mean(x, axis=-1, keepdims=True)
    xc = x - mu
    var = jnp.mean(xc * xc, axis=-1, keepdims=True)
    return xc * lax.rsqrt(var + LN_EPS) * w + b


def _mod_row(i):
    return jnp.where(i < NCB, 0, 1 + (i - NCB) // LAT_BLKS)


def _mod_kernel(c_ref, w_ref, b_ref, o_ref):
    c = c_ref[...]
    s = c * _sigmoid(c)
    o_ref[...] = _dot(s, w_ref[...], HIGHEST) + b_ref[...]


def _modulation(cc, w_mod, b_mod):
    n_col = 6 * D_MODEL // D_MODEL
    return pl.pallas_call(
        _mod_kernel,
        grid=(DEPTH, n_col),
        in_specs=[
            pl.BlockSpec((MOD_ROWS, D_MODEL), lambda l, j: (0, 0)),
            pl.BlockSpec((None, D_MODEL, D_MODEL), lambda l, j: (l, 0, j)),
            pl.BlockSpec((None, 1, D_MODEL), lambda l, j: (l, 0, j)),
        ],
        out_specs=pl.BlockSpec((None, MOD_ROWS, D_MODEL), lambda l, j: (l, 0, j)),
        out_shape=jax.ShapeDtypeStruct((DEPTH, MOD_ROWS, 6 * D_MODEL), F32),
        compiler_params=_cparams(("arbitrary", "arbitrary")),
        name="modulation",
    )(cc, w_mod, b_mod.reshape(DEPTH, 1, 6 * D_MODEL))


N_DR = 2 * NA_ROWS - 1
N_DC = 2 * NA_COLS - 1


def _bias_kernel(rpb_ref, o_ref):
    l = pl.program_id(0)
    q = _iota((GRID_W, LANES), 0)
    x = _iota((GRID_W, LANES), 1)
    c = x % GRID_W
    right = x >= GRID_W
    dc = jnp.clip(c - q, 1 - NA_COLS, NA_COLS - 1) + NA_COLS - 1
    c0 = jnp.clip(q - NA_COLS // 2, 0, GRID_W - NA_COLS)
    in_win = (c >= c0) & (c < c0 + NA_COLS)

    def body(t, carry):
        h = t // (N_DR - 1)
        dr = t % (N_DR - 1)
        base = ((l * H_NA + h) * N_DR + dr) * N_DC
        acc = jnp.zeros((GRID_W, LANES), F32)
        for d in range(N_DC):
            s0 = rpb_ref[base + d]
            s1 = rpb_ref[base + N_DC + d]
            acc = jnp.where(dc == d, jnp.where(right, s1, s0), acc)
        o_ref[h, dr] = jnp.where(in_win, acc, NEG_INF)
        return carry

    lax.fori_loop(0, H_NA * (N_DR - 1), body, 0)


def _bias_tables(na_rpb):
    return pl.pallas_call(
        _bias_kernel,
        grid=(DEPTH,),
        in_specs=[pl.BlockSpec(memory_space=pltpu.SMEM)],
        out_specs=pl.BlockSpec((None, H_NA, N_DR - 1, GRID_W, LANES), lambda l: (l, 0, 0, 0, 0)),
        out_shape=jax.ShapeDtypeStruct((DEPTH, H_NA, N_DR - 1, GRID_W, LANES), F32),
        compiler_params=_cparams(("arbitrary",)),
        name="na_bias_tables",
    )(na_rpb.reshape(-1))


def _rope(x, cos, sin):
    k = x.shape[1] // LANES
    cosf = cos if k == 1 else jnp.concatenate([cos] * k, axis=1)
    sinf = sin if k == 1 else jnp.concatenate([sin] * k, axis=1)
    first = (_iota(x.shape, 1) % (2 * ROPE_FREQ)) < ROPE_FREQ
    partner = jnp.where(first, pltpu.roll(x, x.shape[1] - ROPE_FREQ, axis=1), pltpu.roll(x, ROPE_FREQ, axis=1))
    return x * cosf + partner * sinf


def _inproj_kernel(x_ref, mod_ref, w_ref, qn_ref, kn_ref, cos_ref, sin_ref,
                   feat_ref, naq_ref, nak_ref, nav_ref, gq_ref, gk_ref, gv_ref):
    row = _mod_row(pl.program_id(0))
    shift1 = mod_ref[pl.ds(row, 1), 0:D_MODEL]
    scale1 = mod_ref[pl.ds(row, 1), D_MODEL:2 * D_MODEL]
    xm = (x_ref[...] * (1.0 + scale1) + shift1).astype(BF16)
    proj = _dot(xm, w_ref[...])
    o_na = RWKV_IN
    o_g = RWKV_IN + NA_IN
    feat_ref[...] = proj[:, :RWKV_IN]
    naq_ref[...] = proj[:, o_na:o_na + W_NA] * ATTN_SCALE
    nak_ref[...] = proj[:, o_na + W_NA:o_na + 2 * W_NA]
    nav_ref[...] = proj[:, o_na + 2 * W_NA:o_na + 3 * W_NA]
    q = proj[:, o_g:o_g + W_GQA]
    k = proj[:, o_g + W_GQA:o_g + W_GQA + W_GQA_KV]
    cos = cos_ref[...]
    sin = sin_ref[...]
    q = q * lax.rsqrt(_seg64_sum(q * q) * (1.0 / HEAD_DIM) + RMS_EPS) * qn_ref[...]
    k = k * lax.rsqrt(_seg64_sum(k * k) * (1.0 / HEAD_DIM) + RMS_EPS) * kn_ref[...]
    gq_ref[...] = _rope(q, cos, sin) * ATTN_SCALE
    gk_ref[...] = _rope(k, cos, sin)
    gv_ref[...] = proj[:, o_g + W_GQA + W_GQA_KV:]


def _inproj(x_all, mod_l, w_in_bf, qn, kn, cos_tab, sin_tab):
    tab_idx = lambda i: (jnp.where(i < NCB, 0, 1 + (i - NCB) % LAT_BLKS), 0)
    widths = (RWKV_IN, W_NA, W_NA, W_NA, W_GQA, W_GQA_KV, W_GQA_KV)
    return pl.pallas_call(
        _inproj_kernel,
        grid=(NBLK,),
        in_specs=[
            pl.BlockSpec((TM, D_MODEL), lambda i: (i, 0)),
            pl.BlockSpec((MOD_ROWS, 6 * D_MODEL), lambda i: (0, 0)),
            pl.BlockSpec((D_MODEL, D_IN), lambda i: (0, 0)),
            pl.BlockSpec((1, W_GQA), lambda i: (0, 0)),
            pl.BlockSpec((1, W_GQA_KV), lambda i: (0, 0)),
            pl.BlockSpec((TM, LANES), tab_idx),
            pl.BlockSpec((TM, LANES), tab_idx),
        ],
        out_specs=[pl.BlockSpec((TM, w), lambda i: (i, 0)) for w in widths],
        out_shape=[jax.ShapeDtypeStruct((N_TOK, w), F32) for w in widths],
        compiler_params=_cparams(("arbitrary",)),
        name="inproj",
    )(x_all, mod_l, w_in_bf, qn, kn, cos_tab, sin_tab)


def _rwkv_prep_kernel(f_ref, fp_ref, fn_ref, conv_ref, w0_ref, w2_ref, a0_ref, a2_ref, g2_ref,
                      kk_ref, ka_ref, rk_ref,
                      r_ref, kap_ref, v_ref, lw_ref, ah_ref, kd_ref, g_ref, bonus_ref):
    i = pl.program_id(0)
    pos = (i - NCB) % LAT_BLKS
    lat = i >= NCB
    has_prev = jnp.where(lat & (pos != 0), 1.0, 0.0)
    has_next = jnp.where(lat & (pos != LAT_BLKS - 1), 1.0, 0.0)
    x = f_ref[...]
    rows = _iota(x.shape, 0)
    x_prev = jnp.where(rows == 0, fp_ref[7:8, :] * has_prev, pltpu.roll(x, 1, axis=0))
    x_next = jnp.where(rows == TM - 1, fn_ref[0:1, :] * has_next, pltpu.roll(x, TM - 1, axis=0))
    f = x_prev * conv_ref[0:1, :] + x * conv_ref[1:2, :] + x_next * conv_ref[2:3, :]
    o1, o2, o3 = W_RWKV, 2 * W_RWKV, 3 * W_RWKV
    o4 = o3 + 2 * LORA_W
    o5 = o4 + 2 * LORA_A
    r, k, v = f[:, :o1], f[:, o1:o2], f[:, o2:o3]
    wd, ad, gd = f[:, o3:o4], f[:, o4:o5], f[:, o5:]
    log_w = -_softplus(-(w0_ref[...] + _dot(jnp.tanh(wd), w2_ref[...], HIGHEST))) - 0.5
    a = _sigmoid(a0_ref[...] + _dot(ad, a2_ref[...], HIGHEST))
    g_ref[...] = _dot(_sigmoid(gd), g2_ref[...], HIGHEST)
    kk = k * kk_ref[...]
    kap = kk / jnp.maximum(jnp.sqrt(_seg64_sum(kk * kk)), 1e-12)
    ka = ka_ref[...]
    kd_sum = jnp.zeros_like(k)
    for d in range(2):
        a_d = a[:, d * W_RWKV:(d + 1) * W_RWKV]
        kd = k * (1.0 + (a_d - 1.0) * ka)
        kd_sum = kd_sum + kd
        lw_ref[d] = -jnp.exp(log_w[:, d * W_RWKV:(d + 1) * W_RWKV])
        ah_ref[d] = a_d * kap
        kd_ref[d] = kd
    r_ref[...] = r
    kap_ref[...] = kap
    v_ref[...] = v
    bonus_ref[...] = _seg64_sum(r * kd_sum * rk_ref[...]) * v


def _rwkv_prep(feat, conv, w0, w2bd, a0, a2bd, g2, k_k, k_a, r_k):
    sub = TM // 8
    full = lambda shape: pl.BlockSpec(shape, lambda i: (0,) * len(shape))
    tok = pl.BlockSpec((TM, W_RWKV), lambda i: (i, 0))
    tok2 = pl.BlockSpec((2, TM, W_RWKV), lambda i: (0, i, 0))
    return pl.pallas_call(
        _rwkv_prep_kernel,
        grid=(NBLK,),
        in_specs=[
            pl.BlockSpec((TM, RWKV_IN), lambda i: (i, 0)),
            pl.BlockSpec((8, RWKV_IN), lambda i: (jnp.maximum(i * sub - 1, 0), 0)),
            pl.BlockSpec((8, RWKV_IN), lambda i: (jnp.minimum((i + 1) * sub, NBLK * sub - 1), 0)),
            full((3, RWKV_IN)), full((1, 2 * W_RWKV)), full((2 * LORA_W, 2 * W_RWKV)),
            full((1, 2 * W_RWKV)), full((2 * LORA_A, 2 * W_RWKV)), full((LORA_G, W_RWKV)),
            full((1, W_RWKV)), full((1, W_RWKV)), full((1, W_RWKV)),
        ],
        out_specs=[tok, tok, tok, tok2, tok2, tok2, tok, tok],
        out_shape=[jax.ShapeDtypeStruct((N_TOK, W_RWKV), F32)] * 3
        + [jax.ShapeDtypeStruct((2, N_TOK, W_RWKV), F32)] * 3
        + [jax.ShapeDtypeStruct((N_TOK, W_RWKV), F32)] * 2,
        compiler_params=_cparams(("arbitrary",)),
        name="rwkv_prep",
    )(feat, feat, feat, conv, w0, w2bd, a0, a2bd, g2, k_k, k_a, r_k)


N_PAIR = W_RWKV // LANES


def _scan_kernel(r_ref, kap_ref, v_ref, lw_ref, ah_ref, kd_ref, s0_ref, o_ref, sfin_ref,
                 s_scr, rt_scr, kt_scr, kdt_scr, at_scr, cum_scr):
    d = pl.program_id(0)
    j = pl.program_id(1)
    blk = jnp.where(d == 0, j, NBLK - 1 - j)
    pos = (blk - NCB) % LAT_BLKS
    first = (blk < NCB) | jnp.where(d == 0, pos == 0, pos == LAT_BLKS - 1)
    sgn = 1 - 2 * d

    @pl.when(first)
    def _():
        s_scr[...] = s0_ref[...]

    rr = _iota((TM, TM), 0)
    cc = _iota((TM, TM), 1)
    tri = ((rr // CHUNK) == (cc // CHUNK)) & ((rr - cc) * sgn >= 0)
    lw = lw_ref[...]
    cum = _dot(jnp.where(tri, 1.0, 0.0), lw, HIGHEST)
    e_neg = jnp.exp(-cum)
    rt_scr[...] = r_ref[...] * jnp.exp(cum)
    kt_scr[...] = kap_ref[...] * jnp.exp(cum - lw)
    kdt_scr[...] = kd_ref[...] * e_neg
    at_scr[...] = ah_ref[...] * e_neg
    cum_scr[...] = cum

    row = _iota((CHUNK, LANES), 0)
    col = _iota((CHUNK, LANES), 1) % CHUNK
    left = _iota((CHUNK, LANES), 1) < CHUNK
    diff = (row - col) * sgn
    incl = diff >= 0
    strict = diff > 0
    eye = jnp.where(row == col, 1.0, 0.0)
    bd_mask = (_iota((LANES, LANES), 0) // CHUNK) == (_iota((LANES, LANES), 1) // CHUNK)

    def bd(x):
        return jnp.concatenate([jnp.where(left, x, 0.0), jnp.where(left, 0.0, x)], axis=0)

    def mm(a, b):
        return _dot(a, bd(b), HIGHEST)

    def chunk(c, carry):
        ce = jnp.where(d == 0, c, N_CHUNK - 1 - c)
        off = pl.multiple_of(ce * CHUNK, CHUNK)
        rows = pl.ds(off, CHUNK)
        for p in range(N_PAIR):
            ls = slice(p * LANES, (p + 1) * LANES)
            rt, kt, kdt, at = rt_scr[rows, ls], kt_scr[rows, ls], kdt_scr[rows, ls], at_scr[rows, ls]
            v = v_ref[rows, ls]
            cumc = cum_scr[rows, ls]
            tot = jnp.where(d == 0, cumc[CHUNK - 1:CHUNK, :], cumc[0:1, :])
            a2 = jnp.concatenate([kt, rt], axis=0)
            b2 = jnp.concatenate([jnp.where(left, at, 0.0), jnp.where(left, 0.0, at),
                                  jnp.where(left, kdt, 0.0), jnp.where(left, 0.0, kdt)], axis=0)
            gram = _dot_nt(a2, b2, HIGHEST)
            la = jnp.where(strict, gram[0:CHUNK, 0:LANES], 0.0)
            lk = jnp.where(strict, gram[0:CHUNK, LANES:], 0.0)
            ra = jnp.where(incl, gram[CHUNK:, 0:LANES], 0.0)
            rk = jnp.where(incl, gram[CHUNK:, LANES:], 0.0)
            b = 8
            l8 = jnp.where((row // b) == (col // b), la, 0.0)
            l8_2 = mm(l8, l8)
            l8_4 = mm(l8_2, l8_2)
            t = mm(mm(eye - l8, eye + l8_2), eye + l8_4)
            while b < CHUNK:
                offd = ((row // (2 * b)) == (col // (2 * b))) & ((row // b) != (col // b))
                t = t - mm(mm(t, jnp.where(offd, la, 0.0)), t)
                b *= 2
            s = s_scr[p]
            xp = _dot_nt(a2, s, HIGHEST)
            u = mm(t, xp[0:CHUNK] + mm(lk, v))
            o = xp[CHUNK:] + _dot(jnp.concatenate([rk, ra], axis=1),
                                  jnp.concatenate([bd(v), -bd(u)], axis=0), HIGHEST)
            o_ref[rows, ls] = o
            upd = _dot_tn(jnp.concatenate([v, u], axis=0), jnp.concatenate([kdt, -at], axis=0), HIGHEST)
            s_scr[p] = (s + jnp.where(bd_mask, upd, 0.0)) * jnp.exp(tot)
        return carry

    lax.fori_loop(0, N_CHUNK, chunk, 0)

    @pl.when(blk < NCB)
    def _():
        sfin_ref[...] = s_scr[...]


def _rwkv_scan(r, kap, v, lw, ah, kd, s0):
    def blk_of(d, j):
        return jnp.where(d == 0, j, NBLK - 1 - j)

    def seq_of(d, j):
        b = blk_of(d, j)
        return jnp.where(b < NCB, b, NCB + (b - NCB) // LAT_BLKS)

    tok = pl.BlockSpec((TM, W_RWKV), lambda d, j: (blk_of(d, j), 0))
    tok2 = pl.BlockSpec((None, TM, W_RWKV), lambda d, j: (d, blk_of(d, j), 0))
    return pl.pallas_call(
        _scan_kernel,
        grid=(2, NBLK),
        in_specs=[tok, tok, tok, tok2, tok2, tok2,
                  pl.BlockSpec((None, None, N_PAIR, LANES, LANES), lambda d, j: (d, seq_of(d, j), 0, 0, 0))],
        out_specs=[tok2,
                   pl.BlockSpec((None, None, N_PAIR, LANES, LANES),
                                lambda d, j: (jnp.minimum(blk_of(d, j), NCB - 1), d, 0, 0, 0))],
        out_shape=[jax.ShapeDtypeStruct((2, N_TOK, W_RWKV), F32),
                   jax.ShapeDtypeStruct((BATCH, 2, N_PAIR, LANES, LANES), F32)],
        scratch_shapes=[pltpu.VMEM((N_PAIR, LANES, LANES), F32)] + [pltpu.VMEM((TM, W_RWKV), F32)] * 5,
        compiler_params=_cparams(("arbitrary", "arbitrary")),
        name="rwkv_scan",
    )(r, kap, v, lw, ah, kd, s0)


def _softmax_pv(s, v_bf):
    m = jnp.max(s, axis=-1, keepdims=True)
    p = jnp.exp(s - m)
    l = jnp.sum(p, axis=-1, keepdims=True)
    return _dot(p.astype(BF16), v_bf) * (1.0 / l)


def _head_mask(rows):
    return _iota((rows, LANES), 1) < HEAD_DIM


def _mha_pairs(q, score_fn, v_fn, n_cols):
    rows = q.shape[0]
    left = _head_mask(rows)
    outs = []
    for c in range(n_cols):
        qc = q[:, c * LANES:(c + 1) * LANES]
        halves = []
        for half in range(2):
            keep = left if half == 0 else jnp.logical_not(left)
            qm = jnp.where(keep, qc, 0.0).astype(BF16)
            halves.append(_softmax_pv(score_fn(c, half, qm), v_fn(c, half)))
        outs.append(jnp.where(left, halves[0], halves[1]))
    return outs[0] if n_cols == 1 else jnp.concatenate(outs, axis=1)


def _gqa_heads(q, k_bf, v_bf):
    k_sw = pltpu.roll(k_bf.astype(F32), HEAD_DIM, axis=1).astype(BF16)
    v_sw = pltpu.roll(v_bf.astype(F32), HEAD_DIM, axis=1).astype(BF16)

    def score_fn(c, half, qm):
        return _dot_nt(qm, k_bf if half == c // 2 else k_sw)

    def v_fn(c, half):
        return v_bf if half == c // 2 else v_sw

    return _mha_pairs(q, score_fn, v_fn, W_GQA // LANES)


def _ctx_attn_kernel(naq_ref, nak_ref, nav_ref, gq_ref, gk_ref, gv_ref, ona_ref, og_ref):
    k = nak_ref[...].astype(BF16)
    v = nav_ref[...].astype(BF16)
    ona_ref[...] = _mha_pairs(
        naq_ref[...],
        lambda c, half, qm: _dot_nt(qm, k[:, c * LANES:(c + 1) * LANES]),
        lambda c, half: v[:, c * LANES:(c + 1) * LANES],
        W_NA // LANES)
    og_ref[...] = _gqa_heads(gq_ref[...], gk_ref[...].astype(BF16), gv_ref[...].astype(BF16))


def _ctx_attention(naq, nak, nav, gq, gk, gv):
    spec = lambda w: pl.BlockSpec((SEQ, w), lambda b: (b, 0))
    return pl.pallas_call(
        _ctx_attn_kernel,
        grid=(BATCH,),
        in_specs=[spec(W_NA), spec(W_NA), spec(W_NA), spec(W_GQA), spec(W_GQA_KV), spec(W_GQA_KV)],
        out_specs=[spec(W_NA), spec(W_GQA)],
        out_shape=[jax.ShapeDtypeStruct((N_CTX, W_NA), F32), jax.ShapeDtypeStruct((N_CTX, W_GQA), F32)],
        compiler_params=_cparams(("arbitrary",)),
        name="ctx_attention",
    )(naq, nak, nav, gq, gk, gv)


N_BAND = NA_ROWS * GRID_W


def _lat_na_kernel(q_ref, k_ref, v_ref, kc_ref, vc_ref, tb_ref, o_ref):
    r = pl.program_id(1)
    r0 = jnp.clip(r - NA_ROWS // 2, 0, GRID_ROWS - NA_ROWS)
    band = pl.ds(pl.multiple_of(r0 * GRID_W, GRID_W), N_BAND)
    dr0 = r0 - r + NA_ROWS - 1

    def score_fn(c, half, qm):
        cols = slice(c * LANES, (c + 1) * LANES)
        s_nb = _dot_nt(qm, k_ref[band, cols].astype(BF16))
        h = 2 * c + half
        bias = jnp.concatenate([tb_ref[h, dr0 + jj] for jj in range(0, NA_ROWS, 2)], axis=1)
        s_ctx = _dot_nt(qm, kc_ref[:, cols].astype(BF16))
        return jnp.concatenate([s_nb + bias, s_ctx], axis=1)

    def v_fn(c, half):
        cols = slice(c * LANES, (c + 1) * LANES)
        return jnp.concatenate([v_ref[band, cols], vc_ref[:, cols]], axis=0).astype(BF16)

    o_ref[...] = _mha_pairs(q_ref[...], score_fn, v_fn, W_NA // LANES)


def _lat_na(naq, nak, nav, kc, vc, tb):
    q_blk0 = N_CTX // GRID_W
    seq_blk0 = N_CTX // DEC_SEQ
    seq = pl.BlockSpec((DEC_SEQ, W_NA), lambda b, r: (seq_blk0 + b, 0))
    cache = pl.BlockSpec((None, PAST_LEN, W_NA), lambda b, r: (b, 0, 0))
    return pl.pallas_call(
        _lat_na_kernel,
        grid=(DEC_BATCH, GRID_ROWS),
        in_specs=[pl.BlockSpec((GRID_W, W_NA), lambda b, r: (q_blk0 + b * GRID_ROWS + r, 0)),
                  seq, seq, cache, cache,
                  pl.BlockSpec((H_NA, N_DR - 1, GRID_W, LANES), lambda b, r: (0, 0, 0, 0))],
        out_specs=pl.BlockSpec((GRID_W, W_NA), lambda b, r: (b * GRID_ROWS + r, 0)),
        out_shape=jax.ShapeDtypeStruct((N_LAT, W_NA), F32),
        compiler_params=_cparams(("arbitrary", "arbitrary")),
        name="latent_na",
    )(naq, nak, nav, kc, vc, tb)


def _lat_gqa_kernel(q_ref, k_ref, v_ref, kc_ref, vc_ref, o_ref):
    k = jnp.concatenate([kc_ref[...], k_ref[...]], axis=0).astype(BF16)
    v = jnp.concatenate([vc_ref[...], v_ref[...]], axis=0).astype(BF16)
    o_ref[...] = _gqa_heads(q_ref[...], k, v)


def _lat_gqa(gq, gk, gv, kc, vc):
    seq_blk0 = N_CTX // DEC_SEQ
    seq = pl.BlockSpec((DEC_SEQ, W_GQA_KV), lambda b, i: (seq_blk0 + b, 0))
    cache = pl.BlockSpec((None, PAST_LEN, W_GQA_KV), lambda b, i: (b, 0, 0))
    return pl.pallas_call(
        _lat_gqa_kernel,
        grid=(DEC_BATCH, LAT_BLKS),
        in_specs=[pl.BlockSpec((TM, W_GQA), lambda b, i: (NCB + b * LAT_BLKS + i, 0)), seq, seq, cache, cache],
        out_specs=pl.BlockSpec((TM, W_GQA), lambda b, i: (b * LAT_BLKS + i, 0)),
        out_shape=jax.ShapeDtypeStruct((N_LAT, W_GQA), F32),
        compiler_params=_cparams(("arbitrary", "arbitrary")),
        name="latent_gqa",
    )(gq, gk, gv, kc, vc)


def _mix_ffn_kernel(x_ref, mod_ref, of_ref, ob_ref, g_ref, bonus_ref, lnxw_ref, lnxb_ref, ona_ref, og_ref,
                    wout_ref, ln1w_ref, ln1b_ref, wfi_ref, wfo_ref, ln2w_ref, ln2b_ref, y_ref):
    row = _mod_row(pl.program_id(0))
    mod = lambda n: mod_ref[pl.ds(row, 1), n * D_MODEL:(n + 1) * D_MODEL]
    o = of_ref[...] + ob_ref[...]
    mu = _seg64_sum(o) * (1.0 / HEAD_DIM)
    oc = o - mu
    var = _seg64_sum(oc * oc) * (1.0 / HEAD_DIM)
    o_rwkv = (oc * lax.rsqrt(var + GN_EPS) * lnxw_ref[...] + lnxb_ref[...] + bonus_ref[...]) * g_ref[...]
    mix_in = jnp.concatenate([o_rwkv, ona_ref[...], og_ref[...]], axis=1).astype(BF16)
    mix = _dot(mix_in, wout_ref[...])
    x1 = _layer_norm(DEEPNORM_ALPHA * x_ref[...] + mod(2) * mix, ln1w_ref[...], ln1b_ref[...])
    h = _dot((x1 * (1.0 + mod(4)) + mod(3)).astype(BF16), wfi_ref[...])
    gate = h[:, :D_FF]
    act = (gate * _sigmoid(gate) * h[:, D_FF:]).astype(BF16)
    ffn = _dot(act, wfo_ref[...])
    y_ref[...] = _layer_norm(DEEPNORM_ALPHA * x1 + mod(5) * ffn, ln2w_ref[...], ln2b_ref[...])


def _mix_ffn(x_all, mod_l, o_dirs, g, bonus, lnx_w, lnx_b, o_na, o_g, w_out_bf, ln1_w, ln1_b,
             w_ffn_in_bf, w_ffn_out_bf, ln2_w, ln2_b):
    tok = lambda w: pl.BlockSpec((TM, w), lambda i: (i, 0))
    once = lambda shape: pl.BlockSpec(shape, lambda i: (0,) * len(shape), pipeline_mode=pl.Buffered(1))
    return pl.pallas_call(
        _mix_ffn_kernel,
        grid=(NBLK,),
        in_specs=[
            tok(D_MODEL), once((MOD_ROWS, 6 * D_MODEL)),
            pl.BlockSpec((None, TM, W_RWKV), lambda i: (0, i, 0)),
            pl.BlockSpec((None, TM, W_RWKV), lambda i: (1, i, 0)),
            tok(W_RWKV), tok(W_RWKV), once((1, W_RWKV)), once((1, W_RWKV)), tok(W_NA), tok(W_GQA),
            once((D_MODEL, D_MODEL)), once((1, D_MODEL)), once((1, D_MODEL)),
            once((D_MODEL, 2 * D_FF)), once((D_FF, D_MODEL)), once((1, D_MODEL)), once((1, D_MODEL)),
        ],
        out_specs=tok(D_MODEL),
        out_shape=jax.ShapeDtypeStruct((N_TOK, D_MODEL), F32),
        compiler_params=_cparams(("arbitrary",)),
        name="mix_ffn",
    )(x_all, mod_l, o_dirs, o_dirs, g, bonus, lnx_w, lnx_b, o_na, o_g, w_out_bf, ln1_w, ln1_b,
      w_ffn_in_bf, w_ffn_out_bf, ln2_w, ln2_b)


def _rope_tables():
    t = jnp.arange(DEC_SEQ)
    inv = ROPE_BASE ** (-jnp.arange(ROPE_FREQ, dtype=F32) / ROPE_FREQ)
    ang_r = (t // GRID_W).astype(F32)[:, None] * inv
    ang_c = (t % GRID_W).astype(F32)[:, None] * inv
    cos = jnp.concatenate([jnp.cos(ang_r)] * 2 + [jnp.cos(ang_c)] * 2, axis=1)
    sin = jnp.concatenate([-jnp.sin(ang_r), jnp.sin(ang_r), -jnp.sin(ang_c), jnp.sin(ang_c)], axis=1)
    cos = jnp.concatenate([jnp.ones((TM, HEAD_DIM), F32), cos], axis=0)
    sin = jnp.concatenate([jnp.zeros((TM, HEAD_DIM), F32), sin], axis=0)
    return jnp.tile(cos, (1, LANES // HEAD_DIM)), jnp.tile(sin, (1, LANES // HEAD_DIM))


def _block_diag2(w):
    z = jnp.zeros_like(w[0])
    return jnp.concatenate([jnp.concatenate([w[0], z], axis=1), jnp.concatenate([z, w[1]], axis=1)], axis=0)


def _pair_states(s):
    lead = s.shape[:-3]
    s = s.reshape(lead + (N_PAIR, 2, HEAD_DIM, HEAD_DIM))
    z = jnp.zeros_like(s[..., 0, :, :])
    top = jnp.concatenate([s[..., 0, :, :], z], axis=-1)
    bot = jnp.concatenate([z, s[..., 1, :, :]], axis=-1)
    return jnp.concatenate([top, bot], axis=-2)


def _unpair_states(s):
    lead = s.shape[:-3]
    a = s[..., :HEAD_DIM, :HEAD_DIM]
    b = s[..., HEAD_DIM:, HEAD_DIM:]
    return jnp.stack([a, b], axis=-3).reshape(lead + (H_RWKV, HEAD_DIM, HEAD_DIM))


def kernel(x_prompt, x_sample, state_rwkv, cache_na_k, cache_na_v, cache_gqa_k, cache_gqa_v, c, c_ctx,
           w_mod, b_mod, w_in, rwkv_conv, rwkv_w0, rwkv_w2, rwkv_a0, rwkv_a2, rwkv_g2, rwkv_k_k, rwkv_k_a,
           rwkv_r_k, rwkv_lnx_w, rwkv_lnx_b, na_rpb, gqa_q_norm, gqa_k_norm, w_out, ln1_w, ln1_b,
           w_ffn_in, w_ffn_out, ln2_w, ln2_b):
    x_all = jnp.concatenate([x_prompt.reshape(N_CTX, D_MODEL), x_sample.reshape(N_LAT, D_MODEL)], axis=0)
    cc = jnp.concatenate([c_ctx[None], c, jnp.zeros((MOD_ROWS - 1 - DEC_BATCH, D_MODEL), F32)], axis=0)
    mod_all = _modulation(cc, w_mod, b_mod)
    tb_all = _bias_tables(na_rpb)
    cos_tab, sin_tab = _rope_tables()
    s_lat = _pair_states(jnp.transpose(state_rwkv, (1, 2, 0, 3, 4, 5)))
    s0_all = jnp.concatenate([jnp.zeros((DEPTH, 2, BATCH, N_PAIR, LANES, LANES), F32), s_lat], axis=2)

    st_rwkv, st_na_k, st_na_v, st_g_k, st_g_v = [], [], [], [], []
    row = lambda a: a.reshape(1, -1)
    for l in range(DEPTH):
        mod_l = mod_all[l]
        feat, naq, nak, nav, gq, gk, gv = _inproj(
            x_all, mod_l, w_in[l].astype(BF16), jnp.tile(row(gqa_q_norm[l]), (1, H_GQA)),
            jnp.tile(row(gqa_k_norm[l]), (1, H_GQA_KV)), cos_tab, sin_tab)
        r, kap, v, lw, ah, kd, g, bonus = _rwkv_prep(
            feat, rwkv_conv[l], row(rwkv_w0[l]), _block_diag2(rwkv_w2[l]), row(rwkv_a0[l]),
            _block_diag2(rwkv_a2[l]), rwkv_g2[l], row(rwkv_k_k[l]), row(rwkv_k_a[l]), row(rwkv_r_k[l]))
        o_dirs, s_fin = _rwkv_scan(r, kap, v, lw, ah, kd, s0_all[l])
        o_na_ctx, o_g_ctx = _ctx_attention(naq, nak, nav, gq, gk, gv)
        o_na_lat = _lat_na(naq, nak, nav, cache_na_k[:, l].reshape(DEC_BATCH, PAST_LEN, W_NA),
                           cache_na_v[:, l].reshape(DEC_BATCH, PAST_LEN, W_NA), tb_all[l])
        o_g_lat = _lat_gqa(gq, gk, gv, cache_gqa_k[:, l].reshape(DEC_BATCH, PAST_LEN, W_GQA_KV),
                           cache_gqa_v[:, l].reshape(DEC_BATCH, PAST_LEN, W_GQA_KV))
        o_na = jnp.concatenate([o_na_ctx, o_na_lat], axis=0)
        o_g = jnp.concatenate([o_g_ctx, o_g_lat], axis=0)
        x_all = _mix_ffn(x_all, mod_l, o_dirs, g, bonus, row(rwkv_lnx_w[l]), row(rwkv_lnx_b[l]), o_na, o_g,
                         w_out[l].astype(BF16), row(ln1_w[l]), row(ln1_b[l]), w_ffn_in[l].astype(BF16),
                         w_ffn_out[l].astype(BF16), row(ln2_w[l]), row(ln2_b[l]))
        st_rwkv.append(_unpair_states(s_fin))
        st_na_k.append(nak[:N_CTX].reshape(BATCH, SEQ, H_NA, HEAD_DIM))
        st_na_v.append(nav[:N_CTX].reshape(BATCH, SEQ, H_NA, HEAD_DIM))
        st_g_k.append(gk[:N_CTX].reshape(BATCH, SEQ, H_GQA_KV, HEAD_DIM))
        st_g_v.append(gv[:N_CTX].reshape(BATCH, SEQ, H_GQA_KV, HEAD_DIM))
    y_prompt = x_all[:N_CTX].reshape(BATCH, SEQ, D_MODEL)
    y_sample = x_all[N_CTX:].reshape(DEC_BATCH, DEC_SEQ, D_MODEL)
    return (y_prompt, y_sample, jnp.stack(st_rwkv, 1), jnp.stack(st_na_k, 1), jnp.stack(st_na_v, 1),
            jnp.stack(st_g_k, 1), jnp.stack(st_g_v, 1))
```

```python
import jax
import jax.numpy as jnp
from jax import lax
from jax.experimental import pallas as pl
from jax.experimental.pallas import tpu as pltpu

F32 = jnp.float32
BF16 = jnp.bfloat16
HIGHEST = lax.Precision.HIGHEST

D_MODEL = 1024
BATCH = 16
SEQ = 256
DEPTH = 4
DEC_BATCH = 2
DEC_SEQ = 1024
PAST_LEN = 512
GRID_W = 64
GRID_ROWS = DEC_SEQ // GRID_W
HEAD_DIM = 64
H_RWKV = 4
H_NA = 4
H_GQA = 8
H_GQA_KV = 2
W_RWKV = H_RWKV * HEAD_DIM
W_NA = H_NA * HEAD_DIM
W_GQA = H_GQA * HEAD_DIM
W_GQA_KV = H_GQA_KV * HEAD_DIM
LORA_W = 64
LORA_A = 64
LORA_G = 128
RWKV_IN = 3 * W_RWKV + 2 * LORA_W + 2 * LORA_A + LORA_G
NA_IN = 3 * W_NA
GQA_IN = W_GQA + 2 * W_GQA_KV
D_IN = RWKV_IN + NA_IN + GQA_IN
NA_ROWS = 8
NA_COLS = 16
ROPE_BASE = 10000.0
ROPE_FREQ = HEAD_DIM // 4
D_FF = ((8 * D_MODEL + 3 * 256 - 1) // (3 * 256)) * 256
DEEPNORM_ALPHA = (2 * DEPTH) ** 0.25
LN_EPS = 1e-5
RMS_EPS = 1e-6
GN_EPS = 64e-5
NEG_INF = -1e30
ATTN_SCALE = HEAD_DIM ** -0.5

LANES = 128
TM = 256
N_CTX = BATCH * SEQ
N_LAT = DEC_BATCH * DEC_SEQ
N_TOK = N_CTX + N_LAT
NBLK = N_TOK // TM
NCB = N_CTX // TM
LAT_BLKS = DEC_SEQ // TM
CHUNK = 64
N_CHUNK = TM // CHUNK
N_PAIR = W_RWKV // LANES
MOD_ROWS = 8
VMEM_LIMIT = 48 * 1024 * 1024
N_DR = 2 * NA_ROWS - 1
N_DC = 2 * NA_COLS - 1
N_BAND = NA_ROWS * GRID_W


def _cparams(n_grid):
    return pltpu.CompilerParams(dimension_semantics=("arbitrary",) * n_grid, vmem_limit_bytes=VMEM_LIMIT)


def _iota(shape, dim):
    return lax.broadcasted_iota(jnp.int32, shape, dim)


NN = (((1,), (0,)), ((), ()))
NT = (((1,), (1,)), ((), ()))
TN = (((0,), (0,)), ((), ()))


def _dg(a, b, dims=NN, precision=None):
    return lax.dot_general(a, b, dims, preferred_element_type=F32, precision=precision)


def _split(x):
    hi = x.astype(BF16)
    return hi, (x - hi.astype(F32)).astype(BF16)


def _dgs(a_s, b_s, dims=NN):
    (ah, al), (bh, bl) = a_s, b_s
    ca, cb = dims[0][0][0], dims[0][1][0]
    return _dg(jnp.concatenate([ah, ah, al], axis=ca), jnp.concatenate([bh, bl, bh], axis=cb), dims)


def _dg3(a, b, dims=NN):
    return _dgs(_split(a), _split(b), dims)


def _sigmoid(x):
    return 1.0 / (1.0 + jnp.exp(-x))


def _softplus(x):
    return jnp.maximum(x, 0.0) + jnp.log(1.0 + jnp.exp(-jnp.abs(x)))


def _seg64_sum(x):
    rows, width = x.shape
    lo = _iota((rows, LANES), 1) < HEAD_DIM
    outs = []
    for c in range(width // LANES):
        blk = x[:, c * LANES:(c + 1) * LANES]
        s_lo = jnp.sum(jnp.where(lo, blk, 0.0), axis=-1, keepdims=True)
        s_hi = jnp.sum(jnp.where(lo, 0.0, blk), axis=-1, keepdims=True)
        outs.append(jnp.where(lo, s_lo, s_hi))
    return outs[0] if len(outs) == 1 else jnp.concatenate(outs, axis=1)


def _layer_norm(x, w, b):
    mu = jnp.mean(x, axis=-1, keepdims=True)
    xc = x - mu
    var = jnp.mean(xc * xc, axis=-1, keepdims=True)
    return xc * lax.rsqrt(var + LN_EPS) * w + b


def _mod_row(i):
    return jnp.where(i < NCB, 0, 1 + (i - NCB) // LAT_BLKS)


def _layer_spec(l, tail, n_grid, single=False):
    idx = lambda *g: (l,) + (0,) * len(tail)
    del n_grid
    if single:
        return pl.BlockSpec((None,) + tuple(tail), idx, pipeline_mode=pl.Buffered(1))
    return pl.BlockSpec((None,) + tuple(tail), idx)


ANY_SPEC = pl.BlockSpec(memory_space=pl.ANY)


def _mod_kernel(c_ref, w_ref, b_ref, o_ref):
    c = c_ref[...]
    s = c * _sigmoid(c)
    o_ref[...] = _dg(s, w_ref[...], NN, HIGHEST) + b_ref[...]


def _modulation(cc, w_mod, b_mod):
    return pl.pallas_call(
        _mod_kernel,
        grid=(DEPTH, 6),
        in_specs=[
            pl.BlockSpec((MOD_ROWS, D_MODEL), lambda l, j: (0, 0)),
            pl.BlockSpec((None, D_MODEL, D_MODEL), lambda l, j: (l, 0, j)),
            pl.BlockSpec((None, 1, D_MODEL), lambda l, j: (l, 0, j)),
        ],
        out_specs=pl.BlockSpec((None, MOD_ROWS, D_MODEL), lambda l, j: (l, 0, j)),
        out_shape=jax.ShapeDtypeStruct((DEPTH, MOD_ROWS, 6 * D_MODEL), F32),
        compiler_params=_cparams(2),
        name="modulation",
    )(cc, w_mod, b_mod.reshape(DEPTH, 1, 6 * D_MODEL))


def _bias_kernel(rpb_ref, o_ref):
    l = pl.program_id(0)
    q = _iota((GRID_W, LANES), 0)
    x = _iota((GRID_W, LANES), 1)
    c = x % GRID_W
    right = x >= GRID_W
    dc = jnp.clip(c - q, 1 - NA_COLS, NA_COLS - 1) + NA_COLS - 1
    c0 = jnp.clip(q - NA_COLS // 2, 0, GRID_W - NA_COLS)
    in_win = (c >= c0) & (c < c0 + NA_COLS)

    def body(t, carry):
        h = t // (N_DR - 1)
        dr = t % (N_DR - 1)
        base = ((l * H_NA + h) * N_DR + dr) * N_DC
        acc = jnp.zeros((GRID_W, LANES), F32)
        for d in range(N_DC):
            s0 = rpb_ref[base + d]
            s1 = rpb_ref[base + N_DC + d]
            acc = jnp.where(dc == d, jnp.where(right, s1, s0), acc)
        o_ref[h, dr] = jnp.where(in_win, acc, NEG_INF)
        return carry

    lax.fori_loop(0, H_NA * (N_DR - 1), body, 0)


def _bias_tables(na_rpb):
    return pl.pallas_call(
        _bias_kernel,
        grid=(DEPTH,),
        in_specs=[pl.BlockSpec(memory_space=pltpu.SMEM)],
        out_specs=pl.BlockSpec((None, H_NA, N_DR - 1, GRID_W, LANES), lambda l: (l, 0, 0, 0, 0)),
        out_shape=jax.ShapeDtypeStruct((DEPTH, H_NA, N_DR - 1, GRID_W, LANES), F32),
        compiler_params=_cparams(1),
        name="na_bias_tables",
    )(na_rpb.reshape(-1))


def _rope(x, cos, sin):
    k = x.shape[1] // LANES
    cosf = cos if k == 1 else jnp.concatenate([cos] * k, axis=1)
    sinf = sin if k == 1 else jnp.concatenate([sin] * k, axis=1)
    first = (_iota(x.shape, 1) % (2 * ROPE_FREQ)) < ROPE_FREQ
    partner = jnp.where(first, pltpu.roll(x, x.shape[1] - ROPE_FREQ, axis=1), pltpu.roll(x, ROPE_FREQ, axis=1))
    return x * cosf + partner * sinf


def _inproj_kernel(x_ref, mod_ref, w_ref, qn_ref, kn_ref, cos_ref, sin_ref, c0_ref, c1_ref, c2_ref, c3_ref,
                   feat_ref, naq_ref, nak_ref, nav_ref, gq_ref, gk_ref, gv_ref,
                   cnak_ref, cnav_ref, cgk_ref, cgv_ref):
    del c0_ref, c1_ref, c2_ref, c3_ref
    i = pl.program_id(0)
    row = _mod_row(i)
    shift1 = mod_ref[pl.ds(row, 1), 0:D_MODEL]
    scale1 = mod_ref[pl.ds(row, 1), D_MODEL:2 * D_MODEL]
    xm = (x_ref[...] * (1.0 + scale1) + shift1).astype(BF16)
    proj = _dg(xm, w_ref[...])
    o_na = RWKV_IN
    o_g = RWKV_IN + NA_IN
    feat_ref[...] = proj[:, :RWKV_IN]
    naq_ref[...] = proj[:, o_na:o_na + W_NA] * ATTN_SCALE
    nak = proj[:, o_na + W_NA:o_na + 2 * W_NA]
    nav = proj[:, o_na + 2 * W_NA:o_na + 3 * W_NA]
    q = proj[:, o_g:o_g + W_GQA]
    k = proj[:, o_g + W_GQA:o_g + W_GQA + W_GQA_KV]
    gv = proj[:, o_g + W_GQA + W_GQA_KV:]
    cos = cos_ref[...]
    sin = sin_ref[...]
    q = q * lax.rsqrt(_seg64_sum(q * q) * (1.0 / HEAD_DIM) + RMS_EPS) * qn_ref[...]
    k = k * lax.rsqrt(_seg64_sum(k * k) * (1.0 / HEAD_DIM) + RMS_EPS) * kn_ref[...]
    gk = _rope(k, cos, sin)
    gq_ref[...] = _rope(q, cos, sin) * ATTN_SCALE
    nak_ref[...] = nak
    nav_ref[...] = nav
    gk_ref[...] = gk
    gv_ref[...] = gv

    @pl.when(i < NCB)
    def _():
        cnak_ref[...] = nak
        cnav_ref[...] = nav
        cgk_ref[...] = gk
        cgv_ref[...] = gv


def _inproj(l, x_all, mod_all, w_in_bf, qn, kn, cos_tab, sin_tab, caches):
    tab_idx = lambda i: (jnp.where(i < NCB, 0, 1 + (i - NCB) % LAT_BLKS), 0)
    widths = (RWKV_IN, W_NA, W_NA, W_NA, W_GQA, W_GQA_KV, W_GQA_KV)
    cache_w = (W_NA, W_NA, W_GQA_KV, W_GQA_KV)
    cache_spec = lambda w: pl.BlockSpec((None, None, SEQ, w), lambda i: (jnp.minimum(i, NCB - 1), l, 0, 0))
    n_in = 7
    return pl.pallas_call(
        _inproj_kernel,
        grid=(NBLK,),
        in_specs=[
            pl.BlockSpec((TM, D_MODEL), lambda i: (i, 0)),
            _layer_spec(l, (MOD_ROWS, 6 * D_MODEL), 1),
            _layer_spec(l, (D_MODEL, D_IN), 1),
            _layer_spec(l, (1, W_GQA), 1),
            _layer_spec(l, (1, W_GQA_KV), 1),
            pl.BlockSpec((TM, LANES), tab_idx),
            pl.BlockSpec((TM, LANES), tab_idx),
        ] + [ANY_SPEC] * 4,
        out_specs=[pl.BlockSpec((TM, w), lambda i: (i, 0)) for w in widths] + [cache_spec(w) for w in cache_w],
        out_shape=[jax.ShapeDtypeStruct((N_TOK, w), F32) for w in widths]
        + [jax.ShapeDtypeStruct((BATCH, DEPTH, SEQ, w), F32) for w in cache_w],
        input_output_aliases={n_in + j: len(widths) + j for j in range(4)},
        compiler_params=_cparams(1),
        name="inproj",
    )(x_all, mod_all, w_in_bf, qn, kn, cos_tab, sin_tab, *caches)


def _rwkv_prep_kernel(f_ref, fp_ref, fn_ref, conv_ref, w0_ref, w2_ref, a0_ref, a2_ref, g2_ref,
                      kk_ref, ka_ref, rk_ref,
                      r_ref, kap_ref, v_ref, lw_ref, ah_ref, kd_ref, g_ref, bonus_ref):
    i = pl.program_id(0)
    pos = (i - NCB) % LAT_BLKS
    lat = i >= NCB
    has_prev = jnp.where(lat & (pos != 0), 1.0, 0.0)
    has_next = jnp.where(lat & (pos != LAT_BLKS - 1), 1.0, 0.0)
    x = f_ref[...]
    rows = _iota(x.shape, 0)
    x_prev = jnp.where(rows == 0, fp_ref[7:8, :] * has_prev, pltpu.roll(x, 1, axis=0))
    x_next = jnp.where(rows == TM - 1, fn_ref[0:1, :] * has_next, pltpu.roll(x, TM - 1, axis=0))
    f = x_prev * conv_ref[0:1, :] + x * conv_ref[1:2, :] + x_next * conv_ref[2:3, :]
    o1, o2, o3 = W_RWKV, 2 * W_RWKV, 3 * W_RWKV
    o4 = o3 + 2 * LORA_W
    o5 = o4 + 2 * LORA_A
    r, k, v = f[:, :o1], f[:, o1:o2], f[:, o2:o3]
    wd, ad, gd = f[:, o3:o4], f[:, o4:o5], f[:, o5:]
    log_w = -_softplus(-(w0_ref[...] + _dg3(jnp.tanh(wd), w2_ref[...]))) - 0.5
    a = _sigmoid(a0_ref[...] + _dg3(ad, a2_ref[...]))
    g_ref[...] = _dg3(_sigmoid(gd), g2_ref[...])
    kk = k * kk_ref[...]
    kap = kk / jnp.maximum(jnp.sqrt(_seg64_sum(kk * kk)), 1e-12)
    ka = ka_ref[...]
    kd_sum = jnp.zeros_like(k)
    for d in range(2):
        a_d = a[:, d * W_RWKV:(d + 1) * W_RWKV]
        kd = k * (1.0 + (a_d - 1.0) * ka)
        kd_sum = kd_sum + kd
        lw_ref[d] = -jnp.exp(log_w[:, d * W_RWKV:(d + 1) * W_RWKV])
        ah_ref[d] = a_d * kap
        kd_ref[d] = kd
    r_ref[...] = r
    kap_ref[...] = kap
    v_ref[...] = v
    bonus_ref[...] = _seg64_sum(r * kd_sum * rk_ref[...]) * v


def _rwkv_prep(l, feat, conv, w0, w2bd, a0, a2bd, g2, k_k, k_a, r_k):
    sub = TM // 8
    tok = pl.BlockSpec((TM, W_RWKV), lambda i: (i, 0))
    tok2 = pl.BlockSpec((2, TM, W_RWKV), lambda i: (0, i, 0))
    return pl.pallas_call(
        _rwkv_prep_kernel,
        grid=(NBLK,),
        in_specs=[
            pl.BlockSpec((TM, RWKV_IN), lambda i: (i, 0)),
            pl.BlockSpec((8, RWKV_IN), lambda i: (jnp.maximum(i * sub - 1, 0), 0)),
            pl.BlockSpec((8, RWKV_IN), lambda i: (jnp.minimum((i + 1) * sub, NBLK * sub - 1), 0)),
            _layer_spec(l, (3, RWKV_IN), 1), _layer_spec(l, (1, 2 * W_RWKV), 1),
            _layer_spec(l, (2 * LORA_W, 2 * W_RWKV), 1), _layer_spec(l, (1, 2 * W_RWKV), 1),
            _layer_spec(l, (2 * LORA_A, 2 * W_RWKV), 1), _layer_spec(l, (LORA_G, W_RWKV), 1),
            _layer_spec(l, (1, W_RWKV), 1), _layer_spec(l, (1, W_RWKV), 1), _layer_spec(l, (1, W_RWKV), 1),
        ],
        out_specs=[tok, tok, tok, tok2, tok2, tok2, tok, tok],
        out_shape=[jax.ShapeDtypeStruct((N_TOK, W_RWKV), F32)] * 3
        + [jax.ShapeDtypeStruct((2, N_TOK, W_RWKV), F32)] * 3
        + [jax.ShapeDtypeStruct((N_TOK, W_RWKV), F32)] * 2,
        compiler_params=_cparams(1),
        name="rwkv_prep",
    )(feat, feat, feat, conv, w0, w2bd, a0, a2bd, g2, k_k, k_a, r_k)


def _scan_kernel(r_ref, kap_ref, v_ref, lw_ref, ah_ref, kd_ref, s0_ref, sf_in_ref, o_ref, sfin_ref,
                 s_scr, rt_scr, kt_scr, kdt_scr, at_scr, cum_scr, m_scr, n_scr, q_scr, o1_scr):
    del sf_in_ref
    d = pl.program_id(0)
    j = pl.program_id(1)
    blk = jnp.where(d == 0, j, NBLK - 1 - j)
    pos = (blk - NCB) % LAT_BLKS
    sgn = 1 - 2 * d

    @pl.when(blk < NCB)
    def _():
        s_scr[...] = jnp.zeros_like(s_scr)

    @pl.when((blk >= NCB) & jnp.where(d == 0, pos == 0, pos == LAT_BLKS - 1))
    def _():
        s_scr[...] = s0_ref[...]

    rr = _iota((TM, TM), 0)
    cc = _iota((TM, TM), 1)
    tri = jnp.where(((rr // CHUNK) == (cc // CHUNK)) & ((rr - cc) * sgn >= 0), 1.0, 0.0).astype(BF16)
    lw = lw_ref[...]
    lw_hi = lw.astype(BF16)
    lw_r = lw - lw_hi.astype(F32)
    lw_mid = lw_r.astype(BF16)
    lw_lo = (lw_r - lw_mid.astype(F32)).astype(BF16)
    cum = _dg(jnp.concatenate([tri, tri, tri], axis=1), jnp.concatenate([lw_hi, lw_mid, lw_lo], axis=0))
    e_neg = jnp.exp(-cum)
    rt_scr[...] = r_ref[...] * jnp.exp(cum)
    kt_scr[...] = kap_ref[...] * jnp.exp(cum - lw)
    kdt_scr[...] = kd_ref[...] * e_neg
    at_scr[...] = ah_ref[...] * e_neg
    cum_scr[...] = cum

    row = _iota((CHUNK, LANES), 0)
    col = _iota((CHUNK, LANES), 1) % CHUNK
    left = _iota((CHUNK, LANES), 1) < CHUNK
    diff = (row - col) * sgn
    incl = diff >= 0
    strict = diff > 0
    eye = jnp.where(row == col, 1.0, 0.0)
    r2 = _iota((LANES, LANES), 0)
    c2 = _iota((LANES, LANES), 1)
    bd_mask = (r2 // CHUNK) == (c2 // CHUNK)
    eye2 = jnp.where(r2 == c2, 1.0, 0.0)

    zero_bf = jnp.zeros((CHUNK, LANES), BF16)

    def bd(xs):
        return tuple(jnp.concatenate([jnp.where(left, y, zero_bf), jnp.where(left, zero_bf, y)], axis=0)
                     for y in xs)

    def cat(xs, ys, axis):
        return tuple(jnp.concatenate([x, y], axis=axis) for x, y in zip(xs, ys))

    def mm(a_s, b_s):
        return _dgs(a_s, bd(b_s))

    units = [(c, p) for c in range(N_CHUNK) for p in range(N_PAIR)]
    each = lambda f, *lists: [f(*args) for args in zip(*lists)]

    def load(ref):
        return [ref[c * CHUNK:(c + 1) * CHUNK, p * LANES:(p + 1) * LANES] for c, p in units]

    rt, v = load(rt_scr), load(v_ref)
    kt_s, rt_s, kdt_s, at_s, v_s = (each(_split, x) for x in (load(kt_scr), rt, load(kdt_scr), load(at_scr), v))
    gam = [jnp.exp(jnp.where(d == 0, cum_scr[(c + 1) * CHUNK - 1:(c + 1) * CHUNK, p * LANES:(p + 1) * LANES],
                             cum_scr[c * CHUNK:c * CHUNK + 1, p * LANES:(p + 1) * LANES])) for c, p in units]
    gram = each(lambda k, r, a, kd: _dgs(cat(k, r, 0), cat(bd(a), bd(kd), 0), NT), kt_s, rt_s, at_s, kdt_s)
    la = [jnp.where(strict, g[0:CHUNK, 0:LANES], 0.0) for g in gram]
    lk_s = [_split(jnp.where(strict, g[0:CHUNK, LANES:], 0.0)) for g in gram]
    ra_s = [_split(jnp.where(incl, g[CHUNK:, 0:LANES], 0.0)) for g in gram]
    rk_s = [_split(jnp.where(incl, g[CHUNK:, LANES:], 0.0)) for g in gram]
    lrv = each(lambda lk, rk, vv: mm(cat(lk, rk, 0), vv), lk_s, rk_s, v_s)
    b = 8
    l8 = [jnp.where((row // b) == (col // b), x, 0.0) for x in la]
    l8_s = each(_split, l8)
    l8_2 = each(mm, l8_s, l8_s)
    l8_2s = each(_split, l8_2)
    l8_4 = each(mm, l8_2s, l8_2s)
    p1 = each(lambda x, y: mm(_split(eye - x), _split(eye + y)), l8, l8_2)
    t = each(lambda x, y: mm(_split(x), _split(eye + y)), p1, l8_4)
    while b < CHUNK:
        offd = ((row // (2 * b)) == (col // (2 * b))) & ((row // b) != (col // b))
        t_s = each(_split, t)
        x = each(lambda ts, y: mm(ts, _split(jnp.where(offd, y, 0.0))), t_s, la)
        t = each(lambda tt, xx, ts: tt - mm(_split(xx), ts), t, x, t_s)
        b *= 2
    tx = each(lambda tt, k, y: _dgs(_split(tt), cat(bd(k), bd(_split(y[0:CHUNK])), 1)), t, kt_s, lrv)
    khat_s = [_split(y[:, 0:LANES]) for y in tx]
    w1_s = [_split(y[:, LANES:]) for y in tx]
    rx = each(lambda r, k, w: _dgs(r, cat(bd(k), bd(w), 1)), ra_s, khat_s, w1_s)
    mk = each(lambda k, a: _dgs(k, a, TN), khat_s, at_s)
    nk = each(lambda vv, w, kd, a: _dgs(cat(vv, w, 0), cat(kd, tuple(-y for y in a), 0), TN),
              v_s, w1_s, kdt_s, at_s)
    for i, (c, p) in enumerate(units):
        q_scr[c, p] = rt[i] - rx[i][:, 0:LANES]
        o1_scr[c, p] = lrv[i][CHUNK:] - rx[i][:, LANES:]
        m_scr[c, p] = (eye2 - jnp.where(bd_mask, mk[i], 0.0)) * gam[i]
        n_scr[c, p] = jnp.where(bd_mask, nk[i], 0.0) * gam[i]

    for c in range(N_CHUNK):
        ce = jnp.where(d == 0, c, N_CHUNK - 1 - c)
        rows = pl.ds(pl.multiple_of(ce * CHUNK, CHUNK), CHUNK)
        for p in range(N_PAIR):
            s = s_scr[p]
            o_ref[rows, p * LANES:(p + 1) * LANES] = _dg3(q_scr[ce, p], s, NT) + o1_scr[ce, p]
            s_scr[p] = _dg3(s, m_scr[ce, p]) + n_scr[ce, p]

    @pl.when(blk < NCB)
    def _():
        sfin_ref[...] = s_scr[...]


def _rwkv_scan(l, r, kap, v, lw, ah, kd, s0_lat, s_fin):
    def blk_of(d, j):
        return jnp.where(d == 0, j, NBLK - 1 - j)

    tok = pl.BlockSpec((TM, W_RWKV), lambda d, j: (blk_of(d, j), 0))
    tok2 = pl.BlockSpec((None, TM, W_RWKV), lambda d, j: (d, blk_of(d, j), 0))
    pair = (N_PAIR, LANES, LANES)
    return pl.pallas_call(
        _scan_kernel,
        grid=(2, NBLK),
        in_specs=[tok, tok, tok, tok2, tok2, tok2,
                  pl.BlockSpec((None, None, None) + pair,
                               lambda d, j: (jnp.maximum(blk_of(d, j) - NCB, 0) // LAT_BLKS, l, d, 0, 0, 0)),
                  ANY_SPEC],
        out_specs=[tok2,
                   pl.BlockSpec((None, None, None) + pair,
                                lambda d, j: (jnp.minimum(blk_of(d, j), NCB - 1), l, d, 0, 0, 0))],
        out_shape=[jax.ShapeDtypeStruct((2, N_TOK, W_RWKV), F32),
                   jax.ShapeDtypeStruct((BATCH, DEPTH, 2) + pair, F32)],
        scratch_shapes=[pltpu.VMEM(pair, F32)] + [pltpu.VMEM((TM, W_RWKV), F32)] * 5
        + [pltpu.VMEM((N_CHUNK,) + pair, F32)] * 2 + [pltpu.VMEM((N_CHUNK, N_PAIR, CHUNK, LANES), F32)] * 2,
        input_output_aliases={7: 1},
        compiler_params=_cparams(2),
        name="rwkv_scan",
    )(r, kap, v, lw, ah, kd, s0_lat, s_fin)


def _softmax_pv(s, v_bf):
    m = jnp.max(s, axis=-1, keepdims=True)
    p = jnp.exp(s - m)
    l = jnp.sum(p, axis=-1, keepdims=True)
    return _dg(p.astype(BF16), v_bf) * (1.0 / l)


def _mha_pairs(q, score_fn, v_fn, n_cols):
    rows = q.shape[0]
    left = _iota((rows, LANES), 1) < HEAD_DIM
    outs = []
    for c in range(n_cols):
        qc = q[:, c * LANES:(c + 1) * LANES]
        halves = []
        for half in range(2):
            keep = left if half == 0 else jnp.logical_not(left)
            qm = jnp.where(keep, qc, 0.0).astype(BF16)
            halves.append(_softmax_pv(score_fn(c, half, qm), v_fn(c, half)))
        outs.append(jnp.where(left, halves[0], halves[1]))
    return outs[0] if n_cols == 1 else jnp.concatenate(outs, axis=1)


def _gqa_heads(q, k, v):
    k_bf, v_bf = k.astype(BF16), v.astype(BF16)
    k_sw = pltpu.roll(k, HEAD_DIM, axis=1).astype(BF16)
    v_sw = pltpu.roll(v, HEAD_DIM, axis=1).astype(BF16)

    def score_fn(c, half, qm):
        return _dg(qm, k_bf if half == c // 2 else k_sw, NT)

    def v_fn(c, half):
        return v_bf if half == c // 2 else v_sw

    return _mha_pairs(q, score_fn, v_fn, W_GQA // LANES)


def _ctx_attn_kernel(naq_ref, nak_ref, nav_ref, gq_ref, gk_ref, gv_ref, ona_ref, og_ref):
    k = nak_ref[...].astype(BF16)
    v = nav_ref[...].astype(BF16)
    ona_ref[...] = _mha_pairs(
        naq_ref[...],
        lambda c, half, qm: _dg(qm, k[:, c * LANES:(c + 1) * LANES], NT),
        lambda c, half: v[:, c * LANES:(c + 1) * LANES],
        W_NA // LANES)
    og_ref[...] = _gqa_heads(gq_ref[...], gk_ref[...], gv_ref[...])


def _ctx_attention(naq, nak, nav, gq, gk, gv):
    spec = lambda w: pl.BlockSpec((SEQ, w), lambda b: (b, 0))
    return pl.pallas_call(
        _ctx_attn_kernel,
        grid=(BATCH,),
        in_specs=[spec(W_NA), spec(W_NA), spec(W_NA), spec(W_GQA), spec(W_GQA_KV), spec(W_GQA_KV)],
        out_specs=[spec(W_NA), spec(W_GQA)],
        out_shape=[jax.ShapeDtypeStruct((N_TOK, W_NA), F32), jax.ShapeDtypeStruct((N_TOK, W_GQA), F32)],
        compiler_params=_cparams(1),
        name="ctx_attention",
    )(naq, nak, nav, gq, gk, gv)


def _lat_na_kernel(q_ref, k_ref, v_ref, kc_ref, vc_ref, tb_ref, buf_ref, o_ref):
    del buf_ref
    r = pl.program_id(1)
    r0 = jnp.clip(r - NA_ROWS // 2, 0, GRID_ROWS - NA_ROWS)
    band = pl.ds(pl.multiple_of(r0 * GRID_W, GRID_W), N_BAND)
    dr0 = r0 - r + NA_ROWS - 1

    def score_fn(c, half, qm):
        cols = slice(c * LANES, (c + 1) * LANES)
        s_nb = _dg(qm, k_ref[band, cols].astype(BF16), NT)
        h = 2 * c + half
        bias = jnp.concatenate([tb_ref[h, dr0 + jj] for jj in range(0, NA_ROWS, 2)], axis=1)
        s_ctx = _dg(qm, kc_ref[:, cols].astype(BF16), NT)
        return jnp.concatenate([s_nb + bias, s_ctx], axis=1)

    def v_fn(c, half):
        cols = slice(c * LANES, (c + 1) * LANES)
        return jnp.concatenate([v_ref[band, cols], vc_ref[:, cols]], axis=0).astype(BF16)

    o_ref[...] = _mha_pairs(q_ref[...], score_fn, v_fn, W_NA // LANES)


def _lat_na(l, naq, nak, nav, kc, vc, tb, o_na):
    q_blk0 = N_CTX // GRID_W
    seq_blk0 = N_CTX // DEC_SEQ
    seq = pl.BlockSpec((DEC_SEQ, W_NA), lambda b, r: (seq_blk0 + b, 0))
    cache = pl.BlockSpec((None, None, PAST_LEN, W_NA), lambda b, r: (b, l, 0, 0))
    q_spec = pl.BlockSpec((GRID_W, W_NA), lambda b, r: (q_blk0 + b * GRID_ROWS + r, 0))
    return pl.pallas_call(
        _lat_na_kernel,
        grid=(DEC_BATCH, GRID_ROWS),
        in_specs=[q_spec, seq, seq, cache, cache, _layer_spec(l, (H_NA, N_DR - 1, GRID_W, LANES), 2), ANY_SPEC],
        out_specs=q_spec,
        out_shape=jax.ShapeDtypeStruct((N_TOK, W_NA), F32),
        input_output_aliases={6: 0},
        compiler_params=_cparams(2),
        name="latent_na",
    )(naq, nak, nav, kc, vc, tb, o_na)


def _lat_gqa_kernel(q_ref, k_ref, v_ref, kc_ref, vc_ref, buf_ref, o_ref):
    del buf_ref
    k = jnp.concatenate([kc_ref[...], k_ref[...]], axis=0)
    v = jnp.concatenate([vc_ref[...], v_ref[...]], axis=0)
    o_ref[...] = _gqa_heads(q_ref[...], k, v)


def _lat_gqa(l, gq, gk, gv, kc, vc, o_g):
    seq_blk0 = N_CTX // DEC_SEQ
    seq = pl.BlockSpec((DEC_SEQ, W_GQA_KV), lambda b, i: (seq_blk0 + b, 0))
    cache = pl.BlockSpec((None, None, PAST_LEN, W_GQA_KV), lambda b, i: (b, l, 0, 0))
    q_spec = pl.BlockSpec((TM, W_GQA), lambda b, i: (NCB + b * LAT_BLKS + i, 0))
    return pl.pallas_call(
        _lat_gqa_kernel,
        grid=(DEC_BATCH, LAT_BLKS),
        in_specs=[q_spec, seq, seq, cache, cache, ANY_SPEC],
        out_specs=q_spec,
        out_shape=jax.ShapeDtypeStruct((N_TOK, W_GQA), F32),
        input_output_aliases={5: 0},
        compiler_params=_cparams(2),
        name="latent_gqa",
    )(gq, gk, gv, kc, vc, o_g)


def _mix_ffn_kernel(x_ref, mod_ref, of_ref, ob_ref, g_ref, bonus_ref, lnxw_ref, lnxb_ref, ona_ref, og_ref,
                    wout_ref, ln1w_ref, ln1b_ref, wfi_ref, wfo_ref, ln2w_ref, ln2b_ref, y_ref):
    row = _mod_row(pl.program_id(0))
    mod = lambda n: mod_ref[pl.ds(row, 1), n * D_MODEL:(n + 1) * D_MODEL]
    o = of_ref[...] + ob_ref[...]
    mu = _seg64_sum(o) * (1.0 / HEAD_DIM)
    oc = o - mu
    var = _seg64_sum(oc * oc) * (1.0 / HEAD_DIM)
    o_rwkv = (oc * lax.rsqrt(var + GN_EPS) * lnxw_ref[...] + lnxb_ref[...] + bonus_ref[...]) * g_ref[...]
    mix_in = jnp.concatenate([o_rwkv, ona_ref[...], og_ref[...]], axis=1).astype(BF16)
    mix = _dg(mix_in, wout_ref[...])
    x1 = _layer_norm(DEEPNORM_ALPHA * x_ref[...] + mod(2) * mix, ln1w_ref[...], ln1b_ref[...])
    h = _dg((x1 * (1.0 + mod(4)) + mod(3)).astype(BF16), wfi_ref[...])
    gate = h[:, :D_FF]
    act = (gate * _sigmoid(gate) * h[:, D_FF:]).astype(BF16)
    ffn = _dg(act, wfo_ref[...])
    y_ref[...] = _layer_norm(DEEPNORM_ALPHA * x1 + mod(5) * ffn, ln2w_ref[...], ln2b_ref[...])


def _mix_ffn(l, x_all, mod_all, o_dirs, g, bonus, lnx_w, lnx_b, o_na, o_g, w_out_bf, ln1_w, ln1_b,
             w_ffn_in_bf, w_ffn_out_bf, ln2_w, ln2_b):
    tok = lambda w: pl.BlockSpec((TM, w), lambda i: (i, 0))
    once = lambda *tail: _layer_spec(l, tail, 1, single=True)
    return pl.pallas_call(
        _mix_ffn_kernel,
        grid=(NBLK,),
        in_specs=[
            tok(D_MODEL), once(MOD_ROWS, 6 * D_MODEL),
            pl.BlockSpec((None, TM, W_RWKV), lambda i: (0, i, 0)),
            pl.BlockSpec((None, TM, W_RWKV), lambda i: (1, i, 0)),
            tok(W_RWKV), tok(W_RWKV), once(1, W_RWKV), once(1, W_RWKV), tok(W_NA), tok(W_GQA),
            once(D_MODEL, D_MODEL), once(1, D_MODEL), once(1, D_MODEL),
            once(D_MODEL, 2 * D_FF), once(D_FF, D_MODEL), once(1, D_MODEL), once(1, D_MODEL),
        ],
        out_specs=tok(D_MODEL),
        out_shape=jax.ShapeDtypeStruct((N_TOK, D_MODEL), F32),
        compiler_params=_cparams(1),
        name="mix_ffn",
    )(x_all, mod_all, o_dirs, o_dirs, g, bonus, lnx_w, lnx_b, o_na, o_g, w_out_bf, ln1_w, ln1_b,
      w_ffn_in_bf, w_ffn_out_bf, ln2_w, ln2_b)


def _rope_tables():
    t = jnp.arange(DEC_SEQ)
    inv = ROPE_BASE ** (-jnp.arange(ROPE_FREQ, dtype=F32) / ROPE_FREQ)
    ang_r = (t // GRID_W).astype(F32)[:, None] * inv
    ang_c = (t % GRID_W).astype(F32)[:, None] * inv
    cos = jnp.concatenate([jnp.cos(ang_r)] * 2 + [jnp.cos(ang_c)] * 2, axis=1)
    sin = jnp.concatenate([-jnp.sin(ang_r), jnp.sin(ang_r), -jnp.sin(ang_c), jnp.sin(ang_c)], axis=1)
    cos = jnp.concatenate([jnp.ones((TM, HEAD_DIM), F32), cos], axis=0)
    sin = jnp.concatenate([jnp.zeros((TM, HEAD_DIM), F32), sin], axis=0)
    return jnp.tile(cos, (1, LANES // HEAD_DIM)), jnp.tile(sin, (1, LANES // HEAD_DIM))


def _block_diag2(w):
    z = jnp.zeros_like(w[:, 0])
    return jnp.concatenate([jnp.concatenate([w[:, 0], z], axis=2), jnp.concatenate([z, w[:, 1]], axis=2)], axis=1)


def _pair_states(s):
    lead = s.shape[:-3]
    s = s.reshape(lead + (N_PAIR, 2, HEAD_DIM, HEAD_DIM))
    z = jnp.zeros_like(s[..., 0, :, :])
    top = jnp.concatenate([s[..., 0, :, :], z], axis=-1)
    bot = jnp.concatenate([z, s[..., 1, :, :]], axis=-1)
    return jnp.concatenate([top, bot], axis=-2)


def _unpair_states(s):
    lead = s.shape[:-3]
    a = s[..., :HEAD_DIM, :HEAD_DIM]
    b = s[..., HEAD_DIM:, HEAD_DIM:]
    return jnp.stack([a, b], axis=-3).reshape(lead + (H_RWKV, HEAD_DIM, HEAD_DIM))


def kernel(x_prompt, x_sample, state_rwkv, cache_na_k, cache_na_v, cache_gqa_k, cache_gqa_v, c, c_ctx,
           w_mod, b_mod, w_in, rwkv_conv, rwkv_w0, rwkv_w2, rwkv_a0, rwkv_a2, rwkv_g2, rwkv_k_k, rwkv_k_a,
           rwkv_r_k, rwkv_lnx_w, rwkv_lnx_b, na_rpb, gqa_q_norm, gqa_k_norm, w_out, ln1_w, ln1_b,
           w_ffn_in, w_ffn_out, ln2_w, ln2_b):
    x_all = jnp.concatenate([x_prompt.reshape(N_CTX, D_MODEL), x_sample.reshape(N_LAT, D_MODEL)], axis=0)
    cc = jnp.concatenate([c_ctx[None], c, jnp.zeros((MOD_ROWS - 1 - DEC_BATCH, D_MODEL), F32)], axis=0)
    mod_all = _modulation(cc, w_mod, b_mod)
    tb_all = _bias_tables(na_rpb)
    cos_tab, sin_tab = _rope_tables()
    rows = lambda a: a.reshape(DEPTH, 1, -1)
    w_in_bf, w_out_bf = w_in.astype(BF16), w_out.astype(BF16)
    w_ffn_in_bf, w_ffn_out_bf = w_ffn_in.astype(BF16), w_ffn_out.astype(BF16)
    qn = jnp.tile(rows(gqa_q_norm), (1, 1, H_GQA))
    kn = jnp.tile(rows(gqa_k_norm), (1, 1, H_GQA_KV))
    w2bd, a2bd = _block_diag2(rwkv_w2), _block_diag2(rwkv_a2)
    s0_lat = _pair_states(state_rwkv)
    kc_na = cache_na_k.reshape(DEC_BATCH, DEPTH, PAST_LEN, W_NA)
    vc_na = cache_na_v.reshape(DEC_BATCH, DEPTH, PAST_LEN, W_NA)
    kc_g = cache_gqa_k.reshape(DEC_BATCH, DEPTH, PAST_LEN, W_GQA_KV)
    vc_g = cache_gqa_v.reshape(DEC_BATCH, DEPTH, PAST_LEN, W_GQA_KV)
    caches = [jnp.zeros((BATCH, DEPTH, SEQ, w), F32) for w in (W_NA, W_NA, W_GQA_KV, W_GQA_KV)]
    s_fin = jnp.zeros((BATCH, DEPTH, 2, N_PAIR, LANES, LANES), F32)

    for l in range(DEPTH):
        feat, naq, nak, nav, gq, gk, gv, *caches = _inproj(
            l, x_all, mod_all, w_in_bf, qn, kn, cos_tab, sin_tab, caches)
        r, kap, v, lw, ah, kd, g, bonus = _rwkv_prep(
            l, feat, rwkv_conv, rows(rwkv_w0), w2bd, rows(rwkv_a0), a2bd, rwkv_g2,
            rows(rwkv_k_k), rows(rwkv_k_a), rows(rwkv_r_k))
        o_dirs, s_fin = _rwkv_scan(l, r, kap, v, lw, ah, kd, s0_lat, s_fin)
        o_na, o_g = _ctx_attention(naq, nak, nav, gq, gk, gv)
        o_na = _lat_na(l, naq, nak, nav, kc_na, vc_na, tb_all, o_na)
        o_g = _lat_gqa(l, gq, gk, gv, kc_g, vc_g, o_g)
        x_all = _mix_ffn(l, x_all, mod_all, o_dirs, g, bonus, rows(rwkv_lnx_w), rows(rwkv_lnx_b), o_na, o_g,
                         w_out_bf, rows(ln1_w), rows(ln1_b), w_ffn_in_bf, w_ffn_out_bf, rows(ln2_w), rows(ln2_b))
    y_prompt = x_all[:N_CTX].reshape(BATCH, SEQ, D_MODEL)
    y_sample = x_all[N_CTX:].reshape(DEC_BATCH, DEC_SEQ, D_MODEL)
    return (y_prompt, y_sample, _unpair_states(s_fin),
            caches[0].reshape(BATCH, DEPTH, SEQ, H_NA, HEAD_DIM), caches[1].reshape(BATCH, DEPTH, SEQ, H_NA, HEAD_DIM),
            caches[2].reshape(BATCH, DEPTH, SEQ, H_GQA_KV, HEAD_DIM),
            caches[3].reshape(BATCH, DEPTH, SEQ, H_GQA_KV, HEAD_DIM))
```

```python
import jax
import jax.numpy as jnp
from jax import lax
from jax.experimental import pallas as pl
from jax.experimental.pallas import tpu as pltpu

F32 = jnp.float32
BF16 = jnp.bfloat16
HIGHEST = lax.Precision.HIGHEST

D_MODEL = 1024
BATCH = 16
SEQ = 256
DEPTH = 4
DEC_BATCH = 2
DEC_SEQ = 1024
PAST_LEN = 512
GRID_W = 64
GRID_ROWS = DEC_SEQ // GRID_W
HEAD_DIM = 64
H_RWKV = 4
H_NA = 4
H_GQA = 8
H_GQA_KV = 2
W_RWKV = H_RWKV * HEAD_DIM
W_NA = H_NA * HEAD_DIM
W_GQA = H_GQA * HEAD_DIM
W_GQA_KV = H_GQA_KV * HEAD_DIM
LORA_W = 64
LORA_A = 64
LORA_G = 128
RWKV_IN = 3 * W_RWKV + 2 * LORA_W + 2 * LORA_A + LORA_G
NA_IN = 3 * W_NA
GQA_IN = W_GQA + 2 * W_GQA_KV
D_IN = RWKV_IN + NA_IN + GQA_IN
NA_ROWS = 8
NA_COLS = 16
ROPE_BASE = 10000.0
ROPE_FREQ = HEAD_DIM // 4
D_FF = ((8 * D_MODEL + 3 * 256 - 1) // (3 * 256)) * 256
DEEPNORM_ALPHA = (2 * DEPTH) ** 0.25
LN_EPS = 1e-5
RMS_EPS = 1e-6
GN_EPS = 64e-5
NEG_INF = -1e30
ATTN_SCALE = HEAD_DIM ** -0.5

LANES = 128
TM = 256
N_CTX = BATCH * SEQ
N_LAT = DEC_BATCH * DEC_SEQ
N_TOK = N_CTX + N_LAT
NBLK = N_TOK // TM
NCB = N_CTX // TM
LAT_BLKS = DEC_SEQ // TM
CHUNK = 64
N_CHUNK = TM // CHUNK
TD = 512
NBLK_D = N_TOK // TD
NCB_D = N_CTX // TD
LAT_BLKS_D = DEC_SEQ // TD
FF_CHUNK = D_FF // 2
N_PAIR = W_RWKV // LANES
MOD_ROWS = 8
VMEM_LIMIT = 48 * 1024 * 1024
VMEM_LIMIT_FFN = 56 * 1024 * 1024
N_DR = 2 * NA_ROWS - 1
N_DC = 2 * NA_COLS - 1
N_BAND = NA_ROWS * GRID_W


def _cparams(n_grid, vmem_limit=VMEM_LIMIT):
    return pltpu.CompilerParams(dimension_semantics=("arbitrary",) * n_grid, vmem_limit_bytes=vmem_limit)


def _iota(shape, dim):
    return lax.broadcasted_iota(jnp.int32, shape, dim)


NN = (((1,), (0,)), ((), ()))
NT = (((1,), (1,)), ((), ()))
TN = (((0,), (0,)), ((), ()))


def _dg(a, b, dims=NN, precision=None):
    return lax.dot_general(a, b, dims, preferred_element_type=F32, precision=precision)


def _split(x):
    hi = x.astype(BF16)
    return hi, (x - hi.astype(F32)).astype(BF16)


def _dgs(a_s, b_s, dims=NN):
    if len(a_s) == 1:
        return _dg(a_s[0], b_s[0], dims)
    (ah, al), (bh, bl) = a_s, b_s
    ca, cb = dims[0][0][0], dims[0][1][0]
    return _dg(jnp.concatenate([ah, ah, al], axis=ca), jnp.concatenate([bh, bl, bh], axis=cb), dims)


def _dg3(a, b, dims=NN):
    return _dgs(_split(a), _split(b), dims)


def _sigmoid(x):
    return 1.0 / (1.0 + jnp.exp(-x))


def _softplus(x):
    return jnp.maximum(x, 0.0) + jnp.log(1.0 + jnp.exp(-jnp.abs(x)))


def _seg64_sum(x):
    rows, width = x.shape
    lo = _iota((rows, LANES), 1) < HEAD_DIM
    outs = []
    for c in range(width // LANES):
        blk = x[:, c * LANES:(c + 1) * LANES]
        s_lo = jnp.sum(jnp.where(lo, blk, 0.0), axis=-1, keepdims=True)
        s_hi = jnp.sum(jnp.where(lo, 0.0, blk), axis=-1, keepdims=True)
        outs.append(jnp.where(lo, s_lo, s_hi))
    return outs[0] if len(outs) == 1 else jnp.concatenate(outs, axis=1)


def _layer_norm(x, w, b):
    mu = jnp.mean(x, axis=-1, keepdims=True)
    xc = x - mu
    var = jnp.mean(xc * xc, axis=-1, keepdims=True)
    return xc * lax.rsqrt(var + LN_EPS) * w + b


def _mod_row(i):
    return jnp.where(i < NCB_D, 0, 1 + (i - NCB_D) // LAT_BLKS_D)


def _layer_spec(l, tail, n_grid, single=False):
    idx = lambda *g: (l,) + (0,) * len(tail)
    del n_grid
    if single:
        return pl.BlockSpec((None,) + tuple(tail), idx, pipeline_mode=pl.Buffered(1))
    return pl.BlockSpec((None,) + tuple(tail), idx)


ANY_SPEC = pl.BlockSpec(memory_space=pl.ANY)


def _mod_kernel(c_ref, w_ref, b_ref, o_ref):
    c = c_ref[...]
    s = c * _sigmoid(c)
    o_ref[...] = _dg(s, w_ref[...], NN, HIGHEST) + b_ref[...]


def _modulation(cc, w_mod, b_mod):
    return pl.pallas_call(
        _mod_kernel,
        grid=(DEPTH, 6),
        in_specs=[
            pl.BlockSpec((MOD_ROWS, D_MODEL), lambda l, j: (0, 0)),
            pl.BlockSpec((None, D_MODEL, D_MODEL), lambda l, j: (l, 0, j)),
            pl.BlockSpec((None, 1, D_MODEL), lambda l, j: (l, 0, j)),
        ],
        out_specs=pl.BlockSpec((None, MOD_ROWS, D_MODEL), lambda l, j: (l, 0, j)),
        out_shape=jax.ShapeDtypeStruct((DEPTH, MOD_ROWS, 6 * D_MODEL), F32),
        compiler_params=_cparams(2),
        name="modulation",
    )(cc, w_mod, b_mod.reshape(DEPTH, 1, 6 * D_MODEL))


def _bias_kernel(rpb_ref, o_ref):
    l = pl.program_id(0)
    q = _iota((GRID_W, LANES), 0)
    x = _iota((GRID_W, LANES), 1)
    c = x % GRID_W
    right = x >= GRID_W
    dc = jnp.clip(c - q, 1 - NA_COLS, NA_COLS - 1) + NA_COLS - 1
    c0 = jnp.clip(q - NA_COLS // 2, 0, GRID_W - NA_COLS)
    in_win = (c >= c0) & (c < c0 + NA_COLS)

    def body(t, carry):
        h = t // (N_DR - 1)
        dr = t % (N_DR - 1)
        base = ((l * H_NA + h) * N_DR + dr) * N_DC
        acc = jnp.zeros((GRID_W, LANES), F32)
        for d in range(N_DC):
            s0 = rpb_ref[base + d]
            s1 = rpb_ref[base + N_DC + d]
            acc = jnp.where(dc == d, jnp.where(right, s1, s0), acc)
        o_ref[h, dr] = jnp.where(in_win, acc, NEG_INF)
        return carry

    lax.fori_loop(0, H_NA * (N_DR - 1), body, 0)


def _bias_tables(na_rpb):
    return pl.pallas_call(
        _bias_kernel,
        grid=(DEPTH,),
        in_specs=[pl.BlockSpec(memory_space=pltpu.SMEM)],
        out_specs=pl.BlockSpec((None, H_NA, N_DR - 1, GRID_W, LANES), lambda l: (l, 0, 0, 0, 0)),
        out_shape=jax.ShapeDtypeStruct((DEPTH, H_NA, N_DR - 1, GRID_W, LANES), F32),
        compiler_params=_cparams(1),
        name="na_bias_tables",
    )(na_rpb.reshape(-1))


def _rope(x, cos, sin):
    k = x.shape[1] // LANES
    cosf = cos if k == 1 else jnp.concatenate([cos] * k, axis=1)
    sinf = sin if k == 1 else jnp.concatenate([sin] * k, axis=1)
    first = (_iota(x.shape, 1) % (2 * ROPE_FREQ)) < ROPE_FREQ
    partner = jnp.where(first, pltpu.roll(x, x.shape[1] - ROPE_FREQ, axis=1), pltpu.roll(x, ROPE_FREQ, axis=1))
    return x * cosf + partner * sinf


def _inproj_kernel(x_ref, mod_ref, w_ref, qn_ref, kn_ref, cos_ref, sin_ref, c0_ref, c1_ref, c2_ref, c3_ref,
                   feat_ref, naq_ref, nak_ref, nav_ref, gq_ref, gk_ref, gv_ref,
                   cnak_ref, cnav_ref, cgk_ref, cgv_ref):
    del c0_ref, c1_ref, c2_ref, c3_ref
    i = pl.program_id(0)
    row = _mod_row(i)
    shift1 = mod_ref[pl.ds(row, 1), 0:D_MODEL]
    scale1 = mod_ref[pl.ds(row, 1), D_MODEL:2 * D_MODEL]
    xm = (x_ref[...] * (1.0 + scale1) + shift1).astype(BF16)
    proj = _dg(xm, w_ref[...])
    o_na = RWKV_IN
    o_g = RWKV_IN + NA_IN
    feat_ref[...] = proj[:, :RWKV_IN]
    naq_ref[...] = proj[:, o_na:o_na + W_NA] * ATTN_SCALE
    nak = proj[:, o_na + W_NA:o_na + 2 * W_NA]
    nav = proj[:, o_na + 2 * W_NA:o_na + 3 * W_NA]
    q = proj[:, o_g:o_g + W_GQA]
    k = proj[:, o_g + W_GQA:o_g + W_GQA + W_GQA_KV]
    gv = proj[:, o_g + W_GQA + W_GQA_KV:]
    cos = cos_ref[...]
    sin = sin_ref[...]
    q = q * lax.rsqrt(_seg64_sum(q * q) * (1.0 / HEAD_DIM) + RMS_EPS) * qn_ref[...]
    k = k * lax.rsqrt(_seg64_sum(k * k) * (1.0 / HEAD_DIM) + RMS_EPS) * kn_ref[...]
    gk = _rope(k, cos, sin)
    gq_ref[...] = _rope(q, cos, sin) * ATTN_SCALE
    nak_ref[...] = nak
    nav_ref[...] = nav
    gk_ref[...] = gk
    gv_ref[...] = gv

    @pl.when(i < NCB_D)
    def _():
        for ref, val in ((cnak_ref, nak), (cnav_ref, nav), (cgk_ref, gk), (cgv_ref, gv)):
            ref[...] = val.reshape(ref.shape)


def _inproj(l, x_all, mod_all, w_in_bf, qn, kn, cos_tab, sin_tab, caches):
    tab_idx = lambda i: (jnp.where(i < NCB_D, 0, 1 + (i - NCB_D) % LAT_BLKS_D), 0)
    widths = (RWKV_IN, W_NA, W_NA, W_NA, W_GQA, W_GQA_KV, W_GQA_KV)
    cache_w = (W_NA, W_NA, W_GQA_KV, W_GQA_KV)
    cache_spec = lambda w: pl.BlockSpec((TD // SEQ, None, SEQ, w), lambda i: (jnp.minimum(i, NCB_D - 1), l, 0, 0))
    n_in = 7
    return pl.pallas_call(
        _inproj_kernel,
        grid=(NBLK_D,),
        in_specs=[
            pl.BlockSpec((TD, D_MODEL), lambda i: (i, 0)),
            _layer_spec(l, (MOD_ROWS, 6 * D_MODEL), 1),
            _layer_spec(l, (D_MODEL, D_IN), 1),
            _layer_spec(l, (1, W_GQA), 1),
            _layer_spec(l, (1, W_GQA_KV), 1),
            pl.BlockSpec((TD, LANES), tab_idx),
            pl.BlockSpec((TD, LANES), tab_idx),
        ] + [ANY_SPEC] * 4,
        out_specs=[pl.BlockSpec((TD, w), lambda i: (i, 0)) for w in widths] + [cache_spec(w) for w in cache_w],
        out_shape=[jax.ShapeDtypeStruct((N_TOK, w), F32) for w in widths]
        + [jax.ShapeDtypeStruct((BATCH, DEPTH, SEQ, w), F32) for w in cache_w],
        input_output_aliases={n_in + j: len(widths) + j for j in range(4)},
        compiler_params=_cparams(1),
        name="inproj",
    )(x_all, mod_all, w_in_bf, qn, kn, cos_tab, sin_tab, *caches)


def _rwkv_prep_kernel(f_ref, fp_ref, fn_ref, conv_ref, w0_ref, w2_ref, a0_ref, a2_ref, g2_ref,
                      kk_ref, ka_ref, rk_ref,
                      r_ref, kap_ref, v_ref, lw_ref, ah_ref, kd_ref, g_ref, bonus_ref):
    i = pl.program_id(0)
    pos = (i - NCB) % LAT_BLKS
    lat = i >= NCB
    has_prev = jnp.where(lat & (pos != 0), 1.0, 0.0)
    has_next = jnp.where(lat & (pos != LAT_BLKS - 1), 1.0, 0.0)
    x = f_ref[...]
    rows = _iota(x.shape, 0)
    x_prev = jnp.where(rows == 0, fp_ref[7:8, :] * has_prev, pltpu.roll(x, 1, axis=0))
    x_next = jnp.where(rows == TM - 1, fn_ref[0:1, :] * has_next, pltpu.roll(x, TM - 1, axis=0))
    f = x_prev * conv_ref[0:1, :] + x * conv_ref[1:2, :] + x_next * conv_ref[2:3, :]
    o1, o2, o3 = W_RWKV, 2 * W_RWKV, 3 * W_RWKV
    o4 = o3 + 2 * LORA_W
    o5 = o4 + 2 * LORA_A
    r, k, v = f[:, :o1], f[:, o1:o2], f[:, o2:o3]
    wd, ad, gd = f[:, o3:o4], f[:, o4:o5], f[:, o5:]
    log_w = -_softplus(-(w0_ref[...] + _dg3(jnp.tanh(wd), w2_ref[...]))) - 0.5
    a = _sigmoid(a0_ref[...] + _dg3(ad, a2_ref[...]))
    g_ref[...] = _dg3(_sigmoid(gd), g2_ref[...])
    kk = k * kk_ref[...]
    kap = kk / jnp.maximum(jnp.sqrt(_seg64_sum(kk * kk)), 1e-12)
    ka = ka_ref[...]
    kd_sum = jnp.zeros_like(k)
    for d in range(2):
        a_d = a[:, d * W_RWKV:(d + 1) * W_RWKV]
        kd = k * (1.0 + (a_d - 1.0) * ka)
        kd_sum = kd_sum + kd
        lw_ref[d] = -jnp.exp(log_w[:, d * W_RWKV:(d + 1) * W_RWKV])
        ah_ref[d] = a_d * kap
        kd_ref[d] = kd
    r_ref[...] = r
    kap_ref[...] = kap
    v_ref[...] = v
    bonus_ref[...] = _seg64_sum(r * kd_sum * rk_ref[...]) * v


def _rwkv_prep(l, feat, conv, w0, w2bd, a0, a2bd, g2, k_k, k_a, r_k):
    sub = TM // 8
    tok = pl.BlockSpec((TM, W_RWKV), lambda i: (i, 0))
    tok2 = pl.BlockSpec((2, TM, W_RWKV), lambda i: (0, i, 0))
    return pl.pallas_call(
        _rwkv_prep_kernel,
        grid=(NBLK,),
        in_specs=[
            pl.BlockSpec((TM, RWKV_IN), lambda i: (i, 0)),
            pl.BlockSpec((8, RWKV_IN), lambda i: (jnp.maximum(i * sub - 1, 0), 0)),
            pl.BlockSpec((8, RWKV_IN), lambda i: (jnp.minimum((i + 1) * sub, NBLK * sub - 1), 0)),
            _layer_spec(l, (3, RWKV_IN), 1), _layer_spec(l, (1, 2 * W_RWKV), 1),
            _layer_spec(l, (2 * LORA_W, 2 * W_RWKV), 1), _layer_spec(l, (1, 2 * W_RWKV), 1),
            _layer_spec(l, (2 * LORA_A, 2 * W_RWKV), 1), _layer_spec(l, (LORA_G, W_RWKV), 1),
            _layer_spec(l, (1, W_RWKV), 1), _layer_spec(l, (1, W_RWKV), 1), _layer_spec(l, (1, W_RWKV), 1),
        ],
        out_specs=[tok, tok, tok, tok2, tok2, tok2, tok, tok],
        out_shape=[jax.ShapeDtypeStruct((N_TOK, W_RWKV), F32)] * 3
        + [jax.ShapeDtypeStruct((2, N_TOK, W_RWKV), F32)] * 3
        + [jax.ShapeDtypeStruct((N_TOK, W_RWKV), F32)] * 2,
        compiler_params=_cparams(1),
        name="rwkv_prep",
    )(feat, feat, feat, conv, w0, w2bd, a0, a2bd, g2, k_k, k_a, r_k)


def _scan_kernel(r_ref, kap_ref, v_ref, lw_ref, ah_ref, kd_ref, s0_ref, sf_in_ref, o_ref, sfin_ref,
                 s_scr, rt_scr, kt_scr, kdt_scr, at_scr, cum_scr, m_scr, n_scr, q_scr, o1_scr):
    del sf_in_ref
    d = pl.program_id(0)
    j = pl.program_id(1)
    blk = jnp.where(d == 0, j, NBLK - 1 - j)
    pos = (blk - NCB) % LAT_BLKS
    sgn = 1 - 2 * d

    @pl.when(blk < NCB)
    def _():
        s_scr[...] = jnp.zeros_like(s_scr)

    @pl.when((blk >= NCB) & jnp.where(d == 0, pos == 0, pos == LAT_BLKS - 1))
    def _():
        s_scr[...] = s0_ref[...]

    rr = _iota((TM, TM), 0)
    cc = _iota((TM, TM), 1)
    tri = jnp.where(((rr // CHUNK) == (cc // CHUNK)) & ((rr - cc) * sgn >= 0), 1.0, 0.0).astype(BF16)
    lw = lw_ref[...]
    lw_hi = lw.astype(BF16)
    lw_r = lw - lw_hi.astype(F32)
    lw_mid = lw_r.astype(BF16)
    lw_lo = (lw_r - lw_mid.astype(F32)).astype(BF16)
    cum = _dg(jnp.concatenate([tri, tri, tri], axis=1), jnp.concatenate([lw_hi, lw_mid, lw_lo], axis=0))
    e_neg = jnp.exp(-cum)
    rt_scr[...] = r_ref[...] * jnp.exp(cum)
    kt_scr[...] = kap_ref[...] * jnp.exp(cum - lw)
    kdt_scr[...] = kd_ref[...] * e_neg
    at_scr[...] = ah_ref[...] * e_neg
    cum_scr[...] = cum

    row = _iota((CHUNK, LANES), 0)
    col = _iota((CHUNK, LANES), 1) % CHUNK
    left = _iota((CHUNK, LANES), 1) < CHUNK
    diff = (row - col) * sgn
    incl = diff >= 0
    strict = diff > 0
    eye = jnp.where(row == col, 1.0, 0.0)
    r2 = _iota((LANES, LANES), 0)
    c2 = _iota((LANES, LANES), 1)
    bd_mask = (r2 // CHUNK) == (c2 // CHUNK)
    eye2 = jnp.where(r2 == c2, 1.0, 0.0)

    zero_bf = jnp.zeros((CHUNK, LANES), BF16)
    split = lambda x: (x.astype(BF16),)

    def bd(xs):
        return tuple(jnp.concatenate([jnp.where(left, y, zero_bf), jnp.where(left, zero_bf, y)], axis=0)
                     for y in xs)

    def cat(xs, ys, axis):
        return tuple(jnp.concatenate([x, y], axis=axis) for x, y in zip(xs, ys))

    def mm(a_s, b_s):
        return _dgs(a_s, bd(b_s))

    units = [(c, p) for c in range(N_CHUNK) for p in range(N_PAIR)]
    each = lambda f, *lists: [f(*args) for args in zip(*lists)]

    def load(ref):
        return [ref[c * CHUNK:(c + 1) * CHUNK, p * LANES:(p + 1) * LANES] for c, p in units]

    rt, v = load(rt_scr), load(v_ref)
    kt_s, rt_s, kdt_s, at_s, v_s = (each(split, x) for x in (load(kt_scr), rt, load(kdt_scr), load(at_scr), v))
    gam = [jnp.exp(jnp.where(d == 0, cum_scr[(c + 1) * CHUNK - 1:(c + 1) * CHUNK, p * LANES:(p + 1) * LANES],
                             cum_scr[c * CHUNK:c * CHUNK + 1, p * LANES:(p + 1) * LANES])) for c, p in units]
    gram = each(lambda k, r, a, kd: _dgs(cat(k, r, 0), cat(bd(a), bd(kd), 0), NT), kt_s, rt_s, at_s, kdt_s)
    la = [jnp.where(strict, g[0:CHUNK, 0:LANES], 0.0) for g in gram]
    lk_s = [split(jnp.where(strict, g[0:CHUNK, LANES:], 0.0)) for g in gram]
    ra_s = [split(jnp.where(incl, g[CHUNK:, 0:LANES], 0.0)) for g in gram]
    rk_s = [split(jnp.where(incl, g[CHUNK:, LANES:], 0.0)) for g in gram]
    lrv = each(lambda lk, rk, vv: mm(cat(lk, rk, 0), vv), lk_s, rk_s, v_s)
    b = 8
    l8 = [jnp.where((row // b) == (col // b), x, 0.0) for x in la]
    l8_s = each(split, l8)
    l8_2 = each(mm, l8_s, l8_s)
    l8_2s = each(split, l8_2)
    l8_4 = each(mm, l8_2s, l8_2s)
    p1 = each(lambda x, y: mm(split(eye - x), split(eye + y)), l8, l8_2)
    t = each(lambda x, y: mm(split(x), split(eye + y)), p1, l8_4)
    while b < CHUNK:
        offd = ((row // (2 * b)) == (col // (2 * b))) & ((row // b) != (col // b))
        t_s = each(split, t)
        x = each(lambda ts, y: mm(ts, split(jnp.where(offd, y, 0.0))), t_s, la)
        t = each(lambda tt, xx, ts: tt - mm(split(xx), ts), t, x, t_s)
        b *= 2
    tx = each(lambda tt, k, y: _dgs(split(tt), cat(bd(k), bd(split(y[0:CHUNK])), 1)), t, kt_s, lrv)
    khat_s = [split(y[:, 0:LANES]) for y in tx]
    w1_s = [split(y[:, LANES:]) for y in tx]
    rx = each(lambda r, k, w: _dgs(r, cat(bd(k), bd(w), 1)), ra_s, khat_s, w1_s)
    mk = each(lambda k, a: _dgs(k, a, TN), khat_s, at_s)
    nk = each(lambda vv, w, kd, a: _dgs(cat(vv, w, 0), cat(kd, tuple(-y for y in a), 0), TN),
              v_s, w1_s, kdt_s, at_s)
    for i, (c, p) in enumerate(units):
        q_scr[c, p] = rt[i] - rx[i][:, 0:LANES]
        o1_scr[c, p] = lrv[i][CHUNK:] - rx[i][:, LANES:]
        m_scr[c, p] = (eye2 - jnp.where(bd_mask, mk[i], 0.0)) * gam[i]
        n_scr[c, p] = jnp.where(bd_mask, nk[i], 0.0) * gam[i]

    for c in range(N_CHUNK):
        ce = jnp.where(d == 0, c, N_CHUNK - 1 - c)
        rows = pl.ds(pl.multiple_of(ce * CHUNK, CHUNK), CHUNK)
        for p in range(N_PAIR):
            s = s_scr[p]
            o_ref[rows, p * LANES:(p + 1) * LANES] = _dg3(q_scr[ce, p], s, NT) + o1_scr[ce, p]
            s_scr[p] = _dg3(s, m_scr[ce, p]) + n_scr[ce, p]

    @pl.when(blk < NCB)
    def _():
        sfin_ref[...] = s_scr[...]


def _rwkv_scan(l, r, kap, v, lw, ah, kd, s0_lat, s_fin):
    def blk_of(d, j):
        return jnp.where(d == 0, j, NBLK - 1 - j)

    tok = pl.BlockSpec((TM, W_RWKV), lambda d, j: (blk_of(d, j), 0))
    tok2 = pl.BlockSpec((None, TM, W_RWKV), lambda d, j: (d, blk_of(d, j), 0))
    pair = (N_PAIR, LANES, LANES)
    return pl.pallas_call(
        _scan_kernel,
        grid=(2, NBLK),
        in_specs=[tok, tok, tok, tok2, tok2, tok2,
                  pl.BlockSpec((None, None, None) + pair,
                               lambda d, j: (jnp.maximum(blk_of(d, j) - NCB, 0) // LAT_BLKS, l, d, 0, 0, 0)),
                  ANY_SPEC],
        out_specs=[tok2,
                   pl.BlockSpec((None, None, None) + pair,
                                lambda d, j: (jnp.minimum(blk_of(d, j), NCB - 1), l, d, 0, 0, 0))],
        out_shape=[jax.ShapeDtypeStruct((2, N_TOK, W_RWKV), F32),
                   jax.ShapeDtypeStruct((BATCH, DEPTH, 2) + pair, F32)],
        scratch_shapes=[pltpu.VMEM(pair, F32)] + [pltpu.VMEM((TM, W_RWKV), F32)] * 5
        + [pltpu.VMEM((N_CHUNK,) + pair, F32)] * 2 + [pltpu.VMEM((N_CHUNK, N_PAIR, CHUNK, LANES), F32)] * 2,
        input_output_aliases={7: 1},
        compiler_params=_cparams(2),
        name="rwkv_scan",
    )(r, kap, v, lw, ah, kd, s0_lat, s_fin)


def _attend(groups):
    lhs = []
    for q_cols, _, _, _ in groups:
        left = _iota(q_cols[0].shape, 1) < HEAD_DIM
        parts = []
        for qc in q_cols:
            parts += [jnp.where(left, qc, 0.0), jnp.where(left, 0.0, qc)]
        lhs.append(jnp.concatenate(parts, axis=0).astype(BF16))
    s = [_dg(x, g[1], NT) for x, g in zip(lhs, groups)]
    s = [x if g[3] is None else x + g[3] for x, g in zip(s, groups)]
    p = [jnp.exp(x - jnp.max(x, axis=-1, keepdims=True)) for x in s]
    inv = [1.0 / jnp.sum(x, axis=-1, keepdims=True) for x in p]
    o = [_dg(x.astype(BF16), g[2]) * y for x, y, g in zip(p, inv, groups)]
    outs = []
    for x, (q_cols, _, _, _) in zip(o, groups):
        rows = q_cols[0].shape[0]
        left = _iota(q_cols[0].shape, 1) < HEAD_DIM
        outs.append([jnp.where(left, x[2 * j * rows:(2 * j + 1) * rows], x[(2 * j + 1) * rows:(2 * j + 2) * rows])
                     for j in range(len(q_cols))])
    return outs


def _cols(x):
    return [x[:, c * LANES:(c + 1) * LANES] for c in range(x.shape[1] // LANES)]


def _gqa_groups(q, k, v):
    left = _iota(k.shape, 1) < HEAD_DIM
    k_sw = pltpu.roll(k, HEAD_DIM, axis=1)
    v_sw = pltpu.roll(v, HEAD_DIM, axis=1)
    q_cols = _cols(q)
    groups = []
    for g in range(H_GQA_KV):
        k2 = jnp.where(left, k, k_sw) if g == 0 else jnp.where(left, k_sw, k)
        v2 = jnp.where(left, v, v_sw) if g == 0 else jnp.where(left, v_sw, v)
        groups.append((q_cols[2 * g:2 * g + 2], k2.astype(BF16), v2.astype(BF16), None))
    return groups


def _ctx_attn_kernel(naq_ref, nak_ref, nav_ref, gq_ref, gk_ref, gv_ref, ona_ref, og_ref):
    k_cols = _cols(nak_ref[...].astype(BF16))
    v_cols = _cols(nav_ref[...].astype(BF16))
    na_groups = [([qc], kc, vc, None) for qc, kc, vc in zip(_cols(naq_ref[...]), k_cols, v_cols)]
    outs = _attend(na_groups + _gqa_groups(gq_ref[...], gk_ref[...], gv_ref[...]))
    n_na = len(na_groups)
    ona_ref[...] = jnp.concatenate([o[0] for o in outs[:n_na]], axis=1)
    og_ref[...] = jnp.concatenate([c for o in outs[n_na:] for c in o], axis=1)


def _ctx_attention(naq, nak, nav, gq, gk, gv):
    spec = lambda w: pl.BlockSpec((SEQ, w), lambda b: (b, 0))
    return pl.pallas_call(
        _ctx_attn_kernel,
        grid=(BATCH,),
        in_specs=[spec(W_NA), spec(W_NA), spec(W_NA), spec(W_GQA), spec(W_GQA_KV), spec(W_GQA_KV)],
        out_specs=[spec(W_NA), spec(W_GQA)],
        out_shape=[jax.ShapeDtypeStruct((N_TOK, W_NA), F32), jax.ShapeDtypeStruct((N_TOK, W_GQA), F32)],
        compiler_params=_cparams(1),
        name="ctx_attention",
    )(naq, nak, nav, gq, gk, gv)


def _lat_na_kernel(q_ref, k_ref, v_ref, kc_ref, vc_ref, tb_ref, buf_ref, o_ref):
    del buf_ref
    r = pl.program_id(1)
    r0 = jnp.clip(r - NA_ROWS // 2, 0, GRID_ROWS - NA_ROWS)
    band = pl.ds(pl.multiple_of(r0 * GRID_W, GRID_W), N_BAND)
    dr0 = r0 - r + NA_ROWS - 1

    groups = []
    for c, qc in enumerate(_cols(q_ref[...])):
        cols = slice(c * LANES, (c + 1) * LANES)
        k2 = jnp.concatenate([k_ref[band, cols], kc_ref[:, cols]], axis=0).astype(BF16)
        v2 = jnp.concatenate([v_ref[band, cols], vc_ref[:, cols]], axis=0).astype(BF16)
        bias = jnp.concatenate(
            [jnp.concatenate([tb_ref[2 * c + half, dr0 + jj] for jj in range(0, NA_ROWS, 2)], axis=1)
             for half in range(2)], axis=0)
        bias = jnp.concatenate([bias, jnp.zeros((2 * GRID_W, PAST_LEN), F32)], axis=1)
        groups.append(([qc], k2, v2, bias))
    o_ref[...] = jnp.concatenate([o[0] for o in _attend(groups)], axis=1)


def _lat_na(l, naq, nak, nav, kc, vc, tb, o_na):
    q_blk0 = N_CTX // GRID_W
    seq_blk0 = N_CTX // DEC_SEQ
    seq = pl.BlockSpec((DEC_SEQ, W_NA), lambda b, r: (seq_blk0 + b, 0))
    cache = pl.BlockSpec((None, None, PAST_LEN, W_NA), lambda b, r: (b, l, 0, 0))
    q_spec = pl.BlockSpec((GRID_W, W_NA), lambda b, r: (q_blk0 + b * GRID_ROWS + r, 0))
    return pl.pallas_call(
        _lat_na_kernel,
        grid=(DEC_BATCH, GRID_ROWS),
        in_specs=[q_spec, seq, seq, cache, cache, _layer_spec(l, (H_NA, N_DR - 1, GRID_W, LANES), 2), ANY_SPEC],
        out_specs=q_spec,
        out_shape=jax.ShapeDtypeStruct((N_TOK, W_NA), F32),
        input_output_aliases={6: 0},
        compiler_params=_cparams(2),
        name="latent_na",
    )(naq, nak, nav, kc, vc, tb, o_na)


def _lat_gqa_kernel(q_ref, k_ref, v_ref, kc_ref, vc_ref, buf_ref, o_ref):
    del buf_ref
    k = jnp.concatenate([kc_ref[...], k_ref[...]], axis=0)
    v = jnp.concatenate([vc_ref[...], v_ref[...]], axis=0)
    o_ref[...] = jnp.concatenate([c for o in _attend(_gqa_groups(q_ref[...], k, v)) for c in o], axis=1)


def _lat_gqa(l, gq, gk, gv, kc, vc, o_g):
    seq_blk0 = N_CTX // DEC_SEQ
    seq = pl.BlockSpec((DEC_SEQ, W_GQA_KV), lambda b, i: (seq_blk0 + b, 0))
    cache = pl.BlockSpec((None, None, PAST_LEN, W_GQA_KV), lambda b, i: (b, l, 0, 0))
    q_spec = pl.BlockSpec((TM, W_GQA), lambda b, i: (NCB + b * LAT_BLKS + i, 0))
    return pl.pallas_call(
        _lat_gqa_kernel,
        grid=(DEC_BATCH, LAT_BLKS),
        in_specs=[q_spec, seq, seq, cache, cache, ANY_SPEC],
        out_specs=q_spec,
        out_shape=jax.ShapeDtypeStruct((N_TOK, W_GQA), F32),
        input_output_aliases={5: 0},
        compiler_params=_cparams(2),
        name="latent_gqa",
    )(gq, gk, gv, kc, vc, o_g)


def _mix_ffn_kernel(x_ref, mod_ref, of_ref, ob_ref, g_ref, bonus_ref, lnxw_ref, lnxb_ref, ona_ref, og_ref,
                    wout_ref, ln1w_ref, ln1b_ref, wfi_ref, wfo_ref, ln2w_ref, ln2b_ref, y_ref):
    row = _mod_row(pl.program_id(0))
    mod = lambda n: mod_ref[pl.ds(row, 1), n * D_MODEL:(n + 1) * D_MODEL]
    o = of_ref[...] + ob_ref[...]
    mu = _seg64_sum(o) * (1.0 / HEAD_DIM)
    oc = o - mu
    var = _seg64_sum(oc * oc) * (1.0 / HEAD_DIM)
    o_rwkv = (oc * lax.rsqrt(var + GN_EPS) * lnxw_ref[...] + lnxb_ref[...] + bonus_ref[...]) * g_ref[...]
    mix_in = jnp.concatenate([o_rwkv, ona_ref[...], og_ref[...]], axis=1).astype(BF16)
    mix = _dg(mix_in, wout_ref[...])
    x1 = _layer_norm(DEEPNORM_ALPHA * x_ref[...] + mod(2) * mix, ln1w_ref[...], ln1b_ref[...])
    x_in = (x1 * (1.0 + mod(4)) + mod(3)).astype(BF16)
    ffn = jnp.zeros_like(x1)
    for lo in range(0, D_FF, FF_CHUNK):
        gate = _dg(x_in, wfi_ref[:, lo:lo + FF_CHUNK])
        up = _dg(x_in, wfi_ref[:, D_FF + lo:D_FF + lo + FF_CHUNK])
        act = (gate * _sigmoid(gate) * up).astype(BF16)
        ffn = ffn + _dg(act, wfo_ref[lo:lo + FF_CHUNK, :])
    y_ref[...] = _layer_norm(DEEPNORM_ALPHA * x1 + mod(5) * ffn, ln2w_ref[...], ln2b_ref[...])


def _mix_ffn(l, x_all, mod_all, o_dirs, g, bonus, lnx_w, lnx_b, o_na, o_g, w_out_bf, ln1_w, ln1_b,
             w_ffn_in_bf, w_ffn_out_bf, ln2_w, ln2_b):
    tok = lambda w: pl.BlockSpec((TD, w), lambda i: (i, 0))
    once = lambda *tail: _layer_spec(l, tail, 1, single=True)
    return pl.pallas_call(
        _mix_ffn_kernel,
        grid=(NBLK_D,),
        in_specs=[
            tok(D_MODEL), once(MOD_ROWS, 6 * D_MODEL),
            pl.BlockSpec((None, TD, W_RWKV), lambda i: (0, i, 0)),
            pl.BlockSpec((None, TD, W_RWKV), lambda i: (1, i, 0)),
            tok(W_RWKV), tok(W_RWKV), once(1, W_RWKV), once(1, W_RWKV), tok(W_NA), tok(W_GQA),
            once(D_MODEL, D_MODEL), once(1, D_MODEL), once(1, D_MODEL),
            once(D_MODEL, 2 * D_FF), once(D_FF, D_MODEL), once(1, D_MODEL), once(1, D_MODEL),
        ],
        out_specs=tok(D_MODEL),
        out_shape=jax.ShapeDtypeStruct((N_TOK, D_MODEL), F32),
        compiler_params=_cparams(1, VMEM_LIMIT_FFN),
        name="mix_ffn",
    )(x_all, mod_all, o_dirs, o_dirs, g, bonus, lnx_w, lnx_b, o_na, o_g, w_out_bf, ln1_w, ln1_b,
      w_ffn_in_bf, w_ffn_out_bf, ln2_w, ln2_b)


def _rope_tables():
    t = jnp.arange(DEC_SEQ)
    inv = ROPE_BASE ** (-jnp.arange(ROPE_FREQ, dtype=F32) / ROPE_FREQ)
    ang_r = (t // GRID_W).astype(F32)[:, None] * inv
    ang_c = (t % GRID_W).astype(F32)[:, None] * inv
    cos = jnp.concatenate([jnp.cos(ang_r)] * 2 + [jnp.cos(ang_c)] * 2, axis=1)
    sin = jnp.concatenate([-jnp.sin(ang_r), jnp.sin(ang_r), -jnp.sin(ang_c), jnp.sin(ang_c)], axis=1)
    cos = jnp.concatenate([jnp.ones((TD, HEAD_DIM), F32), cos], axis=0)
    sin = jnp.concatenate([jnp.zeros((TD, HEAD_DIM), F32), sin], axis=0)
    return jnp.tile(cos, (1, LANES // HEAD_DIM)), jnp.tile(sin, (1, LANES // HEAD_DIM))


def _block_diag2(w):
    z = jnp.zeros_like(w[:, 0])
    return jnp.concatenate([jnp.concatenate([w[:, 0], z], axis=2), jnp.concatenate([z, w[:, 1]], axis=2)], axis=1)


def _pair_states(s):
    lead = s.shape[:-3]
    s = s.reshape(lead + (N_PAIR, 2, HEAD_DIM, HEAD_DIM))
    z = jnp.zeros_like(s[..., 0, :, :])
    top = jnp.concatenate([s[..., 0, :, :], z], axis=-1)
    bot = jnp.concatenate([z, s[..., 1, :, :]], axis=-1)
    return jnp.concatenate([top, bot], axis=-2)


def _unpair_states(s):
    lead = s.shape[:-3]
    a = s[..., :HEAD_DIM, :HEAD_DIM]
    b = s[..., HEAD_DIM:, HEAD_DIM:]
    return jnp.stack([a, b], axis=-3).reshape(lead + (H_RWKV, HEAD_DIM, HEAD_DIM))


def kernel(x_prompt, x_sample, state_rwkv, cache_na_k, cache_na_v, cache_gqa_k, cache_gqa_v, c, c_ctx,
           w_mod, b_mod, w_in, rwkv_conv, rwkv_w0, rwkv_w2, rwkv_a0, rwkv_a2, rwkv_g2, rwkv_k_k, rwkv_k_a,
           rwkv_r_k, rwkv_lnx_w, rwkv_lnx_b, na_rpb, gqa_q_norm, gqa_k_norm, w_out, ln1_w, ln1_b,
           w_ffn_in, w_ffn_out, ln2_w, ln2_b):
    x_all = jnp.concatenate([x_prompt.reshape(N_CTX, D_MODEL), x_sample.reshape(N_LAT, D_MODEL)], axis=0)
    cc = jnp.concatenate([c_ctx[None], c, jnp.zeros((MOD_ROWS - 1 - DEC_BATCH, D_MODEL), F32)], axis=0)
    mod_all = _modulation(cc, w_mod, b_mod)
    tb_all = _bias_tables(na_rpb)
    cos_tab, sin_tab = _rope_tables()
    rows = lambda a: a.reshape(DEPTH, 1, -1)
    w_in_bf, w_out_bf = w_in.astype(BF16), w_out.astype(BF16)
    w_ffn_in_bf, w_ffn_out_bf = w_ffn_in.astype(BF16), w_ffn_out.astype(BF16)
    qn = jnp.tile(rows(gqa_q_norm), (1, 1, H_GQA))
    kn = jnp.tile(rows(gqa_k_norm), (1, 1, H_GQA_KV))
    w2bd, a2bd = _block_diag2(rwkv_w2), _block_diag2(rwkv_a2)
    s0_lat = _pair_states(state_rwkv)
    kc_na = cache_na_k.reshape(DEC_BATCH, DEPTH, PAST_LEN, W_NA)
    vc_na = cache_na_v.reshape(DEC_BATCH, DEPTH, PAST_LEN, W_NA)
    kc_g = cache_gqa_k.reshape(DEC_BATCH, DEPTH, PAST_LEN, W_GQA_KV)
    vc_g = cache_gqa_v.reshape(DEC_BATCH, DEPTH, PAST_LEN, W_GQA_KV)
    caches = [jnp.zeros((BATCH, DEPTH, SEQ, w), F32) for w in (W_NA, W_NA, W_GQA_KV, W_GQA_KV)]
    s_fin = jnp.zeros((BATCH, DEPTH, 2, N_PAIR, LANES, LANES), F32)

    for l in range(DEPTH):
        feat, naq, nak, nav, gq, gk, gv, *caches = _inproj(
            l, x_all, mod_all, w_in_bf, qn, kn, cos_tab, sin_tab, caches)
        r, kap, v, lw, ah, kd, g, bonus = _rwkv_prep(
            l, feat, rwkv_conv, rows(rwkv_w0), w2bd, rows(rwkv_a0), a2bd, rwkv_g2,
            rows(rwkv_k_k), rows(rwkv_k_a), rows(rwkv_r_k))
        o_dirs, s_fin = _rwkv_scan(l, r, kap, v, lw, ah, kd, s0_lat, s_fin)
        o_na, o_g = _ctx_attention(naq, nak, nav, gq, gk, gv)
        o_na = _lat_na(l, naq, nak, nav, kc_na, vc_na, tb_all, o_na)
        o_g = _lat_gqa(l, gq, gk, gv, kc_g, vc_g, o_g)
        x_all = _mix_ffn(l, x_all, mod_all, o_dirs, g, bonus, rows(rwkv_lnx_w), rows(rwkv_lnx_b), o_na, o_g,
                         w_out_bf, rows(ln1_w), rows(ln1_b), w_ffn_in_bf, w_ffn_out_bf, rows(ln2_w), rows(ln2_b))
    y_prompt = x_all[:N_CTX].reshape(BATCH, SEQ, D_MODEL)
    y_sample = x_all[N_CTX:].reshape(DEC_BATCH, DEC_SEQ, D_MODEL)
    return (y_prompt, y_sample, _unpair_states(s_fin),
            caches[0].reshape(BATCH, DEPTH, SEQ, H_NA, HEAD_DIM), caches[1].reshape(BATCH, DEPTH, SEQ, H_NA, HEAD_DIM),
            caches[2].reshape(BATCH, DEPTH, SEQ, H_GQA_KV, HEAD_DIM),
            caches[3].reshape(BATCH, DEPTH, SEQ, H_GQA_KV, HEAD_DIM))
```

```python
import jax
import jax.numpy as jnp
from jax import lax
from jax.experimental import pallas as pl
from jax.experimental.pallas import tpu as pltpu

F32 = jnp.float32
BF16 = jnp.bfloat16
HIGHEST = lax.Precision.HIGHEST

D_MODEL = 1024
BATCH = 16
SEQ = 256
DEPTH = 4
DEC_BATCH = 2
DEC_SEQ = 1024
PAST_LEN = 512
GRID_W = 64
GRID_ROWS = DEC_SEQ // GRID_W
HEAD_DIM = 64
H_RWKV = 4
H_NA = 4
H_GQA = 8
H_GQA_KV = 2
W_RWKV = H_RWKV * HEAD_DIM
W_NA = H_NA * HEAD_DIM
W_GQA = H_GQA * HEAD_DIM
W_GQA_KV = H_GQA_KV * HEAD_DIM
LORA_W = 64
LORA_A = 64
LORA_G = 128
RWKV_IN = 3 * W_RWKV + 2 * LORA_W + 2 * LORA_A + LORA_G
NA_IN = 3 * W_NA
GQA_IN = W_GQA + 2 * W_GQA_KV
D_IN = RWKV_IN + NA_IN + GQA_IN
NA_ROWS = 8
NA_COLS = 16
ROPE_BASE = 10000.0
ROPE_FREQ = HEAD_DIM // 4
D_FF = ((8 * D_MODEL + 3 * 256 - 1) // (3 * 256)) * 256
DEEPNORM_ALPHA = (2 * DEPTH) ** 0.25
LN_EPS = 1e-5
RMS_EPS = 1e-6
GN_EPS = 64e-5
NEG_INF = -1e30
ATTN_SCALE = HEAD_DIM ** -0.5

LANES = 128
TM = 256
N_CTX = BATCH * SEQ
N_LAT = DEC_BATCH * DEC_SEQ
N_TOK = N_CTX + N_LAT
NBLK = N_TOK // TM
NCB = N_CTX // TM
LAT_BLKS = DEC_SEQ // TM
CHUNK = 64
N_CHUNK = TM // CHUNK
assert N_CHUNK == 4
TD = 256
NBLK_D = N_TOK // TD
NCB_D = N_CTX // TD
LAT_BLKS_D = DEC_SEQ // TD
FF_CHUNK = D_FF
N_PAIR = W_RWKV // LANES
MOD_ROWS = 8
VMEM_LIMIT = 48 * 1024 * 1024
VMEM_LIMIT_FFN = 56 * 1024 * 1024
N_DR = 2 * NA_ROWS - 1
N_DC = 2 * NA_COLS - 1
N_BAND = NA_ROWS * GRID_W


def _cparams(n_grid, vmem_limit=VMEM_LIMIT):
    return pltpu.CompilerParams(dimension_semantics=("arbitrary",) * n_grid, vmem_limit_bytes=vmem_limit)


def _iota(shape, dim):
    return lax.broadcasted_iota(jnp.int32, shape, dim)


NN = (((1,), (0,)), ((), ()))
NT = (((1,), (1,)), ((), ()))
TN = (((0,), (0,)), ((), ()))


def _dg(a, b, dims=NN, precision=None):
    return lax.dot_general(a, b, dims, preferred_element_type=F32, precision=precision)


def _split(x):
    hi = x.astype(BF16)
    return hi, (x - hi.astype(F32)).astype(BF16)


def _dgs(a_s, b_s, dims=NN):
    if len(a_s) == 1:
        return _dg(a_s[0], b_s[0], dims)
    (ah, al), (bh, bl) = a_s, b_s
    ca, cb = dims[0][0][0], dims[0][1][0]
    return _dg(jnp.concatenate([ah, ah, al], axis=ca), jnp.concatenate([bh, bl, bh], axis=cb), dims)


def _dg3(a, b, dims=NN):
    return _dgs(_split(a), _split(b), dims)


def _sigmoid(x):
    return 1.0 / (1.0 + jnp.exp(-x))


def _softplus(x):
    return jnp.maximum(x, 0.0) + jnp.log(1.0 + jnp.exp(-jnp.abs(x)))


def _seg64_sum(x):
    rows, width = x.shape
    lo = _iota((rows, LANES), 1) < HEAD_DIM
    outs = []
    for c in range(width // LANES):
        blk = x[:, c * LANES:(c + 1) * LANES]
        s_lo = jnp.sum(jnp.where(lo, blk, 0.0), axis=-1, keepdims=True)
        s_hi = jnp.sum(jnp.where(lo, 0.0, blk), axis=-1, keepdims=True)
        outs.append(jnp.where(lo, s_lo, s_hi))
    return outs[0] if len(outs) == 1 else jnp.concatenate(outs, axis=1)


def _layer_norm(x, w, b):
    mu = jnp.mean(x, axis=-1, keepdims=True)
    xc = x - mu
    var = jnp.mean(xc * xc, axis=-1, keepdims=True)
    return xc * lax.rsqrt(var + LN_EPS) * w + b


def _mod_row(i):
    return jnp.where(i < NCB_D, 0, 1 + (i - NCB_D) // LAT_BLKS_D)


def _layer_spec(l, tail, n_grid, single=False):
    idx = lambda *g: (l,) + (0,) * len(tail)
    del n_grid
    if single:
        return pl.BlockSpec((None,) + tuple(tail), idx, pipeline_mode=pl.Buffered(1))
    return pl.BlockSpec((None,) + tuple(tail), idx)


ANY_SPEC = pl.BlockSpec(memory_space=pl.ANY)


def _ctx_spec(w):
    return pl.BlockSpec((TD, w), lambda i: (jnp.minimum(i, NCB_D - 1), 0))


def _lat_spec(w):
    return pl.BlockSpec((TD, w), lambda i: (jnp.maximum(i - NCB_D, 0), 0))


def _pick(i, ctx_ref, lat_ref):
    return jnp.where(i < NCB_D, ctx_ref[...], lat_ref[...])


def _mod_kernel(c_ref, w_ref, b_ref, o_ref):
    c = c_ref[...]
    s = c * _sigmoid(c)
    o_ref[...] = _dg(s, w_ref[...], NN, HIGHEST) + b_ref[...]


def _modulation(cc, w_mod, b_mod):
    return pl.pallas_call(
        _mod_kernel,
        grid=(DEPTH, 6),
        in_specs=[
            pl.BlockSpec((MOD_ROWS, D_MODEL), lambda l, j: (0, 0)),
            pl.BlockSpec((None, D_MODEL, D_MODEL), lambda l, j: (l, 0, j)),
            pl.BlockSpec((None, 1, D_MODEL), lambda l, j: (l, 0, j)),
        ],
        out_specs=pl.BlockSpec((None, MOD_ROWS, D_MODEL), lambda l, j: (l, 0, j)),
        out_shape=jax.ShapeDtypeStruct((DEPTH, MOD_ROWS, 6 * D_MODEL), F32),
        compiler_params=_cparams(2),
        name="modulation",
    )(cc, w_mod, b_mod.reshape(DEPTH, 1, 6 * D_MODEL))


def _bias_kernel(rpb_ref, o_ref):
    l = pl.program_id(0)
    q = _iota((GRID_W, LANES), 0)
    x = _iota((GRID_W, LANES), 1)
    c = x % GRID_W
    right = x >= GRID_W
    dc = jnp.clip(c - q, 1 - NA_COLS, NA_COLS - 1) + NA_COLS - 1
    c0 = jnp.clip(q - NA_COLS // 2, 0, GRID_W - NA_COLS)
    in_win = (c >= c0) & (c < c0 + NA_COLS)

    def body(t, carry):
        h = t // (N_DR - 1)
        dr = t % (N_DR - 1)
        base = ((l * H_NA + h) * N_DR + dr) * N_DC
        acc = jnp.zeros((GRID_W, LANES), F32)
        for d in range(N_DC):
            s0 = rpb_ref[base + d]
            s1 = rpb_ref[base + N_DC + d]
            acc = jnp.where(dc == d, jnp.where(right, s1, s0), acc)
        o_ref[h, dr] = jnp.where(in_win, acc, NEG_INF)
        return carry

    lax.fori_loop(0, H_NA * (N_DR - 1), body, 0)


def _bias_tables(na_rpb):
    return pl.pallas_call(
        _bias_kernel,
        grid=(DEPTH,),
        in_specs=[pl.BlockSpec(memory_space=pltpu.SMEM)],
        out_specs=pl.BlockSpec((None, H_NA, N_DR - 1, GRID_W, LANES), lambda l: (l, 0, 0, 0, 0)),
        out_shape=jax.ShapeDtypeStruct((DEPTH, H_NA, N_DR - 1, GRID_W, LANES), F32),
        compiler_params=_cparams(1),
        name="na_bias_tables",
    )(na_rpb.reshape(-1))


def _rope(x, cos, sin):
    k = x.shape[1] // LANES
    cosf = cos if k == 1 else jnp.concatenate([cos] * k, axis=1)
    sinf = sin if k == 1 else jnp.concatenate([sin] * k, axis=1)
    first = (_iota(x.shape, 1) % (2 * ROPE_FREQ)) < ROPE_FREQ
    partner = jnp.where(first, pltpu.roll(x, x.shape[1] - ROPE_FREQ, axis=1), pltpu.roll(x, ROPE_FREQ, axis=1))
    return x * cosf + partner * sinf


def _inproj_kernel(xc_ref, xl_ref, mod_ref, w_ref, qn_ref, kn_ref, cos_ref, sin_ref, c0_ref, c1_ref, c2_ref, c3_ref,
                   feat_ref, naq_ref, nak_ref, nav_ref, gq_ref, gk_ref, gv_ref,
                   cnak_ref, cnav_ref, cgk_ref, cgv_ref):
    del c0_ref, c1_ref, c2_ref, c3_ref
    i = pl.program_id(0)
    row = _mod_row(i)
    shift1 = mod_ref[pl.ds(row, 1), 0:D_MODEL]
    scale1 = mod_ref[pl.ds(row, 1), D_MODEL:2 * D_MODEL]
    xm = (_pick(i, xc_ref, xl_ref) * (1.0 + scale1) + shift1).astype(BF16)
    proj = _dg(xm, w_ref[...])
    o_na = RWKV_IN
    o_g = RWKV_IN + NA_IN
    feat_ref[...] = proj[:, :RWKV_IN]
    naq_ref[...] = proj[:, o_na:o_na + W_NA] * ATTN_SCALE
    nak = proj[:, o_na + W_NA:o_na + 2 * W_NA]
    nav = proj[:, o_na + 2 * W_NA:o_na + 3 * W_NA]
    q = proj[:, o_g:o_g + W_GQA]
    k = proj[:, o_g + W_GQA:o_g + W_GQA + W_GQA_KV]
    gv = proj[:, o_g + W_GQA + W_GQA_KV:]
    cos = cos_ref[...]
    sin = sin_ref[...]
    q = q * lax.rsqrt(_seg64_sum(q * q) * (1.0 / HEAD_DIM) + RMS_EPS) * qn_ref[...]
    k = k * lax.rsqrt(_seg64_sum(k * k) * (1.0 / HEAD_DIM) + RMS_EPS) * kn_ref[...]
    gk = _rope(k, cos, sin)
    gq_ref[...] = _rope(q, cos, sin) * ATTN_SCALE
    nak_ref[...] = nak
    nav_ref[...] = nav
    gk_ref[...] = gk
    gv_ref[...] = gv

    @pl.when(i < NCB_D)
    def _():
        for ref, val in ((cnak_ref, nak), (cnav_ref, nav), (cgk_ref, gk), (cgv_ref, gv)):
            ref[...] = val.reshape(ref.shape)


def _inproj(l, x_ctx, x_lat, mod_all, w_in_bf, qn, kn, cos_tab, sin_tab, caches):
    tab_idx = lambda i: (jnp.where(i < NCB_D, 0, 1 + (i - NCB_D) % LAT_BLKS_D), 0)
    widths = (RWKV_IN, W_NA, W_NA, W_NA, W_GQA, W_GQA_KV, W_GQA_KV)
    cache_w = (W_NA, W_NA, W_GQA_KV, W_GQA_KV)
    cache_spec = lambda w: pl.BlockSpec((TD // SEQ, None, SEQ, w), lambda i: (jnp.minimum(i, NCB_D - 1), l, 0, 0))
    n_in = 8
    return pl.pallas_call(
        _inproj_kernel,
        grid=(NBLK_D,),
        in_specs=[
            _ctx_spec(D_MODEL), _lat_spec(D_MODEL),
            _layer_spec(l, (MOD_ROWS, 6 * D_MODEL), 1),
            _layer_spec(l, (D_MODEL, D_IN), 1),
            _layer_spec(l, (1, W_GQA), 1),
            _layer_spec(l, (1, W_GQA_KV), 1),
            pl.BlockSpec((TD, LANES), tab_idx),
            pl.BlockSpec((TD, LANES), tab_idx),
        ] + [ANY_SPEC] * 4,
        out_specs=[pl.BlockSpec((TD, w), lambda i: (i, 0)) for w in widths] + [cache_spec(w) for w in cache_w],
        out_shape=[jax.ShapeDtypeStruct((N_TOK, w), F32) for w in widths]
        + [jax.ShapeDtypeStruct((BATCH, DEPTH, SEQ, w), F32) for w in cache_w],
        input_output_aliases={n_in + j: len(widths) + j for j in range(4)},
        compiler_params=_cparams(1),
        name="inproj",
    )(x_ctx, x_lat, mod_all, w_in_bf, qn, kn, cos_tab, sin_tab, *caches)


def _rwkv_prep_kernel(f_ref, fp_ref, fn_ref, conv_ref, w0_ref, w2_ref, a0_ref, a2_ref, g2_ref,
                      kk_ref, ka_ref, rk_ref,
                      r_ref, kap_ref, v_ref, lw_ref, ah_ref, kd_ref, g_ref, bonus_ref):
    i = pl.program_id(0)
    pos = (i - NCB) % LAT_BLKS
    lat = i >= NCB
    has_prev = jnp.where(lat & (pos != 0), 1.0, 0.0)
    has_next = jnp.where(lat & (pos != LAT_BLKS - 1), 1.0, 0.0)
    x = f_ref[...]
    rows = _iota(x.shape, 0)
    x_prev = jnp.where(rows == 0, fp_ref[7:8, :] * has_prev, pltpu.roll(x, 1, axis=0))
    x_next = jnp.where(rows == TM - 1, fn_ref[0:1, :] * has_next, pltpu.roll(x, TM - 1, axis=0))
    f = x_prev * conv_ref[0:1, :] + x * conv_ref[1:2, :] + x_next * conv_ref[2:3, :]
    o1, o2, o3 = W_RWKV, 2 * W_RWKV, 3 * W_RWKV
    o4 = o3 + 2 * LORA_W
    o5 = o4 + 2 * LORA_A
    r, k, v = f[:, :o1], f[:, o1:o2], f[:, o2:o3]
    wd, ad, gd = f[:, o3:o4], f[:, o4:o5], f[:, o5:]
    log_w = -_softplus(-(w0_ref[...] + _dg3(jnp.tanh(wd), w2_ref[...]))) - 0.5
    a = _sigmoid(a0_ref[...] + _dg3(ad, a2_ref[...]))
    g_ref[...] = _dg3(_sigmoid(gd), g2_ref[...])
    kk = k * kk_ref[...]
    kap = kk / jnp.maximum(jnp.sqrt(_seg64_sum(kk * kk)), 1e-12)
    ka = ka_ref[...]
    kd_sum = jnp.zeros_like(k)
    for d in range(2):
        a_d = a[:, d * W_RWKV:(d + 1) * W_RWKV]
        kd = k * (1.0 + (a_d - 1.0) * ka)
        kd_sum = kd_sum + kd
        lw_ref[d] = -jnp.exp(log_w[:, d * W_RWKV:(d + 1) * W_RWKV])
        ah_ref[d] = a_d * kap
        kd_ref[d] = kd
    r_ref[...] = r
    kap_ref[...] = kap
    v_ref[...] = v
    bonus_ref[...] = _seg64_sum(r * kd_sum * rk_ref[...]) * v


def _rwkv_prep(l, feat, conv, w0, w2bd, a0, a2bd, g2, k_k, k_a, r_k):
    sub = TM // 8
    tok = pl.BlockSpec((TM, W_RWKV), lambda i: (i, 0))
    tok2 = pl.BlockSpec((2, TM, W_RWKV), lambda i: (0, i, 0))
    return pl.pallas_call(
        _rwkv_prep_kernel,
        grid=(NBLK,),
        in_specs=[
            pl.BlockSpec((TM, RWKV_IN), lambda i: (i, 0)),
            pl.BlockSpec((8, RWKV_IN), lambda i: (jnp.maximum(i * sub - 1, 0), 0)),
            pl.BlockSpec((8, RWKV_IN), lambda i: (jnp.minimum((i + 1) * sub, NBLK * sub - 1), 0)),
            _layer_spec(l, (3, RWKV_IN), 1), _layer_spec(l, (1, 2 * W_RWKV), 1),
            _layer_spec(l, (2 * LORA_W, 2 * W_RWKV), 1), _layer_spec(l, (1, 2 * W_RWKV), 1),
            _layer_spec(l, (2 * LORA_A, 2 * W_RWKV), 1), _layer_spec(l, (LORA_G, W_RWKV), 1),
            _layer_spec(l, (1, W_RWKV), 1), _layer_spec(l, (1, W_RWKV), 1), _layer_spec(l, (1, W_RWKV), 1),
        ],
        out_specs=[tok, tok, tok, tok2, tok2, tok2, tok, tok],
        out_shape=[jax.ShapeDtypeStruct((N_TOK, W_RWKV), F32)] * 3
        + [jax.ShapeDtypeStruct((2, N_TOK, W_RWKV), F32)] * 3
        + [jax.ShapeDtypeStruct((N_TOK, W_RWKV), F32)] * 2,
        compiler_params=_cparams(1),
        name="rwkv_prep",
    )(feat, feat, feat, conv, w0, w2bd, a0, a2bd, g2, k_k, k_a, r_k)


def _scan_kernel(r0_ref, kap0_ref, v0_ref, r1_ref, kap1_ref, v1_ref,
                 lw0_ref, ah0_ref, kd0_ref, lw1_ref, ah1_ref, kd1_ref, s00_ref, s01_ref, sf0_in_ref, sf1_in_ref,
                 o0_ref, o1_ref, sf0_ref, sf1_ref,
                 s_scr, rt_scr, kt_scr, kdt_scr, at_scr, cum_scr, m_scr, n_scr, q_scr, oo_scr):
    del sf0_in_ref, sf1_in_ref
    j = pl.program_id(0)
    par = j % 2
    jb = jnp.maximum(j - 1, 0)
    upd_blk = (jb, NBLK - 1 - jb)
    first_pos = (0, LAT_BLKS - 1)
    r_refs, kap_refs, v_refs = (r0_ref, r1_ref), (kap0_ref, kap1_ref), (v0_ref, v1_ref)
    lw_refs, ah_refs, kd_refs = (lw0_ref, lw1_ref), (ah0_ref, ah1_ref), (kd0_ref, kd1_ref)
    s0_refs, o_refs, sf_refs = (s00_ref, s01_ref), (o0_ref, o1_ref), (sf0_ref, sf1_ref)

    @pl.when(j == 0)
    def _():
        for ref in (s_scr, m_scr, n_scr, q_scr, oo_scr):
            ref[...] = jnp.zeros_like(ref)

    for dn in range(2):
        blk = upd_blk[dn]

        @pl.when((j > 0) & (blk < NCB))
        def _():
            s_scr[dn] = jnp.zeros(s_scr.shape[1:], F32)

        @pl.when((j > 0) & (blk >= NCB) & ((blk - NCB) % LAT_BLKS == first_pos[dn]))
        def _():
            s_scr[dn] = s0_refs[dn][...]

    def update(c):
        for dn in range(2):
            ce = c if dn == 0 else N_CHUNK - 1 - c
            rows = slice(ce * CHUNK, (ce + 1) * CHUNK)
            for p in range(N_PAIR):
                s = s_scr[dn, p]
                o_refs[dn][rows, p * LANES:(p + 1) * LANES] = (
                    _dg3(q_scr[1 - par, dn, ce, p], s, NT) + oo_scr[1 - par, dn, ce, p])
                s_scr[dn, p] = _dg3(s, m_scr[1 - par, dn, ce, p]) + n_scr[1 - par, dn, ce, p]

    update(0)
    rr = _iota((TM, TM), 0)
    cc = _iota((TM, TM), 1)
    same_chunk = (rr // CHUNK) == (cc // CHUNK)
    for dn in range(2):
        order = (rr >= cc) if dn == 0 else (rr <= cc)
        tri = jnp.where(same_chunk & order, 1.0, 0.0).astype(BF16)
        lw = lw_refs[dn][...]
        lw_hi = lw.astype(BF16)
        lw_r = lw - lw_hi.astype(F32)
        lw_mid = lw_r.astype(BF16)
        lw_lo = (lw_r - lw_mid.astype(F32)).astype(BF16)
        cum = _dg(jnp.concatenate([tri, tri, tri], axis=1), jnp.concatenate([lw_hi, lw_mid, lw_lo], axis=0))
        e_neg = jnp.exp(-cum)
        rt_scr[dn] = r_refs[dn][...] * jnp.exp(cum)
        kt_scr[dn] = kap_refs[dn][...] * jnp.exp(cum - lw)
        kdt_scr[dn] = kd_refs[dn][...] * e_neg
        at_scr[dn] = ah_refs[dn][...] * e_neg
        cum_scr[dn] = cum

    row = _iota((CHUNK, LANES), 0)
    col = _iota((CHUNK, LANES), 1) % CHUNK
    left = _iota((CHUNK, LANES), 1) < CHUNK
    incl = (row >= col, row <= col)
    strict = (row > col, row < col)
    eye = jnp.where(row == col, 1.0, 0.0)
    r2 = _iota((LANES, LANES), 0)
    c2 = _iota((LANES, LANES), 1)
    bd_mask = (r2 // CHUNK) == (c2 // CHUNK)
    eye2 = jnp.where(r2 == c2, 1.0, 0.0)

    zero_bf = jnp.zeros((CHUNK, LANES), BF16)
    split = lambda x: (x.astype(BF16),)

    def bd(xs):
        return tuple(jnp.concatenate([jnp.where(left, y, zero_bf), jnp.where(left, zero_bf, y)], axis=0)
                     for y in xs)

    def cat(xs, ys, axis):
        return tuple(jnp.concatenate([x, y], axis=axis) for x, y in zip(xs, ys))

    def mm(a_s, b_s):
        return _dgs(a_s, bd(b_s))

    units = [(dn, c, p) for c in range(N_CHUNK) for dn in range(2) for p in range(N_PAIR)]
    each = lambda f, *lists: [f(*args) for args in zip(*lists)]

    def load(ref):
        return [ref[dn, c * CHUNK:(c + 1) * CHUNK, p * LANES:(p + 1) * LANES] for dn, c, p in units]

    rt = load(rt_scr)
    v = [v_refs[dn][c * CHUNK:(c + 1) * CHUNK, p * LANES:(p + 1) * LANES] for dn, c, p in units]
    kt_s, rt_s, kdt_s, at_s, v_s = (each(split, x) for x in (load(kt_scr), rt, load(kdt_scr), load(at_scr), v))
    gam = [jnp.exp(cum_scr[dn, (c + 1) * CHUNK - 1:(c + 1) * CHUNK, p * LANES:(p + 1) * LANES] if dn == 0 else
                   cum_scr[dn, c * CHUNK:c * CHUNK + 1, p * LANES:(p + 1) * LANES]) for dn, c, p in units]
    gram = each(lambda k, r, a, kd: _dgs(cat(k, r, 0), cat(bd(a), bd(kd), 0), NT), kt_s, rt_s, at_s, kdt_s)
    dirs = [u[0] for u in units]
    la = [jnp.where(strict[dn], g[0:CHUNK, 0:LANES], 0.0) for dn, g in zip(dirs, gram)]
    lk_s = [split(jnp.where(strict[dn], g[0:CHUNK, LANES:], 0.0)) for dn, g in zip(dirs, gram)]
    ra_s = [split(jnp.where(incl[dn], g[CHUNK:, 0:LANES], 0.0)) for dn, g in zip(dirs, gram)]
    rk_s = [split(jnp.where(incl[dn], g[CHUNK:, LANES:], 0.0)) for dn, g in zip(dirs, gram)]
    update(1)
    lrv = each(lambda lk, rk, vv: mm(cat(lk, rk, 0), vv), lk_s, rk_s, v_s)
    b = 8
    l8 = [jnp.where((row // b) == (col // b), x, 0.0) for x in la]
    l8_s = each(split, l8)
    l8_2 = each(mm, l8_s, l8_s)
    update(2)
    l8_2s = each(split, l8_2)
    l8_4 = each(mm, l8_2s, l8_2s)
    p1 = each(lambda x, y: mm(split(eye - x), split(eye + y)), l8, l8_2)
    update(3)
    t = each(lambda x, y: mm(split(x), split(eye + y)), p1, l8_4)
    while b < CHUNK:
        offd = ((row // (2 * b)) == (col // (2 * b))) & ((row // b) != (col // b))
        t_s = each(split, t)
        x = each(lambda ts, y: mm(ts, split(jnp.where(offd, y, 0.0))), t_s, la)
        t = each(lambda tt, xx, ts: tt - mm(split(xx), ts), t, x, t_s)
        b *= 2
    tx = each(lambda tt, k, y: _dgs(split(tt), cat(bd(k), bd(split(y[0:CHUNK])), 1)), t, kt_s, lrv)
    khat_s = [split(y[:, 0:LANES]) for y in tx]
    w1_s = [split(y[:, LANES:]) for y in tx]
    rx = each(lambda r, k, w: _dgs(r, cat(bd(k), bd(w), 1)), ra_s, khat_s, w1_s)
    mk = each(lambda k, a: _dgs(k, a, TN), khat_s, at_s)
    nk = each(lambda vv, w, kd, a: _dgs(cat(vv, w, 0), cat(kd, tuple(-y for y in a), 0), TN),
              v_s, w1_s, kdt_s, at_s)
    for i, (dn, c, p) in enumerate(units):
        q_scr[par, dn, c, p] = rt[i] - rx[i][:, 0:LANES]
        oo_scr[par, dn, c, p] = lrv[i][CHUNK:] - rx[i][:, LANES:]
        m_scr[par, dn, c, p] = (eye2 - jnp.where(bd_mask, mk[i], 0.0)) * gam[i]
        n_scr[par, dn, c, p] = jnp.where(bd_mask, nk[i], 0.0) * gam[i]

    for dn in range(2):
        @pl.when((j > 0) & (upd_blk[dn] < NCB))
        def _():
            sf_refs[dn][...] = s_scr[dn]


def _rwkv_scan(l, r, kap, v, lw, ah, kd, s0_lat, s_fin):
    def stage_blk(dn, j):
        jj = jnp.minimum(j, NBLK - 1)
        return jj if dn == 0 else NBLK - 1 - jj

    def update_blk(dn, j):
        jj = jnp.maximum(j - 1, 0)
        return jj if dn == 0 else NBLK - 1 - jj

    pair = (N_PAIR, LANES, LANES)
    tok = lambda dn: pl.BlockSpec((TM, W_RWKV), lambda j: (stage_blk(dn, j), 0))
    tok2 = lambda dn: pl.BlockSpec((None, TM, W_RWKV), lambda j: (dn, stage_blk(dn, j), 0))
    s0_spec = lambda dn: pl.BlockSpec(
        (None, None, None) + pair, lambda j: (jnp.maximum(update_blk(dn, j) - NCB, 0) // LAT_BLKS, l, dn, 0, 0, 0))
    sf_spec = lambda dn: pl.BlockSpec(
        (None, None) + pair, lambda j: (jnp.minimum(update_blk(dn, j), NCB - 1), l, 0, 0, 0))
    o_spec = lambda dn: pl.BlockSpec((TM, W_RWKV), lambda j: (update_blk(dn, j), 0))
    dir_scr = lambda *shape: pltpu.VMEM((2,) + shape, F32)
    return pl.pallas_call(
        _scan_kernel,
        grid=(NBLK + 1,),
        in_specs=[tok(0), tok(0), tok(0), tok(1), tok(1), tok(1),
                  tok2(0), tok2(0), tok2(0), tok2(1), tok2(1), tok2(1),
                  s0_spec(0), s0_spec(1), ANY_SPEC, ANY_SPEC],
        out_specs=[o_spec(0), o_spec(1), sf_spec(0), sf_spec(1)],
        out_shape=[jax.ShapeDtypeStruct((N_TOK, W_RWKV), F32)] * 2
        + [jax.ShapeDtypeStruct((BATCH, DEPTH) + pair, F32)] * 2,
        scratch_shapes=[dir_scr(*pair)] + [dir_scr(TM, W_RWKV)] * 5
        + [dir_scr(2, N_CHUNK, *pair)] * 2 + [dir_scr(2, N_CHUNK, N_PAIR, CHUNK, LANES)] * 2,
        input_output_aliases={14: 2, 15: 3},
        compiler_params=_cparams(1),
        name="rwkv_scan",
    )(r, kap, v, r, kap, v, lw, ah, kd, lw, ah, kd, s0_lat, s0_lat, s_fin[0], s_fin[1])


def _attend(groups):
    lhs = []
    for q_cols, _, _, _ in groups:
        left = _iota(q_cols[0].shape, 1) < HEAD_DIM
        parts = []
        for qc in q_cols:
            parts += [jnp.where(left, qc, 0.0), jnp.where(left, 0.0, qc)]
        lhs.append(jnp.concatenate(parts, axis=0).astype(BF16))
    s = [_dg(x, g[1], NT) for x, g in zip(lhs, groups)]
    s = [x if g[3] is None else x + g[3] for x, g in zip(s, groups)]
    p = [jnp.exp(x - jnp.max(x, axis=-1, keepdims=True)) for x in s]
    inv = [1.0 / jnp.sum(x, axis=-1, keepdims=True) for x in p]
    o = [_dg(x.astype(BF16), g[2]) * y for x, y, g in zip(p, inv, groups)]
    outs = []
    for x, (q_cols, _, _, _) in zip(o, groups):
        rows = q_cols[0].shape[0]
        left = _iota(q_cols[0].shape, 1) < HEAD_DIM
        outs.append([jnp.where(left, x[2 * j * rows:(2 * j + 1) * rows], x[(2 * j + 1) * rows:(2 * j + 2) * rows])
                     for j in range(len(q_cols))])
    return outs


def _cols(x):
    return [x[:, c * LANES:(c + 1) * LANES] for c in range(x.shape[1] // LANES)]


def _gqa_groups(q, k, v):
    left = _iota(k.shape, 1) < HEAD_DIM
    k_sw = pltpu.roll(k, HEAD_DIM, axis=1)
    v_sw = pltpu.roll(v, HEAD_DIM, axis=1)
    q_cols = _cols(q)
    groups = []
    for g in range(H_GQA_KV):
        k2 = jnp.where(left, k, k_sw) if g == 0 else jnp.where(left, k_sw, k)
        v2 = jnp.where(left, v, v_sw) if g == 0 else jnp.where(left, v_sw, v)
        groups.append((q_cols[2 * g:2 * g + 2], k2.astype(BF16), v2.astype(BF16), None))
    return groups


def _ctx_attn_kernel(naq_ref, nak_ref, nav_ref, gq_ref, gk_ref, gv_ref, ona_ref, og_ref):
    k_cols = _cols(nak_ref[...].astype(BF16))
    v_cols = _cols(nav_ref[...].astype(BF16))
    na_groups = [([qc], kc, vc, None) for qc, kc, vc in zip(_cols(naq_ref[...]), k_cols, v_cols)]
    outs = _attend(na_groups + _gqa_groups(gq_ref[...], gk_ref[...], gv_ref[...]))
    n_na = len(na_groups)
    ona_ref[...] = jnp.concatenate([o[0] for o in outs[:n_na]], axis=1)
    og_ref[...] = jnp.concatenate([c for o in outs[n_na:] for c in o], axis=1)


def _ctx_attention(naq, nak, nav, gq, gk, gv):
    spec = lambda w: pl.BlockSpec((SEQ, w), lambda b: (b, 0))
    return pl.pallas_call(
        _ctx_attn_kernel,
        grid=(BATCH,),
        in_specs=[spec(W_NA), spec(W_NA), spec(W_NA), spec(W_GQA), spec(W_GQA_KV), spec(W_GQA_KV)],
        out_specs=[spec(W_NA), spec(W_GQA)],
        out_shape=[jax.ShapeDtypeStruct((N_CTX, W_NA), F32), jax.ShapeDtypeStruct((N_CTX, W_GQA), F32)],
        compiler_params=_cparams(1),
        name="ctx_attention",
    )(naq, nak, nav, gq, gk, gv)


NA_STEP_ROWS = 2


def _lat_na_kernel(q_ref, k_ref, v_ref, kc_ref, vc_ref, tb_ref, o_ref):
    q = q_ref[...]
    groups = []
    for rr in range(NA_STEP_ROWS):
        r = pl.program_id(1) * NA_STEP_ROWS + rr
        r0 = jnp.clip(r - NA_ROWS // 2, 0, GRID_ROWS - NA_ROWS)
        band = pl.ds(pl.multiple_of(r0 * GRID_W, GRID_W), N_BAND)
        dr0 = r0 - r + NA_ROWS - 1
        for c, qc in enumerate(_cols(q[rr * GRID_W:(rr + 1) * GRID_W])):
            cols = slice(c * LANES, (c + 1) * LANES)
            k2 = jnp.concatenate([k_ref[band, cols], kc_ref[:, cols]], axis=0).astype(BF16)
            v2 = jnp.concatenate([v_ref[band, cols], vc_ref[:, cols]], axis=0).astype(BF16)
            bias = jnp.concatenate(
                [jnp.concatenate([tb_ref[2 * c + half, dr0 + jj] for jj in range(0, NA_ROWS, 2)], axis=1)
                 for half in range(2)], axis=0)
            bias = jnp.concatenate([bias, jnp.zeros((2 * GRID_W, PAST_LEN), F32)], axis=1)
            groups.append(([qc], k2, v2, bias))
    outs = [o[0] for o in _attend(groups)]
    n_col = W_NA // LANES
    o_ref[...] = jnp.concatenate(
        [jnp.concatenate(outs[rr * n_col:(rr + 1) * n_col], axis=1) for rr in range(NA_STEP_ROWS)], axis=0)


def _lat_na(l, naq, nak, nav, kc, vc, tb):
    rows = NA_STEP_ROWS * GRID_W
    steps = GRID_ROWS // NA_STEP_ROWS
    seq_blk0 = N_CTX // DEC_SEQ
    seq = pl.BlockSpec((DEC_SEQ, W_NA), lambda b, r: (seq_blk0 + b, 0))
    cache = pl.BlockSpec((None, None, PAST_LEN, W_NA), lambda b, r: (b, l, 0, 0))
    return pl.pallas_call(
        _lat_na_kernel,
        grid=(DEC_BATCH, steps),
        in_specs=[pl.BlockSpec((rows, W_NA), lambda b, r: (N_CTX // rows + b * steps + r, 0)),
                  seq, seq, cache, cache, _layer_spec(l, (H_NA, N_DR - 1, GRID_W, LANES), 2)],
        out_specs=pl.BlockSpec((rows, W_NA), lambda b, r: (b * steps + r, 0)),
        out_shape=jax.ShapeDtypeStruct((N_LAT, W_NA), F32),
        compiler_params=_cparams(2),
        name="latent_na",
    )(naq, nak, nav, kc, vc, tb)


def _lat_gqa_kernel(q_ref, k_ref, v_ref, kc_ref, vc_ref, o_ref):
    k = jnp.concatenate([kc_ref[...], k_ref[...]], axis=0)
    v = jnp.concatenate([vc_ref[...], v_ref[...]], axis=0)
    o_ref[...] = jnp.concatenate([c for o in _attend(_gqa_groups(q_ref[...], k, v)) for c in o], axis=1)


def _lat_gqa(l, gq, gk, gv, kc, vc):
    seq_blk0 = N_CTX // DEC_SEQ
    seq = pl.BlockSpec((DEC_SEQ, W_GQA_KV), lambda b, i: (seq_blk0 + b, 0))
    cache = pl.BlockSpec((None, None, PAST_LEN, W_GQA_KV), lambda b, i: (b, l, 0, 0))
    return pl.pallas_call(
        _lat_gqa_kernel,
        grid=(DEC_BATCH, LAT_BLKS),
        in_specs=[pl.BlockSpec((TM, W_GQA), lambda b, i: (NCB + b * LAT_BLKS + i, 0)), seq, seq, cache, cache],
        out_specs=pl.BlockSpec((TM, W_GQA), lambda b, i: (b * LAT_BLKS + i, 0)),
        out_shape=jax.ShapeDtypeStruct((N_LAT, W_GQA), F32),
        compiler_params=_cparams(2),
        name="latent_gqa",
    )(gq, gk, gv, kc, vc)


def _mix_ffn_kernel(xc_ref, xl_ref, mod_ref, of_ref, ob_ref, g_ref, bonus_ref, lnxw_ref, lnxb_ref,
                    onac_ref, onal_ref, ogc_ref, ogl_ref,
                    wout_ref, ln1w_ref, ln1b_ref, wfi_ref, wfo_ref, ln2w_ref, ln2b_ref, yc_ref, yl_ref):
    i = pl.program_id(0)
    row = _mod_row(i)
    mod = lambda n: mod_ref[pl.ds(row, 1), n * D_MODEL:(n + 1) * D_MODEL]
    o = of_ref[...] + ob_ref[...]
    mu = _seg64_sum(o) * (1.0 / HEAD_DIM)
    oc = o - mu
    var = _seg64_sum(oc * oc) * (1.0 / HEAD_DIM)
    o_rwkv = (oc * lax.rsqrt(var + GN_EPS) * lnxw_ref[...] + lnxb_ref[...] + bonus_ref[...]) * g_ref[...]
    mix_in = jnp.concatenate([o_rwkv, _pick(i, onac_ref, onal_ref), _pick(i, ogc_ref, ogl_ref)], axis=1).astype(BF16)
    mix = _dg(mix_in, wout_ref[...])
    x1 = _layer_norm(DEEPNORM_ALPHA * _pick(i, xc_ref, xl_ref) + mod(2) * mix, ln1w_ref[...], ln1b_ref[...])
    x_in = (x1 * (1.0 + mod(4)) + mod(3)).astype(BF16)
    ffn = jnp.zeros_like(x1)
    for lo in range(0, D_FF, FF_CHUNK):
        gate = _dg(x_in, wfi_ref[:, lo:lo + FF_CHUNK])
        up = _dg(x_in, wfi_ref[:, D_FF + lo:D_FF + lo + FF_CHUNK])
        act = (gate * _sigmoid(gate) * up).astype(BF16)
        ffn = ffn + _dg(act, wfo_ref[lo:lo + FF_CHUNK, :])
    y = _layer_norm(DEEPNORM_ALPHA * x1 + mod(5) * ffn, ln2w_ref[...], ln2b_ref[...])

    @pl.when(i < NCB_D)
    def _():
        yc_ref[...] = y

    @pl.when(i >= NCB_D)
    def _():
        yl_ref[...] = y


def _mix_ffn(l, x_ctx, x_lat, mod_all, o_fwd, o_bwd, g, bonus, lnx_w, lnx_b, o_na_ctx, o_na_lat, o_g_ctx, o_g_lat,
             w_out_bf, ln1_w, ln1_b, w_ffn_in_bf, w_ffn_out_bf, ln2_w, ln2_b):
    tok = lambda w: pl.BlockSpec((TD, w), lambda i: (i, 0))
    once = lambda *tail: _layer_spec(l, tail, 1, single=True)
    return pl.pallas_call(
        _mix_ffn_kernel,
        grid=(NBLK_D,),
        in_specs=[
            _ctx_spec(D_MODEL), _lat_spec(D_MODEL), once(MOD_ROWS, 6 * D_MODEL),
            tok(W_RWKV), tok(W_RWKV),
            tok(W_RWKV), tok(W_RWKV), once(1, W_RWKV), once(1, W_RWKV),
            _ctx_spec(W_NA), _lat_spec(W_NA), _ctx_spec(W_GQA), _lat_spec(W_GQA),
            once(D_MODEL, D_MODEL), once(1, D_MODEL), once(1, D_MODEL),
            once(D_MODEL, 2 * D_FF), once(D_FF, D_MODEL), once(1, D_MODEL), once(1, D_MODEL),
        ],
        out_specs=[_ctx_spec(D_MODEL), _lat_spec(D_MODEL)],
        out_shape=[jax.ShapeDtypeStruct((N_CTX, D_MODEL), F32), jax.ShapeDtypeStruct((N_LAT, D_MODEL), F32)],
        compiler_params=_cparams(1, VMEM_LIMIT_FFN),
        name="mix_ffn",
    )(x_ctx, x_lat, mod_all, o_fwd, o_bwd, g, bonus, lnx_w, lnx_b, o_na_ctx, o_na_lat, o_g_ctx, o_g_lat,
      w_out_bf, ln1_w, ln1_b, w_ffn_in_bf, w_ffn_out_bf, ln2_w, ln2_b)


def _rope_tables():
    t = jnp.arange(DEC_SEQ)
    inv = ROPE_BASE ** (-jnp.arange(ROPE_FREQ, dtype=F32) / ROPE_FREQ)
    ang_r = (t // GRID_W).astype(F32)[:, None] * inv
    ang_c = (t % GRID_W).astype(F32)[:, None] * inv
    cos = jnp.concatenate([jnp.cos(ang_r)] * 2 + [jnp.cos(ang_c)] * 2, axis=1)
    sin = jnp.concatenate([-jnp.sin(ang_r), jnp.sin(ang_r), -jnp.sin(ang_c), jnp.sin(ang_c)], axis=1)
    cos = jnp.concatenate([jnp.ones((TD, HEAD_DIM), F32), cos], axis=0)
    sin = jnp.concatenate([jnp.zeros((TD, HEAD_DIM), F32), sin], axis=0)
    return jnp.tile(cos, (1, LANES // HEAD_DIM)), jnp.tile(sin, (1, LANES // HEAD_DIM))


def _block_diag2(w):
    z = jnp.zeros_like(w[:, 0])
    return jnp.concatenate([jnp.concatenate([w[:, 0], z], axis=2), jnp.concatenate([z, w[:, 1]], axis=2)], axis=1)


def _pair_states(s):
    lead = s.shape[:-3]
    s = s.reshape(lead + (N_PAIR, 2, HEAD_DIM, HEAD_DIM))
    z = jnp.zeros_like(s[..., 0, :, :])
    top = jnp.concatenate([s[..., 0, :, :], z], axis=-1)
    bot = jnp.concatenate([z, s[..., 1, :, :]], axis=-1)
    return jnp.concatenate([top, bot], axis=-2)


def _unpair_states(s):
    lead = s.shape[:-3]
    a = s[..., :HEAD_DIM, :HEAD_DIM]
    b = s[..., HEAD_DIM:, HEAD_DIM:]
    return jnp.stack([a, b], axis=-3).reshape(lead + (H_RWKV, HEAD_DIM, HEAD_DIM))


def kernel(x_prompt, x_sample, state_rwkv, cache_na_k, cache_na_v, cache_gqa_k, cache_gqa_v, c, c_ctx,
           w_mod, b_mod, w_in, rwkv_conv, rwkv_w0, rwkv_w2, rwkv_a0, rwkv_a2, rwkv_g2, rwkv_k_k, rwkv_k_a,
           rwkv_r_k, rwkv_lnx_w, rwkv_lnx_b, na_rpb, gqa_q_norm, gqa_k_norm, w_out, ln1_w, ln1_b,
           w_ffn_in, w_ffn_out, ln2_w, ln2_b):
    x_ctx, x_lat = x_prompt.reshape(N_CTX, D_MODEL), x_sample.reshape(N_LAT, D_MODEL)
    cc = jnp.concatenate([c_ctx[None], c, jnp.zeros((MOD_ROWS - 1 - DEC_BATCH, D_MODEL), F32)], axis=0)
    mod_all = _modulation(cc, w_mod, b_mod)
    tb_all = _bias_tables(na_rpb)
    cos_tab, sin_tab = _rope_tables()
    rows = lambda a: a.reshape(DEPTH, 1, -1)
    w_in_bf, w_out_bf = w_in.astype(BF16), w_out.astype(BF16)
    w_ffn_in_bf, w_ffn_out_bf = w_ffn_in.astype(BF16), w_ffn_out.astype(BF16)
    qn = jnp.tile(rows(gqa_q_norm), (1, 1, H_GQA))
    kn = jnp.tile(rows(gqa_k_norm), (1, 1, H_GQA_KV))
    w2bd, a2bd = _block_diag2(rwkv_w2), _block_diag2(rwkv_a2)
    s0_lat = _pair_states(state_rwkv)
    kc_na = cache_na_k.reshape(DEC_BATCH, DEPTH, PAST_LEN, W_NA)
    vc_na = cache_na_v.reshape(DEC_BATCH, DEPTH, PAST_LEN, W_NA)
    kc_g = cache_gqa_k.reshape(DEC_BATCH, DEPTH, PAST_LEN, W_GQA_KV)
    vc_g = cache_gqa_v.reshape(DEC_BATCH, DEPTH, PAST_LEN, W_GQA_KV)
    caches = [jnp.zeros((BATCH, DEPTH, SEQ, w), F32) for w in (W_NA, W_NA, W_GQA_KV, W_GQA_KV)]
    s_fin = [jnp.zeros((BATCH, DEPTH, N_PAIR, LANES, LANES), F32) for _ in range(2)]

    for l in range(DEPTH):
        feat, naq, nak, nav, gq, gk, gv, *caches = _inproj(
            l, x_ctx, x_lat, mod_all, w_in_bf, qn, kn, cos_tab, sin_tab, caches)
        r, kap, v, lw, ah, kd, g, bonus = _rwkv_prep(
            l, feat, rwkv_conv, rows(rwkv_w0), w2bd, rows(rwkv_a0), a2bd, rwkv_g2,
            rows(rwkv_k_k), rows(rwkv_k_a), rows(rwkv_r_k))
        o_fwd, o_bwd, *s_fin = _rwkv_scan(l, r, kap, v, lw, ah, kd, s0_lat, s_fin)
        o_na_ctx, o_g_ctx = _ctx_attention(naq, nak, nav, gq, gk, gv)
        o_na_lat = _lat_na(l, naq, nak, nav, kc_na, vc_na, tb_all)
        o_g_lat = _lat_gqa(l, gq, gk, gv, kc_g, vc_g)
        x_ctx, x_lat = _mix_ffn(
            l, x_ctx, x_lat, mod_all, o_fwd, o_bwd, g, bonus, rows(rwkv_lnx_w), rows(rwkv_lnx_b),
            o_na_ctx, o_na_lat, o_g_ctx, o_g_lat, w_out_bf, rows(ln1_w), rows(ln1_b),
            w_ffn_in_bf, w_ffn_out_bf, rows(ln2_w), rows(ln2_b))
    y_prompt = x_ctx.reshape(BATCH, SEQ, D_MODEL)
    y_sample = x_lat.reshape(DEC_BATCH, DEC_SEQ, D_MODEL)
    new_state = jnp.stack([_unpair_states(s) for s in s_fin], axis=2)
    return (y_prompt, y_sample, new_state,
            caches[0].reshape(BATCH, DEPTH, SEQ, H_NA, HEAD_DIM), caches[1].reshape(BATCH, DEPTH, SEQ, H_NA, HEAD_DIM),
            caches[2].reshape(BATCH, DEPTH, SEQ, H_GQA_KV, HEAD_DIM),
            caches[3].reshape(BATCH, DEPTH, SEQ, H_GQA_KV, HEAD_DIM))
```

```python
import jax
import jax.numpy as jnp
from jax import lax
from jax.experimental import pallas as pl
from jax.experimental.pallas import tpu as pltpu

F32 = jnp.float32
BF16 = jnp.bfloat16
HIGHEST = lax.Precision.HIGHEST

D_MODEL = 1024
BATCH = 16
SEQ = 256
DEPTH = 4
DEC_BATCH = 2
DEC_SEQ = 1024
PAST_LEN = 512
GRID_W = 64
GRID_ROWS = DEC_SEQ // GRID_W
HEAD_DIM = 64
H_RWKV = 4
H_NA = 4
H_GQA = 8
H_GQA_KV = 2
W_RWKV = H_RWKV * HEAD_DIM
W_NA = H_NA * HEAD_DIM
W_GQA = H_GQA * HEAD_DIM
W_GQA_KV = H_GQA_KV * HEAD_DIM
LORA_W = 64
LORA_A = 64
LORA_G = 128
RWKV_IN = 3 * W_RWKV + 2 * LORA_W + 2 * LORA_A + LORA_G
NA_IN = 3 * W_NA
GQA_IN = W_GQA + 2 * W_GQA_KV
D_IN = RWKV_IN + NA_IN + GQA_IN
NA_ROWS = 8
NA_COLS = 16
ROPE_BASE = 10000.0
ROPE_FREQ = HEAD_DIM // 4
D_FF = ((8 * D_MODEL + 3 * 256 - 1) // (3 * 256)) * 256
DEEPNORM_ALPHA = (2 * DEPTH) ** 0.25
LN_EPS = 1e-5
RMS_EPS = 1e-6
GN_EPS = 64e-5
NEG_INF = -1e30
ATTN_SCALE = HEAD_DIM ** -0.5

LANES = 128
TM = 256
N_CTX = BATCH * SEQ
N_LAT = DEC_BATCH * DEC_SEQ
N_TOK = N_CTX + N_LAT
NBLK = N_TOK // TM
NCB = N_CTX // TM
LAT_BLKS = DEC_SEQ // TM
CHUNK = 64
N_CHUNK = TM // CHUNK
assert N_CHUNK == 4
TD = 512
SUB_ROWS = [slice(h * SEQ, (h + 1) * SEQ) for h in range(TD // SEQ)]
NBLK_D = N_TOK // TD
NCB_D = N_CTX // TD
LAT_BLKS_D = DEC_SEQ // TD
FF_CHUNK = D_FF
N_PAIR = W_RWKV // LANES
MOD_ROWS = 8
VMEM_LIMIT = 48 * 1024 * 1024
VMEM_LIMIT_FFN = 56 * 1024 * 1024
N_DR = 2 * NA_ROWS - 1
N_DC = 2 * NA_COLS - 1
N_BAND = NA_ROWS * GRID_W


def _cparams(n_grid, vmem_limit=VMEM_LIMIT):
    return pltpu.CompilerParams(dimension_semantics=("arbitrary",) * n_grid, vmem_limit_bytes=vmem_limit)


def _iota(shape, dim):
    return lax.broadcasted_iota(jnp.int32, shape, dim)


NN = (((1,), (0,)), ((), ()))
NT = (((1,), (1,)), ((), ()))
TN = (((0,), (0,)), ((), ()))


def _dg(a, b, dims=NN, precision=None):
    return lax.dot_general(a, b, dims, preferred_element_type=F32, precision=precision)


def _split(x):
    hi = x.astype(BF16)
    return hi, (x - hi.astype(F32)).astype(BF16)


def _dgs(a_s, b_s, dims=NN):
    if len(a_s) == 1:
        return _dg(a_s[0], b_s[0], dims)
    (ah, al), (bh, bl) = a_s, b_s
    ca, cb = dims[0][0][0], dims[0][1][0]
    return _dg(jnp.concatenate([ah, ah, al], axis=ca), jnp.concatenate([bh, bl, bh], axis=cb), dims)


def _dg3(a, b, dims=NN):
    return _dgs(_split(a), _split(b), dims)


def _sigmoid(x):
    return 1.0 / (1.0 + jnp.exp(-x))


def _softplus(x):
    return jnp.maximum(x, 0.0) + jnp.log(1.0 + jnp.exp(-jnp.abs(x)))


def _seg64_sum(x):
    rows, width = x.shape
    lo = _iota((rows, LANES), 1) < HEAD_DIM
    outs = []
    for c in range(width // LANES):
        blk = x[:, c * LANES:(c + 1) * LANES]
        s_lo = jnp.sum(jnp.where(lo, blk, 0.0), axis=-1, keepdims=True)
        s_hi = jnp.sum(jnp.where(lo, 0.0, blk), axis=-1, keepdims=True)
        outs.append(jnp.where(lo, s_lo, s_hi))
    return outs[0] if len(outs) == 1 else jnp.concatenate(outs, axis=1)


def _layer_norm(x, w, b):
    mu = jnp.mean(x, axis=-1, keepdims=True)
    xc = x - mu
    var = jnp.mean(xc * xc, axis=-1, keepdims=True)
    return xc * lax.rsqrt(var + LN_EPS) * w + b


def _mod_row(i):
    return jnp.where(i < NCB_D, 0, 1 + (i - NCB_D) // LAT_BLKS_D)


def _layer_spec(l, tail, n_grid, single=False):
    idx = lambda *g: (l,) + (0,) * len(tail)
    del n_grid
    if single:
        return pl.BlockSpec((None,) + tuple(tail), idx, pipeline_mode=pl.Buffered(1))
    return pl.BlockSpec((None,) + tuple(tail), idx)


ANY_SPEC = pl.BlockSpec(memory_space=pl.ANY)


def _ctx_spec(w):
    return pl.BlockSpec((TD, w), lambda i: (jnp.minimum(i, NCB_D - 1), 0))


def _lat_spec(w):
    return pl.BlockSpec((TD, w), lambda i: (jnp.maximum(i - NCB_D, 0), 0))


def _pick(i, ctx_ref, lat_ref, rows=slice(None)):
    return jnp.where(i < NCB_D, ctx_ref[rows, :], lat_ref[rows, :])


def _mod_kernel(c_ref, w_ref, b_ref, o_ref):
    c = c_ref[...]
    s = c * _sigmoid(c)
    o_ref[...] = _dg(s, w_ref[...], NN, HIGHEST) + b_ref[...]


def _modulation(cc, w_mod, b_mod):
    return pl.pallas_call(
        _mod_kernel,
        grid=(DEPTH, 6),
        in_specs=[
            pl.BlockSpec((MOD_ROWS, D_MODEL), lambda l, j: (0, 0)),
            pl.BlockSpec((None, D_MODEL, D_MODEL), lambda l, j: (l, 0, j)),
            pl.BlockSpec((None, 1, D_MODEL), lambda l, j: (l, 0, j)),
        ],
        out_specs=pl.BlockSpec((None, MOD_ROWS, D_MODEL), lambda l, j: (l, 0, j)),
        out_shape=jax.ShapeDtypeStruct((DEPTH, MOD_ROWS, 6 * D_MODEL), F32),
        compiler_params=_cparams(2),
        name="modulation",
    )(cc, w_mod, b_mod.reshape(DEPTH, 1, 6 * D_MODEL))


def _bias_kernel(rpb_ref, o_ref):
    q = _iota((GRID_W, LANES), 0)
    x = _iota((GRID_W, LANES), 1)
    c = x % GRID_W
    right = x >= GRID_W
    dc = jnp.clip(c - q, 1 - NA_COLS, NA_COLS - 1) + NA_COLS - 1
    c0 = jnp.clip(q - NA_COLS // 2, 0, GRID_W - NA_COLS)
    in_win = (c >= c0) & (c < c0 + NA_COLS)

    def body(t, carry):
        h = t // (N_DR - 1)
        dr = t % (N_DR - 1)
        rows = [jnp.broadcast_to(rpb_ref[pl.ds(h * N_DR + dr + k, 1), :], (GRID_W, LANES)) for k in range(2)]
        lo, hi = (jnp.take_along_axis(r, dc, axis=1) for r in rows)
        o_ref[h, dr] = jnp.where(in_win, jnp.where(right, hi, lo), NEG_INF)
        return carry

    lax.fori_loop(0, H_NA * (N_DR - 1), body, 0, unroll=N_DR - 1)


RPB_ROWS = -(-H_NA * N_DR // 8) * 8


def _bias_tables(na_rpb):
    rpb = jnp.pad(na_rpb.reshape(DEPTH, H_NA * N_DR, N_DC), ((0, 0), (0, RPB_ROWS - H_NA * N_DR), (0, LANES - N_DC)))
    return pl.pallas_call(
        _bias_kernel,
        grid=(DEPTH,),
        in_specs=[pl.BlockSpec((None, RPB_ROWS, LANES), lambda l: (l, 0, 0))],
        out_specs=pl.BlockSpec((None, H_NA, N_DR - 1, GRID_W, LANES), lambda l: (l, 0, 0, 0, 0)),
        out_shape=jax.ShapeDtypeStruct((DEPTH, H_NA, N_DR - 1, GRID_W, LANES), F32),
        compiler_params=_cparams(1),
        name="na_bias_tables",
    )(rpb)


def _rope(x, cos, sin):
    k = x.shape[1] // LANES
    cosf = cos if k == 1 else jnp.concatenate([cos] * k, axis=1)
    sinf = sin if k == 1 else jnp.concatenate([sin] * k, axis=1)
    first = (_iota(x.shape, 1) % (2 * ROPE_FREQ)) < ROPE_FREQ
    partner = jnp.where(first, pltpu.roll(x, x.shape[1] - ROPE_FREQ, axis=1), pltpu.roll(x, ROPE_FREQ, axis=1))
    return x * cosf + partner * sinf


def _inproj_kernel(xc_ref, xl_ref, mod_ref, w_ref, qn_ref, kn_ref, cos_ref, sin_ref, c0_ref, c1_ref, c2_ref, c3_ref,
                   feat_ref, naq_ref, nak_ref, nav_ref, gq_ref, gk_ref, gv_ref,
                   cnak_ref, cnav_ref, cgk_ref, cgv_ref):
    del c0_ref, c1_ref, c2_ref, c3_ref
    i = pl.program_id(0)
    row = _mod_row(i)
    shift1 = mod_ref[pl.ds(row, 1), 0:D_MODEL]
    scale1 = mod_ref[pl.ds(row, 1), D_MODEL:2 * D_MODEL]
    xm = [(_pick(i, xc_ref, xl_ref, rs) * (1.0 + scale1) + shift1).astype(BF16) for rs in SUB_ROWS]
    projs = [_dg(x, w_ref[...]) for x in xm]
    o_na = RWKV_IN
    o_g = RWKV_IN + NA_IN
    new_kv = []
    for rs, proj in zip(SUB_ROWS, projs):
        feat_ref[rs, :] = proj[:, :RWKV_IN]
        naq_ref[rs, :] = proj[:, o_na:o_na + W_NA] * ATTN_SCALE
        nak = proj[:, o_na + W_NA:o_na + 2 * W_NA]
        nav = proj[:, o_na + 2 * W_NA:o_na + 3 * W_NA]
        q = proj[:, o_g:o_g + W_GQA]
        k = proj[:, o_g + W_GQA:o_g + W_GQA + W_GQA_KV]
        gv = proj[:, o_g + W_GQA + W_GQA_KV:]
        cos = cos_ref[rs, :]
        sin = sin_ref[rs, :]
        q = q * lax.rsqrt(_seg64_sum(q * q) * (1.0 / HEAD_DIM) + RMS_EPS) * qn_ref[...]
        k = k * lax.rsqrt(_seg64_sum(k * k) * (1.0 / HEAD_DIM) + RMS_EPS) * kn_ref[...]
        gk = _rope(k, cos, sin)
        gq_ref[rs, :] = _rope(q, cos, sin) * ATTN_SCALE
        nak_ref[rs, :] = nak
        nav_ref[rs, :] = nav
        gk_ref[rs, :] = gk
        gv_ref[rs, :] = gv
        new_kv.append((nak, nav, gk, gv))

    @pl.when(i < NCB_D)
    def _():
        for h, vals in enumerate(new_kv):
            for ref, val in zip((cnak_ref, cnav_ref, cgk_ref, cgv_ref), vals):
                ref[h] = val


def _inproj(l, x_ctx, x_lat, mod_all, w_in_bf, qn, kn, cos_tab, sin_tab, caches):
    tab_idx = lambda i: (jnp.where(i < NCB_D, 0, 1 + (i - NCB_D) % LAT_BLKS_D), 0)
    widths = (RWKV_IN, W_NA, W_NA, W_NA, W_GQA, W_GQA_KV, W_GQA_KV)
    cache_w = (W_NA, W_NA, W_GQA_KV, W_GQA_KV)
    cache_spec = lambda w: pl.BlockSpec((TD // SEQ, None, SEQ, w), lambda i: (jnp.minimum(i, NCB_D - 1), l, 0, 0))
    n_in = 8
    return pl.pallas_call(
        _inproj_kernel,
        grid=(NBLK_D,),
        in_specs=[
            _ctx_spec(D_MODEL), _lat_spec(D_MODEL),
            _layer_spec(l, (MOD_ROWS, 6 * D_MODEL), 1),
            _layer_spec(l, (D_MODEL, D_IN), 1),
            _layer_spec(l, (1, W_GQA), 1),
            _layer_spec(l, (1, W_GQA_KV), 1),
            pl.BlockSpec((TD, LANES), tab_idx),
            pl.BlockSpec((TD, LANES), tab_idx),
        ] + [ANY_SPEC] * 4,
        out_specs=[pl.BlockSpec((TD, w), lambda i: (i, 0)) for w in widths] + [cache_spec(w) for w in cache_w],
        out_shape=[jax.ShapeDtypeStruct((N_TOK, w), F32) for w in widths]
        + [jax.ShapeDtypeStruct((BATCH, DEPTH, SEQ, w), F32) for w in cache_w],
        input_output_aliases={n_in + j: len(widths) + j for j in range(4)},
        compiler_params=_cparams(1),
        name="inproj",
    )(x_ctx, x_lat, mod_all, w_in_bf, qn, kn, cos_tab, sin_tab, *caches)


def _rwkv_prep_kernel(f_ref, fp_ref, fn_ref, conv_ref, w0_ref, w2_ref, a0_ref, a2_ref, g2_ref,
                      kk_ref, ka_ref, rk_ref,
                      r_ref, kap_ref, v_ref, lw_ref, ah_ref, kd_ref, g_ref, bonus_ref):
    i = pl.program_id(0)
    pos = (i - NCB) % LAT_BLKS
    lat = i >= NCB
    has_prev = jnp.where(lat & (pos != 0), 1.0, 0.0)
    has_next = jnp.where(lat & (pos != LAT_BLKS - 1), 1.0, 0.0)
    x = f_ref[...]
    rows = _iota(x.shape, 0)
    x_prev = jnp.where(rows == 0, fp_ref[7:8, :] * has_prev, pltpu.roll(x, 1, axis=0))
    x_next = jnp.where(rows == TM - 1, fn_ref[0:1, :] * has_next, pltpu.roll(x, TM - 1, axis=0))
    f = x_prev * conv_ref[0:1, :] + x * conv_ref[1:2, :] + x_next * conv_ref[2:3, :]
    o1, o2, o3 = W_RWKV, 2 * W_RWKV, 3 * W_RWKV
    o4 = o3 + 2 * LORA_W
    o5 = o4 + 2 * LORA_A
    r, k, v = f[:, :o1], f[:, o1:o2], f[:, o2:o3]
    wd, ad, gd = f[:, o3:o4], f[:, o4:o5], f[:, o5:]
    log_w = -_softplus(-(w0_ref[...] + _dg3(jnp.tanh(wd), w2_ref[...]))) - 0.5
    a = _sigmoid(a0_ref[...] + _dg3(ad, a2_ref[...]))
    g_ref[...] = _dg3(_sigmoid(gd), g2_ref[...])
    kk = k * kk_ref[...]
    kap = kk / jnp.maximum(jnp.sqrt(_seg64_sum(kk * kk)), 1e-12)
    ka = ka_ref[...]
    kd_sum = jnp.zeros_like(k)
    for d in range(2):
        a_d = a[:, d * W_RWKV:(d + 1) * W_RWKV]
        kd = k * (1.0 + (a_d - 1.0) * ka)
        kd_sum = kd_sum + kd
        lw_ref[d] = -jnp.exp(log_w[:, d * W_RWKV:(d + 1) * W_RWKV])
        ah_ref[d] = a_d * kap
        kd_ref[d] = kd
    r_ref[...] = r
    kap_ref[...] = kap
    v_ref[...] = v
    bonus_ref[...] = _seg64_sum(r * kd_sum * rk_ref[...]) * v


def _rwkv_prep(l, feat, conv, w0, w2bd, a0, a2bd, g2, k_k, k_a, r_k):
    sub = TM // 8
    tok = pl.BlockSpec((TM, W_RWKV), lambda i: (i, 0))
    tok2 = pl.BlockSpec((2, TM, W_RWKV), lambda i: (0, i, 0))
    return pl.pallas_call(
        _rwkv_prep_kernel,
        grid=(NBLK,),
        in_specs=[
            pl.BlockSpec((TM, RWKV_IN), lambda i: (i, 0)),
            pl.BlockSpec((8, RWKV_IN), lambda i: (jnp.maximum(i * sub - 1, 0), 0)),
            pl.BlockSpec((8, RWKV_IN), lambda i: (jnp.minimum((i + 1) * sub, NBLK * sub - 1), 0)),
            _layer_spec(l, (3, RWKV_IN), 1), _layer_spec(l, (1, 2 * W_RWKV), 1),
            _layer_spec(l, (2 * LORA_W, 2 * W_RWKV), 1), _layer_spec(l, (1, 2 * W_RWKV), 1),
            _layer_spec(l, (2 * LORA_A, 2 * W_RWKV), 1), _layer_spec(l, (LORA_G, W_RWKV), 1),
            _layer_spec(l, (1, W_RWKV), 1), _layer_spec(l, (1, W_RWKV), 1), _layer_spec(l, (1, W_RWKV), 1),
        ],
        out_specs=[tok, tok, tok, tok2, tok2, tok2, tok, tok],
        out_shape=[jax.ShapeDtypeStruct((N_TOK, W_RWKV), F32)] * 3
        + [jax.ShapeDtypeStruct((2, N_TOK, W_RWKV), F32)] * 3
        + [jax.ShapeDtypeStruct((N_TOK, W_RWKV), F32)] * 2,
        compiler_params=_cparams(1),
        name="rwkv_prep",
    )(feat, feat, feat, conv, w0, w2bd, a0, a2bd, g2, k_k, k_a, r_k)


def _scan_kernel(r0_ref, kap0_ref, v0_ref, r1_ref, kap1_ref, v1_ref,
                 lw0_ref, ah0_ref, kd0_ref, lw1_ref, ah1_ref, kd1_ref, s00_ref, s01_ref, sf0_in_ref, sf1_in_ref,
                 o0_ref, o1_ref, sf0_ref, sf1_ref,
                 s_scr, rt_scr, kt_scr, kdt_scr, at_scr, cum_scr, m_scr, n_scr, q_scr, oo_scr):
    del sf0_in_ref, sf1_in_ref
    j = pl.program_id(0)
    par = j % 2
    jb = jnp.maximum(j - 1, 0)
    upd_blk = (jb, NBLK - 1 - jb)
    first_pos = (0, LAT_BLKS - 1)
    r_refs, kap_refs, v_refs = (r0_ref, r1_ref), (kap0_ref, kap1_ref), (v0_ref, v1_ref)
    lw_refs, ah_refs, kd_refs = (lw0_ref, lw1_ref), (ah0_ref, ah1_ref), (kd0_ref, kd1_ref)
    s0_refs, o_refs, sf_refs = (s00_ref, s01_ref), (o0_ref, o1_ref), (sf0_ref, sf1_ref)

    @pl.when(j == 0)
    def _():
        for ref in (s_scr, m_scr, n_scr, q_scr, oo_scr):
            ref[...] = jnp.zeros_like(ref)

    for dn in range(2):
        blk = upd_blk[dn]

        @pl.when((j > 0) & (blk < NCB))
        def _():
            s_scr[dn] = jnp.zeros(s_scr.shape[1:], F32)

        @pl.when((j > 0) & (blk >= NCB) & ((blk - NCB) % LAT_BLKS == first_pos[dn]))
        def _():
            s_scr[dn] = s0_refs[dn][...]

    def update(c):
        for dn in range(2):
            ce = c if dn == 0 else N_CHUNK - 1 - c
            rows = slice(ce * CHUNK, (ce + 1) * CHUNK)
            for p in range(N_PAIR):
                s2 = jnp.concatenate(_split(s_scr[dn, p]), axis=1)
                q = q_scr[1 - par, dn, ce, p]
                m = m_scr[1 - par, dn, ce, p]
                o_refs[dn][rows, p * LANES:(p + 1) * LANES] = (
                    _dg(jnp.concatenate([q, q], axis=1), s2, NT) + oo_scr[1 - par, dn, ce, p])
                s_scr[dn, p] = _dg(s2, jnp.concatenate([m, m], axis=0)) + n_scr[1 - par, dn, ce, p]

    update(0)
    rr = _iota((TM, TM), 0)
    cc = _iota((TM, TM), 1)
    same_chunk = (rr // CHUNK) == (cc // CHUNK)
    for dn in range(2):
        order = (rr >= cc) if dn == 0 else (rr <= cc)
        tri = jnp.where(same_chunk & order, 1.0, 0.0).astype(BF16)
        lw = lw_refs[dn][...]
        lw_hi = lw.astype(BF16)
        lw_r = lw - lw_hi.astype(F32)
        lw_mid = lw_r.astype(BF16)
        lw_lo = (lw_r - lw_mid.astype(F32)).astype(BF16)
        cum = _dg(jnp.concatenate([tri, tri, tri], axis=1), jnp.concatenate([lw_hi, lw_mid, lw_lo], axis=0))
        e_neg = jnp.exp(-cum)
        rt_scr[dn] = r_refs[dn][...] * jnp.exp(cum)
        kt_scr[dn] = kap_refs[dn][...] * jnp.exp(cum - lw)
        kdt_scr[dn] = kd_refs[dn][...] * e_neg
        at_scr[dn] = ah_refs[dn][...] * e_neg
        cum_scr[dn] = cum

    row = _iota((CHUNK, LANES), 0)
    col = _iota((CHUNK, LANES), 1) % CHUNK
    left = _iota((CHUNK, LANES), 1) < CHUNK
    incl = (row >= col, row <= col)
    strict = (row > col, row < col)
    eye = jnp.where(row == col, 1.0, 0.0)
    r2 = _iota((LANES, LANES), 0)
    c2 = _iota((LANES, LANES), 1)
    bd_mask = (r2 // CHUNK) == (c2 // CHUNK)
    eye2 = jnp.where(r2 == c2, 1.0, 0.0)

    zero_bf = jnp.zeros((CHUNK, LANES), BF16)
    split = lambda x: (x.astype(BF16),)

    def bd(xs):
        return tuple(jnp.concatenate([jnp.where(left, y, zero_bf), jnp.where(left, zero_bf, y)], axis=0)
                     for y in xs)

    def cat(xs, ys, axis):
        return tuple(jnp.concatenate([x, y], axis=axis) for x, y in zip(xs, ys))

    def diag2(x_s, y_s):
        return tuple(jnp.concatenate([jnp.concatenate([x, jnp.zeros_like(x)], axis=1),
                                      jnp.concatenate([jnp.zeros_like(y), y], axis=1)], axis=0)
                     for x, y in zip(x_s, y_s))

    def halves(x):
        return [x[:, 0:LANES], x[:, LANES:]]

    def paired(f, *lists):
        outs = []
        for args in zip(*[lst[k::2] for lst in lists for k in range(2)]):
            outs += halves(f(*args))
        return outs

    def mm(a_list, b_list):
        return paired(lambda a0, a1, b0, b1: _dgs(cat(a0, a1, 1), diag2(bd(b0), bd(b1))), a_list, b_list)

    units = [(dn, c, p) for c in range(N_CHUNK) for dn in range(2) for p in range(N_PAIR)]
    each = lambda f, *lists: [f(*args) for args in zip(*lists)]

    def load(ref):
        return [ref[dn, c * CHUNK:(c + 1) * CHUNK, p * LANES:(p + 1) * LANES] for dn, c, p in units]

    rt = load(rt_scr)
    v = [v_refs[dn][c * CHUNK:(c + 1) * CHUNK, p * LANES:(p + 1) * LANES] for dn, c, p in units]
    kt_s, rt_s, kdt_s, at_s, v_s = (each(split, x) for x in (load(kt_scr), rt, load(kdt_scr), load(at_scr), v))
    gam = [jnp.exp(cum_scr[dn, (c + 1) * CHUNK - 1:(c + 1) * CHUNK, p * LANES:(p + 1) * LANES] if dn == 0 else
                   cum_scr[dn, c * CHUNK:c * CHUNK + 1, p * LANES:(p + 1) * LANES]) for dn, c, p in units]
    gram = each(lambda k, r, a, kd: _dgs(cat(k, r, 0), cat(bd(a), bd(kd), 0), NT), kt_s, rt_s, at_s, kdt_s)
    dirs = [u[0] for u in units]
    la = [jnp.where(strict[dn], g[0:CHUNK, 0:LANES], 0.0) for dn, g in zip(dirs, gram)]
    lk_s = [split(jnp.where(strict[dn], g[0:CHUNK, LANES:], 0.0)) for dn, g in zip(dirs, gram)]
    ra_s = [split(jnp.where(incl[dn], g[CHUNK:, 0:LANES], 0.0)) for dn, g in zip(dirs, gram)]
    rk_s = [split(jnp.where(incl[dn], g[CHUNK:, LANES:], 0.0)) for dn, g in zip(dirs, gram)]
    update(1)
    lrv = mm(each(lambda lk, rk: cat(lk, rk, 0), lk_s, rk_s), v_s)
    b = 8
    l8 = [jnp.where((row // b) == (col // b), x, 0.0) for x in la]
    l8_s = each(split, l8)
    l8_2 = mm(l8_s, l8_s)
    update(2)
    l8_2s = each(split, l8_2)
    l8_4 = mm(l8_2s, l8_2s)
    p1 = mm([split(eye - x) for x in l8], [split(eye + y) for y in l8_2])
    update(3)
    t = mm(each(split, p1), [split(eye + y) for y in l8_4])
    while b < CHUNK:
        offd = ((row // (2 * b)) == (col // (2 * b))) & ((row // b) != (col // b))
        t_s = each(split, t)
        x = mm(t_s, [split(jnp.where(offd, y, 0.0)) for y in la])
        t = each(lambda tt, z: tt - z, t, mm(each(split, x), t_s))
        b *= 2
    tx = each(lambda tt, k, y: _dgs(split(tt), cat(bd(k), bd(split(y[0:CHUNK])), 1)), t, kt_s, lrv)
    khat_s = [split(y[:, 0:LANES]) for y in tx]
    w1_s = [split(y[:, LANES:]) for y in tx]
    rx = each(lambda r, k, w: _dgs(r, cat(bd(k), bd(w), 1)), ra_s, khat_s, w1_s)
    neg = lambda xs: tuple(-y for y in xs)
    mk = paired(lambda k0, k1, a0, a1: _dgs(cat(k0, k1, 0), diag2(a0, a1), TN), khat_s, at_s)
    nk = paired(lambda v0, v1, w0, w1, kd0, kd1, a0, a1: _dgs(
        cat(cat(v0, w0, 0), cat(v1, w1, 0), 0), diag2(cat(kd0, neg(a0), 0), cat(kd1, neg(a1), 0)), TN),
        v_s, w1_s, kdt_s, at_s)
    for i, (dn, c, p) in enumerate(units):
        q_scr[par, dn, c, p] = (rt[i] - rx[i][:, 0:LANES]).astype(BF16)
        oo_scr[par, dn, c, p] = lrv[i][CHUNK:] - rx[i][:, LANES:]
        m_scr[par, dn, c, p] = ((eye2 - jnp.where(bd_mask, mk[i], 0.0)) * gam[i]).astype(BF16)
        n_scr[par, dn, c, p] = jnp.where(bd_mask, nk[i], 0.0) * gam[i]

    for dn in range(2):
        @pl.when((j > 0) & (upd_blk[dn] < NCB))
        def _():
            sf_refs[dn][...] = s_scr[dn]


def _rwkv_scan(l, r, kap, v, lw, ah, kd, s0_lat, s_fin):
    def stage_blk(dn, j):
        jj = jnp.minimum(j, NBLK - 1)
        return jj if dn == 0 else NBLK - 1 - jj

    def update_blk(dn, j):
        jj = jnp.maximum(j - 1, 0)
        return jj if dn == 0 else NBLK - 1 - jj

    pair = (N_PAIR, LANES, LANES)
    tok = lambda dn: pl.BlockSpec((TM, W_RWKV), lambda j: (stage_blk(dn, j), 0))
    tok2 = lambda dn: pl.BlockSpec((None, TM, W_RWKV), lambda j: (dn, stage_blk(dn, j), 0))
    s0_spec = lambda dn: pl.BlockSpec(
        (None, None, None) + pair, lambda j: (jnp.maximum(update_blk(dn, j) - NCB, 0) // LAT_BLKS, l, dn, 0, 0, 0))
    sf_spec = lambda dn: pl.BlockSpec(
        (None, None) + pair, lambda j: (jnp.minimum(update_blk(dn, j), NCB - 1), l, 0, 0, 0))
    o_spec = lambda dn: pl.BlockSpec((TM, W_RWKV), lambda j: (update_blk(dn, j), 0))
    dir_scr = lambda *shape, dtype=F32: pltpu.VMEM((2,) + shape, dtype)
    stage_scr = (2, N_CHUNK, N_PAIR)
    return pl.pallas_call(
        _scan_kernel,
        grid=(NBLK + 1,),
        in_specs=[tok(0), tok(0), tok(0), tok(1), tok(1), tok(1),
                  tok2(0), tok2(0), tok2(0), tok2(1), tok2(1), tok2(1),
                  s0_spec(0), s0_spec(1), ANY_SPEC, ANY_SPEC],
        out_specs=[o_spec(0), o_spec(1), sf_spec(0), sf_spec(1)],
        out_shape=[jax.ShapeDtypeStruct((N_TOK, W_RWKV), F32)] * 2
        + [jax.ShapeDtypeStruct((BATCH, DEPTH) + pair, F32)] * 2,
        scratch_shapes=[dir_scr(*pair)] + [dir_scr(TM, W_RWKV)] * 5
        + [dir_scr(*stage_scr, LANES, LANES, dtype=BF16), dir_scr(*stage_scr, LANES, LANES),
           dir_scr(*stage_scr, CHUNK, LANES, dtype=BF16), dir_scr(*stage_scr, CHUNK, LANES)],
        input_output_aliases={14: 2, 15: 3},
        compiler_params=_cparams(1),
        name="rwkv_scan",
    )(r, kap, v, r, kap, v, lw, ah, kd, lw, ah, kd, s0_lat, s0_lat, s_fin[0], s_fin[1])


def _attend(groups):
    lhs = []
    for q_cols, _, _, _ in groups:
        left = _iota(q_cols[0].shape, 1) < HEAD_DIM
        parts = []
        for qc in q_cols:
            parts += [jnp.where(left, qc, 0.0), jnp.where(left, 0.0, qc)]
        lhs.append(jnp.concatenate(parts, axis=0).astype(BF16))
    s = [_dg(x, g[1], NT) for x, g in zip(lhs, groups)]
    s = [x if g[3] is None else x + g[3] for x, g in zip(s, groups)]
    p = [jnp.exp(x - jnp.max(x, axis=-1, keepdims=True)) for x in s]
    inv = [1.0 / jnp.sum(x, axis=-1, keepdims=True) for x in p]
    o = [_dg(x.astype(BF16), g[2]) * y for x, y, g in zip(p, inv, groups)]
    outs = []
    for x, (q_cols, _, _, _) in zip(o, groups):
        rows = q_cols[0].shape[0]
        left = _iota(q_cols[0].shape, 1) < HEAD_DIM
        outs.append([jnp.where(left, x[2 * j * rows:(2 * j + 1) * rows], x[(2 * j + 1) * rows:(2 * j + 2) * rows])
                     for j in range(len(q_cols))])
    return outs


def _cols(x):
    return [x[:, c * LANES:(c + 1) * LANES] for c in range(x.shape[1] // LANES)]


def _gqa_groups(q, k, v):
    left = _iota(k.shape, 1) < HEAD_DIM
    k_sw = pltpu.roll(k, HEAD_DIM, axis=1)
    v_sw = pltpu.roll(v, HEAD_DIM, axis=1)
    q_cols = _cols(q)
    groups = []
    for g in range(H_GQA_KV):
        k2 = jnp.where(left, k, k_sw) if g == 0 else jnp.where(left, k_sw, k)
        v2 = jnp.where(left, v, v_sw) if g == 0 else jnp.where(left, v_sw, v)
        groups.append((q_cols[2 * g:2 * g + 2], k2.astype(BF16), v2.astype(BF16), None))
    return groups


def _ctx_attn_kernel(naq_ref, nak_ref, nav_ref, gq_ref, gk_ref, gv_ref, ona_ref, og_ref):
    k_cols = _cols(nak_ref[...].astype(BF16))
    v_cols = _cols(nav_ref[...].astype(BF16))
    na_groups = [([qc], kc, vc, None) for qc, kc, vc in zip(_cols(naq_ref[...]), k_cols, v_cols)]
    outs = _attend(na_groups + _gqa_groups(gq_ref[...], gk_ref[...], gv_ref[...]))
    n_na = len(na_groups)
    ona_ref[...] = jnp.concatenate([o[0] for o in outs[:n_na]], axis=1)
    og_ref[...] = jnp.concatenate([c for o in outs[n_na:] for c in o], axis=1)


def _ctx_attention(naq, nak, nav, gq, gk, gv):
    spec = lambda w: pl.BlockSpec((SEQ, w), lambda b: (b, 0))
    return pl.pallas_call(
        _ctx_attn_kernel,
        grid=(BATCH,),
        in_specs=[spec(W_NA), spec(W_NA), spec(W_NA), spec(W_GQA), spec(W_GQA_KV), spec(W_GQA_KV)],
        out_specs=[spec(W_NA), spec(W_GQA)],
        out_shape=[jax.ShapeDtypeStruct((N_CTX, W_NA), F32), jax.ShapeDtypeStruct((N_CTX, W_GQA), F32)],
        compiler_params=_cparams(1),
        name="ctx_attention",
    )(naq, nak, nav, gq, gk, gv)


NA_STEP_ROWS = 2


def _lat_na_kernel(q_ref, k_ref, v_ref, kc_ref, vc_ref, tb_ref, o_ref):
    q = q_ref[...]
    groups = []
    for rr in range(NA_STEP_ROWS):
        r = pl.program_id(1) * NA_STEP_ROWS + rr
        r0 = jnp.clip(r - NA_ROWS // 2, 0, GRID_ROWS - NA_ROWS)
        band = pl.ds(pl.multiple_of(r0 * GRID_W, GRID_W), N_BAND)
        dr0 = r0 - r + NA_ROWS - 1
        for c, qc in enumerate(_cols(q[rr * GRID_W:(rr + 1) * GRID_W])):
            cols = slice(c * LANES, (c + 1) * LANES)
            k2 = jnp.concatenate([k_ref[band, cols], kc_ref[:, cols]], axis=0).astype(BF16)
            v2 = jnp.concatenate([v_ref[band, cols], vc_ref[:, cols]], axis=0).astype(BF16)
            bias = jnp.concatenate(
                [jnp.concatenate([tb_ref[2 * c + half, dr0 + jj] for jj in range(0, NA_ROWS, 2)], axis=1)
                 for half in range(2)], axis=0)
            bias = jnp.concatenate([bias, jnp.zeros((2 * GRID_W, PAST_LEN), F32)], axis=1)
            groups.append(([qc], k2, v2, bias))
    outs = [o[0] for o in _attend(groups)]
    n_col = W_NA // LANES
    o_ref[...] = jnp.concatenate(
        [jnp.concatenate(outs[rr * n_col:(rr + 1) * n_col], axis=1) for rr in range(NA_STEP_ROWS)], axis=0)


def _lat_na(l, naq, nak, nav, kc, vc, tb):
    rows = NA_STEP_ROWS * GRID_W
    steps = GRID_ROWS // NA_STEP_ROWS
    seq_blk0 = N_CTX // DEC_SEQ
    seq = pl.BlockSpec((DEC_SEQ, W_NA), lambda b, r: (seq_blk0 + b, 0))
    cache = pl.BlockSpec((None, None, PAST_LEN, W_NA), lambda b, r: (b, l, 0, 0))
    return pl.pallas_call(
        _lat_na_kernel,
        grid=(DEC_BATCH, steps),
        in_specs=[pl.BlockSpec((rows, W_NA), lambda b, r: (N_CTX // rows + b * steps + r, 0)),
                  seq, seq, cache, cache, _layer_spec(l, (H_NA, N_DR - 1, GRID_W, LANES), 2)],
        out_specs=pl.BlockSpec((rows, W_NA), lambda b, r: (b * steps + r, 0)),
        out_shape=jax.ShapeDtypeStruct((N_LAT, W_NA), F32),
        compiler_params=_cparams(2),
        name="latent_na",
    )(naq, nak, nav, kc, vc, tb)


def _lat_gqa_kernel(q_ref, k_ref, v_ref, kc_ref, vc_ref, o_ref):
    k = jnp.concatenate([kc_ref[...], k_ref[...]], axis=0)
    v = jnp.concatenate([vc_ref[...], v_ref[...]], axis=0)
    o_ref[...] = jnp.concatenate([c for o in _attend(_gqa_groups(q_ref[...], k, v)) for c in o], axis=1)


def _lat_gqa(l, gq, gk, gv, kc, vc):
    seq_blk0 = N_CTX // DEC_SEQ
    seq = pl.BlockSpec((DEC_SEQ, W_GQA_KV), lambda b, i: (seq_blk0 + b, 0))
    cache = pl.BlockSpec((None, None, PAST_LEN, W_GQA_KV), lambda b, i: (b, l, 0, 0))
    return pl.pallas_call(
        _lat_gqa_kernel,
        grid=(DEC_BATCH, LAT_BLKS),
        in_specs=[pl.BlockSpec((TM, W_GQA), lambda b, i: (NCB + b * LAT_BLKS + i, 0)), seq, seq, cache, cache],
        out_specs=pl.BlockSpec((TM, W_GQA), lambda b, i: (b * LAT_BLKS + i, 0)),
        out_shape=jax.ShapeDtypeStruct((N_LAT, W_GQA), F32),
        compiler_params=_cparams(2),
        name="latent_gqa",
    )(gq, gk, gv, kc, vc)


def _mix_ffn_kernel(xc_ref, xl_ref, mod_ref, of_ref, ob_ref, g_ref, bonus_ref, lnxw_ref, lnxb_ref,
                    onac_ref, onal_ref, ogc_ref, ogl_ref,
                    wout_ref, ln1w_ref, ln1b_ref, wfi_ref, wfo_ref, ln2w_ref, ln2b_ref, yc_ref, yl_ref):
    i = pl.program_id(0)
    row = _mod_row(i)
    mod = lambda n: mod_ref[pl.ds(row, 1), n * D_MODEL:(n + 1) * D_MODEL]
    each = lambda f, *lists: [f(*args) for args in zip(*lists)]

    def rwkv_out(rs):
        o = of_ref[rs, :] + ob_ref[rs, :]
        mu = _seg64_sum(o) * (1.0 / HEAD_DIM)
        oc = o - mu
        var = _seg64_sum(oc * oc) * (1.0 / HEAD_DIM)
        o_rwkv = (oc * lax.rsqrt(var + GN_EPS) * lnxw_ref[...] + lnxb_ref[...] + bonus_ref[rs, :]) * g_ref[rs, :]
        return jnp.concatenate([o_rwkv, _pick(i, onac_ref, onal_ref, rs), _pick(i, ogc_ref, ogl_ref, rs)],
                               axis=1).astype(BF16)

    mix = [_dg(rwkv_out(rs), wout_ref[...]) for rs in SUB_ROWS]
    x1 = [_layer_norm(DEEPNORM_ALPHA * _pick(i, xc_ref, xl_ref, rs) + mod(2) * m, ln1w_ref[...], ln1b_ref[...])
          for rs, m in zip(SUB_ROWS, mix)]
    x_in = [(x * (1.0 + mod(4)) + mod(3)).astype(BF16) for x in x1]
    ffn = [jnp.zeros_like(x) for x in x1]
    for lo in range(0, D_FF, FF_CHUNK):
        gate = [_dg(x, wfi_ref[:, lo:lo + FF_CHUNK]) for x in x_in]
        up = [_dg(x, wfi_ref[:, D_FF + lo:D_FF + lo + FF_CHUNK]) for x in x_in]
        act = each(lambda gt, u: (gt * _sigmoid(gt) * u).astype(BF16), gate, up)
        ffn = each(lambda f, a: f + _dg(a, wfo_ref[lo:lo + FF_CHUNK, :]), ffn, act)
    y = each(lambda x, f: _layer_norm(DEEPNORM_ALPHA * x + mod(5) * f, ln2w_ref[...], ln2b_ref[...]), x1, ffn)

    @pl.when(i < NCB_D)
    def _():
        for rs, yy in zip(SUB_ROWS, y):
            yc_ref[rs, :] = yy

    @pl.when(i >= NCB_D)
    def _():
        for rs, yy in zip(SUB_ROWS, y):
            yl_ref[rs, :] = yy


def _mix_ffn(l, x_ctx, x_lat, mod_all, o_fwd, o_bwd, g, bonus, lnx_w, lnx_b, o_na_ctx, o_na_lat, o_g_ctx, o_g_lat,
             w_out_bf, ln1_w, ln1_b, w_ffn_in_bf, w_ffn_out_bf, ln2_w, ln2_b):
    tok = lambda w: pl.BlockSpec((TD, w), lambda i: (i, 0))
    once = lambda *tail: _layer_spec(l, tail, 1, single=True)
    return pl.pallas_call(
        _mix_ffn_kernel,
        grid=(NBLK_D,),
        in_specs=[
            _ctx_spec(D_MODEL), _lat_spec(D_MODEL), once(MOD_ROWS, 6 * D_MODEL),
            tok(W_RWKV), tok(W_RWKV),
            tok(W_RWKV), tok(W_RWKV), once(1, W_RWKV), once(1, W_RWKV),
            _ctx_spec(W_NA), _lat_spec(W_NA), _ctx_spec(W_GQA), _lat_spec(W_GQA),
            once(D_MODEL, D_MODEL), once(1, D_MODEL), once(1, D_MODEL),
            once(D_MODEL, 2 * D_FF), once(D_FF, D_MODEL), once(1, D_MODEL), once(1, D_MODEL),
        ],
        out_specs=[_ctx_spec(D_MODEL), _lat_spec(D_MODEL)],
        out_shape=[jax.ShapeDtypeStruct((N_CTX, D_MODEL), F32), jax.ShapeDtypeStruct((N_LAT, D_MODEL), F32)],
        compiler_params=_cparams(1, VMEM_LIMIT_FFN),
        name="mix_ffn",
    )(x_ctx, x_lat, mod_all, o_fwd, o_bwd, g, bonus, lnx_w, lnx_b, o_na_ctx, o_na_lat, o_g_ctx, o_g_lat,
      w_out_bf, ln1_w, ln1_b, w_ffn_in_bf, w_ffn_out_bf, ln2_w, ln2_b)


def _rope_tables():
    t = jnp.arange(DEC_SEQ)
    inv = ROPE_BASE ** (-jnp.arange(ROPE_FREQ, dtype=F32) / ROPE_FREQ)
    ang_r = (t // GRID_W).astype(F32)[:, None] * inv
    ang_c = (t % GRID_W).astype(F32)[:, None] * inv
    cos = jnp.concatenate([jnp.cos(ang_r)] * 2 + [jnp.cos(ang_c)] * 2, axis=1)
    sin = jnp.concatenate([-jnp.sin(ang_r), jnp.sin(ang_r), -jnp.sin(ang_c), jnp.sin(ang_c)], axis=1)
    cos = jnp.concatenate([jnp.ones((TD, HEAD_DIM), F32), cos], axis=0)
    sin = jnp.concatenate([jnp.zeros((TD, HEAD_DIM), F32), sin], axis=0)
    return jnp.tile(cos, (1, LANES // HEAD_DIM)), jnp.tile(sin, (1, LANES // HEAD_DIM))


def _block_diag2(w):
    z = jnp.zeros_like(w[:, 0])
    return jnp.concatenate([jnp.concatenate([w[:, 0], z], axis=2), jnp.concatenate([z, w[:, 1]], axis=2)], axis=1)


def _pair_states(s):
    lead = s.shape[:-3]
    s = s.reshape(lead + (N_PAIR, 2, HEAD_DIM, HEAD_DIM))
    z = jnp.zeros_like(s[..., 0, :, :])
    top = jnp.concatenate([s[..., 0, :, :], z], axis=-1)
    bot = jnp.concatenate([z, s[..., 1, :, :]], axis=-1)
    return jnp.concatenate([top, bot], axis=-2)


def _unpair_states(s):
    lead = s.shape[:-3]
    a = s[..., :HEAD_DIM, :HEAD_DIM]
    b = s[..., HEAD_DIM:, HEAD_DIM:]
    return jnp.stack([a, b], axis=-3).reshape(lead + (H_RWKV, HEAD_DIM, HEAD_DIM))


def kernel(x_prompt, x_sample, state_rwkv, cache_na_k, cache_na_v, cache_gqa_k, cache_gqa_v, c, c_ctx,
           w_mod, b_mod, w_in, rwkv_conv, rwkv_w0, rwkv_w2, rwkv_a0, rwkv_a2, rwkv_g2, rwkv_k_k, rwkv_k_a,
           rwkv_r_k, rwkv_lnx_w, rwkv_lnx_b, na_rpb, gqa_q_norm, gqa_k_norm, w_out, ln1_w, ln1_b,
           w_ffn_in, w_ffn_out, ln2_w, ln2_b):
    x_ctx, x_lat = x_prompt.reshape(N_CTX, D_MODEL), x_sample.reshape(N_LAT, D_MODEL)
    cc = jnp.concatenate([c_ctx[None], c, jnp.zeros((MOD_ROWS - 1 - DEC_BATCH, D_MODEL), F32)], axis=0)
    mod_all = _modulation(cc, w_mod, b_mod)
    tb_all = _bias_tables(na_rpb)
    cos_tab, sin_tab = _rope_tables()
    rows = lambda a: a.reshape(DEPTH, 1, -1)
    w_in_bf, w_out_bf = w_in.astype(BF16), w_out.astype(BF16)
    w_ffn_in_bf, w_ffn_out_bf = w_ffn_in.astype(BF16), w_ffn_out.astype(BF16)
    qn = jnp.tile(rows(gqa_q_norm), (1, 1, H_GQA))
    kn = jnp.tile(rows(gqa_k_norm), (1, 1, H_GQA_KV))
    w2bd, a2bd = _block_diag2(rwkv_w2), _block_diag2(rwkv_a2)
    s0_lat = _pair_states(state_rwkv)
    kc_na = cache_na_k.reshape(DEC_BATCH, DEPTH, PAST_LEN, W_NA)
    vc_na = cache_na_v.reshape(DEC_BATCH, DEPTH, PAST_LEN, W_NA)
    kc_g = cache_gqa_k.reshape(DEC_BATCH, DEPTH, PAST_LEN, W_GQA_KV)
    vc_g = cache_gqa_v.reshape(DEC_BATCH, DEPTH, PAST_LEN, W_GQA_KV)
    caches = [jnp.zeros((BATCH, DEPTH, SEQ, w), F32) for w in (W_NA, W_NA, W_GQA_KV, W_GQA_KV)]
    s_fin = [jnp.zeros((BATCH, DEPTH, N_PAIR, LANES, LANES), F32) for _ in range(2)]

    for l in range(DEPTH):
        feat, naq, nak, nav, gq, gk, gv, *caches = _inproj(
            l, x_ctx, x_lat, mod_all, w_in_bf, qn, kn, cos_tab, sin_tab, caches)
        r, kap, v, lw, ah, kd, g, bonus = _rwkv_prep(
            l, feat, rwkv_conv, rows(rwkv_w0), w2bd, rows(rwkv_a0), a2bd, rwkv_g2,
            rows(rwkv_k_k), rows(rwkv_k_a), rows(rwkv_r_k))
        o_fwd, o_bwd, *s_fin = _rwkv_scan(l, r, kap, v, lw, ah, kd, s0_lat, s_fin)
        o_na_ctx, o_g_ctx = _ctx_attention(naq, nak, nav, gq, gk, gv)
        o_na_lat = _lat_na(l, naq, nak, nav, kc_na, vc_na, tb_all)
        o_g_lat = _lat_gqa(l, gq, gk, gv, kc_g, vc_g)
        x_ctx, x_lat = _mix_ffn(
            l, x_ctx, x_lat, mod_all, o_fwd, o_bwd, g, bonus, rows(rwkv_lnx_w), rows(rwkv_lnx_b),
            o_na_ctx, o_na_lat, o_g_ctx, o_g_lat, w_out_bf, rows(ln1_w), rows(ln1_b),
            w_ffn_in_bf, w_ffn_out_bf, rows(ln2_w), rows(ln2_b))
    y_prompt = x_ctx.reshape(BATCH, SEQ, D_MODEL)
    y_sample = x_lat.reshape(DEC_BATCH, DEC_SEQ, D_MODEL)
    new_state = jnp.stack([_unpair_states(s) for s in s_fin], axis=2)
    return (y_prompt, y_sample, new_state,
            caches[0].reshape(BATCH, DEPTH, SEQ, H_NA, HEAD_DIM), caches[1].reshape(BATCH, DEPTH, SEQ, H_NA, HEAD_DIM),
            caches[2].reshape(BATCH, DEPTH, SEQ, H_GQA_KV, HEAD_DIM),
            caches[3].reshape(BATCH, DEPTH, SEQ, H_GQA_KV, HEAD_DIM))
```

```python
import jax
import jax.numpy as jnp
from jax import lax
from jax.experimental import pallas as pl
from jax.experimental.pallas import tpu as pltpu

F32 = jnp.float32
BF16 = jnp.bfloat16
HIGHEST = lax.Precision.HIGHEST

D_MODEL = 1024
BATCH = 16
SEQ = 256
DEPTH = 4
DEC_BATCH = 2
DEC_SEQ = 1024
PAST_LEN = 512
GRID_W = 64
GRID_ROWS = DEC_SEQ // GRID_W
HEAD_DIM = 64
H_RWKV = 4
H_NA = 4
H_GQA = 8
H_GQA_KV = 2
W_RWKV = H_RWKV * HEAD_DIM
W_NA = H_NA * HEAD_DIM
W_GQA = H_GQA * HEAD_DIM
W_GQA_KV = H_GQA_KV * HEAD_DIM
LORA_W = 64
LORA_A = 64
LORA_G = 128
RWKV_IN = 3 * W_RWKV + 2 * LORA_W + 2 * LORA_A + LORA_G
NA_IN = 3 * W_NA
GQA_IN = W_GQA + 2 * W_GQA_KV
D_IN = RWKV_IN + NA_IN + GQA_IN
NA_ROWS = 8
NA_COLS = 16
ROPE_BASE = 10000.0
ROPE_FREQ = HEAD_DIM // 4
D_FF = ((8 * D_MODEL + 3 * 256 - 1) // (3 * 256)) * 256
DEEPNORM_ALPHA = (2 * DEPTH) ** 0.25
LN_EPS = 1e-5
RMS_EPS = 1e-6
GN_EPS = 64e-5
NEG_INF = -1e30
ATTN_SCALE = HEAD_DIM ** -0.5

LANES = 128
TM = 256
N_CTX = BATCH * SEQ
N_LAT = DEC_BATCH * DEC_SEQ
N_TOK = N_CTX + N_LAT
NBLK = N_TOK // TM
NCB = N_CTX // TM
LAT_BLKS = DEC_SEQ // TM
CHUNK = 64
N_CHUNK = TM // CHUNK
TS = 512
NCS = TS // CHUNK
NBLK_S = N_TOK // TS
NCB_S = N_CTX // TS
LAT_BLKS_S = DEC_SEQ // TS
SEQS_S = TS // SEQ
SEQ_CHUNKS = SEQ // CHUNK
TD = 512
SUB_ROWS = [slice(h * SEQ, (h + 1) * SEQ) for h in range(TD // SEQ)]
NBLK_D = N_TOK // TD
NCB_D = N_CTX // TD
LAT_BLKS_D = DEC_SEQ // TD
FF_CHUNK = D_FF
N_PAIR = W_RWKV // LANES
MOD_ROWS = 8
VMEM_LIMIT = 48 * 1024 * 1024
VMEM_LIMIT_FFN = 56 * 1024 * 1024
N_DR = 2 * NA_ROWS - 1
N_DC = 2 * NA_COLS - 1
N_BAND = NA_ROWS * GRID_W


def _cparams(n_grid, vmem_limit=VMEM_LIMIT):
    return pltpu.CompilerParams(dimension_semantics=("arbitrary",) * n_grid, vmem_limit_bytes=vmem_limit)


def _iota(shape, dim):
    return lax.broadcasted_iota(jnp.int32, shape, dim)


NN = (((1,), (0,)), ((), ()))
NT = (((1,), (1,)), ((), ()))
TN = (((0,), (0,)), ((), ()))


def _dg(a, b, dims=NN, precision=None):
    return lax.dot_general(a, b, dims, preferred_element_type=F32, precision=precision)


def _split(x):
    hi = x.astype(BF16)
    return hi, (x - hi.astype(F32)).astype(BF16)


def _dgs(a_s, b_s, dims=NN):
    if len(a_s) == 1:
        return _dg(a_s[0], b_s[0], dims)
    (ah, al), (bh, bl) = a_s, b_s
    ca, cb = dims[0][0][0], dims[0][1][0]
    return _dg(jnp.concatenate([ah, ah, al], axis=ca), jnp.concatenate([bh, bl, bh], axis=cb), dims)


def _dg3(a, b, dims=NN):
    return _dgs(_split(a), _split(b), dims)


def _sigmoid(x):
    return 1.0 / (1.0 + jnp.exp(-x))


def _softplus(x):
    return jnp.maximum(x, 0.0) + jnp.log(1.0 + jnp.exp(-jnp.abs(x)))


def _seg64_sum(x):
    rows, width = x.shape
    lo = _iota((rows, LANES), 1) < HEAD_DIM
    outs = []
    for c in range(width // LANES):
        blk = x[:, c * LANES:(c + 1) * LANES]
        s_lo = jnp.sum(jnp.where(lo, blk, 0.0), axis=-1, keepdims=True)
        s_hi = jnp.sum(jnp.where(lo, 0.0, blk), axis=-1, keepdims=True)
        outs.append(jnp.where(lo, s_lo, s_hi))
    return outs[0] if len(outs) == 1 else jnp.concatenate(outs, axis=1)


def _layer_norm(x, w, b):
    mu = jnp.mean(x, axis=-1, keepdims=True)
    xc = x - mu
    var = jnp.mean(xc * xc, axis=-1, keepdims=True)
    return xc * lax.rsqrt(var + LN_EPS) * w + b


def _mod_row(i):
    return jnp.where(i < NCB_D, 0, 1 + (i - NCB_D) // LAT_BLKS_D)


def _layer_spec(l, tail, n_grid, single=False):
    idx = lambda *g: (l,) + (0,) * len(tail)
    del n_grid
    if single:
        return pl.BlockSpec((None,) + tuple(tail), idx, pipeline_mode=pl.Buffered(1))
    return pl.BlockSpec((None,) + tuple(tail), idx)


ANY_SPEC = pl.BlockSpec(memory_space=pl.ANY)


def _ctx_spec(w):
    return pl.BlockSpec((TD, w), lambda i: (jnp.minimum(i, NCB_D - 1), 0))


def _lat_spec(w):
    return pl.BlockSpec((TD, w), lambda i: (jnp.maximum(i - NCB_D, 0), 0))


def _pick(i, ctx_ref, lat_ref, rows=slice(None)):
    return jnp.where(i < NCB_D, ctx_ref[rows, :], lat_ref[rows, :])


def _mod_kernel(c_ref, w_ref, b_ref, o_ref):
    c = c_ref[...]
    s = c * _sigmoid(c)
    o_ref[...] = _dg(s, w_ref[...], NN, HIGHEST) + b_ref[...]


def _modulation(cc, w_mod, b_mod):
    return pl.pallas_call(
        _mod_kernel,
        grid=(DEPTH, 6),
        in_specs=[
            pl.BlockSpec((MOD_ROWS, D_MODEL), lambda l, j: (0, 0)),
            pl.BlockSpec((None, D_MODEL, D_MODEL), lambda l, j: (l, 0, j)),
            pl.BlockSpec((None, 1, D_MODEL), lambda l, j: (l, 0, j)),
        ],
        out_specs=pl.BlockSpec((None, MOD_ROWS, D_MODEL), lambda l, j: (l, 0, j)),
        out_shape=jax.ShapeDtypeStruct((DEPTH, MOD_ROWS, 6 * D_MODEL), F32),
        compiler_params=_cparams(2),
        name="modulation",
    )(cc, w_mod, b_mod.reshape(DEPTH, 1, 6 * D_MODEL))


def _bias_kernel(rpb_ref, o_ref):
    q = _iota((GRID_W, LANES), 0)
    x = _iota((GRID_W, LANES), 1)
    c = x % GRID_W
    right = x >= GRID_W
    dc = jnp.clip(c - q, 1 - NA_COLS, NA_COLS - 1) + NA_COLS - 1
    c0 = jnp.clip(q - NA_COLS // 2, 0, GRID_W - NA_COLS)
    in_win = (c >= c0) & (c < c0 + NA_COLS)

    def body(t, carry):
        h = t // (N_DR - 1)
        dr = t % (N_DR - 1)
        rows = [jnp.broadcast_to(rpb_ref[pl.ds(h * N_DR + dr + k, 1), :], (GRID_W, LANES)) for k in range(2)]
        lo, hi = (jnp.take_along_axis(r, dc, axis=1) for r in rows)
        o_ref[h, dr] = jnp.where(in_win, jnp.where(right, hi, lo), NEG_INF)
        return carry

    lax.fori_loop(0, H_NA * (N_DR - 1), body, 0, unroll=N_DR - 1)


RPB_ROWS = -(-H_NA * N_DR // 8) * 8


def _bias_tables(na_rpb):
    rpb = jnp.pad(na_rpb.reshape(DEPTH, H_NA * N_DR, N_DC), ((0, 0), (0, RPB_ROWS - H_NA * N_DR), (0, LANES - N_DC)))
    return pl.pallas_call(
        _bias_kernel,
        grid=(DEPTH,),
        in_specs=[pl.BlockSpec((None, RPB_ROWS, LANES), lambda l: (l, 0, 0))],
        out_specs=pl.BlockSpec((None, H_NA, N_DR - 1, GRID_W, LANES), lambda l: (l, 0, 0, 0, 0)),
        out_shape=jax.ShapeDtypeStruct((DEPTH, H_NA, N_DR - 1, GRID_W, LANES), F32),
        compiler_params=_cparams(1),
        name="na_bias_tables",
    )(rpb)


def _rope(x, cos, sin):
    k = x.shape[1] // LANES
    cosf = cos if k == 1 else jnp.concatenate([cos] * k, axis=1)
    sinf = sin if k == 1 else jnp.concatenate([sin] * k, axis=1)
    first = (_iota(x.shape, 1) % (2 * ROPE_FREQ)) < ROPE_FREQ
    partner = jnp.where(first, pltpu.roll(x, x.shape[1] - ROPE_FREQ, axis=1), pltpu.roll(x, ROPE_FREQ, axis=1))
    return x * cosf + partner * sinf


def _rwkv_features(x, prev_row, next_row, conv_ref, w0_ref, w2_ref, a0_ref, a2_ref, g2_ref, kk_ref, ka_ref, rk_ref):
    rows = _iota(x.shape, 0)
    x_prev = jnp.where(rows == 0, prev_row, pltpu.roll(x, 1, axis=0))
    x_next = jnp.where(rows == SEQ - 1, next_row, pltpu.roll(x, SEQ - 1, axis=0))
    f = x_prev * conv_ref[0:1, :] + x * conv_ref[1:2, :] + x_next * conv_ref[2:3, :]
    o1, o2, o3 = W_RWKV, 2 * W_RWKV, 3 * W_RWKV
    o4 = o3 + 2 * LORA_W
    o5 = o4 + 2 * LORA_A
    r, k, v = f[:, :o1], f[:, o1:o2], f[:, o2:o3]
    wd, ad, gd = f[:, o3:o4], f[:, o4:o5], f[:, o5:]
    log_w = -_softplus(-(w0_ref[...] + _dg3(jnp.tanh(wd), w2_ref[...]))) - 0.5
    a = _sigmoid(a0_ref[...] + _dg3(ad, a2_ref[...]))
    g = _dg3(_sigmoid(gd), g2_ref[...])
    kk = k * kk_ref[...]
    kap = kk / jnp.maximum(jnp.sqrt(_seg64_sum(kk * kk)), 1e-12)
    per_dir = []
    kd_sum = jnp.zeros_like(k)
    for d in range(2):
        a_d = a[:, d * W_RWKV:(d + 1) * W_RWKV]
        kd = k * (1.0 + (a_d - 1.0) * ka_ref[...])
        kd_sum = kd_sum + kd
        per_dir.append((-jnp.exp(log_w[:, d * W_RWKV:(d + 1) * W_RWKV]), a_d * kap, kd))
    bonus = _seg64_sum(r * kd_sum * rk_ref[...]) * v
    return r, kap, v, per_dir, g, bonus


HALO = 8


def _inproj_kernel(xc_ref, xl_ref, xp_ref, xn_ref, mod_ref, w_ref, qn_ref, kn_ref, cos_ref, sin_ref,
                   conv_ref, w0_ref, w2_ref, a0_ref, a2_ref, g2_ref, kk_ref, ka_ref, rk_ref,
                   c0_ref, c1_ref, c2_ref, c3_ref,
                   naq_ref, nak_ref, nav_ref, gq_ref, gk_ref, gv_ref,
                   r_ref, kap_ref, v_ref, lw_ref, ah_ref, kd_ref, g_ref, bonus_ref,
                   cnak_ref, cnav_ref, cgk_ref, cgv_ref):
    del c0_ref, c1_ref, c2_ref, c3_ref
    i = pl.program_id(0)
    row = _mod_row(i)
    shift1 = mod_ref[pl.ds(row, 1), 0:D_MODEL]
    scale1 = mod_ref[pl.ds(row, 1), D_MODEL:2 * D_MODEL]
    modulate = lambda x: (x * (1.0 + scale1) + shift1).astype(BF16)
    lat = i >= NCB_D
    pos = (i - NCB_D) % LAT_BLKS_D
    in_seq = jnp.where(lat, 1.0, 0.0)
    has_prev = jnp.where(lat & (pos != 0), 1.0, 0.0)
    has_next = jnp.where(lat & (pos != LAT_BLKS_D - 1), 1.0, 0.0)
    xm = [modulate(_pick(i, xc_ref, xl_ref, rs)) for rs in SUB_ROWS]
    xm[0] = jnp.concatenate([xm[0], modulate(xp_ref[...]), modulate(xn_ref[...])], axis=0)
    projs = [_dg(x, w_ref[...]) for x in xm]
    halo = projs[0][SEQ:, :RWKV_IN]
    projs[0] = projs[0][:SEQ]
    feats = [proj[:, :RWKV_IN] for proj in projs]
    o_na = RWKV_IN
    o_g = RWKV_IN + NA_IN
    new_kv = []
    n_sub = len(SUB_ROWS)
    for h, (rs, proj) in enumerate(zip(SUB_ROWS, projs)):
        prev_row = halo[HALO - 1:HALO] * has_prev if h == 0 else feats[h - 1][SEQ - 1:SEQ] * in_seq
        next_row = halo[HALO:HALO + 1] * has_next if h == n_sub - 1 else feats[h + 1][0:1] * in_seq
        r, kap, v, per_dir, g, bonus = _rwkv_features(
            feats[h], prev_row, next_row, conv_ref, w0_ref, w2_ref, a0_ref, a2_ref, g2_ref, kk_ref, ka_ref, rk_ref)
        r_ref[rs, :] = r
        kap_ref[rs, :] = kap
        v_ref[rs, :] = v
        g_ref[rs, :] = g
        bonus_ref[rs, :] = bonus
        for d, (lw, ah, kd) in enumerate(per_dir):
            lw_ref[d, rs, :] = lw
            ah_ref[d, rs, :] = ah
            kd_ref[d, rs, :] = kd
        naq_ref[rs, :] = proj[:, o_na:o_na + W_NA] * ATTN_SCALE
        nak = proj[:, o_na + W_NA:o_na + 2 * W_NA]
        nav = proj[:, o_na + 2 * W_NA:o_na + 3 * W_NA]
        q = proj[:, o_g:o_g + W_GQA]
        k = proj[:, o_g + W_GQA:o_g + W_GQA + W_GQA_KV]
        gv = proj[:, o_g + W_GQA + W_GQA_KV:]
        cos = cos_ref[rs, :]
        sin = sin_ref[rs, :]
        q = q * lax.rsqrt(_seg64_sum(q * q) * (1.0 / HEAD_DIM) + RMS_EPS) * qn_ref[...]
        k = k * lax.rsqrt(_seg64_sum(k * k) * (1.0 / HEAD_DIM) + RMS_EPS) * kn_ref[...]
        gk = _rope(k, cos, sin)
        gq_ref[rs, :] = _rope(q, cos, sin) * ATTN_SCALE
        nak_ref[rs, :] = nak
        nav_ref[rs, :] = nav
        gk_ref[rs, :] = gk
        gv_ref[rs, :] = gv
        new_kv.append((nak, nav, gk, gv))

    @pl.when(i < NCB_D)
    def _():
        for h, vals in enumerate(new_kv):
            for ref, val in zip((cnak_ref, cnav_ref, cgk_ref, cgv_ref), vals):
                ref[h] = val


def _inproj(l, x_ctx, x_lat, mod_all, w_in_bf, qn, kn, cos_tab, sin_tab, rwkv_params, caches):
    tab_idx = lambda i: (jnp.where(i < NCB_D, 0, 1 + (i - NCB_D) % LAT_BLKS_D), 0)
    lat_blk = lambda i: jnp.maximum(i - NCB_D, 0)
    halo_blocks = TD // HALO
    widths = (W_NA, W_NA, W_NA, W_GQA, W_GQA_KV, W_GQA_KV, W_RWKV, W_RWKV, W_RWKV)
    cache_w = (W_NA, W_NA, W_GQA_KV, W_GQA_KV)
    tok = lambda w: pl.BlockSpec((TD, w), lambda i: (i, 0))
    tok2 = pl.BlockSpec((2, TD, W_RWKV), lambda i: (0, i, 0))
    cache_spec = lambda w: pl.BlockSpec((TD // SEQ, None, SEQ, w), lambda i: (jnp.minimum(i, NCB_D - 1), l, 0, 0))
    rwkv_tails = ((3, RWKV_IN), (1, 2 * W_RWKV), (2 * LORA_W, 2 * W_RWKV), (1, 2 * W_RWKV), (2 * LORA_A, 2 * W_RWKV),
                  (LORA_G, W_RWKV), (1, W_RWKV), (1, W_RWKV), (1, W_RWKV))
    in_specs = [
        _ctx_spec(D_MODEL), _lat_spec(D_MODEL),
        pl.BlockSpec((HALO, D_MODEL), lambda i: (jnp.maximum(lat_blk(i) * halo_blocks - 1, 0), 0)),
        pl.BlockSpec((HALO, D_MODEL), lambda i: (jnp.minimum((lat_blk(i) + 1) * halo_blocks, N_LAT // HALO - 1), 0)),
        _layer_spec(l, (MOD_ROWS, 6 * D_MODEL), 1),
        _layer_spec(l, (D_MODEL, D_IN), 1),
        _layer_spec(l, (1, W_GQA), 1),
        _layer_spec(l, (1, W_GQA_KV), 1),
        pl.BlockSpec((TD, LANES), tab_idx),
        pl.BlockSpec((TD, LANES), tab_idx),
    ] + [_layer_spec(l, t, 1) for t in rwkv_tails]
    n_in = len(in_specs)
    out_specs = [tok(w) for w in widths] + [tok2] * 3 + [tok(W_RWKV)] * 2
    out_shape = ([jax.ShapeDtypeStruct((N_TOK, w), F32) for w in widths]
                 + [jax.ShapeDtypeStruct((2, N_TOK, W_RWKV), F32)] * 3
                 + [jax.ShapeDtypeStruct((N_TOK, W_RWKV), F32)] * 2)
    return pl.pallas_call(
        _inproj_kernel,
        grid=(NBLK_D,),
        in_specs=in_specs + [ANY_SPEC] * 4,
        out_specs=out_specs + [cache_spec(w) for w in cache_w],
        out_shape=out_shape + [jax.ShapeDtypeStruct((BATCH, DEPTH, SEQ, w), F32) for w in cache_w],
        input_output_aliases={n_in + j: len(out_specs) + j for j in range(4)},
        compiler_params=_cparams(1),
        name="inproj",
    )(x_ctx, x_lat, x_lat, x_lat, mod_all, w_in_bf, qn, kn, cos_tab, sin_tab, *rwkv_params, *caches)


def _scan_kernel(r0_ref, kap0_ref, v0_ref, r1_ref, kap1_ref, v1_ref,
                 lw0_ref, ah0_ref, kd0_ref, lw1_ref, ah1_ref, kd1_ref, s00_ref, s01_ref, sf0_in_ref, sf1_in_ref,
                 o0_ref, o1_ref, sf0_ref, sf1_ref,
                 s_scr, sdone_scr, rt_scr, kt_scr, kdt_scr, at_scr, cum_scr, m_scr, n_scr, q_scr, oo_scr):
    del sf0_in_ref, sf1_in_ref
    j = pl.program_id(0)
    par = j % 2
    jb = jnp.maximum(j - 1, 0)
    upd_blk = (jb, NBLK_S - 1 - jb)
    is_ctx = tuple(b < NCB_S for b in upd_blk)
    first_pos = (0, LAT_BLKS_S - 1)
    r_refs, kap_refs, v_refs = (r0_ref, r1_ref), (kap0_ref, kap1_ref), (v0_ref, v1_ref)
    lw_refs, ah_refs, kd_refs = (lw0_ref, lw1_ref), (ah0_ref, ah1_ref), (kd0_ref, kd1_ref)
    s0_refs, o_refs, sf_refs = (s00_ref, s01_ref), (o0_ref, o1_ref), (sf0_ref, sf1_ref)

    @pl.when(j == 0)
    def _():
        for ref in (s_scr, sdone_scr, m_scr, n_scr, q_scr, oo_scr):
            ref[...] = jnp.zeros_like(ref)

    for dn in range(2):
        blk = upd_blk[dn]

        @pl.when((j > 0) & is_ctx[dn])
        def _():
            s_scr[dn] = jnp.zeros(s_scr.shape[1:], F32)

        @pl.when((j > 0) & (blk >= NCB_S) & ((blk - NCB_S) % LAT_BLKS_S == first_pos[dn]))
        def _():
            s_scr[dn] = s0_refs[dn][...]

    def seq_slot(dn, n):
        return n if dn == 0 else SEQS_S - 1 - n

    def update(c):
        for dn in range(2):
            ce = c if dn == 0 else NCS - 1 - c
            rows = slice(ce * CHUNK, (ce + 1) * CHUNK)
            for p in range(N_PAIR):
                s = s_scr[dn, p]
                if c > 0 and c % SEQ_CHUNKS == 0:
                    sdone_scr[dn, seq_slot(dn, c // SEQ_CHUNKS - 1), p] = s
                    s = jnp.where(is_ctx[dn], 0.0, s)
                s2 = jnp.concatenate(_split(s), axis=1)
                q = q_scr[1 - par, dn, ce, p]
                m = m_scr[1 - par, dn, ce, p]
                o_refs[dn][rows, p * LANES:(p + 1) * LANES] = (
                    _dg(jnp.concatenate([q, q], axis=1), s2, NT) + oo_scr[1 - par, dn, ce, p])
                s_scr[dn, p] = _dg(s2, jnp.concatenate([m, m], axis=0)) + n_scr[1 - par, dn, ce, p]

    pending = list(range(NCS))

    def next_update():
        if pending:
            update(pending.pop(0))

    next_update()
    rr = _iota((TM, TM), 0)
    cc = _iota((TM, TM), 1)
    same_chunk = (rr // CHUNK) == (cc // CHUNK)
    for dn in range(2):
        order = (rr >= cc) if dn == 0 else (rr <= cc)
        tri = jnp.where(same_chunk & order, 1.0, 0.0).astype(BF16)
        for h in range(TS // TM):
            rs = slice(h * TM, (h + 1) * TM)
            lw = lw_refs[dn][rs, :]
            lw_hi = lw.astype(BF16)
            lw_r = lw - lw_hi.astype(F32)
            lw_mid = lw_r.astype(BF16)
            lw_lo = (lw_r - lw_mid.astype(F32)).astype(BF16)
            cum = _dg(jnp.concatenate([tri, tri, tri], axis=1), jnp.concatenate([lw_hi, lw_mid, lw_lo], axis=0))
            e_neg = jnp.exp(-cum)
            rt_scr[dn, rs, :] = r_refs[dn][rs, :] * jnp.exp(cum)
            kt_scr[dn, rs, :] = kap_refs[dn][rs, :] * jnp.exp(cum - lw)
            kdt_scr[dn, rs, :] = kd_refs[dn][rs, :] * e_neg
            at_scr[dn, rs, :] = ah_refs[dn][rs, :] * e_neg
            cum_scr[dn, rs, :] = cum
        if dn == 0:
            next_update()

    row = _iota((CHUNK, LANES), 0)
    col = _iota((CHUNK, LANES), 1) % CHUNK
    left = _iota((CHUNK, LANES), 1) < CHUNK
    incl = (row >= col, row <= col)
    strict = (row > col, row < col)
    eye = jnp.where(row == col, 1.0, 0.0)
    r2 = _iota((LANES, LANES), 0)
    c2 = _iota((LANES, LANES), 1)
    bd_mask = (r2 // CHUNK) == (c2 // CHUNK)
    eye2 = jnp.where(r2 == c2, 1.0, 0.0)

    zero_bf = jnp.zeros((CHUNK, LANES), BF16)
    split = lambda x: (x.astype(BF16),)

    def bd(xs):
        return tuple(jnp.concatenate([jnp.where(left, y, zero_bf), jnp.where(left, zero_bf, y)], axis=0)
                     for y in xs)

    def cat(xs, ys, axis):
        return tuple(jnp.concatenate([x, y], axis=axis) for x, y in zip(xs, ys))

    def mm(a_list, b_list):
        return [_dgs(a, bd(b)) for a, b in zip(a_list, b_list)]

    units = [(dn, c, p) for c in range(NCS) for dn in range(2) for p in range(N_PAIR)]
    each = lambda f, *lists: [f(*args) for args in zip(*lists)]

    def load(ref):
        return [ref[dn, c * CHUNK:(c + 1) * CHUNK, p * LANES:(p + 1) * LANES] for dn, c, p in units]

    rt = load(rt_scr)
    v = [v_refs[dn][c * CHUNK:(c + 1) * CHUNK, p * LANES:(p + 1) * LANES] for dn, c, p in units]
    kt_s, rt_s, kdt_s, at_s, v_s = (each(split, x) for x in (load(kt_scr), rt, load(kdt_scr), load(at_scr), v))
    gam = [jnp.exp(cum_scr[dn, (c + 1) * CHUNK - 1:(c + 1) * CHUNK, p * LANES:(p + 1) * LANES] if dn == 0 else
                   cum_scr[dn, c * CHUNK:c * CHUNK + 1, p * LANES:(p + 1) * LANES]) for dn, c, p in units]
    gram = each(lambda k, r, a, kd: _dgs(cat(k, r, 0), cat(bd(a), bd(kd), 0), NT), kt_s, rt_s, at_s, kdt_s)
    dirs = [u[0] for u in units]
    la = [jnp.where(strict[dn], g[0:CHUNK, 0:LANES], 0.0) for dn, g in zip(dirs, gram)]
    lk_s = [split(jnp.where(strict[dn], g[0:CHUNK, LANES:], 0.0)) for dn, g in zip(dirs, gram)]
    ra_s = [split(jnp.where(incl[dn], g[CHUNK:, 0:LANES], 0.0)) for dn, g in zip(dirs, gram)]
    rk_s = [split(jnp.where(incl[dn], g[CHUNK:, LANES:], 0.0)) for dn, g in zip(dirs, gram)]
    next_update()
    lrv = mm(each(lambda lk, rk: cat(lk, rk, 0), lk_s, rk_s), v_s)
    next_update()
    b = 8
    l8 = [jnp.where((row // b) == (col // b), x, 0.0) for x in la]
    l8_s = each(split, l8)
    l8_2 = mm(l8_s, l8_s)
    next_update()
    l8_2s = each(split, l8_2)
    l8_4 = mm(l8_2s, l8_2s)
    p1 = mm([split(eye - x) for x in l8], [split(eye + y) for y in l8_2])
    next_update()
    t = mm(each(split, p1), [split(eye + y) for y in l8_4])
    while b < CHUNK:
        next_update()
        offd = ((row // (2 * b)) == (col // (2 * b))) & ((row // b) != (col // b))
        t_s = each(split, t)
        x = mm(t_s, [split(jnp.where(offd, y, 0.0)) for y in la])
        t = each(lambda tt, z: tt - z, t, mm(each(split, x), t_s))
        b *= 2
    assert not pending, "more chunk updates than stages to place them between"
    tx = each(lambda tt, k, y: _dgs(split(tt), cat(bd(k), bd(split(y[0:CHUNK])), 1)), t, kt_s, lrv)
    khat_s = [split(y[:, 0:LANES]) for y in tx]
    w1_s = [split(y[:, LANES:]) for y in tx]
    rx = each(lambda r, k, w: _dgs(r, cat(bd(k), bd(w), 1)), ra_s, khat_s, w1_s)
    mk = each(lambda k, a: _dgs(k, a, TN), khat_s, at_s)
    nk = each(lambda vv, w, kd, a: _dgs(cat(vv, w, 0), cat(kd, tuple(-y for y in a), 0), TN),
              v_s, w1_s, kdt_s, at_s)
    for i, (dn, c, p) in enumerate(units):
        q_scr[par, dn, c, p] = (rt[i] - rx[i][:, 0:LANES]).astype(BF16)
        oo_scr[par, dn, c, p] = lrv[i][CHUNK:] - rx[i][:, LANES:]
        m_scr[par, dn, c, p] = ((eye2 - jnp.where(bd_mask, mk[i], 0.0)) * gam[i]).astype(BF16)
        n_scr[par, dn, c, p] = jnp.where(bd_mask, nk[i], 0.0) * gam[i]

    for dn in range(2):
        @pl.when((j > 0) & is_ctx[dn])
        def _():
            for n in range(SEQS_S - 1):
                sf_refs[dn][seq_slot(dn, n)] = sdone_scr[dn, seq_slot(dn, n)]
            sf_refs[dn][seq_slot(dn, SEQS_S - 1)] = s_scr[dn]


def _rwkv_scan(l, r, kap, v, lw, ah, kd, s0_lat, s_fin):
    def stage_blk(dn, j):
        jj = jnp.minimum(j, NBLK_S - 1)
        return jj if dn == 0 else NBLK_S - 1 - jj

    def update_blk(dn, j):
        jj = jnp.maximum(j - 1, 0)
        return jj if dn == 0 else NBLK_S - 1 - jj

    pair = (N_PAIR, LANES, LANES)
    tok = lambda dn: pl.BlockSpec((TS, W_RWKV), lambda j: (stage_blk(dn, j), 0))
    tok2 = lambda dn: pl.BlockSpec((None, TS, W_RWKV), lambda j: (dn, stage_blk(dn, j), 0))
    s0_spec = lambda dn: pl.BlockSpec(
        (None, None, None) + pair,
        lambda j: (jnp.maximum(update_blk(dn, j) - NCB_S, 0) // LAT_BLKS_S, l, dn, 0, 0, 0))
    sf_spec = lambda dn: pl.BlockSpec(
        (SEQS_S, None) + pair, lambda j: (jnp.minimum(update_blk(dn, j), NCB_S - 1), l, 0, 0, 0))
    o_spec = lambda dn: pl.BlockSpec((TS, W_RWKV), lambda j: (update_blk(dn, j), 0))
    dir_scr = lambda *shape, dtype=F32: pltpu.VMEM((2,) + shape, dtype)
    stage_scr = (2, NCS, N_PAIR)
    return pl.pallas_call(
        _scan_kernel,
        grid=(NBLK_S + 1,),
        in_specs=[tok(0), tok(0), tok(0), tok(1), tok(1), tok(1),
                  tok2(0), tok2(0), tok2(0), tok2(1), tok2(1), tok2(1),
                  s0_spec(0), s0_spec(1), ANY_SPEC, ANY_SPEC],
        out_specs=[o_spec(0), o_spec(1), sf_spec(0), sf_spec(1)],
        out_shape=[jax.ShapeDtypeStruct((N_TOK, W_RWKV), F32)] * 2
        + [jax.ShapeDtypeStruct((BATCH, DEPTH) + pair, F32)] * 2,
        scratch_shapes=[dir_scr(*pair), dir_scr(SEQS_S, *pair)] + [dir_scr(TS, W_RWKV)] * 5
        + [dir_scr(*stage_scr, LANES, LANES, dtype=BF16), dir_scr(*stage_scr, LANES, LANES),
           dir_scr(*stage_scr, CHUNK, LANES, dtype=BF16), dir_scr(*stage_scr, CHUNK, LANES)],
        input_output_aliases={14: 2, 15: 3},
        compiler_params=_cparams(1),
        name="rwkv_scan",
    )(r, kap, v, r, kap, v, lw, ah, kd, lw, ah, kd, s0_lat, s0_lat, s_fin[0], s_fin[1])


def _attend(groups):
    lhs = []
    for q_cols, _, _, _ in groups:
        left = _iota(q_cols[0].shape, 1) < HEAD_DIM
        parts = []
        for qc in q_cols:
            parts += [jnp.where(left, qc, 0.0), jnp.where(left, 0.0, qc)]
        lhs.append(jnp.concatenate(parts, axis=0).astype(BF16))
    s = [_dg(x, g[1], NT) for x, g in zip(lhs, groups)]
    s = [x if g[3] is None else x + g[3] for x, g in zip(s, groups)]
    p = [jnp.exp(x - jnp.max(x, axis=-1, keepdims=True)) for x in s]
    inv = [1.0 / jnp.sum(x, axis=-1, keepdims=True) for x in p]
    o = [_dg(x.astype(BF16), g[2]) * y for x, y, g in zip(p, inv, groups)]
    outs = []
    for x, (q_cols, _, _, _) in zip(o, groups):
        rows = q_cols[0].shape[0]
        left = _iota(q_cols[0].shape, 1) < HEAD_DIM
        outs.append([jnp.where(left, x[2 * j * rows:(2 * j + 1) * rows], x[(2 * j + 1) * rows:(2 * j + 2) * rows])
                     for j in range(len(q_cols))])
    return outs


def _cols(x):
    return [x[:, c * LANES:(c + 1) * LANES] for c in range(x.shape[1] // LANES)]


def _gqa_groups(q, k, v):
    left = _iota(k.shape, 1) < HEAD_DIM
    k_sw = pltpu.roll(k, HEAD_DIM, axis=1)
    v_sw = pltpu.roll(v, HEAD_DIM, axis=1)
    q_cols = _cols(q)
    groups = []
    for g in range(H_GQA_KV):
        k2 = jnp.where(left, k, k_sw) if g == 0 else jnp.where(left, k_sw, k)
        v2 = jnp.where(left, v, v_sw) if g == 0 else jnp.where(left, v_sw, v)
        groups.append((q_cols[2 * g:2 * g + 2], k2.astype(BF16), v2.astype(BF16), None))
    return groups


def _ctx_attn_kernel(naq_ref, nak_ref, nav_ref, gq_ref, gk_ref, gv_ref, ona_ref, og_ref):
    k_cols = _cols(nak_ref[...].astype(BF16))
    v_cols = _cols(nav_ref[...].astype(BF16))
    na_groups = [([qc], kc, vc, None) for qc, kc, vc in zip(_cols(naq_ref[...]), k_cols, v_cols)]
    outs = _attend(na_groups + _gqa_groups(gq_ref[...], gk_ref[...], gv_ref[...]))
    n_na = len(na_groups)
    ona_ref[...] = jnp.concatenate([o[0] for o in outs[:n_na]], axis=1)
    og_ref[...] = jnp.concatenate([c for o in outs[n_na:] for c in o], axis=1)


def _ctx_attention(naq, nak, nav, gq, gk, gv):
    spec = lambda w: pl.BlockSpec((SEQ, w), lambda b: (b, 0))
    return pl.pallas_call(
        _ctx_attn_kernel,
        grid=(BATCH,),
        in_specs=[spec(W_NA), spec(W_NA), spec(W_NA), spec(W_GQA), spec(W_GQA_KV), spec(W_GQA_KV)],
        out_specs=[spec(W_NA), spec(W_GQA)],
        out_shape=[jax.ShapeDtypeStruct((N_CTX, W_NA), F32), jax.ShapeDtypeStruct((N_CTX, W_GQA), F32)],
        compiler_params=_cparams(1),
        name="ctx_attention",
    )(naq, nak, nav, gq, gk, gv)


NA_STEP_ROWS = 2


def _lat_na_kernel(q_ref, k_ref, v_ref, kc_ref, vc_ref, tb_ref, o_ref):
    q = q_ref[...]
    groups = []
    for rr in range(NA_STEP_ROWS):
        r = pl.program_id(1) * NA_STEP_ROWS + rr
        r0 = jnp.clip(r - NA_ROWS // 2, 0, GRID_ROWS - NA_ROWS)
        band = pl.ds(pl.multiple_of(r0 * GRID_W, GRID_W), N_BAND)
        dr0 = r0 - r + NA_ROWS - 1
        for c, qc in enumerate(_cols(q[rr * GRID_W:(rr + 1) * GRID_W])):
            cols = slice(c * LANES, (c + 1) * LANES)
            k2 = jnp.concatenate([k_ref[band, cols], kc_ref[:, cols]], axis=0).astype(BF16)
            v2 = jnp.concatenate([v_ref[band, cols], vc_ref[:, cols]], axis=0).astype(BF16)
            bias = jnp.concatenate(
                [jnp.concatenate([tb_ref[2 * c + half, dr0 + jj] for jj in range(0, NA_ROWS, 2)], axis=1)
                 for half in range(2)], axis=0)
            bias = jnp.concatenate([bias, jnp.zeros((2 * GRID_W, PAST_LEN), F32)], axis=1)
            groups.append(([qc], k2, v2, bias))
    outs = [o[0] for o in _attend(groups)]
    n_col = W_NA // LANES
    o_ref[...] = jnp.concatenate(
        [jnp.concatenate(outs[rr * n_col:(rr + 1) * n_col], axis=1) for rr in range(NA_STEP_ROWS)], axis=0)


def _lat_na(l, naq, nak, nav, kc, vc, tb):
    rows = NA_STEP_ROWS * GRID_W
    steps = GRID_ROWS // NA_STEP_ROWS
    seq_blk0 = N_CTX // DEC_SEQ
    seq = pl.BlockSpec((DEC_SEQ, W_NA), lambda b, r: (seq_blk0 + b, 0))
    cache = pl.BlockSpec((None, None, PAST_LEN, W_NA), lambda b, r: (b, l, 0, 0))
    return pl.pallas_call(
        _lat_na_kernel,
        grid=(DEC_BATCH, steps),
        in_specs=[pl.BlockSpec((rows, W_NA), lambda b, r: (N_CTX // rows + b * steps + r, 0)),
                  seq, seq, cache, cache, _layer_spec(l, (H_NA, N_DR - 1, GRID_W, LANES), 2)],
        out_specs=pl.BlockSpec((rows, W_NA), lambda b, r: (b * steps + r, 0)),
        out_shape=jax.ShapeDtypeStruct((N_LAT, W_NA), F32),
        compiler_params=_cparams(2),
        name="latent_na",
    )(naq, nak, nav, kc, vc, tb)


def _lat_gqa_kernel(q_ref, k_ref, v_ref, kc_ref, vc_ref, o_ref):
    k = jnp.concatenate([kc_ref[...], k_ref[...]], axis=0)
    v = jnp.concatenate([vc_ref[...], v_ref[...]], axis=0)
    o_ref[...] = jnp.concatenate([c for o in _attend(_gqa_groups(q_ref[...], k, v)) for c in o], axis=1)


def _lat_gqa(l, gq, gk, gv, kc, vc):
    seq_blk0 = N_CTX // DEC_SEQ
    seq = pl.BlockSpec((DEC_SEQ, W_GQA_KV), lambda b, i: (seq_blk0 + b, 0))
    cache = pl.BlockSpec((None, None, PAST_LEN, W_GQA_KV), lambda b, i: (b, l, 0, 0))
    return pl.pallas_call(
        _lat_gqa_kernel,
        grid=(DEC_BATCH, LAT_BLKS),
        in_specs=[pl.BlockSpec((TM, W_GQA), lambda b, i: (NCB + b * LAT_BLKS + i, 0)), seq, seq, cache, cache],
        out_specs=pl.BlockSpec((TM, W_GQA), lambda b, i: (b * LAT_BLKS + i, 0)),
        out_shape=jax.ShapeDtypeStruct((N_LAT, W_GQA), F32),
        compiler_params=_cparams(2),
        name="latent_gqa",
    )(gq, gk, gv, kc, vc)


def _mix_ffn_kernel(xc_ref, xl_ref, mod_ref, of_ref, ob_ref, g_ref, bonus_ref, lnxw_ref, lnxb_ref,
                    onac_ref, onal_ref, ogc_ref, ogl_ref,
                    wout_ref, ln1w_ref, ln1b_ref, wfi_ref, wfo_ref, ln2w_ref, ln2b_ref, yc_ref, yl_ref):
    i = pl.program_id(0)
    row = _mod_row(i)
    mod = lambda n: mod_ref[pl.ds(row, 1), n * D_MODEL:(n + 1) * D_MODEL]
    each = lambda f, *lists: [f(*args) for args in zip(*lists)]

    def rwkv_out(rs):
        o = of_ref[rs, :] + ob_ref[rs, :]
        mu = _seg64_sum(o) * (1.0 / HEAD_DIM)
        oc = o - mu
        var = _seg64_sum(oc * oc) * (1.0 / HEAD_DIM)
        o_rwkv = (oc * lax.rsqrt(var + GN_EPS) * lnxw_ref[...] + lnxb_ref[...] + bonus_ref[rs, :]) * g_ref[rs, :]
        return jnp.concatenate([o_rwkv, _pick(i, onac_ref, onal_ref, rs), _pick(i, ogc_ref, ogl_ref, rs)],
                               axis=1).astype(BF16)

    mix = [_dg(rwkv_out(rs), wout_ref[...]) for rs in SUB_ROWS]
    x1 = [_layer_norm(DEEPNORM_ALPHA * _pick(i, xc_ref, xl_ref, rs) + mod(2) * m, ln1w_ref[...], ln1b_ref[...])
          for rs, m in zip(SUB_ROWS, mix)]
    x_in = [(x * (1.0 + mod(4)) + mod(3)).astype(BF16) for x in x1]
    ffn = [jnp.zeros_like(x) for x in x1]
    for lo in range(0, D_FF, FF_CHUNK):
        gate = [_dg(x, wfi_ref[:, lo:lo + FF_CHUNK]) for x in x_in]
        up = [_dg(x, wfi_ref[:, D_FF + lo:D_FF + lo + FF_CHUNK]) for x in x_in]
        act = each(lambda gt, u: (gt * _sigmoid(gt) * u).astype(BF16), gate, up)
        ffn = each(lambda f, a: f + _dg(a, wfo_ref[lo:lo + FF_CHUNK, :]), ffn, act)
    y = each(lambda x, f: _layer_norm(DEEPNORM_ALPHA * x + mod(5) * f, ln2w_ref[...], ln2b_ref[...]), x1, ffn)

    @pl.when(i < NCB_D)
    def _():
        for rs, yy in zip(SUB_ROWS, y):
            yc_ref[rs, :] = yy

    @pl.when(i >= NCB_D)
    def _():
        for rs, yy in zip(SUB_ROWS, y):
            yl_ref[rs, :] = yy


def _mix_ffn(l, x_ctx, x_lat, mod_all, o_fwd, o_bwd, g, bonus, lnx_w, lnx_b, o_na_ctx, o_na_lat, o_g_ctx, o_g_lat,
             w_out_bf, ln1_w, ln1_b, w_ffn_in_bf, w_ffn_out_bf, ln2_w, ln2_b):
    tok = lambda w: pl.BlockSpec((TD, w), lambda i: (i, 0))
    once = lambda *tail: _layer_spec(l, tail, 1, single=True)
    return pl.pallas_call(
        _mix_ffn_kernel,
        grid=(NBLK_D,),
        in_specs=[
            _ctx_spec(D_MODEL), _lat_spec(D_MODEL), once(MOD_ROWS, 6 * D_MODEL),
            tok(W_RWKV), tok(W_RWKV),
            tok(W_RWKV), tok(W_RWKV), once(1, W_RWKV), once(1, W_RWKV),
            _ctx_spec(W_NA), _lat_spec(W_NA), _ctx_spec(W_GQA), _lat_spec(W_GQA),
            once(D_MODEL, D_MODEL), once(1, D_MODEL), once(1, D_MODEL),
            once(D_MODEL, 2 * D_FF), once(D_FF, D_MODEL), once(1, D_MODEL), once(1, D_MODEL),
        ],
        out_specs=[_ctx_spec(D_MODEL), _lat_spec(D_MODEL)],
        out_shape=[jax.ShapeDtypeStruct((N_CTX, D_MODEL), F32), jax.ShapeDtypeStruct((N_LAT, D_MODEL), F32)],
        compiler_params=_cparams(1, VMEM_LIMIT_FFN),
        name="mix_ffn",
    )(x_ctx, x_lat, mod_all, o_fwd, o_bwd, g, bonus, lnx_w, lnx_b, o_na_ctx, o_na_lat, o_g_ctx, o_g_lat,
      w_out_bf, ln1_w, ln1_b, w_ffn_in_bf, w_ffn_out_bf, ln2_w, ln2_b)


def _rope_tables():
    t = jnp.arange(DEC_SEQ)
    inv = ROPE_BASE ** (-jnp.arange(ROPE_FREQ, dtype=F32) / ROPE_FREQ)
    ang_r = (t // GRID_W).astype(F32)[:, None] * inv
    ang_c = (t % GRID_W).astype(F32)[:, None] * inv
    cos = jnp.concatenate([jnp.cos(ang_r)] * 2 + [jnp.cos(ang_c)] * 2, axis=1)
    sin = jnp.concatenate([-jnp.sin(ang_r), jnp.sin(ang_r), -jnp.sin(ang_c), jnp.sin(ang_c)], axis=1)
    cos = jnp.concatenate([jnp.ones((TD, HEAD_DIM), F32), cos], axis=0)
    sin = jnp.concatenate([jnp.zeros((TD, HEAD_DIM), F32), sin], axis=0)
    return jnp.tile(cos, (1, LANES // HEAD_DIM)), jnp.tile(sin, (1, LANES // HEAD_DIM))


def _block_diag2(w):
    z = jnp.zeros_like(w[:, 0])
    return jnp.concatenate([jnp.concatenate([w[:, 0], z], axis=2), jnp.concatenate([z, w[:, 1]], axis=2)], axis=1)


def _pair_states(s):
    lead = s.shape[:-3]
    s = s.reshape(lead + (N_PAIR, 2, HEAD_DIM, HEAD_DIM))
    z = jnp.zeros_like(s[..., 0, :, :])
    top = jnp.concatenate([s[..., 0, :, :], z], axis=-1)
    bot = jnp.concatenate([z, s[..., 1, :, :]], axis=-1)
    return jnp.concatenate([top, bot], axis=-2)


def _unpair_states(s):
    lead = s.shape[:-3]
    a = s[..., :HEAD_DIM, :HEAD_DIM]
    b = s[..., HEAD_DIM:, HEAD_DIM:]
    return jnp.stack([a, b], axis=-3).reshape(lead + (H_RWKV, HEAD_DIM, HEAD_DIM))


def kernel(x_prompt, x_sample, state_rwkv, cache_na_k, cache_na_v, cache_gqa_k, cache_gqa_v, c, c_ctx,
           w_mod, b_mod, w_in, rwkv_conv, rwkv_w0, rwkv_w2, rwkv_a0, rwkv_a2, rwkv_g2, rwkv_k_k, rwkv_k_a,
           rwkv_r_k, rwkv_lnx_w, rwkv_lnx_b, na_rpb, gqa_q_norm, gqa_k_norm, w_out, ln1_w, ln1_b,
           w_ffn_in, w_ffn_out, ln2_w, ln2_b):
    x_ctx, x_lat = x_prompt.reshape(N_CTX, D_MODEL), x_sample.reshape(N_LAT, D_MODEL)
    cc = jnp.concatenate([c_ctx[None], c, jnp.zeros((MOD_ROWS - 1 - DEC_BATCH, D_MODEL), F32)], axis=0)
    mod_all = _modulation(cc, w_mod, b_mod)
    tb_all = _bias_tables(na_rpb)
    cos_tab, sin_tab = _rope_tables()
    rows = lambda a: a.reshape(DEPTH, 1, -1)
    w_in_bf, w_out_bf = w_in.astype(BF16), w_out.astype(BF16)
    w_ffn_in_bf, w_ffn_out_bf = w_ffn_in.astype(BF16), w_ffn_out.astype(BF16)
    qn = jnp.tile(rows(gqa_q_norm), (1, 1, H_GQA))
    kn = jnp.tile(rows(gqa_k_norm), (1, 1, H_GQA_KV))
    rwkv_params = (rwkv_conv, rows(rwkv_w0), _block_diag2(rwkv_w2), rows(rwkv_a0), _block_diag2(rwkv_a2), rwkv_g2,
                   rows(rwkv_k_k), rows(rwkv_k_a), rows(rwkv_r_k))
    s0_lat = _pair_states(state_rwkv)
    kc_na = cache_na_k.reshape(DEC_BATCH, DEPTH, PAST_LEN, W_NA)
    vc_na = cache_na_v.reshape(DEC_BATCH, DEPTH, PAST_LEN, W_NA)
    kc_g = cache_gqa_k.reshape(DEC_BATCH, DEPTH, PAST_LEN, W_GQA_KV)
    vc_g = cache_gqa_v.reshape(DEC_BATCH, DEPTH, PAST_LEN, W_GQA_KV)
    caches = [jnp.zeros((BATCH, DEPTH, SEQ, w), F32) for w in (W_NA, W_NA, W_GQA_KV, W_GQA_KV)]
    s_fin = [jnp.zeros((BATCH, DEPTH, N_PAIR, LANES, LANES), F32) for _ in range(2)]

    for l in range(DEPTH):
        naq, nak, nav, gq, gk, gv, r, kap, v, lw, ah, kd, g, bonus, *caches = _inproj(
            l, x_ctx, x_lat, mod_all, w_in_bf, qn, kn, cos_tab, sin_tab, rwkv_params, caches)
        o_fwd, o_bwd, *s_fin = _rwkv_scan(l, r, kap, v, lw, ah, kd, s0_lat, s_fin)
        o_na_ctx, o_g_ctx = _ctx_attention(naq, nak, nav, gq, gk, gv)
        o_na_lat = _lat_na(l, naq, nak, nav, kc_na, vc_na, tb_all)
        o_g_lat = _lat_gqa(l, gq, gk, gv, kc_g, vc_g)
        x_ctx, x_lat = _mix_ffn(
            l, x_ctx, x_lat, mod_all, o_fwd, o_bwd, g, bonus, rows(rwkv_lnx_w), rows(rwkv_lnx_b),
            o_na_ctx, o_na_lat, o_g_ctx, o_g_lat, w_out_bf, rows(ln1_w), rows(ln1_b),
            w_ffn_in_bf, w_ffn_out_bf, rows(ln2_w), rows(ln2_b))
    y_prompt = x_ctx.reshape(BATCH, SEQ, D_MODEL)
    y_sample = x_lat.reshape(DEC_BATCH, DEC_SEQ, D_MODEL)
    new_state = jnp.stack([_unpair_states(s) for s in s_fin], axis=2)
    return (y_prompt, y_sample, new_state,
            caches[0].reshape(BATCH, DEPTH, SEQ, H_NA, HEAD_DIM), caches[1].reshape(BATCH, DEPTH, SEQ, H_NA, HEAD_DIM),
            caches[2].reshape(BATCH, DEPTH, SEQ, H_GQA_KV, HEAD_DIM),
            caches[3].reshape(BATCH, DEPTH, SEQ, H_GQA_KV, HEAD_DIM))
```

```python
import jax
import jax.numpy as jnp
from jax import lax
from jax.experimental import pallas as pl
from jax.experimental.pallas import tpu as pltpu

F32 = jnp.float32
BF16 = jnp.bfloat16
HIGHEST = lax.Precision.HIGHEST

D_MODEL = 1024
BATCH = 16
SEQ = 256
DEPTH = 4
DEC_BATCH = 2
DEC_SEQ = 1024
PAST_LEN = 512
GRID_W = 64
GRID_ROWS = DEC_SEQ // GRID_W
HEAD_DIM = 64
H_RWKV = 4
H_NA = 4
H_GQA = 8
H_GQA_KV = 2
W_RWKV = H_RWKV * HEAD_DIM
W_NA = H_NA * HEAD_DIM
W_GQA = H_GQA * HEAD_DIM
W_GQA_KV = H_GQA_KV * HEAD_DIM
LORA_W = 64
LORA_A = 64
LORA_G = 128
RWKV_IN = 3 * W_RWKV + 2 * LORA_W + 2 * LORA_A + LORA_G
NA_IN = 3 * W_NA
GQA_IN = W_GQA + 2 * W_GQA_KV
D_IN = RWKV_IN + NA_IN + GQA_IN
NA_ROWS = 8
NA_COLS = 16
ROPE_BASE = 10000.0
ROPE_FREQ = HEAD_DIM // 4
D_FF = ((8 * D_MODEL + 3 * 256 - 1) // (3 * 256)) * 256
DEEPNORM_ALPHA = (2 * DEPTH) ** 0.25
LN_EPS = 1e-5
RMS_EPS = 1e-6
GN_EPS = 64e-5
NEG_INF = -1e30
ATTN_SCALE = HEAD_DIM ** -0.5

LANES = 128
TM = 256
N_CTX = BATCH * SEQ
N_LAT = DEC_BATCH * DEC_SEQ
N_TOK = N_CTX + N_LAT
NBLK = N_TOK // TM
NCB = N_CTX // TM
LAT_BLKS = DEC_SEQ // TM
CHUNK = 64
N_CHUNK = TM // CHUNK
TS = 512
NCS = TS // CHUNK
NBLK_S = N_TOK // TS
NCB_S = N_CTX // TS
LAT_BLKS_S = DEC_SEQ // TS
SEQS_S = TS // SEQ
SEQ_CHUNKS = SEQ // CHUNK
TD = 512
SUB_ROWS = [slice(h * SEQ, (h + 1) * SEQ) for h in range(TD // SEQ)]
NBLK_D = N_TOK // TD
NCB_D = N_CTX // TD
LAT_BLKS_D = DEC_SEQ // TD
FF_CHUNK = D_FF
N_PAIR = W_RWKV // LANES
MOD_ROWS = 8
VMEM_LIMIT = 48 * 1024 * 1024
VMEM_LIMIT_FFN = 56 * 1024 * 1024
N_DR = 2 * NA_ROWS - 1
N_DC = 2 * NA_COLS - 1
N_BAND = NA_ROWS * GRID_W


def _cparams(n_grid, vmem_limit=VMEM_LIMIT):
    return pltpu.CompilerParams(dimension_semantics=("arbitrary",) * n_grid, vmem_limit_bytes=vmem_limit)


def _iota(shape, dim):
    return lax.broadcasted_iota(jnp.int32, shape, dim)


NN = (((1,), (0,)), ((), ()))
NT = (((1,), (1,)), ((), ()))
TN = (((0,), (0,)), ((), ()))


def _dg(a, b, dims=NN, precision=None):
    return lax.dot_general(a, b, dims, preferred_element_type=F32, precision=precision)


def _split(x):
    hi = x.astype(BF16)
    return hi, (x - hi.astype(F32)).astype(BF16)


def _dgs(a_s, b_s, dims=NN):
    if len(a_s) == 1:
        return _dg(a_s[0], b_s[0], dims)
    (ah, al), (bh, bl) = a_s, b_s
    ca, cb = dims[0][0][0], dims[0][1][0]
    return _dg(jnp.concatenate([ah, ah, al], axis=ca), jnp.concatenate([bh, bl, bh], axis=cb), dims)


def _dg3(a, b, dims=NN):
    return _dgs(_split(a), _split(b), dims)


def _sigmoid(x):
    return 1.0 / (1.0 + jnp.exp(-x))


def _softplus(x):
    return jnp.maximum(x, 0.0) + jnp.log(1.0 + jnp.exp(-jnp.abs(x)))


def _seg64_sum(x):
    rows, width = x.shape
    lo = _iota((rows, LANES), 1) < HEAD_DIM
    outs = []
    for c in range(width // LANES):
        blk = x[:, c * LANES:(c + 1) * LANES]
        s_lo = jnp.sum(jnp.where(lo, blk, 0.0), axis=-1, keepdims=True)
        s_hi = jnp.sum(jnp.where(lo, 0.0, blk), axis=-1, keepdims=True)
        outs.append(jnp.where(lo, s_lo, s_hi))
    return outs[0] if len(outs) == 1 else jnp.concatenate(outs, axis=1)


def _layer_norm(x, w, b):
    mu = jnp.mean(x, axis=-1, keepdims=True)
    xc = x - mu
    var = jnp.mean(xc * xc, axis=-1, keepdims=True)
    return xc * lax.rsqrt(var + LN_EPS) * w + b


def _mod_row(i):
    return jnp.where(i < NCB_D, 0, 1 + (i - NCB_D) // LAT_BLKS_D)


def _layer_spec(l, tail, n_grid, single=False):
    idx = lambda *g: (l,) + (0,) * len(tail)
    del n_grid
    if single:
        return pl.BlockSpec((None,) + tuple(tail), idx, pipeline_mode=pl.Buffered(1))
    return pl.BlockSpec((None,) + tuple(tail), idx)


ANY_SPEC = pl.BlockSpec(memory_space=pl.ANY)


def _ctx_spec(w):
    return pl.BlockSpec((TD, w), lambda i: (jnp.minimum(i, NCB_D - 1), 0))


def _lat_spec(w):
    return pl.BlockSpec((TD, w), lambda i: (jnp.maximum(i - NCB_D, 0), 0))


def _pick(i, ctx_ref, lat_ref, rows=slice(None)):
    return jnp.where(i < NCB_D, ctx_ref[rows, :], lat_ref[rows, :])


N_COND = 1 + DEC_BATCH


def _mod_kernel(c_ref, w_ref, b_ref, o_ref, sb_scr):
    @pl.when((pl.program_id(0) == 0) & (pl.program_id(1) == 0))
    def _():
        c = c_ref[...]
        s = c * _sigmoid(c)
        for r in range(N_COND):
            sb_scr[r] = jnp.broadcast_to(s[r:r + 1, :], (LANES, D_MODEL)).T

    def body(kc, accs):
        rows = pl.ds(pl.multiple_of(kc * 8, 8), 8)
        wk = w_ref[rows, :]
        return tuple(a + wk * jnp.concatenate([sb_scr[r, rows, :]] * (D_MODEL // LANES), axis=1)
                     for r, a in enumerate(accs))

    accs = lax.fori_loop(0, D_MODEL // 8, body, tuple(jnp.zeros((8, D_MODEL), F32) for _ in range(N_COND)),
                         unroll=8)
    rowid = _iota((MOD_ROWS, D_MODEL), 0)
    out = jnp.zeros((MOD_ROWS, D_MODEL), F32)
    for r, a in enumerate(accs):
        out = jnp.where(rowid == r, jnp.sum(a, axis=0, keepdims=True), out)
    o_ref[...] = out + b_ref[...]


def _modulation(cc, w_mod, b_mod):
    return pl.pallas_call(
        _mod_kernel,
        grid=(DEPTH, 6),
        in_specs=[
            pl.BlockSpec((MOD_ROWS, D_MODEL), lambda l, j: (0, 0)),
            pl.BlockSpec((None, D_MODEL, D_MODEL), lambda l, j: (l, 0, j)),
            pl.BlockSpec((None, 1, D_MODEL), lambda l, j: (l, 0, j)),
        ],
        out_specs=pl.BlockSpec((None, MOD_ROWS, D_MODEL), lambda l, j: (l, 0, j)),
        out_shape=jax.ShapeDtypeStruct((DEPTH, MOD_ROWS, 6 * D_MODEL), F32),
        scratch_shapes=[pltpu.VMEM((N_COND, D_MODEL, LANES), F32)],
        compiler_params=_cparams(2),
        name="modulation",
    )(cc, w_mod, b_mod.reshape(DEPTH, 1, 6 * D_MODEL))


def _bias_kernel(rpb_ref, o_ref):
    q = _iota((GRID_W, LANES), 0)
    x = _iota((GRID_W, LANES), 1)
    c = x % GRID_W
    right = x >= GRID_W
    dc = jnp.clip(c - q, 1 - NA_COLS, NA_COLS - 1) + NA_COLS - 1
    c0 = jnp.clip(q - NA_COLS // 2, 0, GRID_W - NA_COLS)
    in_win = (c >= c0) & (c < c0 + NA_COLS)

    def body(t, carry):
        h = t // (N_DR - 1)
        dr = t % (N_DR - 1)
        rows = [jnp.broadcast_to(rpb_ref[pl.ds(h * N_DR + dr + k, 1), :], (GRID_W, LANES)) for k in range(2)]
        lo, hi = (jnp.take_along_axis(r, dc, axis=1) for r in rows)
        o_ref[h, dr] = jnp.where(in_win, jnp.where(right, hi, lo), NEG_INF)
        return carry

    lax.fori_loop(0, H_NA * (N_DR - 1), body, 0, unroll=N_DR - 1)


RPB_ROWS = -(-H_NA * N_DR // 8) * 8


def _bias_tables(na_rpb):
    rpb = jnp.pad(na_rpb.reshape(DEPTH, H_NA * N_DR, N_DC), ((0, 0), (0, RPB_ROWS - H_NA * N_DR), (0, LANES - N_DC)))
    return pl.pallas_call(
        _bias_kernel,
        grid=(DEPTH,),
        in_specs=[pl.BlockSpec((None, RPB_ROWS, LANES), lambda l: (l, 0, 0))],
        out_specs=pl.BlockSpec((None, H_NA, N_DR - 1, GRID_W, LANES), lambda l: (l, 0, 0, 0, 0)),
        out_shape=jax.ShapeDtypeStruct((DEPTH, H_NA, N_DR - 1, GRID_W, LANES), F32),
        compiler_params=_cparams(1),
        name="na_bias_tables",
    )(rpb)


def _rope(x, cos, sin):
    k = x.shape[1] // LANES
    cosf = cos if k == 1 else jnp.concatenate([cos] * k, axis=1)
    sinf = sin if k == 1 else jnp.concatenate([sin] * k, axis=1)
    first = (_iota(x.shape, 1) % (2 * ROPE_FREQ)) < ROPE_FREQ
    partner = jnp.where(first, pltpu.roll(x, x.shape[1] - ROPE_FREQ, axis=1), pltpu.roll(x, ROPE_FREQ, axis=1))
    return x * cosf + partner * sinf


def _rwkv_features(x, prev_row, next_row, conv_ref, w0_ref, w2_ref, a0_ref, a2_ref, g2_ref, kk_ref, ka_ref, rk_ref):
    rows = _iota(x.shape, 0)
    x_prev = jnp.where(rows == 0, prev_row, pltpu.roll(x, 1, axis=0))
    x_next = jnp.where(rows == SEQ - 1, next_row, pltpu.roll(x, SEQ - 1, axis=0))
    f = x_prev * conv_ref[0:1, :] + x * conv_ref[1:2, :] + x_next * conv_ref[2:3, :]
    o1, o2, o3 = W_RWKV, 2 * W_RWKV, 3 * W_RWKV
    o4 = o3 + 2 * LORA_W
    o5 = o4 + 2 * LORA_A
    r, k, v = f[:, :o1], f[:, o1:o2], f[:, o2:o3]
    wd, ad, gd = f[:, o3:o4], f[:, o4:o5], f[:, o5:]
    lora = lambda y, w_ref: _dgs(_split(y), (w_ref[0], w_ref[1]))
    log_w = -_softplus(-(w0_ref[...] + lora(jnp.tanh(wd), w2_ref))) - 0.5
    a = _sigmoid(a0_ref[...] + lora(ad, a2_ref))
    g = lora(_sigmoid(gd), g2_ref)
    kk = k * kk_ref[...]
    kap = kk / jnp.maximum(jnp.sqrt(_seg64_sum(kk * kk)), 1e-12)
    per_dir = []
    kd_sum = jnp.zeros_like(k)
    for d in range(2):
        a_d = a[:, d * W_RWKV:(d + 1) * W_RWKV]
        kd = k * (1.0 + (a_d - 1.0) * ka_ref[...])
        kd_sum = kd_sum + kd
        per_dir.append((-jnp.exp(log_w[:, d * W_RWKV:(d + 1) * W_RWKV]), a_d * kap, kd))
    bonus = _seg64_sum(r * kd_sum * rk_ref[...]) * v
    return r, kap, v, per_dir, g, bonus


HALO = 8


def _inproj_kernel(xc_ref, xl_ref, xp_ref, xn_ref, mod_ref, w_ref, qn_ref, kn_ref, cos_ref, sin_ref,
                   conv_ref, w0_ref, w2_ref, a0_ref, a2_ref, g2_ref, kk_ref, ka_ref, rk_ref,
                   c0_ref, c1_ref, c2_ref, c3_ref,
                   naq_ref, nak_ref, nav_ref, gq_ref, gk_ref, gv_ref,
                   r_ref, kap_ref, v_ref, lw_ref, ah_ref, kd_ref, g_ref, bonus_ref,
                   cnak_ref, cnav_ref, cgk_ref, cgv_ref):
    del c0_ref, c1_ref, c2_ref, c3_ref
    i = pl.program_id(0)
    row = _mod_row(i)
    shift1 = mod_ref[pl.ds(row, 1), 0:D_MODEL]
    scale1 = mod_ref[pl.ds(row, 1), D_MODEL:2 * D_MODEL]
    modulate = lambda x: (x * (1.0 + scale1) + shift1).astype(BF16)
    lat = i >= NCB_D
    pos = (i - NCB_D) % LAT_BLKS_D
    in_seq = jnp.where(lat, 1.0, 0.0)
    has_prev = jnp.where(lat & (pos != 0), 1.0, 0.0)
    has_next = jnp.where(lat & (pos != LAT_BLKS_D - 1), 1.0, 0.0)
    xm = [modulate(_pick(i, xc_ref, xl_ref, rs)) for rs in SUB_ROWS]
    xm[0] = jnp.concatenate([xm[0], modulate(xp_ref[...]), modulate(xn_ref[...])], axis=0)
    projs = [_dg(x, w_ref[...]) for x in xm]
    halo = projs[0][SEQ:, :RWKV_IN]
    projs[0] = projs[0][:SEQ]
    feats = [proj[:, :RWKV_IN] for proj in projs]
    o_na = RWKV_IN
    o_g = RWKV_IN + NA_IN
    new_kv = []
    n_sub = len(SUB_ROWS)
    for h, (rs, proj) in enumerate(zip(SUB_ROWS, projs)):
        prev_row = halo[HALO - 1:HALO] * has_prev if h == 0 else feats[h - 1][SEQ - 1:SEQ] * in_seq
        next_row = halo[HALO:HALO + 1] * has_next if h == n_sub - 1 else feats[h + 1][0:1] * in_seq
        r, kap, v, per_dir, g, bonus = _rwkv_features(
            feats[h], prev_row, next_row, conv_ref, w0_ref, w2_ref, a0_ref, a2_ref, g2_ref, kk_ref, ka_ref, rk_ref)
        r_ref[rs, :] = r.astype(BF16)
        kap_ref[rs, :] = kap.astype(BF16)
        v_ref[rs, :] = v.astype(BF16)
        g_ref[rs, :] = g
        bonus_ref[rs, :] = bonus
        for d, (lw, ah, kd) in enumerate(per_dir):
            lw_ref[d, rs, :] = lw
            ah_ref[d, rs, :] = ah.astype(BF16)
            kd_ref[d, rs, :] = kd.astype(BF16)
        naq_ref[rs, :] = proj[:, o_na:o_na + W_NA] * ATTN_SCALE
        nak = proj[:, o_na + W_NA:o_na + 2 * W_NA]
        nav = proj[:, o_na + 2 * W_NA:o_na + 3 * W_NA]
        q = proj[:, o_g:o_g + W_GQA]
        k = proj[:, o_g + W_GQA:o_g + W_GQA + W_GQA_KV]
        gv = proj[:, o_g + W_GQA + W_GQA_KV:]
        cos = cos_ref[rs, :]
        sin = sin_ref[rs, :]
        q = q * lax.rsqrt(_seg64_sum(q * q) * (1.0 / HEAD_DIM) + RMS_EPS) * qn_ref[...]
        k = k * lax.rsqrt(_seg64_sum(k * k) * (1.0 / HEAD_DIM) + RMS_EPS) * kn_ref[...]
        gk = _rope(k, cos, sin)
        gq_ref[rs, :] = _rope(q, cos, sin) * ATTN_SCALE
        nak_ref[rs, :] = nak
        nav_ref[rs, :] = nav
        gk_ref[rs, :] = gk
        gv_ref[rs, :] = gv
        new_kv.append((nak, nav, gk, gv))

    @pl.when(i < NCB_D)
    def _():
        for h, vals in enumerate(new_kv):
            for ref, val in zip((cnak_ref, cnav_ref, cgk_ref, cgv_ref), vals):
                ref[h] = val


def _inproj(l, x_ctx, x_lat, mod_all, w_in_bf, qn, kn, cos_tab, sin_tab, rwkv_params, caches):
    tab_idx = lambda i: (jnp.where(i < NCB_D, 0, 1 + (i - NCB_D) % LAT_BLKS_D), 0)
    lat_blk = lambda i: jnp.maximum(i - NCB_D, 0)
    halo_blocks = TD // HALO
    widths = (W_NA, W_NA, W_NA, W_GQA, W_GQA_KV, W_GQA_KV, W_RWKV, W_RWKV, W_RWKV)
    cache_w = (W_NA, W_NA, W_GQA_KV, W_GQA_KV)
    tok = lambda w: pl.BlockSpec((TD, w), lambda i: (i, 0))
    tok2 = pl.BlockSpec((2, TD, W_RWKV), lambda i: (0, i, 0))
    cache_spec = lambda w: pl.BlockSpec((TD // SEQ, None, SEQ, w), lambda i: (jnp.minimum(i, NCB_D - 1), l, 0, 0))
    rwkv_tails = ((3, RWKV_IN), (1, 2 * W_RWKV), (2, 2 * LORA_W, 2 * W_RWKV), (1, 2 * W_RWKV),
                  (2, 2 * LORA_A, 2 * W_RWKV), (2, LORA_G, W_RWKV), (1, W_RWKV), (1, W_RWKV), (1, W_RWKV))
    in_specs = [
        _ctx_spec(D_MODEL), _lat_spec(D_MODEL),
        pl.BlockSpec((HALO, D_MODEL), lambda i: (jnp.maximum(lat_blk(i) * halo_blocks - 1, 0), 0)),
        pl.BlockSpec((HALO, D_MODEL), lambda i: (jnp.minimum((lat_blk(i) + 1) * halo_blocks, N_LAT // HALO - 1), 0)),
        _layer_spec(l, (MOD_ROWS, 6 * D_MODEL), 1),
        _layer_spec(l, (D_MODEL, D_IN), 1),
        _layer_spec(l, (1, W_GQA), 1),
        _layer_spec(l, (1, W_GQA_KV), 1),
        pl.BlockSpec((TD, LANES), tab_idx),
        pl.BlockSpec((TD, LANES), tab_idx),
    ] + [_layer_spec(l, t, 1) for t in rwkv_tails]
    n_in = len(in_specs)
    out_specs = [tok(w) for w in widths] + [tok2] * 3 + [tok(W_RWKV)] * 2
    n_attn = 6
    out_shape = ([jax.ShapeDtypeStruct((N_TOK, w), F32 if k < n_attn else BF16) for k, w in enumerate(widths)]
                 + [jax.ShapeDtypeStruct((2, N_TOK, W_RWKV), dt) for dt in (F32, BF16, BF16)]
                 + [jax.ShapeDtypeStruct((N_TOK, W_RWKV), F32)] * 2)
    return pl.pallas_call(
        _inproj_kernel,
        grid=(NBLK_D,),
        in_specs=in_specs + [ANY_SPEC] * 4,
        out_specs=out_specs + [cache_spec(w) for w in cache_w],
        out_shape=out_shape + [jax.ShapeDtypeStruct((BATCH, DEPTH, SEQ, w), F32) for w in cache_w],
        input_output_aliases={n_in + j: len(out_specs) + j for j in range(4)},
        compiler_params=_cparams(1),
        name="inproj",
    )(x_ctx, x_lat, x_lat, x_lat, mod_all, w_in_bf, qn, kn, cos_tab, sin_tab, *rwkv_params, *caches)


def _scan_kernel(r0_ref, kap0_ref, v0_ref, r1_ref, kap1_ref, v1_ref,
                 lw0_ref, ah0_ref, kd0_ref, lw1_ref, ah1_ref, kd1_ref, s00_ref, s01_ref, sf0_in_ref, sf1_in_ref,
                 o0_ref, o1_ref, sf0_ref, sf1_ref,
                 s_scr, sdone_scr, rt_scr, kt_scr, kdt_scr, at_scr, cum_scr, m_scr, n_scr, q_scr, oo_scr):
    del sf0_in_ref, sf1_in_ref
    j = pl.program_id(0)
    par = j % 2
    jb = jnp.maximum(j - 1, 0)
    upd_blk = (jb, NBLK_S - 1 - jb)
    is_ctx = tuple(b < NCB_S for b in upd_blk)
    first_pos = (0, LAT_BLKS_S - 1)
    r_refs, kap_refs, v_refs = (r0_ref, r1_ref), (kap0_ref, kap1_ref), (v0_ref, v1_ref)
    lw_refs, ah_refs, kd_refs = (lw0_ref, lw1_ref), (ah0_ref, ah1_ref), (kd0_ref, kd1_ref)
    s0_refs, o_refs, sf_refs = (s00_ref, s01_ref), (o0_ref, o1_ref), (sf0_ref, sf1_ref)

    @pl.when(j == 0)
    def _():
        for ref in (s_scr, sdone_scr, m_scr, n_scr, q_scr, oo_scr):
            ref[...] = jnp.zeros_like(ref)

    for dn in range(2):
        blk = upd_blk[dn]

        @pl.when((j > 0) & is_ctx[dn])
        def _():
            s_scr[dn] = jnp.zeros(s_scr.shape[1:], F32)

        @pl.when((j > 0) & (blk >= NCB_S) & ((blk - NCB_S) % LAT_BLKS_S == first_pos[dn]))
        def _():
            s_scr[dn] = s0_refs[dn][...]

    def seq_slot(dn, n):
        return n if dn == 0 else SEQS_S - 1 - n

    def update(c):
        for dn in range(2):
            ce = c if dn == 0 else NCS - 1 - c
            rows = slice(ce * CHUNK, (ce + 1) * CHUNK)
            for p in range(N_PAIR):
                s = s_scr[dn, p]
                if c > 0 and c % SEQ_CHUNKS == 0:
                    sdone_scr[dn, seq_slot(dn, c // SEQ_CHUNKS - 1), p] = s
                    s = jnp.where(is_ctx[dn], 0.0, s)
                s2 = jnp.concatenate(_split(s), axis=1)
                q = q_scr[1 - par, dn, ce, p]
                m = m_scr[1 - par, dn, ce, p]
                o_refs[dn][rows, p * LANES:(p + 1) * LANES] = (
                    _dg(jnp.concatenate([q, q], axis=1), s2, NT) + oo_scr[1 - par, dn, ce, p])
                s_scr[dn, p] = _dg(s2, jnp.concatenate([m, m], axis=0)) + n_scr[1 - par, dn, ce, p]

    pending = list(range(NCS))

    def next_update():
        if pending:
            update(pending.pop(0))

    next_update()
    rr = _iota((TM, TM), 0)
    cc = _iota((TM, TM), 1)
    same_chunk = (rr // CHUNK) == (cc // CHUNK)
    for dn in range(2):
        order = (rr >= cc) if dn == 0 else (rr <= cc)
        tri = jnp.where(same_chunk & order, 1.0, 0.0).astype(BF16)
        for h in range(TS // TM):
            rs = slice(h * TM, (h + 1) * TM)
            lw = lw_refs[dn][rs, :]
            lw_hi = lw.astype(BF16)
            lw_r = lw - lw_hi.astype(F32)
            lw_mid = lw_r.astype(BF16)
            lw_lo = (lw_r - lw_mid.astype(F32)).astype(BF16)
            cum = _dg(jnp.concatenate([tri, tri, tri], axis=1), jnp.concatenate([lw_hi, lw_mid, lw_lo], axis=0))
            e_neg = jnp.exp(-cum)
            rt_scr[dn, rs, :] = r_refs[dn][rs, :] * jnp.exp(cum)
            kt_scr[dn, rs, :] = kap_refs[dn][rs, :] * jnp.exp(cum - lw)
            kdt_scr[dn, rs, :] = kd_refs[dn][rs, :] * e_neg
            at_scr[dn, rs, :] = ah_refs[dn][rs, :] * e_neg
            cum_scr[dn, rs, :] = cum
        if dn == 0:
            next_update()

    row = _iota((CHUNK, LANES), 0)
    col = _iota((CHUNK, LANES), 1) % CHUNK
    left = _iota((CHUNK, LANES), 1) < CHUNK
    incl = (row >= col, row <= col)
    strict = (row > col, row < col)
    eye = jnp.where(row == col, 1.0, 0.0)
    r2 = _iota((LANES, LANES), 0)
    c2 = _iota((LANES, LANES), 1)
    bd_mask = (r2 // CHUNK) == (c2 // CHUNK)
    eye2 = jnp.where(r2 == c2, 1.0, 0.0)

    zero_bf = jnp.zeros((CHUNK, LANES), BF16)
    split = lambda x: (x.astype(BF16),)

    def bd(xs):
        return tuple(jnp.concatenate([jnp.where(left, y, zero_bf), jnp.where(left, zero_bf, y)], axis=0)
                     for y in xs)

    def cat(xs, ys, axis):
        return tuple(jnp.concatenate([x, y], axis=axis) for x, y in zip(xs, ys))

    def mm(a_list, b_list):
        return [_dgs(a, bd(b)) for a, b in zip(a_list, b_list)]

    units = [(dn, c, p) for c in range(NCS) for dn in range(2) for p in range(N_PAIR)]
    each = lambda f, *lists: [f(*args) for args in zip(*lists)]

    def load(ref):
        return [ref[dn, c * CHUNK:(c + 1) * CHUNK, p * LANES:(p + 1) * LANES] for dn, c, p in units]

    rt = load(rt_scr)
    v = [v_refs[dn][c * CHUNK:(c + 1) * CHUNK, p * LANES:(p + 1) * LANES] for dn, c, p in units]
    kt_s, rt_s, kdt_s, at_s, v_s = (each(split, x) for x in (load(kt_scr), rt, load(kdt_scr), load(at_scr), v))
    gam = [jnp.exp(cum_scr[dn, (c + 1) * CHUNK - 1:(c + 1) * CHUNK, p * LANES:(p + 1) * LANES] if dn == 0 else
                   cum_scr[dn, c * CHUNK:c * CHUNK + 1, p * LANES:(p + 1) * LANES]) for dn, c, p in units]
    gram = each(lambda k, r, a, kd: _dgs(cat(k, r, 0), cat(bd(a), bd(kd), 0), NT), kt_s, rt_s, at_s, kdt_s)
    dirs = [u[0] for u in units]
    la = [jnp.where(strict[dn], g[0:CHUNK, 0:LANES], 0.0) for dn, g in zip(dirs, gram)]
    lk_s = [split(jnp.where(strict[dn], g[0:CHUNK, LANES:], 0.0)) for dn, g in zip(dirs, gram)]
    ra_s = [split(jnp.where(incl[dn], g[CHUNK:, 0:LANES], 0.0)) for dn, g in zip(dirs, gram)]
    rk_s = [split(jnp.where(incl[dn], g[CHUNK:, LANES:], 0.0)) for dn, g in zip(dirs, gram)]
    next_update()
    lrv = mm(each(lambda lk, rk: cat(lk, rk, 0), lk_s, rk_s), v_s)
    next_update()
    b = 8
    l8 = [jnp.where((row // b) == (col // b), x, 0.0) for x in la]
    l8_s = each(split, l8)
    l8_2 = mm(l8_s, l8_s)
    next_update()
    l8_2s = each(split, l8_2)
    l8_4 = mm(l8_2s, l8_2s)
    p1 = mm([split(eye - x) for x in l8], [split(eye + y) for y in l8_2])
    next_update()
    t = mm(each(split, p1), [split(eye + y) for y in l8_4])
    while b < CHUNK:
        next_update()
        offd = ((row // (2 * b)) == (col // (2 * b))) & ((row // b) != (col // b))
        t_s = each(split, t)
        x = mm(t_s, [split(jnp.where(offd, y, 0.0)) for y in la])
        t = each(lambda tt, z: tt - z, t, mm(each(split, x), t_s))
        b *= 2
    assert not pending, "more chunk updates than stages to place them between"
    tx = each(lambda tt, k, y: _dgs(split(tt), cat(bd(k), bd(split(y[0:CHUNK])), 1)), t, kt_s, lrv)
    khat_s = [split(y[:, 0:LANES]) for y in tx]
    w1_s = [split(y[:, LANES:]) for y in tx]
    rx = each(lambda r, k, w: _dgs(r, cat(bd(k), bd(w), 1)), ra_s, khat_s, w1_s)
    mk = each(lambda k, a: _dgs(k, a, TN), khat_s, at_s)
    nk = each(lambda vv, w, kd, a: _dgs(cat(vv, w, 0), cat(kd, tuple(-y for y in a), 0), TN),
              v_s, w1_s, kdt_s, at_s)
    for i, (dn, c, p) in enumerate(units):
        q_scr[par, dn, c, p] = (rt[i] - rx[i][:, 0:LANES]).astype(BF16)
        oo_scr[par, dn, c, p] = lrv[i][CHUNK:] - rx[i][:, LANES:]
        m_scr[par, dn, c, p] = ((eye2 - jnp.where(bd_mask, mk[i], 0.0)) * gam[i]).astype(BF16)
        n_scr[par, dn, c, p] = jnp.where(bd_mask, nk[i], 0.0) * gam[i]

    for dn in range(2):
        @pl.when((j > 0) & is_ctx[dn])
        def _():
            for n in range(SEQS_S - 1):
                sf_refs[dn][seq_slot(dn, n)] = sdone_scr[dn, seq_slot(dn, n)]
            sf_refs[dn][seq_slot(dn, SEQS_S - 1)] = s_scr[dn]


def _rwkv_scan(l, r, kap, v, lw, ah, kd, s0_lat, s_fin):
    def stage_blk(dn, j):
        jj = jnp.minimum(j, NBLK_S - 1)
        return jj if dn == 0 else NBLK_S - 1 - jj

    def update_blk(dn, j):
        jj = jnp.maximum(j - 1, 0)
        return jj if dn == 0 else NBLK_S - 1 - jj

    pair = (N_PAIR, LANES, LANES)
    tok = lambda dn: pl.BlockSpec((TS, W_RWKV), lambda j: (stage_blk(dn, j), 0))
    tok2 = lambda dn: pl.BlockSpec((None, TS, W_RWKV), lambda j: (dn, stage_blk(dn, j), 0))
    s0_spec = lambda dn: pl.BlockSpec(
        (None, None, None) + pair,
        lambda j: (jnp.maximum(update_blk(dn, j) - NCB_S, 0) // LAT_BLKS_S, l, dn, 0, 0, 0))
    sf_spec = lambda dn: pl.BlockSpec(
        (SEQS_S, None) + pair, lambda j: (jnp.minimum(update_blk(dn, j), NCB_S - 1), l, 0, 0, 0))
    o_spec = lambda dn: pl.BlockSpec((TS, W_RWKV), lambda j: (update_blk(dn, j), 0))
    dir_scr = lambda *shape, dtype=F32: pltpu.VMEM((2,) + shape, dtype)
    stage_scr = (2, NCS, N_PAIR)
    return pl.pallas_call(
        _scan_kernel,
        grid=(NBLK_S + 1,),
        in_specs=[tok(0), tok(0), tok(0), tok(1), tok(1), tok(1),
                  tok2(0), tok2(0), tok2(0), tok2(1), tok2(1), tok2(1),
                  s0_spec(0), s0_spec(1), ANY_SPEC, ANY_SPEC],
        out_specs=[o_spec(0), o_spec(1), sf_spec(0), sf_spec(1)],
        out_shape=[jax.ShapeDtypeStruct((N_TOK, W_RWKV), F32)] * 2
        + [jax.ShapeDtypeStruct((BATCH, DEPTH) + pair, F32)] * 2,
        scratch_shapes=[dir_scr(*pair), dir_scr(SEQS_S, *pair)] + [dir_scr(TS, W_RWKV)] * 5
        + [dir_scr(*stage_scr, LANES, LANES, dtype=BF16), dir_scr(*stage_scr, LANES, LANES),
           dir_scr(*stage_scr, CHUNK, LANES, dtype=BF16), dir_scr(*stage_scr, CHUNK, LANES)],
        input_output_aliases={14: 2, 15: 3},
        compiler_params=_cparams(1),
        name="rwkv_scan",
    )(r, kap, v, r, kap, v, lw, ah, kd, lw, ah, kd, s0_lat, s0_lat, s_fin[0], s_fin[1])


def _attend(groups):
    lhs = []
    for q_cols, _, _, _ in groups:
        left = _iota(q_cols[0].shape, 1) < HEAD_DIM
        parts = []
        for qc in q_cols:
            parts += [jnp.where(left, qc, 0.0), jnp.where(left, 0.0, qc)]
        lhs.append(jnp.concatenate(parts, axis=0).astype(BF16))
    s = [_dg(x, g[1], NT) for x, g in zip(lhs, groups)]
    s = [x if g[3] is None else x + g[3] for x, g in zip(s, groups)]
    p = [jnp.exp(x - jnp.max(x, axis=-1, keepdims=True)) for x in s]
    inv = [1.0 / jnp.sum(x, axis=-1, keepdims=True) for x in p]
    o = [_dg(x.astype(BF16), g[2]) * y for x, y, g in zip(p, inv, groups)]
    outs = []
    for x, (q_cols, _, _, _) in zip(o, groups):
        rows = q_cols[0].shape[0]
        left = _iota(q_cols[0].shape, 1) < HEAD_DIM
        outs.append([jnp.where(left, x[2 * j * rows:(2 * j + 1) * rows], x[(2 * j + 1) * rows:(2 * j + 2) * rows])
                     for j in range(len(q_cols))])
    return outs


def _cols(x):
    return [x[:, c * LANES:(c + 1) * LANES] for c in range(x.shape[1] // LANES)]


def _gqa_groups(q, k, v):
    left = _iota(k.shape, 1) < HEAD_DIM
    k_sw = pltpu.roll(k, HEAD_DIM, axis=1)
    v_sw = pltpu.roll(v, HEAD_DIM, axis=1)
    q_cols = _cols(q)
    groups = []
    for g in range(H_GQA_KV):
        k2 = jnp.where(left, k, k_sw) if g == 0 else jnp.where(left, k_sw, k)
        v2 = jnp.where(left, v, v_sw) if g == 0 else jnp.where(left, v_sw, v)
        groups.append((q_cols[2 * g:2 * g + 2], k2.astype(BF16), v2.astype(BF16), None))
    return groups


def _ctx_attn_kernel(naq_ref, nak_ref, nav_ref, gq_ref, gk_ref, gv_ref, ona_ref, og_ref):
    k_cols = _cols(nak_ref[...].astype(BF16))
    v_cols = _cols(nav_ref[...].astype(BF16))
    na_groups = [([qc], kc, vc, None) for qc, kc, vc in zip(_cols(naq_ref[...]), k_cols, v_cols)]
    outs = _attend(na_groups + _gqa_groups(gq_ref[...], gk_ref[...], gv_ref[...]))
    n_na = len(na_groups)
    ona_ref[...] = jnp.concatenate([o[0] for o in outs[:n_na]], axis=1)
    og_ref[...] = jnp.concatenate([c for o in outs[n_na:] for c in o], axis=1)


def _ctx_attention(naq, nak, nav, gq, gk, gv):
    spec = lambda w: pl.BlockSpec((SEQ, w), lambda b: (b, 0))
    return pl.pallas_call(
        _ctx_attn_kernel,
        grid=(BATCH,),
        in_specs=[spec(W_NA), spec(W_NA), spec(W_NA), spec(W_GQA), spec(W_GQA_KV), spec(W_GQA_KV)],
        out_specs=[spec(W_NA), spec(W_GQA)],
        out_shape=[jax.ShapeDtypeStruct((N_CTX, W_NA), F32), jax.ShapeDtypeStruct((N_CTX, W_GQA), F32)],
        compiler_params=_cparams(1),
        name="ctx_attention",
    )(naq, nak, nav, gq, gk, gv)


NA_STEP_ROWS = 2


def _lat_na_kernel(q_ref, k_ref, v_ref, kc_ref, vc_ref, tb_ref, o_ref):
    q = q_ref[...]
    groups = []
    for rr in range(NA_STEP_ROWS):
        r = pl.program_id(1) * NA_STEP_ROWS + rr
        r0 = jnp.clip(r - NA_ROWS // 2, 0, GRID_ROWS - NA_ROWS)
        band = pl.ds(pl.multiple_of(r0 * GRID_W, GRID_W), N_BAND)
        dr0 = r0 - r + NA_ROWS - 1
        for c, qc in enumerate(_cols(q[rr * GRID_W:(rr + 1) * GRID_W])):
            cols = slice(c * LANES, (c + 1) * LANES)
            k2 = jnp.concatenate([k_ref[band, cols], kc_ref[:, cols]], axis=0).astype(BF16)
            v2 = jnp.concatenate([v_ref[band, cols], vc_ref[:, cols]], axis=0).astype(BF16)
            bias = jnp.concatenate(
                [jnp.concatenate([tb_ref[2 * c + half, dr0 + jj] for jj in range(0, NA_ROWS, 2)], axis=1)
                 for half in range(2)], axis=0)
            bias = jnp.concatenate([bias, jnp.zeros((2 * GRID_W, PAST_LEN), F32)], axis=1)
            groups.append(([qc], k2, v2, bias))
    outs = [o[0] for o in _attend(groups)]
    n_col = W_NA // LANES
    o_ref[...] = jnp.concatenate(
        [jnp.concatenate(outs[rr * n_col:(rr + 1) * n_col], axis=1) for rr in range(NA_STEP_ROWS)], axis=0)


def _lat_na(l, naq, nak, nav, kc, vc, tb):
    rows = NA_STEP_ROWS * GRID_W
    steps = GRID_ROWS // NA_STEP_ROWS
    seq_blk0 = N_CTX // DEC_SEQ
    seq = pl.BlockSpec((DEC_SEQ, W_NA), lambda b, r: (seq_blk0 + b, 0))
    cache = pl.BlockSpec((None, None, PAST_LEN, W_NA), lambda b, r: (b, l, 0, 0))
    return pl.pallas_call(
        _lat_na_kernel,
        grid=(DEC_BATCH, steps),
        in_specs=[pl.BlockSpec((rows, W_NA), lambda b, r: (N_CTX // rows + b * steps + r, 0)),
                  seq, seq, cache, cache, _layer_spec(l, (H_NA, N_DR - 1, GRID_W, LANES), 2)],
        out_specs=pl.BlockSpec((rows, W_NA), lambda b, r: (b * steps + r, 0)),
        out_shape=jax.ShapeDtypeStruct((N_LAT, W_NA), F32),
        compiler_params=_cparams(2),
        name="latent_na",
    )(naq, nak, nav, kc, vc, tb)


def _lat_gqa_kernel(q_ref, k_ref, v_ref, kc_ref, vc_ref, o_ref):
    k = jnp.concatenate([kc_ref[...], k_ref[...]], axis=0)
    v = jnp.concatenate([vc_ref[...], v_ref[...]], axis=0)
    o_ref[...] = jnp.concatenate([c for o in _attend(_gqa_groups(q_ref[...], k, v)) for c in o], axis=1)


def _lat_gqa(l, gq, gk, gv, kc, vc):
    seq_blk0 = N_CTX // DEC_SEQ
    seq = pl.BlockSpec((DEC_SEQ, W_GQA_KV), lambda b, i: (seq_blk0 + b, 0))
    cache = pl.BlockSpec((None, None, PAST_LEN, W_GQA_KV), lambda b, i: (b, l, 0, 0))
    return pl.pallas_call(
        _lat_gqa_kernel,
        grid=(DEC_BATCH, LAT_BLKS),
        in_specs=[pl.BlockSpec((TM, W_GQA), lambda b, i: (NCB + b * LAT_BLKS + i, 0)), seq, seq, cache, cache],
        out_specs=pl.BlockSpec((TM, W_GQA), lambda b, i: (b * LAT_BLKS + i, 0)),
        out_shape=jax.ShapeDtypeStruct((N_LAT, W_GQA), F32),
        compiler_params=_cparams(2),
        name="latent_gqa",
    )(gq, gk, gv, kc, vc)


def _mix_ffn_kernel(xc_ref, xl_ref, mod_ref, of_ref, ob_ref, g_ref, bonus_ref, lnxw_ref, lnxb_ref,
                    onac_ref, onal_ref, ogc_ref, ogl_ref,
                    wout_ref, ln1w_ref, ln1b_ref, wfi_ref, wfo_ref, ln2w_ref, ln2b_ref, yc_ref, yl_ref):
    i = pl.program_id(0)
    row = _mod_row(i)
    mod = lambda n: mod_ref[pl.ds(row, 1), n * D_MODEL:(n + 1) * D_MODEL]
    each = lambda f, *lists: [f(*args) for args in zip(*lists)]

    def rwkv_out(rs):
        o = of_ref[rs, :] + ob_ref[rs, :]
        mu = _seg64_sum(o) * (1.0 / HEAD_DIM)
        oc = o - mu
        var = _seg64_sum(oc * oc) * (1.0 / HEAD_DIM)
        o_rwkv = (oc * lax.rsqrt(var + GN_EPS) * lnxw_ref[...] + lnxb_ref[...] + bonus_ref[rs, :]) * g_ref[rs, :]
        return jnp.concatenate([o_rwkv, _pick(i, onac_ref, onal_ref, rs), _pick(i, ogc_ref, ogl_ref, rs)],
                               axis=1).astype(BF16)

    mix = [_dg(rwkv_out(rs), wout_ref[...]) for rs in SUB_ROWS]
    x1 = [_layer_norm(DEEPNORM_ALPHA * _pick(i, xc_ref, xl_ref, rs) + mod(2) * m, ln1w_ref[...], ln1b_ref[...])
          for rs, m in zip(SUB_ROWS, mix)]
    x_in = [(x * (1.0 + mod(4)) + mod(3)).astype(BF16) for x in x1]
    ffn = [jnp.zeros_like(x) for x in x1]
    for lo in range(0, D_FF, FF_CHUNK):
        gate = [_dg(x, wfi_ref[:, lo:lo + FF_CHUNK]) for x in x_in]
        up = [_dg(x, wfi_ref[:, D_FF + lo:D_FF + lo + FF_CHUNK]) for x in x_in]
        act = each(lambda gt, u: (gt * _sigmoid(gt) * u).astype(BF16), gate, up)
        ffn = each(lambda f, a: f + _dg(a, wfo_ref[lo:lo + FF_CHUNK, :]), ffn, act)
    y = each(lambda x, f: _layer_norm(DEEPNORM_ALPHA * x + mod(5) * f, ln2w_ref[...], ln2b_ref[...]), x1, ffn)

    @pl.when(i < NCB_D)
    def _():
        for rs, yy in zip(SUB_ROWS, y):
            yc_ref[rs, :] = yy

    @pl.when(i >= NCB_D)
    def _():
        for rs, yy in zip(SUB_ROWS, y):
            yl_ref[rs, :] = yy


def _mix_ffn(l, x_ctx, x_lat, mod_all, o_fwd, o_bwd, g, bonus, lnx_w, lnx_b, o_na_ctx, o_na_lat, o_g_ctx, o_g_lat,
             w_out_bf, ln1_w, ln1_b, w_ffn_in_bf, w_ffn_out_bf, ln2_w, ln2_b):
    tok = lambda w: pl.BlockSpec((TD, w), lambda i: (i, 0))
    once = lambda *tail: _layer_spec(l, tail, 1, single=True)
    return pl.pallas_call(
        _mix_ffn_kernel,
        grid=(NBLK_D,),
        in_specs=[
            _ctx_spec(D_MODEL), _lat_spec(D_MODEL), once(MOD_ROWS, 6 * D_MODEL),
            tok(W_RWKV), tok(W_RWKV),
            tok(W_RWKV), tok(W_RWKV), once(1, W_RWKV), once(1, W_RWKV),
            _ctx_spec(W_NA), _lat_spec(W_NA), _ctx_spec(W_GQA), _lat_spec(W_GQA),
            once(D_MODEL, D_MODEL), once(1, D_MODEL), once(1, D_MODEL),
            once(D_MODEL, 2 * D_FF), once(D_FF, D_MODEL), once(1, D_MODEL), once(1, D_MODEL),
        ],
        out_specs=[_ctx_spec(D_MODEL), _lat_spec(D_MODEL)],
        out_shape=[jax.ShapeDtypeStruct((N_CTX, D_MODEL), F32), jax.ShapeDtypeStruct((N_LAT, D_MODEL), F32)],
        compiler_params=_cparams(1, VMEM_LIMIT_FFN),
        name="mix_ffn",
    )(x_ctx, x_lat, mod_all, o_fwd, o_bwd, g, bonus, lnx_w, lnx_b, o_na_ctx, o_na_lat, o_g_ctx, o_g_lat,
      w_out_bf, ln1_w, ln1_b, w_ffn_in_bf, w_ffn_out_bf, ln2_w, ln2_b)


def _rope_tables():
    t = jnp.arange(DEC_SEQ)
    inv = ROPE_BASE ** (-jnp.arange(ROPE_FREQ, dtype=F32) / ROPE_FREQ)
    ang_r = (t // GRID_W).astype(F32)[:, None] * inv
    ang_c = (t % GRID_W).astype(F32)[:, None] * inv
    cos = jnp.concatenate([jnp.cos(ang_r)] * 2 + [jnp.cos(ang_c)] * 2, axis=1)
    sin = jnp.concatenate([-jnp.sin(ang_r), jnp.sin(ang_r), -jnp.sin(ang_c), jnp.sin(ang_c)], axis=1)
    cos = jnp.concatenate([jnp.ones((TD, HEAD_DIM), F32), cos], axis=0)
    sin = jnp.concatenate([jnp.zeros((TD, HEAD_DIM), F32), sin], axis=0)
    return jnp.tile(cos, (1, LANES // HEAD_DIM)), jnp.tile(sin, (1, LANES // HEAD_DIM))


def _block_diag2(w):
    z = jnp.zeros_like(w[:, 0])
    return jnp.concatenate([jnp.concatenate([w[:, 0], z], axis=2), jnp.concatenate([z, w[:, 1]], axis=2)], axis=1)


def _hi_lo(w):
    return jnp.stack(_split(w), axis=1)


def _pair_states(s):
    lead = s.shape[:-3]
    s = s.reshape(lead + (N_PAIR, 2, HEAD_DIM, HEAD_DIM))
    z = jnp.zeros_like(s[..., 0, :, :])
    top = jnp.concatenate([s[..., 0, :, :], z], axis=-1)
    bot = jnp.concatenate([z, s[..., 1, :, :]], axis=-1)
    return jnp.concatenate([top, bot], axis=-2)


def _unpair_states(s):
    lead = s.shape[:-3]
    a = s[..., :HEAD_DIM, :HEAD_DIM]
    b = s[..., HEAD_DIM:, HEAD_DIM:]
    return jnp.stack([a, b], axis=-3).reshape(lead + (H_RWKV, HEAD_DIM, HEAD_DIM))


def kernel(x_prompt, x_sample, state_rwkv, cache_na_k, cache_na_v, cache_gqa_k, cache_gqa_v, c, c_ctx,
           w_mod, b_mod, w_in, rwkv_conv, rwkv_w0, rwkv_w2, rwkv_a0, rwkv_a2, rwkv_g2, rwkv_k_k, rwkv_k_a,
           rwkv_r_k, rwkv_lnx_w, rwkv_lnx_b, na_rpb, gqa_q_norm, gqa_k_norm, w_out, ln1_w, ln1_b,
           w_ffn_in, w_ffn_out, ln2_w, ln2_b):
    x_ctx, x_lat = x_prompt.reshape(N_CTX, D_MODEL), x_sample.reshape(N_LAT, D_MODEL)
    cc = jnp.concatenate([c_ctx[None], c, jnp.zeros((MOD_ROWS - 1 - DEC_BATCH, D_MODEL), F32)], axis=0)
    mod_all = _modulation(cc, w_mod, b_mod)
    tb_all = _bias_tables(na_rpb)
    cos_tab, sin_tab = _rope_tables()
    rows = lambda a: a.reshape(DEPTH, 1, -1)
    w_in_bf, w_out_bf = w_in.astype(BF16), w_out.astype(BF16)
    w_ffn_in_bf, w_ffn_out_bf = w_ffn_in.astype(BF16), w_ffn_out.astype(BF16)
    qn = jnp.tile(rows(gqa_q_norm), (1, 1, H_GQA))
    kn = jnp.tile(rows(gqa_k_norm), (1, 1, H_GQA_KV))
    rwkv_params = (rwkv_conv, rows(rwkv_w0), _hi_lo(_block_diag2(rwkv_w2)), rows(rwkv_a0),
                   _hi_lo(_block_diag2(rwkv_a2)), _hi_lo(rwkv_g2), rows(rwkv_k_k), rows(rwkv_k_a), rows(rwkv_r_k))
    s0_lat = _pair_states(state_rwkv)
    kc_na = cache_na_k.reshape(DEC_BATCH, DEPTH, PAST_LEN, W_NA)
    vc_na = cache_na_v.reshape(DEC_BATCH, DEPTH, PAST_LEN, W_NA)
    kc_g = cache_gqa_k.reshape(DEC_BATCH, DEPTH, PAST_LEN, W_GQA_KV)
    vc_g = cache_gqa_v.reshape(DEC_BATCH, DEPTH, PAST_LEN, W_GQA_KV)
    caches = [jnp.zeros((BATCH, DEPTH, SEQ, w), F32) for w in (W_NA, W_NA, W_GQA_KV, W_GQA_KV)]
    s_fin = [jnp.zeros((BATCH, DEPTH, N_PAIR, LANES, LANES), F32) for _ in range(2)]

    for l in range(DEPTH):
        naq, nak, nav, gq, gk, gv, r, kap, v, lw, ah, kd, g, bonus, *caches = _inproj(
            l, x_ctx, x_lat, mod_all, w_in_bf, qn, kn, cos_tab, sin_tab, rwkv_params, caches)
        o_fwd, o_bwd, *s_fin = _rwkv_scan(l, r, kap, v, lw, ah, kd, s0_lat, s_fin)
        o_na_ctx, o_g_ctx = _ctx_attention(naq, nak, nav, gq, gk, gv)
        o_na_lat = _lat_na(l, naq, nak, nav, kc_na, vc_na, tb_all)
        o_g_lat = _lat_gqa(l, gq, gk, gv, kc_g, vc_g)
        x_ctx, x_lat = _mix_ffn(
            l, x_ctx, x_lat, mod_all, o_fwd, o_bwd, g, bonus, rows(rwkv_lnx_w), rows(rwkv_lnx_b),
            o_na_ctx, o_na_lat, o_g_ctx, o_g_lat, w_out_bf, rows(ln1_w), rows(ln1_b),
            w_ffn_in_bf, w_ffn_out_bf, rows(ln2_w), rows(ln2_b))
    y_prompt = x_ctx.reshape(BATCH, SEQ, D_MODEL)
    y_sample = x_lat.reshape(DEC_BATCH, DEC_SEQ, D_MODEL)
    new_state = jnp.stack([_unpair_states(s) for s in s_fin], axis=2)
    return (y_prompt, y_sample, new_state,
            caches[0].reshape(BATCH, DEPTH, SEQ, H_NA, HEAD_DIM), caches[1].reshape(BATCH, DEPTH, SEQ, H_NA, HEAD_DIM),
            caches[2].reshape(BATCH, DEPTH, SEQ, H_GQA_KV, HEAD_DIM),
            caches[3].reshape(BATCH, DEPTH, SEQ, H_GQA_KV, HEAD_DIM))
```

```python
import jax
import jax.numpy as jnp
from jax import lax
from jax.experimental import pallas as pl
from jax.experimental.pallas import tpu as pltpu

F32 = jnp.float32
BF16 = jnp.bfloat16
HIGHEST = lax.Precision.HIGHEST

D_MODEL = 1024
BATCH = 16
SEQ = 256
DEPTH = 4
DEC_BATCH = 2
DEC_SEQ = 1024
PAST_LEN = 512
GRID_W = 64
GRID_ROWS = DEC_SEQ // GRID_W
HEAD_DIM = 64
H_RWKV = 4
H_NA = 4
H_GQA = 8
H_GQA_KV = 2
W_RWKV = H_RWKV * HEAD_DIM
W_NA = H_NA * HEAD_DIM
W_GQA = H_GQA * HEAD_DIM
W_GQA_KV = H_GQA_KV * HEAD_DIM
LORA_W = 64
LORA_A = 64
LORA_G = 128
RWKV_IN = 3 * W_RWKV + 2 * LORA_W + 2 * LORA_A + LORA_G
NA_IN = 3 * W_NA
GQA_IN = W_GQA + 2 * W_GQA_KV
D_IN = RWKV_IN + NA_IN + GQA_IN
NA_ROWS = 8
NA_COLS = 16
ROPE_BASE = 10000.0
ROPE_FREQ = HEAD_DIM // 4
D_FF = ((8 * D_MODEL + 3 * 256 - 1) // (3 * 256)) * 256
DEEPNORM_ALPHA = (2 * DEPTH) ** 0.25
LN_EPS = 1e-5
RMS_EPS = 1e-6
GN_EPS = 64e-5
NEG_INF = -1e30
ATTN_SCALE = HEAD_DIM ** -0.5

LANES = 128
TM = 256
N_CTX = BATCH * SEQ
N_LAT = DEC_BATCH * DEC_SEQ
N_TOK = N_CTX + N_LAT
NBLK = N_TOK // TM
NCB = N_CTX // TM
LAT_BLKS = DEC_SEQ // TM
CHUNK = 64
N_CHUNK = TM // CHUNK
TS = 512
NCS = TS // CHUNK
NBLK_S = N_TOK // TS
NCB_S = N_CTX // TS
LAT_BLKS_S = DEC_SEQ // TS
SEQS_S = TS // SEQ
SEQ_CHUNKS = SEQ // CHUNK
TD = 512
SUB_ROWS = [slice(h * SEQ, (h + 1) * SEQ) for h in range(TD // SEQ)]
NBLK_D = N_TOK // TD
NCB_D = N_CTX // TD
LAT_BLKS_D = DEC_SEQ // TD
FF_CHUNK = D_FF
N_PAIR = W_RWKV // LANES
MOD_ROWS = 8
VMEM_LIMIT = 48 * 1024 * 1024
VMEM_LIMIT_FFN = 56 * 1024 * 1024
N_DR = 2 * NA_ROWS - 1
N_DC = 2 * NA_COLS - 1
N_BAND = NA_ROWS * GRID_W


def _cparams(n_grid, vmem_limit=VMEM_LIMIT):
    return pltpu.CompilerParams(dimension_semantics=("arbitrary",) * n_grid, vmem_limit_bytes=vmem_limit)


def _iota(shape, dim):
    return lax.broadcasted_iota(jnp.int32, shape, dim)


NN = (((1,), (0,)), ((), ()))
NT = (((1,), (1,)), ((), ()))
TN = (((0,), (0,)), ((), ()))


def _dg(a, b, dims=NN, precision=None):
    return lax.dot_general(a, b, dims, preferred_element_type=F32, precision=precision)


def _split(x):
    hi = x.astype(BF16)
    return hi, (x - hi.astype(F32)).astype(BF16)


def _dgs(a_s, b_s, dims=NN):
    if len(a_s) == 1:
        return _dg(a_s[0], b_s[0], dims)
    (ah, al), (bh, bl) = a_s, b_s
    ca, cb = dims[0][0][0], dims[0][1][0]
    return _dg(jnp.concatenate([ah, ah, al], axis=ca), jnp.concatenate([bh, bl, bh], axis=cb), dims)


def _dg3(a, b, dims=NN):
    return _dgs(_split(a), _split(b), dims)


def _sigmoid(x):
    return 1.0 / (1.0 + jnp.exp(-x))


def _softplus(x):
    return jnp.maximum(x, 0.0) + jnp.log(1.0 + jnp.exp(-jnp.abs(x)))


def _seg64_sum(x):
    rows, width = x.shape
    lo = _iota((rows, LANES), 1) < HEAD_DIM
    outs = []
    for c in range(width // LANES):
        blk = x[:, c * LANES:(c + 1) * LANES]
        s_lo = jnp.sum(jnp.where(lo, blk, 0.0), axis=-1, keepdims=True)
        s_hi = jnp.sum(jnp.where(lo, 0.0, blk), axis=-1, keepdims=True)
        outs.append(jnp.where(lo, s_lo, s_hi))
    return outs[0] if len(outs) == 1 else jnp.concatenate(outs, axis=1)


def _layer_norm(x, w, b):
    mu = jnp.mean(x, axis=-1, keepdims=True)
    xc = x - mu
    var = jnp.mean(xc * xc, axis=-1, keepdims=True)
    return xc * lax.rsqrt(var + LN_EPS) * w + b


def _mod_row(i):
    return jnp.where(i < NCB_D, 0, 1 + (i - NCB_D) // LAT_BLKS_D)


def _layer_spec(l, tail, n_grid, single=False):
    idx = lambda *g: (l,) + (0,) * len(tail)
    del n_grid
    if single:
        return pl.BlockSpec((None,) + tuple(tail), idx, pipeline_mode=pl.Buffered(1))
    return pl.BlockSpec((None,) + tuple(tail), idx)


ANY_SPEC = pl.BlockSpec(memory_space=pl.ANY)


def _ctx_spec(w):
    return pl.BlockSpec((TD, w), lambda i: (jnp.minimum(i, NCB_D - 1), 0))


def _lat_spec(w):
    return pl.BlockSpec((TD, w), lambda i: (jnp.maximum(i - NCB_D, 0), 0))


def _pick(i, ctx_ref, lat_ref, rows=slice(None)):
    return jnp.where(i < NCB_D, ctx_ref[rows, :], lat_ref[rows, :])


N_COND = 1 + DEC_BATCH


def _mod_kernel(c_ref, w_ref, b_ref, o_ref, sb_scr):
    @pl.when((pl.program_id(0) == 0) & (pl.program_id(1) == 0))
    def _():
        c = c_ref[...]
        s = c * _sigmoid(c)
        for r in range(N_COND):
            sb_scr[r] = jnp.broadcast_to(s[r:r + 1, :], (LANES, D_MODEL)).T

    def body(kc, accs):
        rows = pl.ds(pl.multiple_of(kc * 8, 8), 8)
        wk = w_ref[rows, :]
        return tuple(a + wk * jnp.concatenate([sb_scr[r, rows, :]] * (D_MODEL // LANES), axis=1)
                     for r, a in enumerate(accs))

    accs = lax.fori_loop(0, D_MODEL // 8, body, tuple(jnp.zeros((8, D_MODEL), F32) for _ in range(N_COND)),
                         unroll=8)
    rowid = _iota((MOD_ROWS, D_MODEL), 0)
    out = jnp.zeros((MOD_ROWS, D_MODEL), F32)
    for r, a in enumerate(accs):
        out = jnp.where(rowid == r, jnp.sum(a, axis=0, keepdims=True), out)
    o_ref[...] = out + b_ref[...]


def _modulation(cc, w_mod, b_mod):
    return pl.pallas_call(
        _mod_kernel,
        grid=(DEPTH, 6),
        in_specs=[
            pl.BlockSpec((MOD_ROWS, D_MODEL), lambda l, j: (0, 0)),
            pl.BlockSpec((None, D_MODEL, D_MODEL), lambda l, j: (l, 0, j)),
            pl.BlockSpec((None, 1, D_MODEL), lambda l, j: (l, 0, j)),
        ],
        out_specs=pl.BlockSpec((None, MOD_ROWS, D_MODEL), lambda l, j: (l, 0, j)),
        out_shape=jax.ShapeDtypeStruct((DEPTH, MOD_ROWS, 6 * D_MODEL), F32),
        scratch_shapes=[pltpu.VMEM((N_COND, D_MODEL, LANES), F32)],
        compiler_params=_cparams(2),
        name="modulation",
    )(cc, w_mod, b_mod.reshape(DEPTH, 1, 6 * D_MODEL))


def _bias_kernel(rpb_ref, o_ref):
    q = _iota((GRID_W, LANES), 0)
    x = _iota((GRID_W, LANES), 1)
    c = x % GRID_W
    right = x >= GRID_W
    dc = jnp.clip(c - q, 1 - NA_COLS, NA_COLS - 1) + NA_COLS - 1
    c0 = jnp.clip(q - NA_COLS // 2, 0, GRID_W - NA_COLS)
    in_win = (c >= c0) & (c < c0 + NA_COLS)

    def body(t, carry):
        h = t // (N_DR - 1)
        dr = t % (N_DR - 1)
        rows = [jnp.broadcast_to(rpb_ref[pl.ds(h * N_DR + dr + k, 1), :], (GRID_W, LANES)) for k in range(2)]
        lo, hi = (jnp.take_along_axis(r, dc, axis=1) for r in rows)
        o_ref[h, dr] = jnp.where(in_win, jnp.where(right, hi, lo), NEG_INF)
        return carry

    lax.fori_loop(0, H_NA * (N_DR - 1), body, 0, unroll=N_DR - 1)


RPB_ROWS = -(-H_NA * N_DR // 8) * 8


def _bias_tables(na_rpb):
    rpb = jnp.pad(na_rpb.reshape(DEPTH, H_NA * N_DR, N_DC), ((0, 0), (0, RPB_ROWS - H_NA * N_DR), (0, LANES - N_DC)))
    return pl.pallas_call(
        _bias_kernel,
        grid=(DEPTH,),
        in_specs=[pl.BlockSpec((None, RPB_ROWS, LANES), lambda l: (l, 0, 0))],
        out_specs=pl.BlockSpec((None, H_NA, N_DR - 1, GRID_W, LANES), lambda l: (l, 0, 0, 0, 0)),
        out_shape=jax.ShapeDtypeStruct((DEPTH, H_NA, N_DR - 1, GRID_W, LANES), F32),
        compiler_params=_cparams(1),
        name="na_bias_tables",
    )(rpb)


def _rope(x, cos, sin):
    k = x.shape[1] // LANES
    cosf = cos if k == 1 else jnp.concatenate([cos] * k, axis=1)
    sinf = sin if k == 1 else jnp.concatenate([sin] * k, axis=1)
    first = (_iota(x.shape, 1) % (2 * ROPE_FREQ)) < ROPE_FREQ
    partner = jnp.where(first, pltpu.roll(x, x.shape[1] - ROPE_FREQ, axis=1), pltpu.roll(x, ROPE_FREQ, axis=1))
    return x * cosf + partner * sinf


def _rwkv_features(x, prev_row, next_row, conv_ref, w0_ref, w2_ref, a0_ref, a2_ref, g2_ref, kk_ref, ka_ref, rk_ref):
    rows = _iota(x.shape, 0)
    x_prev = jnp.where(rows == 0, prev_row, pltpu.roll(x, 1, axis=0))
    x_next = jnp.where(rows == SEQ - 1, next_row, pltpu.roll(x, SEQ - 1, axis=0))
    f = x_prev * conv_ref[0:1, :] + x * conv_ref[1:2, :] + x_next * conv_ref[2:3, :]
    o1, o2, o3 = W_RWKV, 2 * W_RWKV, 3 * W_RWKV
    o4 = o3 + 2 * LORA_W
    o5 = o4 + 2 * LORA_A
    r, k, v = f[:, :o1], f[:, o1:o2], f[:, o2:o3]
    wd, ad, gd = f[:, o3:o4], f[:, o4:o5], f[:, o5:]
    lora = lambda y, w_ref: _dgs(_split(y), (w_ref[0], w_ref[1]))
    log_w = -_softplus(-(w0_ref[...] + lora(jnp.tanh(wd), w2_ref))) - 0.5
    a = _sigmoid(a0_ref[...] + lora(ad, a2_ref))
    g = lora(_sigmoid(gd), g2_ref)
    kk = k * kk_ref[...]
    kap = kk / jnp.maximum(jnp.sqrt(_seg64_sum(kk * kk)), 1e-12)
    per_dir = []
    kd_sum = jnp.zeros_like(k)
    for d in range(2):
        a_d = a[:, d * W_RWKV:(d + 1) * W_RWKV]
        kd = k * (1.0 + (a_d - 1.0) * ka_ref[...])
        kd_sum = kd_sum + kd
        per_dir.append((-jnp.exp(log_w[:, d * W_RWKV:(d + 1) * W_RWKV]), a_d * kap, kd))
    bonus = _seg64_sum(r * kd_sum * rk_ref[...]) * v
    return r, kap, v, per_dir, g, bonus


HALO = 8


def _inproj_kernel(xc_ref, xl_ref, xp_ref, xn_ref, mod_ref, w_ref, qn_ref, kn_ref, cos_ref, sin_ref,
                   conv_ref, w0_ref, w2_ref, a0_ref, a2_ref, g2_ref, kk_ref, ka_ref, rk_ref,
                   c0_ref, c1_ref, c2_ref, c3_ref,
                   naq_ref, nak_ref, nav_ref, gq_ref, gk_ref, gv_ref,
                   r_ref, kap_ref, v_ref, lw_ref, ah_ref, kd_ref, g_ref, bonus_ref,
                   cnak_ref, cnav_ref, cgk_ref, cgv_ref):
    del c0_ref, c1_ref, c2_ref, c3_ref
    i = pl.program_id(0)
    row = _mod_row(i)
    shift1 = mod_ref[pl.ds(row, 1), 0:D_MODEL]
    scale1 = mod_ref[pl.ds(row, 1), D_MODEL:2 * D_MODEL]
    modulate = lambda x: (x * (1.0 + scale1) + shift1).astype(BF16)
    lat = i >= NCB_D
    pos = (i - NCB_D) % LAT_BLKS_D
    in_seq = jnp.where(lat, 1.0, 0.0)
    has_prev = jnp.where(lat & (pos != 0), 1.0, 0.0)
    has_next = jnp.where(lat & (pos != LAT_BLKS_D - 1), 1.0, 0.0)
    xm = [modulate(_pick(i, xc_ref, xl_ref, rs)) for rs in SUB_ROWS]
    xm[0] = jnp.concatenate([xm[0], modulate(xp_ref[...]), modulate(xn_ref[...])], axis=0)
    projs = [_dg(x, w_ref[...]) for x in xm]
    halo = projs[0][SEQ:, :RWKV_IN]
    projs[0] = projs[0][:SEQ]
    feats = [proj[:, :RWKV_IN] for proj in projs]
    o_na = RWKV_IN
    o_g = RWKV_IN + NA_IN
    new_kv = []
    n_sub = len(SUB_ROWS)
    for h, (rs, proj) in enumerate(zip(SUB_ROWS, projs)):
        prev_row = halo[HALO - 1:HALO] * has_prev if h == 0 else feats[h - 1][SEQ - 1:SEQ] * in_seq
        next_row = halo[HALO:HALO + 1] * has_next if h == n_sub - 1 else feats[h + 1][0:1] * in_seq
        r, kap, v, per_dir, g, bonus = _rwkv_features(
            feats[h], prev_row, next_row, conv_ref, w0_ref, w2_ref, a0_ref, a2_ref, g2_ref, kk_ref, ka_ref, rk_ref)
        r_ref[rs, :] = r.astype(BF16)
        kap_ref[rs, :] = kap.astype(BF16)
        v_ref[rs, :] = v.astype(BF16)
        g_ref[rs, :] = g
        bonus_ref[rs, :] = bonus
        for d, (lw, ah, kd) in enumerate(per_dir):
            lw_ref[d, rs, :] = lw
            ah_ref[d, rs, :] = ah.astype(BF16)
            kd_ref[d, rs, :] = kd.astype(BF16)
        naq_ref[rs, :] = proj[:, o_na:o_na + W_NA] * ATTN_SCALE
        nak = proj[:, o_na + W_NA:o_na + 2 * W_NA]
        nav = proj[:, o_na + 2 * W_NA:o_na + 3 * W_NA]
        q = proj[:, o_g:o_g + W_GQA]
        k = proj[:, o_g + W_GQA:o_g + W_GQA + W_GQA_KV]
        gv = proj[:, o_g + W_GQA + W_GQA_KV:]
        cos = cos_ref[rs, :]
        sin = sin_ref[rs, :]
        q = q * lax.rsqrt(_seg64_sum(q * q) * (1.0 / HEAD_DIM) + RMS_EPS) * qn_ref[...]
        k = k * lax.rsqrt(_seg64_sum(k * k) * (1.0 / HEAD_DIM) + RMS_EPS) * kn_ref[...]
        gk = _rope(k, cos, sin)
        gq_ref[rs, :] = _rope(q, cos, sin) * ATTN_SCALE
        nak_ref[rs, :] = nak
        nav_ref[rs, :] = nav
        gk_ref[rs, :] = gk
        gv_ref[rs, :] = gv
        new_kv.append((nak, nav, gk, gv))

    @pl.when(i < NCB_D)
    def _():
        for h, vals in enumerate(new_kv):
            for ref, val in zip((cnak_ref, cnav_ref, cgk_ref, cgv_ref), vals):
                ref[h] = val


def _inproj(l, x_ctx, x_lat, mod_all, w_in_bf, qn, kn, cos_tab, sin_tab, rwkv_params, caches):
    tab_idx = lambda i: (jnp.where(i < NCB_D, 0, 1 + (i - NCB_D) % LAT_BLKS_D), 0)
    lat_blk = lambda i: jnp.maximum(i - NCB_D, 0)
    halo_blocks = TD // HALO
    widths = (W_NA, W_NA, W_NA, W_GQA, W_GQA_KV, W_GQA_KV, W_RWKV, W_RWKV, W_RWKV)
    cache_w = (W_NA, W_NA, W_GQA_KV, W_GQA_KV)
    tok = lambda w: pl.BlockSpec((TD, w), lambda i: (i, 0))
    tok2 = pl.BlockSpec((2, TD, W_RWKV), lambda i: (0, i, 0))
    cache_spec = lambda w: pl.BlockSpec((TD // SEQ, None, SEQ, w), lambda i: (jnp.minimum(i, NCB_D - 1), l, 0, 0))
    rwkv_tails = ((3, RWKV_IN), (1, 2 * W_RWKV), (2, 2 * LORA_W, 2 * W_RWKV), (1, 2 * W_RWKV),
                  (2, 2 * LORA_A, 2 * W_RWKV), (2, LORA_G, W_RWKV), (1, W_RWKV), (1, W_RWKV), (1, W_RWKV))
    in_specs = [
        _ctx_spec(D_MODEL), _lat_spec(D_MODEL),
        pl.BlockSpec((HALO, D_MODEL), lambda i: (jnp.maximum(lat_blk(i) * halo_blocks - 1, 0), 0)),
        pl.BlockSpec((HALO, D_MODEL), lambda i: (jnp.minimum((lat_blk(i) + 1) * halo_blocks, N_LAT // HALO - 1), 0)),
        _layer_spec(l, (MOD_ROWS, 6 * D_MODEL), 1),
        _layer_spec(l, (D_MODEL, D_IN), 1),
        _layer_spec(l, (1, W_GQA), 1),
        _layer_spec(l, (1, W_GQA_KV), 1),
        pl.BlockSpec((TD, LANES), tab_idx),
        pl.BlockSpec((TD, LANES), tab_idx),
    ] + [_layer_spec(l, t, 1) for t in rwkv_tails]
    n_in = len(in_specs)
    out_specs = [tok(w) for w in widths] + [tok2] * 3 + [tok(W_RWKV)] * 2
    n_attn = 6
    out_shape = ([jax.ShapeDtypeStruct((N_TOK, w), F32 if k < n_attn else BF16) for k, w in enumerate(widths)]
                 + [jax.ShapeDtypeStruct((2, N_TOK, W_RWKV), dt) for dt in (F32, BF16, BF16)]
                 + [jax.ShapeDtypeStruct((N_TOK, W_RWKV), F32)] * 2)
    return pl.pallas_call(
        _inproj_kernel,
        grid=(NBLK_D,),
        in_specs=in_specs + [ANY_SPEC] * 4,
        out_specs=out_specs + [cache_spec(w) for w in cache_w],
        out_shape=out_shape + [jax.ShapeDtypeStruct((BATCH, DEPTH, SEQ, w), F32) for w in cache_w],
        input_output_aliases={n_in + j: len(out_specs) + j for j in range(4)},
        compiler_params=_cparams(1),
        name="inproj",
    )(x_ctx, x_lat, x_lat, x_lat, mod_all, w_in_bf, qn, kn, cos_tab, sin_tab, *rwkv_params, *caches)


def _scan_kernel(r0_ref, kap0_ref, v0_ref, r1_ref, kap1_ref, v1_ref,
                 lw0_ref, ah0_ref, kd0_ref, lw1_ref, ah1_ref, kd1_ref, s00_ref, s01_ref, sf0_in_ref, sf1_in_ref,
                 o0_ref, o1_ref, sf0_ref, sf1_ref,
                 s_scr, sdone_scr, rt_scr, kt_scr, kdt_scr, at_scr, cum_scr, m_scr, n_scr, q_scr, oo_scr):
    del sf0_in_ref, sf1_in_ref
    j = pl.program_id(0)
    par = j % 2
    jb = jnp.maximum(j - 1, 0)
    upd_blk = (jb, NBLK_S - 1 - jb)
    is_ctx = tuple(b < NCB_S for b in upd_blk)
    first_pos = (0, LAT_BLKS_S - 1)
    r_refs, kap_refs, v_refs = (r0_ref, r1_ref), (kap0_ref, kap1_ref), (v0_ref, v1_ref)
    lw_refs, ah_refs, kd_refs = (lw0_ref, lw1_ref), (ah0_ref, ah1_ref), (kd0_ref, kd1_ref)
    s0_refs, o_refs, sf_refs = (s00_ref, s01_ref), (o0_ref, o1_ref), (sf0_ref, sf1_ref)

    @pl.when(j == 0)
    def _():
        for ref in (s_scr, sdone_scr, m_scr, n_scr, q_scr, oo_scr):
            ref[...] = jnp.zeros_like(ref)

    for dn in range(2):
        blk = upd_blk[dn]

        @pl.when((j > 0) & is_ctx[dn])
        def _():
            s_scr[dn] = jnp.zeros(s_scr.shape[1:], F32)

        @pl.when((j > 0) & (blk >= NCB_S) & ((blk - NCB_S) % LAT_BLKS_S == first_pos[dn]))
        def _():
            s_scr[dn] = s0_refs[dn][...]

    def seq_slot(dn, n):
        return n if dn == 0 else SEQS_S - 1 - n

    def update(c):
        for dn in range(2):
            ce = c if dn == 0 else NCS - 1 - c
            rows = slice(ce * CHUNK, (ce + 1) * CHUNK)
            for p in range(N_PAIR):
                s = s_scr[dn, p]
                if c > 0 and c % SEQ_CHUNKS == 0:
                    sdone_scr[dn, seq_slot(dn, c // SEQ_CHUNKS - 1), p] = s
                    s = jnp.where(is_ctx[dn], 0.0, s)
                s2 = jnp.concatenate(_split(s), axis=1)
                q = q_scr[1 - par, dn, ce, p]
                m = m_scr[1 - par, dn, ce, p]
                o_refs[dn][rows, p * LANES:(p + 1) * LANES] = (
                    _dg(jnp.concatenate([q, q], axis=1), s2, NT) + oo_scr[1 - par, dn, ce, p])
                s_scr[dn, p] = _dg(s2, jnp.concatenate([m, m], axis=0)) + n_scr[1 - par, dn, ce, p]

    pending = list(range(NCS))

    def next_update():
        if pending:
            update(pending.pop(0))

    next_update()
    rr = _iota((TM, TM), 0)
    cc = _iota((TM, TM), 1)
    same_chunk = (rr // CHUNK) == (cc // CHUNK)
    for dn in range(2):
        order = (rr >= cc) if dn == 0 else (rr <= cc)
        tri = jnp.where(same_chunk & order, 1.0, 0.0).astype(BF16)
        for h in range(TS // TM):
            rs = slice(h * TM, (h + 1) * TM)
            lw = lw_refs[dn][rs, :]
            lw_hi = lw.astype(BF16)
            lw_r = lw - lw_hi.astype(F32)
            lw_mid = lw_r.astype(BF16)
            lw_lo = (lw_r - lw_mid.astype(F32)).astype(BF16)
            cum = _dg(jnp.concatenate([tri, tri, tri], axis=1), jnp.concatenate([lw_hi, lw_mid, lw_lo], axis=0))
            e_neg = jnp.exp(-cum)
            rt_scr[dn, rs, :] = r_refs[dn][rs, :] * jnp.exp(cum)
            kt_scr[dn, rs, :] = kap_refs[dn][rs, :] * jnp.exp(cum - lw)
            kdt_scr[dn, rs, :] = kd_refs[dn][rs, :] * e_neg
            at_scr[dn, rs, :] = ah_refs[dn][rs, :] * e_neg
            cum_scr[dn, rs, :] = cum
        if dn == 0:
            next_update()

    row = _iota((CHUNK, LANES), 0)
    col = _iota((CHUNK, LANES), 1) % CHUNK
    left = _iota((CHUNK, LANES), 1) < CHUNK
    incl = (row >= col, row <= col)
    strict = (row > col, row < col)
    eye = jnp.where(row == col, 1.0, 0.0)
    r2 = _iota((LANES, LANES), 0)
    c2 = _iota((LANES, LANES), 1)
    bd_mask = (r2 // CHUNK) == (c2 // CHUNK)
    eye2 = jnp.where(r2 == c2, 1.0, 0.0)

    zero_bf = jnp.zeros((CHUNK, LANES), BF16)
    split = lambda x: (x.astype(BF16),)

    def bd(xs):
        return tuple(jnp.concatenate([jnp.where(left, y, zero_bf), jnp.where(left, zero_bf, y)], axis=0)
                     for y in xs)

    def cat(xs, ys, axis):
        return tuple(jnp.concatenate([x, y], axis=axis) for x, y in zip(xs, ys))

    def mm(a_list, b_list):
        return [_dgs(a, bd(b)) for a, b in zip(a_list, b_list)]

    units = [(dn, c, p) for c in range(NCS) for dn in range(2) for p in range(N_PAIR)]
    each = lambda f, *lists: [f(*args) for args in zip(*lists)]

    def load(ref):
        return [ref[dn, c * CHUNK:(c + 1) * CHUNK, p * LANES:(p + 1) * LANES] for dn, c, p in units]

    rt = load(rt_scr)
    v = [v_refs[dn][c * CHUNK:(c + 1) * CHUNK, p * LANES:(p + 1) * LANES] for dn, c, p in units]
    kt_s, rt_s, kdt_s, at_s, v_s = (each(split, x) for x in (load(kt_scr), rt, load(kdt_scr), load(at_scr), v))
    gam = [jnp.exp(cum_scr[dn, (c + 1) * CHUNK - 1:(c + 1) * CHUNK, p * LANES:(p + 1) * LANES] if dn == 0 else
                   cum_scr[dn, c * CHUNK:c * CHUNK + 1, p * LANES:(p + 1) * LANES]) for dn, c, p in units]
    gram = each(lambda k, r, a, kd: _dgs(cat(k, r, 0), cat(bd(a), bd(kd), 0), NT), kt_s, rt_s, at_s, kdt_s)
    dirs = [u[0] for u in units]
    la = [jnp.where(strict[dn], g[0:CHUNK, 0:LANES], 0.0) for dn, g in zip(dirs, gram)]
    lk_s = [split(jnp.where(strict[dn], g[0:CHUNK, LANES:], 0.0)) for dn, g in zip(dirs, gram)]
    ra_s = [split(jnp.where(incl[dn], g[CHUNK:, 0:LANES], 0.0)) for dn, g in zip(dirs, gram)]
    rk_s = [split(jnp.where(incl[dn], g[CHUNK:, LANES:], 0.0)) for dn, g in zip(dirs, gram)]
    next_update()
    lrv = mm(each(lambda lk, rk: cat(lk, rk, 0), lk_s, rk_s), v_s)
    next_update()
    b = 8
    l8 = [jnp.where((row // b) == (col // b), x, 0.0) for x in la]
    l8_s = each(split, l8)
    l8_2 = mm(l8_s, l8_s)
    next_update()
    l8_2s = each(split, l8_2)
    l8_4 = mm(l8_2s, l8_2s)
    p1 = mm([split(eye - x) for x in l8], [split(eye + y) for y in l8_2])
    next_update()
    t = mm(each(split, p1), [split(eye + y) for y in l8_4])
    while b < CHUNK:
        next_update()
        offd = ((row // (2 * b)) == (col // (2 * b))) & ((row // b) != (col // b))
        t_s = each(split, t)
        x = mm(t_s, [split(jnp.where(offd, y, 0.0)) for y in la])
        t = each(lambda tt, z: tt - z, t, mm(each(split, x), t_s))
        b *= 2
    assert not pending, "more chunk updates than stages to place them between"
    tx = each(lambda tt, k, y: _dgs(split(tt), cat(bd(k), bd(split(y[0:CHUNK])), 1)), t, kt_s, lrv)
    khat_s = [split(y[:, 0:LANES]) for y in tx]
    w1_s = [split(y[:, LANES:]) for y in tx]
    rx = each(lambda r, k, w: _dgs(r, cat(bd(k), bd(w), 1)), ra_s, khat_s, w1_s)
    mk = each(lambda k, a: _dgs(k, a, TN), khat_s, at_s)
    nk = each(lambda vv, w, kd, a: _dgs(cat(vv, w, 0), cat(kd, tuple(-y for y in a), 0), TN),
              v_s, w1_s, kdt_s, at_s)
    for i, (dn, c, p) in enumerate(units):
        q_scr[par, dn, c, p] = (rt[i] - rx[i][:, 0:LANES]).astype(BF16)
        oo_scr[par, dn, c, p] = lrv[i][CHUNK:] - rx[i][:, LANES:]
        m_scr[par, dn, c, p] = ((eye2 - jnp.where(bd_mask, mk[i], 0.0)) * gam[i]).astype(BF16)
        n_scr[par, dn, c, p] = jnp.where(bd_mask, nk[i], 0.0) * gam[i]

    for dn in range(2):
        @pl.when((j > 0) & is_ctx[dn])
        def _():
            for n in range(SEQS_S - 1):
                sf_refs[dn][seq_slot(dn, n)] = sdone_scr[dn, seq_slot(dn, n)]
            sf_refs[dn][seq_slot(dn, SEQS_S - 1)] = s_scr[dn]


def _rwkv_scan(l, r, kap, v, lw, ah, kd, s0_lat, s_fin):
    def stage_blk(dn, j):
        jj = jnp.minimum(j, NBLK_S - 1)
        return jj if dn == 0 else NBLK_S - 1 - jj

    def update_blk(dn, j):
        jj = jnp.maximum(j - 1, 0)
        return jj if dn == 0 else NBLK_S - 1 - jj

    pair = (N_PAIR, LANES, LANES)
    tok = lambda dn: pl.BlockSpec((TS, W_RWKV), lambda j: (stage_blk(dn, j), 0))
    tok2 = lambda dn: pl.BlockSpec((None, TS, W_RWKV), lambda j: (dn, stage_blk(dn, j), 0))
    s0_spec = lambda dn: pl.BlockSpec(
        (None, None, None) + pair,
        lambda j: (jnp.maximum(update_blk(dn, j) - NCB_S, 0) // LAT_BLKS_S, l, dn, 0, 0, 0))
    sf_spec = lambda dn: pl.BlockSpec(
        (SEQS_S, None) + pair, lambda j: (jnp.minimum(update_blk(dn, j), NCB_S - 1), l, 0, 0, 0))
    o_spec = lambda dn: pl.BlockSpec((TS, W_RWKV), lambda j: (update_blk(dn, j), 0))
    dir_scr = lambda *shape, dtype=F32: pltpu.VMEM((2,) + shape, dtype)
    stage_scr = (2, NCS, N_PAIR)
    return pl.pallas_call(
        _scan_kernel,
        grid=(NBLK_S + 1,),
        in_specs=[tok(0), tok(0), tok(0), tok(1), tok(1), tok(1),
                  tok2(0), tok2(0), tok2(0), tok2(1), tok2(1), tok2(1),
                  s0_spec(0), s0_spec(1), ANY_SPEC, ANY_SPEC],
        out_specs=[o_spec(0), o_spec(1), sf_spec(0), sf_spec(1)],
        out_shape=[jax.ShapeDtypeStruct((N_TOK, W_RWKV), F32)] * 2
        + [jax.ShapeDtypeStruct((BATCH, DEPTH) + pair, F32)] * 2,
        scratch_shapes=[dir_scr(*pair), dir_scr(SEQS_S, *pair)] + [dir_scr(TS, W_RWKV)] * 5
        + [dir_scr(*stage_scr, LANES, LANES, dtype=BF16), dir_scr(*stage_scr, LANES, LANES),
           dir_scr(*stage_scr, CHUNK, LANES, dtype=BF16), dir_scr(*stage_scr, CHUNK, LANES)],
        input_output_aliases={14: 2, 15: 3},
        compiler_params=_cparams(1),
        name="rwkv_scan",
    )(r, kap, v, r, kap, v, lw, ah, kd, lw, ah, kd, s0_lat, s0_lat, s_fin[0], s_fin[1])


def _attend(groups):
    lhs = []
    for q_cols, _, _, _ in groups:
        left = _iota(q_cols[0].shape, 1) < HEAD_DIM
        parts = []
        for qc in q_cols:
            parts += [jnp.where(left, qc, 0.0), jnp.where(left, 0.0, qc)]
        lhs.append(jnp.concatenate(parts, axis=0).astype(BF16))
    s = [_dg(x, g[1], NT) for x, g in zip(lhs, groups)]
    s = [x if g[3] is None else x + g[3] for x, g in zip(s, groups)]
    p = [jnp.exp(x - jnp.max(x, axis=-1, keepdims=True)) for x in s]
    inv = [1.0 / jnp.sum(x, axis=-1, keepdims=True) for x in p]
    o = [_dg(x.astype(BF16), g[2]) * y for x, y, g in zip(p, inv, groups)]
    outs = []
    for x, (q_cols, _, _, _) in zip(o, groups):
        rows = q_cols[0].shape[0]
        left = _iota(q_cols[0].shape, 1) < HEAD_DIM
        outs.append([jnp.where(left, x[2 * j * rows:(2 * j + 1) * rows], x[(2 * j + 1) * rows:(2 * j + 2) * rows])
                     for j in range(len(q_cols))])
    return outs


def _cols(x):
    return [x[:, c * LANES:(c + 1) * LANES] for c in range(x.shape[1] // LANES)]


def _gqa_groups(q, k, v):
    left = _iota(k.shape, 1) < HEAD_DIM
    k_sw = pltpu.roll(k, HEAD_DIM, axis=1)
    v_sw = pltpu.roll(v, HEAD_DIM, axis=1)
    q_cols = _cols(q)
    groups = []
    for g in range(H_GQA_KV):
        k2 = jnp.where(left, k, k_sw) if g == 0 else jnp.where(left, k_sw, k)
        v2 = jnp.where(left, v, v_sw) if g == 0 else jnp.where(left, v_sw, v)
        groups.append((q_cols[2 * g:2 * g + 2], k2.astype(BF16), v2.astype(BF16), None))
    return groups


CTX_STEP_SEQS = 4


def _ctx_attn_kernel(naq_ref, nak_ref, nav_ref, gq_ref, gk_ref, gv_ref, ona_ref, og_ref):
    groups = []
    n_na = W_NA // LANES
    for n in range(CTX_STEP_SEQS):
        rs = slice(n * SEQ, (n + 1) * SEQ)
        k_cols = _cols(nak_ref[rs, :].astype(BF16))
        v_cols = _cols(nav_ref[rs, :].astype(BF16))
        groups += [([qc], kc, vc, None) for qc, kc, vc in zip(_cols(naq_ref[rs, :]), k_cols, v_cols)]
        groups += _gqa_groups(gq_ref[rs, :], gk_ref[rs, :], gv_ref[rs, :])
    outs = _attend(groups)
    per_seq = len(groups) // CTX_STEP_SEQS
    for n in range(CTX_STEP_SEQS):
        rs = slice(n * SEQ, (n + 1) * SEQ)
        o = outs[n * per_seq:(n + 1) * per_seq]
        ona_ref[rs, :] = jnp.concatenate([x[0] for x in o[:n_na]], axis=1)
        og_ref[rs, :] = jnp.concatenate([c for x in o[n_na:] for c in x], axis=1)


def _ctx_attention(naq, nak, nav, gq, gk, gv):
    spec = lambda w: pl.BlockSpec((CTX_STEP_SEQS * SEQ, w), lambda b: (b, 0))
    return pl.pallas_call(
        _ctx_attn_kernel,
        grid=(BATCH // CTX_STEP_SEQS,),
        in_specs=[spec(W_NA), spec(W_NA), spec(W_NA), spec(W_GQA), spec(W_GQA_KV), spec(W_GQA_KV)],
        out_specs=[spec(W_NA), spec(W_GQA)],
        out_shape=[jax.ShapeDtypeStruct((N_CTX, W_NA), F32), jax.ShapeDtypeStruct((N_CTX, W_GQA), F32)],
        compiler_params=_cparams(1),
        name="ctx_attention",
    )(naq, nak, nav, gq, gk, gv)


NA_STEP_ROWS = 8


def _lat_na_kernel(q_ref, k_ref, v_ref, kc_ref, vc_ref, tb_ref, o_ref):
    q = q_ref[...]
    groups = []
    for rr in range(NA_STEP_ROWS):
        r = pl.program_id(1) * NA_STEP_ROWS + rr
        r0 = jnp.clip(r - NA_ROWS // 2, 0, GRID_ROWS - NA_ROWS)
        band = pl.ds(pl.multiple_of(r0 * GRID_W, GRID_W), N_BAND)
        dr0 = r0 - r + NA_ROWS - 1
        for c, qc in enumerate(_cols(q[rr * GRID_W:(rr + 1) * GRID_W])):
            cols = slice(c * LANES, (c + 1) * LANES)
            k2 = jnp.concatenate([k_ref[band, cols], kc_ref[:, cols]], axis=0).astype(BF16)
            v2 = jnp.concatenate([v_ref[band, cols], vc_ref[:, cols]], axis=0).astype(BF16)
            bias = jnp.concatenate(
                [jnp.concatenate([tb_ref[2 * c + half, dr0 + jj] for jj in range(0, NA_ROWS, 2)], axis=1)
                 for half in range(2)], axis=0)
            bias = jnp.concatenate([bias, jnp.zeros((2 * GRID_W, PAST_LEN), F32)], axis=1)
            groups.append(([qc], k2, v2, bias))
    outs = [o[0] for o in _attend(groups)]
    n_col = W_NA // LANES
    o_ref[...] = jnp.concatenate(
        [jnp.concatenate(outs[rr * n_col:(rr + 1) * n_col], axis=1) for rr in range(NA_STEP_ROWS)], axis=0)


def _lat_na(l, naq, nak, nav, kc, vc, tb):
    rows = NA_STEP_ROWS * GRID_W
    steps = GRID_ROWS // NA_STEP_ROWS
    seq_blk0 = N_CTX // DEC_SEQ
    seq = pl.BlockSpec((DEC_SEQ, W_NA), lambda b, r: (seq_blk0 + b, 0))
    cache = pl.BlockSpec((None, None, PAST_LEN, W_NA), lambda b, r: (b, l, 0, 0))
    return pl.pallas_call(
        _lat_na_kernel,
        grid=(DEC_BATCH, steps),
        in_specs=[pl.BlockSpec((rows, W_NA), lambda b, r: (N_CTX // rows + b * steps + r, 0)),
                  seq, seq, cache, cache, _layer_spec(l, (H_NA, N_DR - 1, GRID_W, LANES), 2)],
        out_specs=pl.BlockSpec((rows, W_NA), lambda b, r: (b * steps + r, 0)),
        out_shape=jax.ShapeDtypeStruct((N_LAT, W_NA), F32),
        compiler_params=_cparams(2),
        name="latent_na",
    )(naq, nak, nav, kc, vc, tb)


def _lat_gqa_kernel(q_ref, k_ref, v_ref, kc_ref, vc_ref, o_ref):
    k = jnp.concatenate([kc_ref[...], k_ref[...]], axis=0)
    v = jnp.concatenate([vc_ref[...], v_ref[...]], axis=0)
    o_ref[...] = jnp.concatenate([c for o in _attend(_gqa_groups(q_ref[...], k, v)) for c in o], axis=1)


def _lat_gqa(l, gq, gk, gv, kc, vc):
    seq_blk0 = N_CTX // DEC_SEQ
    seq = pl.BlockSpec((DEC_SEQ, W_GQA_KV), lambda b, i: (seq_blk0 + b, 0))
    cache = pl.BlockSpec((None, None, PAST_LEN, W_GQA_KV), lambda b, i: (b, l, 0, 0))
    return pl.pallas_call(
        _lat_gqa_kernel,
        grid=(DEC_BATCH, LAT_BLKS),
        in_specs=[pl.BlockSpec((TM, W_GQA), lambda b, i: (NCB + b * LAT_BLKS + i, 0)), seq, seq, cache, cache],
        out_specs=pl.BlockSpec((TM, W_GQA), lambda b, i: (b * LAT_BLKS + i, 0)),
        out_shape=jax.ShapeDtypeStruct((N_LAT, W_GQA), F32),
        compiler_params=_cparams(2),
        name="latent_gqa",
    )(gq, gk, gv, kc, vc)


def _mix_ffn_kernel(xc_ref, xl_ref, mod_ref, of_ref, ob_ref, g_ref, bonus_ref, lnxw_ref, lnxb_ref,
                    onac_ref, onal_ref, ogc_ref, ogl_ref,
                    wout_ref, ln1w_ref, ln1b_ref, wfi_ref, wfo_ref, ln2w_ref, ln2b_ref, yc_ref, yl_ref):
    i = pl.program_id(0)
    row = _mod_row(i)
    mod = lambda n: mod_ref[pl.ds(row, 1), n * D_MODEL:(n + 1) * D_MODEL]
    each = lambda f, *lists: [f(*args) for args in zip(*lists)]

    def rwkv_out(rs):
        o = of_ref[rs, :] + ob_ref[rs, :]
        mu = _seg64_sum(o) * (1.0 / HEAD_DIM)
        oc = o - mu
        var = _seg64_sum(oc * oc) * (1.0 / HEAD_DIM)
        o_rwkv = (oc * lax.rsqrt(var + GN_EPS) * lnxw_ref[...] + lnxb_ref[...] + bonus_ref[rs, :]) * g_ref[rs, :]
        return jnp.concatenate([o_rwkv, _pick(i, onac_ref, onal_ref, rs), _pick(i, ogc_ref, ogl_ref, rs)],
                               axis=1).astype(BF16)

    mix = [_dg(rwkv_out(rs), wout_ref[...]) for rs in SUB_ROWS]
    x1 = [_layer_norm(DEEPNORM_ALPHA * _pick(i, xc_ref, xl_ref, rs) + mod(2) * m, ln1w_ref[...], ln1b_ref[...])
          for rs, m in zip(SUB_ROWS, mix)]
    x_in = [(x * (1.0 + mod(4)) + mod(3)).astype(BF16) for x in x1]
    ffn = [jnp.zeros_like(x) for x in x1]
    for lo in range(0, D_FF, FF_CHUNK):
        gate = [_dg(x, wfi_ref[:, lo:lo + FF_CHUNK]) for x in x_in]
        up = [_dg(x, wfi_ref[:, D_FF + lo:D_FF + lo + FF_CHUNK]) for x in x_in]
        act = each(lambda gt, u: (gt * _sigmoid(gt) * u).astype(BF16), gate, up)
        ffn = each(lambda f, a: f + _dg(a, wfo_ref[lo:lo + FF_CHUNK, :]), ffn, act)
    y = each(lambda x, f: _layer_norm(DEEPNORM_ALPHA * x + mod(5) * f, ln2w_ref[...], ln2b_ref[...]), x1, ffn)

    @pl.when(i < NCB_D)
    def _():
        for rs, yy in zip(SUB_ROWS, y):
            yc_ref[rs, :] = yy

    @pl.when(i >= NCB_D)
    def _():
        for rs, yy in zip(SUB_ROWS, y):
            yl_ref[rs, :] = yy


def _mix_ffn(l, x_ctx, x_lat, mod_all, o_fwd, o_bwd, g, bonus, lnx_w, lnx_b, o_na_ctx, o_na_lat, o_g_ctx, o_g_lat,
             w_out_bf, ln1_w, ln1_b, w_ffn_in_bf, w_ffn_out_bf, ln2_w, ln2_b):
    tok = lambda w: pl.BlockSpec((TD, w), lambda i: (i, 0))
    once = lambda *tail: _layer_spec(l, tail, 1, single=True)
    return pl.pallas_call(
        _mix_ffn_kernel,
        grid=(NBLK_D,),
        in_specs=[
            _ctx_spec(D_MODEL), _lat_spec(D_MODEL), once(MOD_ROWS, 6 * D_MODEL),
            tok(W_RWKV), tok(W_RWKV),
            tok(W_RWKV), tok(W_RWKV), once(1, W_RWKV), once(1, W_RWKV),
            _ctx_spec(W_NA), _lat_spec(W_NA), _ctx_spec(W_GQA), _lat_spec(W_GQA),
            once(D_MODEL, D_MODEL), once(1, D_MODEL), once(1, D_MODEL),
            once(D_MODEL, 2 * D_FF), once(D_FF, D_MODEL), once(1, D_MODEL), once(1, D_MODEL),
        ],
        out_specs=[_ctx_spec(D_MODEL), _lat_spec(D_MODEL)],
        out_shape=[jax.ShapeDtypeStruct((N_CTX, D_MODEL), F32), jax.ShapeDtypeStruct((N_LAT, D_MODEL), F32)],
        compiler_params=_cparams(1, VMEM_LIMIT_FFN),
        name="mix_ffn",
    )(x_ctx, x_lat, mod_all, o_fwd, o_bwd, g, bonus, lnx_w, lnx_b, o_na_ctx, o_na_lat, o_g_ctx, o_g_lat,
      w_out_bf, ln1_w, ln1_b, w_ffn_in_bf, w_ffn_out_bf, ln2_w, ln2_b)


def _rope_tables():
    t = jnp.arange(DEC_SEQ)
    inv = ROPE_BASE ** (-jnp.arange(ROPE_FREQ, dtype=F32) / ROPE_FREQ)
    ang_r = (t // GRID_W).astype(F32)[:, None] * inv
    ang_c = (t % GRID_W).astype(F32)[:, None] * inv
    cos = jnp.concatenate([jnp.cos(ang_r)] * 2 + [jnp.cos(ang_c)] * 2, axis=1)
    sin = jnp.concatenate([-jnp.sin(ang_r), jnp.sin(ang_r), -jnp.sin(ang_c), jnp.sin(ang_c)], axis=1)
    cos = jnp.concatenate([jnp.ones((TD, HEAD_DIM), F32), cos], axis=0)
    sin = jnp.concatenate([jnp.zeros((TD, HEAD_DIM), F32), sin], axis=0)
    return jnp.tile(cos, (1, LANES // HEAD_DIM)), jnp.tile(sin, (1, LANES // HEAD_DIM))


def _block_diag2(w):
    z = jnp.zeros_like(w[:, 0])
    return jnp.concatenate([jnp.concatenate([w[:, 0], z], axis=2), jnp.concatenate([z, w[:, 1]], axis=2)], axis=1)


def _hi_lo(w):
    return jnp.stack(_split(w), axis=1)


def _pair_states(s):
    lead = s.shape[:-3]
    s = s.reshape(lead + (N_PAIR, 2, HEAD_DIM, HEAD_DIM))
    z = jnp.zeros_like(s[..., 0, :, :])
    top = jnp.concatenate([s[..., 0, :, :], z], axis=-1)
    bot = jnp.concatenate([z, s[..., 1, :, :]], axis=-1)
    return jnp.concatenate([top, bot], axis=-2)


def _unpair_states(s):
    lead = s.shape[:-3]
    a = s[..., :HEAD_DIM, :HEAD_DIM]
    b = s[..., HEAD_DIM:, HEAD_DIM:]
    return jnp.stack([a, b], axis=-3).reshape(lead + (H_RWKV, HEAD_DIM, HEAD_DIM))


def kernel(x_prompt, x_sample, state_rwkv, cache_na_k, cache_na_v, cache_gqa_k, cache_gqa_v, c, c_ctx,
           w_mod, b_mod, w_in, rwkv_conv, rwkv_w0, rwkv_w2, rwkv_a0, rwkv_a2, rwkv_g2, rwkv_k_k, rwkv_k_a,
           rwkv_r_k, rwkv_lnx_w, rwkv_lnx_b, na_rpb, gqa_q_norm, gqa_k_norm, w_out, ln1_w, ln1_b,
           w_ffn_in, w_ffn_out, ln2_w, ln2_b):
    x_ctx, x_lat = x_prompt.reshape(N_CTX, D_MODEL), x_sample.reshape(N_LAT, D_MODEL)
    cc = jnp.concatenate([c_ctx[None], c, jnp.zeros((MOD_ROWS - 1 - DEC_BATCH, D_MODEL), F32)], axis=0)
    mod_all = _modulation(cc, w_mod, b_mod)
    tb_all = _bias_tables(na_rpb)
    cos_tab, sin_tab = _rope_tables()
    rows = lambda a: a.reshape(DEPTH, 1, -1)
    w_in_bf, w_out_bf = w_in.astype(BF16), w_out.astype(BF16)
    w_ffn_in_bf, w_ffn_out_bf = w_ffn_in.astype(BF16), w_ffn_out.astype(BF16)
    qn = jnp.tile(rows(gqa_q_norm), (1, 1, H_GQA))
    kn = jnp.tile(rows(gqa_k_norm), (1, 1, H_GQA_KV))
    rwkv_params = (rwkv_conv, rows(rwkv_w0), _hi_lo(_block_diag2(rwkv_w2)), rows(rwkv_a0),
                   _hi_lo(_block_diag2(rwkv_a2)), _hi_lo(rwkv_g2), rows(rwkv_k_k), rows(rwkv_k_a), rows(rwkv_r_k))
    s0_lat = _pair_states(state_rwkv)
    kc_na = cache_na_k.reshape(DEC_BATCH, DEPTH, PAST_LEN, W_NA)
    vc_na = cache_na_v.reshape(DEC_BATCH, DEPTH, PAST_LEN, W_NA)
    kc_g = cache_gqa_k.reshape(DEC_BATCH, DEPTH, PAST_LEN, W_GQA_KV)
    vc_g = cache_gqa_v.reshape(DEC_BATCH, DEPTH, PAST_LEN, W_GQA_KV)
    caches = [jnp.zeros((BATCH, DEPTH, SEQ, w), F32) for w in (W_NA, W_NA, W_GQA_KV, W_GQA_KV)]
    s_fin = [jnp.zeros((BATCH, DEPTH, N_PAIR, LANES, LANES), F32) for _ in range(2)]

    for l in range(DEPTH):
        naq, nak, nav, gq, gk, gv, r, kap, v, lw, ah, kd, g, bonus, *caches = _inproj(
            l, x_ctx, x_lat, mod_all, w_in_bf, qn, kn, cos_tab, sin_tab, rwkv_params, caches)
        o_fwd, o_bwd, *s_fin = _rwkv_scan(l, r, kap, v, lw, ah, kd, s0_lat, s_fin)
        o_na_ctx, o_g_ctx = _ctx_attention(naq, nak, nav, gq, gk, gv)
        o_na_lat = _lat_na(l, naq, nak, nav, kc_na, vc_na, tb_all)
        o_g_lat = _lat_gqa(l, gq, gk, gv, kc_g, vc_g)
        x_ctx, x_lat = _mix_ffn(
            l, x_ctx, x_lat, mod_all, o_fwd, o_bwd, g, bonus, rows(rwkv_lnx_w), rows(rwkv_lnx_b),
            o_na_ctx, o_na_lat, o_g_ctx, o_g_lat, w_out_bf, rows(ln1_w), rows(ln1_b),
            w_ffn_in_bf, w_ffn_out_bf, rows(ln2_w), rows(ln2_b))
    y_prompt = x_ctx.reshape(BATCH, SEQ, D_MODEL)
    y_sample = x_lat.reshape(DEC_BATCH, DEC_SEQ, D_MODEL)
    new_state = jnp.stack([_unpair_states(s) for s in s_fin], axis=2)
    return (y_prompt, y_sample, new_state,
            caches[0].reshape(BATCH, DEPTH, SEQ, H_NA, HEAD_DIM), caches[1].reshape(BATCH, DEPTH, SEQ, H_NA, HEAD_DIM),
            caches[2].reshape(BATCH, DEPTH, SEQ, H_GQA_KV, HEAD_DIM),
            caches[3].reshape(BATCH, DEPTH, SEQ, H_GQA_KV, HEAD_DIM))
```

```python
import jax
import jax.numpy as jnp
from jax import lax
from jax.experimental import pallas as pl
from jax.experimental.pallas import tpu as pltpu

F32 = jnp.float32
BF16 = jnp.bfloat16

D_MODEL = 1024
BATCH = 16
SEQ = 256
DEPTH = 4
DEC_BATCH = 2
DEC_SEQ = 1024
PAST_LEN = 512
GRID_W = 64
GRID_ROWS = DEC_SEQ // GRID_W
HEAD_DIM = 64
H_RWKV = 4
H_NA = 4
H_GQA = 8
H_GQA_KV = 2
W_RWKV = H_RWKV * HEAD_DIM
W_NA = H_NA * HEAD_DIM
W_GQA = H_GQA * HEAD_DIM
W_GQA_KV = H_GQA_KV * HEAD_DIM
LORA_W = 64
LORA_A = 64
LORA_G = 128
RWKV_IN = 3 * W_RWKV + 2 * LORA_W + 2 * LORA_A + LORA_G
NA_IN = 3 * W_NA
GQA_IN = W_GQA + 2 * W_GQA_KV
D_IN = RWKV_IN + NA_IN + GQA_IN
NA_ROWS = 8
NA_COLS = 16
ROPE_BASE = 10000.0
ROPE_FREQ = HEAD_DIM // 4
D_FF = ((8 * D_MODEL + 3 * 256 - 1) // (3 * 256)) * 256
DEEPNORM_ALPHA = (2 * DEPTH) ** 0.25
LN_EPS = 1e-5
RMS_EPS = 1e-6
GN_EPS = 64e-5
NEG_INF = -1e30
ATTN_SCALE = HEAD_DIM ** -0.5

LANES = 128
TM = 256
N_CTX = BATCH * SEQ
N_LAT = DEC_BATCH * DEC_SEQ
N_TOK = N_CTX + N_LAT
CHUNK = 64
TS = 512
NCS = TS // CHUNK
NBLK_S = N_TOK // TS
NCB_S = N_CTX // TS
LAT_BLKS_S = DEC_SEQ // TS
SEQS_S = TS // SEQ
SEQ_CHUNKS = SEQ // CHUNK
TD = 512
SUB_ROWS = [slice(h * SEQ, (h + 1) * SEQ) for h in range(TD // SEQ)]
IN_PIECES = SUB_ROWS
NBLK_D = N_TOK // TD
NCB_D = N_CTX // TD
LAT_BLKS_D = DEC_SEQ // TD
FF_CHUNK = D_FF
N_PAIR = W_RWKV // LANES
MOD_ROWS = 8
VMEM_LIMIT = 48 * 1024 * 1024
VMEM_LIMIT_FFN = 56 * 1024 * 1024
N_DR = 2 * NA_ROWS - 1
N_DC = 2 * NA_COLS - 1
N_BAND = NA_ROWS * GRID_W


def _cparams(n_grid, vmem_limit=VMEM_LIMIT):
    return pltpu.CompilerParams(dimension_semantics=("arbitrary",) * n_grid, vmem_limit_bytes=vmem_limit)


def _iota(shape, dim):
    return lax.broadcasted_iota(jnp.int32, shape, dim)


NN = (((1,), (0,)), ((), ()))
NT = (((1,), (1,)), ((), ()))
TN = (((0,), (0,)), ((), ()))


def _dg(a, b, dims=NN, precision=None):
    return lax.dot_general(a, b, dims, preferred_element_type=F32, precision=precision)


def _split(x):
    hi = x.astype(BF16)
    return hi, (x - hi.astype(F32)).astype(BF16)


def _dgs(a_s, b_s, dims=NN):
    if len(a_s) == 1:
        return _dg(a_s[0], b_s[0], dims)
    (ah, al), (bh, bl) = a_s, b_s
    ca, cb = dims[0][0][0], dims[0][1][0]
    return _dg(jnp.concatenate([ah, ah, al], axis=ca), jnp.concatenate([bh, bl, bh], axis=cb), dims)


def _sigmoid(x):
    return 1.0 / (1.0 + jnp.exp(-x))


def _softplus(x):
    return jnp.maximum(x, 0.0) + jnp.log(1.0 + jnp.exp(-jnp.abs(x)))


def _seg64_sum(x):
    rows, width = x.shape
    lo = _iota((rows, LANES), 1) < HEAD_DIM
    outs = []
    for c in range(width // LANES):
        blk = x[:, c * LANES:(c + 1) * LANES]
        s_lo = jnp.sum(jnp.where(lo, blk, 0.0), axis=-1, keepdims=True)
        s_hi = jnp.sum(jnp.where(lo, 0.0, blk), axis=-1, keepdims=True)
        outs.append(jnp.where(lo, s_lo, s_hi))
    return outs[0] if len(outs) == 1 else jnp.concatenate(outs, axis=1)


def _layer_norm(x, w, b):
    mu = jnp.mean(x, axis=-1, keepdims=True)
    xc = x - mu
    var = jnp.mean(xc * xc, axis=-1, keepdims=True)
    return xc * lax.rsqrt(var + LN_EPS) * w + b


def _mod_row(i):
    return jnp.where(i < NCB_D, 0, 1 + (i - NCB_D) // LAT_BLKS_D)


def _layer_spec(l, tail, single=False):
    idx = lambda *g: (l,) + (0,) * len(tail)
    if single:
        return pl.BlockSpec((None,) + tuple(tail), idx, pipeline_mode=pl.Buffered(1))
    return pl.BlockSpec((None,) + tuple(tail), idx)


ANY_SPEC = pl.BlockSpec(memory_space=pl.ANY)


def _ctx_spec(w):
    return pl.BlockSpec((TD, w), lambda i: (jnp.minimum(i, NCB_D - 1), 0))


def _lat_spec(w):
    return pl.BlockSpec((TD, w), lambda i: (jnp.maximum(i - NCB_D, 0), 0))


def _pick(i, ctx_ref, lat_ref, rows=slice(None)):
    return jnp.where(i < NCB_D, ctx_ref[rows, :], lat_ref[rows, :])


N_COND = 1 + DEC_BATCH


def _mod_kernel(c_ref, w_ref, b_ref, o_ref, sb_scr):
    @pl.when((pl.program_id(0) == 0) & (pl.program_id(1) == 0))
    def _():
        c = c_ref[...]
        s = c * _sigmoid(c)
        for r in range(N_COND):
            sb_scr[r] = jnp.broadcast_to(s[r:r + 1, :], (LANES, D_MODEL)).T

    def body(kc, accs):
        rows = pl.ds(pl.multiple_of(kc * 8, 8), 8)
        wk = w_ref[rows, :]
        return tuple(a + wk * jnp.concatenate([sb_scr[r, rows, :]] * (D_MODEL // LANES), axis=1)
                     for r, a in enumerate(accs))

    accs = lax.fori_loop(0, D_MODEL // 8, body, tuple(jnp.zeros((8, D_MODEL), F32) for _ in range(N_COND)),
                         unroll=8)
    rowid = _iota((MOD_ROWS, D_MODEL), 0)
    out = jnp.zeros((MOD_ROWS, D_MODEL), F32)
    for r, a in enumerate(accs):
        out = jnp.where(rowid == r, jnp.sum(a, axis=0, keepdims=True), out)
    o_ref[...] = out + b_ref[...]


def _modulation(cc, w_mod, b_mod):
    return pl.pallas_call(
        _mod_kernel,
        grid=(DEPTH, 6),
        in_specs=[
            pl.BlockSpec((MOD_ROWS, D_MODEL), lambda l, j: (0, 0)),
            pl.BlockSpec((None, D_MODEL, D_MODEL), lambda l, j: (l, 0, j)),
            pl.BlockSpec((None, 1, D_MODEL), lambda l, j: (l, 0, j)),
        ],
        out_specs=pl.BlockSpec((None, MOD_ROWS, D_MODEL), lambda l, j: (l, 0, j)),
        out_shape=jax.ShapeDtypeStruct((DEPTH, MOD_ROWS, 6 * D_MODEL), F32),
        scratch_shapes=[pltpu.VMEM((N_COND, D_MODEL, LANES), F32)],
        compiler_params=_cparams(2),
        name="modulation",
    )(cc, w_mod, b_mod.reshape(DEPTH, 1, 6 * D_MODEL))


def _bias_kernel(rpb_ref, o_ref):
    q = _iota((GRID_W, LANES), 0)
    x = _iota((GRID_W, LANES), 1)
    c = x % GRID_W
    right = x >= GRID_W
    dc = jnp.clip(c - q, 1 - NA_COLS, NA_COLS - 1) + NA_COLS - 1
    c0 = jnp.clip(q - NA_COLS // 2, 0, GRID_W - NA_COLS)
    in_win = (c >= c0) & (c < c0 + NA_COLS)

    def body(t, carry):
        h = t // (N_DR - 1)
        dr = t % (N_DR - 1)
        rows = [jnp.broadcast_to(rpb_ref[pl.ds(h * N_DR + dr + k, 1), :], (GRID_W, LANES)) for k in range(2)]
        lo, hi = (jnp.take_along_axis(r, dc, axis=1) for r in rows)
        o_ref[h, dr] = jnp.where(in_win, jnp.where(right, hi, lo), NEG_INF)
        return carry

    lax.fori_loop(0, H_NA * (N_DR - 1), body, 0, unroll=N_DR - 1)


RPB_ROWS = -(-H_NA * N_DR // 8) * 8


def _bias_tables(na_rpb):
    rpb = jnp.pad(na_rpb.reshape(DEPTH, H_NA * N_DR, N_DC), ((0, 0), (0, RPB_ROWS - H_NA * N_DR), (0, LANES - N_DC)))
    return pl.pallas_call(
        _bias_kernel,
        grid=(DEPTH,),
        in_specs=[pl.BlockSpec((None, RPB_ROWS, LANES), lambda l: (l, 0, 0))],
        out_specs=pl.BlockSpec((None, H_NA, N_DR - 1, GRID_W, LANES), lambda l: (l, 0, 0, 0, 0)),
        out_shape=jax.ShapeDtypeStruct((DEPTH, H_NA, N_DR - 1, GRID_W, LANES), F32),
        compiler_params=_cparams(1),
        name="na_bias_tables",
    )(rpb)


def _rope(x, cos, sin):
    k = x.shape[1] // LANES
    cosf = cos if k == 1 else jnp.concatenate([cos] * k, axis=1)
    sinf = sin if k == 1 else jnp.concatenate([sin] * k, axis=1)
    first = (_iota(x.shape, 1) % (2 * ROPE_FREQ)) < ROPE_FREQ
    partner = jnp.where(first, pltpu.roll(x, x.shape[1] - ROPE_FREQ, axis=1), pltpu.roll(x, ROPE_FREQ, axis=1))
    return x * cosf + partner * sinf


def _rwkv_features(x, prev_row, next_row, conv_ref, w0_ref, w2_ref, a0_ref, a2_ref, g2_ref, kk_ref, ka_ref, rk_ref):
    n = x.shape[0]
    rows = _iota(x.shape, 0)
    x_prev = jnp.where(rows == 0, prev_row, pltpu.roll(x, 1, axis=0))
    x_next = jnp.where(rows == n - 1, next_row, pltpu.roll(x, n - 1, axis=0))
    f = x_prev * conv_ref[0:1, :] + x * conv_ref[1:2, :] + x_next * conv_ref[2:3, :]
    o1, o2, o3 = W_RWKV, 2 * W_RWKV, 3 * W_RWKV
    o4 = o3 + 2 * LORA_W
    o5 = o4 + 2 * LORA_A
    r, k, v = f[:, :o1], f[:, o1:o2], f[:, o2:o3]
    wd, ad, gd = f[:, o3:o4], f[:, o4:o5], f[:, o5:]
    def lora(y, w_ref):
        y_bf = y.astype(BF16)
        return _dg(jnp.concatenate([y_bf, y_bf], axis=1), jnp.concatenate([w_ref[0], w_ref[1]], axis=0))
    log_w = -_softplus(-(w0_ref[...] + lora(jnp.tanh(wd), w2_ref))) - 0.5
    a = _sigmoid(a0_ref[...] + lora(ad, a2_ref))
    g = lora(_sigmoid(gd), g2_ref)
    kk = k * kk_ref[...]
    kap = kk / jnp.maximum(jnp.sqrt(_seg64_sum(kk * kk)), 1e-12)
    per_dir = []
    kd_sum = jnp.zeros_like(k)
    for d in range(2):
        a_d = a[:, d * W_RWKV:(d + 1) * W_RWKV]
        kd = k * (1.0 + (a_d - 1.0) * ka_ref[...])
        kd_sum = kd_sum + kd
        per_dir.append((-jnp.exp(log_w[:, d * W_RWKV:(d + 1) * W_RWKV]), a_d * kap, kd))
    bonus = _seg64_sum(r * kd_sum * rk_ref[...]) * v
    return r, kap, v, per_dir, g, bonus


HALO = 8


def _inproj_kernel(xc_ref, xl_ref, xp_ref, xn_ref, mod_ref, w_ref, qn_ref, kn_ref, cos_ref, sin_ref,
                   conv_ref, w0_ref, w2_ref, a0_ref, a2_ref, g2_ref, kk_ref, ka_ref, rk_ref,
                   c0_ref, c1_ref, c2_ref, c3_ref,
                   naq_ref, nak_ref, nav_ref, gq_ref, gk_ref, gv_ref,
                   r_ref, kap_ref, v_ref, lw_ref, ah_ref, kd_ref, g_ref, bonus_ref,
                   cnak_ref, cnav_ref, cgk_ref, cgv_ref):
    del c0_ref, c1_ref, c2_ref, c3_ref
    i = pl.program_id(0)
    row = _mod_row(i)
    shift1 = mod_ref[pl.ds(row, 1), 0:D_MODEL]
    scale1 = mod_ref[pl.ds(row, 1), D_MODEL:2 * D_MODEL]
    modulate = lambda x: (x * (1.0 + scale1) + shift1).astype(BF16)
    lat = i >= NCB_D
    pos = (i - NCB_D) % LAT_BLKS_D
    in_seq = jnp.where(lat, 1.0, 0.0)
    has_prev = jnp.where(lat & (pos != 0), 1.0, 0.0)
    has_next = jnp.where(lat & (pos != LAT_BLKS_D - 1), 1.0, 0.0)
    xm = [modulate(_pick(i, xc_ref, xl_ref, rs)) for rs in IN_PIECES]
    xm[0] = jnp.concatenate([xm[0], modulate(xp_ref[...]), modulate(xn_ref[...])], axis=0)
    projs = [_dg(x, w_ref[...]) for x in xm]
    n0 = IN_PIECES[0].stop
    halo = projs[0][n0:, :RWKV_IN]
    projs[0] = projs[0][:n0]
    feats = [proj[:, :RWKV_IN] for proj in projs]
    o_na = RWKV_IN
    o_g = RWKV_IN + NA_IN
    new_kv = []
    for h, (rs, proj) in enumerate(zip(IN_PIECES, projs)):
        if h == 0:
            prev_row = halo[HALO - 1:HALO] * has_prev
        else:
            prev_row = feats[h - 1][-1:] * (in_seq if rs.start % SEQ == 0 else 1.0)
        if h == len(IN_PIECES) - 1:
            next_row = halo[HALO:HALO + 1] * has_next
        else:
            next_row = feats[h + 1][0:1] * (in_seq if rs.stop % SEQ == 0 else 1.0)
        r, kap, v, per_dir, g, bonus = _rwkv_features(
            feats[h], prev_row, next_row, conv_ref, w0_ref, w2_ref, a0_ref, a2_ref, g2_ref, kk_ref, ka_ref, rk_ref)
        r_ref[rs, :] = r.astype(BF16)
        kap_ref[rs, :] = kap.astype(BF16)
        v_ref[rs, :] = v.astype(BF16)
        g_ref[rs, :] = g
        bonus_ref[rs, :] = bonus
        for d, (lw, ah, kd) in enumerate(per_dir):
            lw_ref[d, rs, :] = lw
            ah_ref[d, rs, :] = ah.astype(BF16)
            kd_ref[d, rs, :] = kd.astype(BF16)
        naq_ref[rs, :] = proj[:, o_na:o_na + W_NA] * ATTN_SCALE
        nak = proj[:, o_na + W_NA:o_na + 2 * W_NA]
        nav = proj[:, o_na + 2 * W_NA:o_na + 3 * W_NA]
        q = proj[:, o_g:o_g + W_GQA]
        k = proj[:, o_g + W_GQA:o_g + W_GQA + W_GQA_KV]
        gv = proj[:, o_g + W_GQA + W_GQA_KV:]
        cos = cos_ref[rs, :]
        sin = sin_ref[rs, :]
        q = q * lax.rsqrt(_seg64_sum(q * q) * (1.0 / HEAD_DIM) + RMS_EPS) * qn_ref[...]
        k = k * lax.rsqrt(_seg64_sum(k * k) * (1.0 / HEAD_DIM) + RMS_EPS) * kn_ref[...]
        gk = _rope(k, cos, sin)
        gq_ref[rs, :] = _rope(q, cos, sin) * ATTN_SCALE
        nak_ref[rs, :] = nak
        nav_ref[rs, :] = nav
        gk_ref[rs, :] = gk
        gv_ref[rs, :] = gv
        new_kv.append((nak, nav, gk, gv))

    @pl.when(i < NCB_D)
    def _():
        for rs, vals in zip(IN_PIECES, new_kv):
            for ref, val in zip((cnak_ref, cnav_ref, cgk_ref, cgv_ref), vals):
                ref[rs.start // SEQ, rs.start % SEQ:rs.start % SEQ + rs.stop - rs.start, :] = val


def _inproj(l, x_ctx, x_lat, mod_all, w_in_bf, qn, kn, cos_tab, sin_tab, rwkv_params, caches):
    tab_idx = lambda i: (jnp.where(i < NCB_D, 0, 1 + (i - NCB_D) % LAT_BLKS_D), 0)
    lat_blk = lambda i: jnp.maximum(i - NCB_D, 0)
    halo_blocks = TD // HALO
    widths = (W_NA, W_NA, W_NA, W_GQA, W_GQA_KV, W_GQA_KV, W_RWKV, W_RWKV, W_RWKV)
    cache_w = (W_NA, W_NA, W_GQA_KV, W_GQA_KV)
    tok = lambda w: pl.BlockSpec((TD, w), lambda i: (i, 0))
    tok2 = pl.BlockSpec((2, TD, W_RWKV), lambda i: (0, i, 0))
    cache_spec = lambda w: pl.BlockSpec((TD // SEQ, None, SEQ, w), lambda i: (jnp.minimum(i, NCB_D - 1), l, 0, 0))
    rwkv_tails = ((3, RWKV_IN), (1, 2 * W_RWKV), (2, 2 * LORA_W, 2 * W_RWKV), (1, 2 * W_RWKV),
                  (2, 2 * LORA_A, 2 * W_RWKV), (2, LORA_G, W_RWKV), (1, W_RWKV), (1, W_RWKV), (1, W_RWKV))
    in_specs = [
        _ctx_spec(D_MODEL), _lat_spec(D_MODEL),
        pl.BlockSpec((HALO, D_MODEL), lambda i: (jnp.maximum(lat_blk(i) * halo_blocks - 1, 0), 0)),
        pl.BlockSpec((HALO, D_MODEL), lambda i: (jnp.minimum((lat_blk(i) + 1) * halo_blocks, N_LAT // HALO - 1), 0)),
        _layer_spec(l, (MOD_ROWS, 6 * D_MODEL)),
        _layer_spec(l, (D_MODEL, D_IN)),
        _layer_spec(l, (1, W_GQA)),
        _layer_spec(l, (1, W_GQA_KV)),
        pl.BlockSpec((TD, LANES), tab_idx),
        pl.BlockSpec((TD, LANES), tab_idx),
    ] + [_layer_spec(l, t) for t in rwkv_tails]
    n_in = len(in_specs)
    out_specs = [tok(w) for w in widths] + [tok2] * 3 + [tok(W_RWKV)] * 2
    n_attn = 6
    out_shape = ([jax.ShapeDtypeStruct((N_TOK, w), F32 if k < n_attn else BF16) for k, w in enumerate(widths)]
                 + [jax.ShapeDtypeStruct((2, N_TOK, W_RWKV), dt) for dt in (F32, BF16, BF16)]
                 + [jax.ShapeDtypeStruct((N_TOK, W_RWKV), F32)] * 2)
    return pl.pallas_call(
        _inproj_kernel,
        grid=(NBLK_D,),
        in_specs=in_specs + [ANY_SPEC] * 4,
        out_specs=out_specs + [cache_spec(w) for w in cache_w],
        out_shape=out_shape + [jax.ShapeDtypeStruct((BATCH, DEPTH, SEQ, w), F32) for w in cache_w],
        input_output_aliases={n_in + j: len(out_specs) + j for j in range(4)},
        compiler_params=_cparams(1),
        name="inproj",
    )(x_ctx, x_lat, x_lat, x_lat, mod_all, w_in_bf, qn, kn, cos_tab, sin_tab, *rwkv_params, *caches)


def _scan_kernel(r0_ref, kap0_ref, v0_ref, r1_ref, kap1_ref, v1_ref,
                 lw0_ref, ah0_ref, kd0_ref, lw1_ref, ah1_ref, kd1_ref, s00_ref, s01_ref, sf0_in_ref, sf1_in_ref,
                 o0_ref, o1_ref, sf0_ref, sf1_ref,
                 s_scr, sdone_scr, rt_scr, kt_scr, kdt_scr, at_scr, cum_scr, m_scr, n_scr, q_scr, oo_scr):
    del sf0_in_ref, sf1_in_ref
    j = pl.program_id(0)
    par = j % 2
    jb = jnp.maximum(j - 1, 0)
    upd_blk = (jb, NBLK_S - 1 - jb)
    is_ctx = tuple(b < NCB_S for b in upd_blk)
    first_pos = (0, LAT_BLKS_S - 1)
    r_refs, kap_refs, v_refs = (r0_ref, r1_ref), (kap0_ref, kap1_ref), (v0_ref, v1_ref)
    lw_refs, ah_refs, kd_refs = (lw0_ref, lw1_ref), (ah0_ref, ah1_ref), (kd0_ref, kd1_ref)
    s0_refs, o_refs, sf_refs = (s00_ref, s01_ref), (o0_ref, o1_ref), (sf0_ref, sf1_ref)

    @pl.when(j == 0)
    def _():
        for ref in (s_scr, sdone_scr, m_scr, n_scr, q_scr, oo_scr):
            ref[...] = jnp.zeros_like(ref)

    for dn in range(2):
        blk = upd_blk[dn]

        @pl.when((j > 0) & is_ctx[dn])
        def _():
            s_scr[dn] = jnp.zeros(s_scr.shape[1:], F32)

        @pl.when((j > 0) & (blk >= NCB_S) & ((blk - NCB_S) % LAT_BLKS_S == first_pos[dn]))
        def _():
            s_scr[dn] = s0_refs[dn][...]

    def seq_slot(dn, n):
        return n if dn == 0 else SEQS_S - 1 - n

    def update(c):
        for dn in range(2):
            ce = c if dn == 0 else NCS - 1 - c
            rows = slice(ce * CHUNK, (ce + 1) * CHUNK)
            for p in range(N_PAIR):
                s = s_scr[dn, p]
                if c > 0 and c % SEQ_CHUNKS == 0:
                    sdone_scr[dn, seq_slot(dn, c // SEQ_CHUNKS - 1), p] = s
                    s = jnp.where(is_ctx[dn], 0.0, s)
                s2 = jnp.concatenate(_split(s), axis=1)
                q = q_scr[1 - par, dn, ce, p]
                m = m_scr[1 - par, dn, ce, p]
                o_refs[dn][rows, p * LANES:(p + 1) * LANES] = (
                    _dg(jnp.concatenate([q, q], axis=1), s2, NT) + oo_scr[1 - par, dn, ce, p])
                s_scr[dn, p] = _dg(s2, jnp.concatenate([m, m], axis=0)) + n_scr[1 - par, dn, ce, p]

    pending = list(range(NCS))

    def next_update():
        if pending:
            update(pending.pop(0))

    next_update()
    rr = _iota((TM, TM), 0)
    cc = _iota((TM, TM), 1)
    same_chunk = (rr // CHUNK) == (cc // CHUNK)
    for dn in range(2):
        order = (rr >= cc) if dn == 0 else (rr <= cc)
        tri = jnp.where(same_chunk & order, 1.0, 0.0).astype(BF16)
        for h in range(TS // TM):
            rs = slice(h * TM, (h + 1) * TM)
            lw = lw_refs[dn][rs, :]
            lw_hi = lw.astype(BF16)
            lw_r = lw - lw_hi.astype(F32)
            lw_mid = lw_r.astype(BF16)
            lw_lo = (lw_r - lw_mid.astype(F32)).astype(BF16)
            cum = _dg(jnp.concatenate([tri, tri, tri], axis=1), jnp.concatenate([lw_hi, lw_mid, lw_lo], axis=0))
            e_neg = jnp.exp(-cum)
            rt_scr[dn, rs, :] = r_refs[dn][rs, :] * jnp.exp(cum)
            kt_scr[dn, rs, :] = kap_refs[dn][rs, :] * jnp.exp(cum - lw)
            kdt_scr[dn, rs, :] = kd_refs[dn][rs, :] * e_neg
            at_scr[dn, rs, :] = ah_refs[dn][rs, :] * e_neg
            cum_scr[dn, rs, :] = cum
        if dn == 0:
            next_update()

    row = _iota((CHUNK, LANES), 0)
    col = _iota((CHUNK, LANES), 1) % CHUNK
    left = _iota((CHUNK, LANES), 1) < CHUNK
    incl = (row >= col, row <= col)
    strict = (row > col, row < col)
    eye = jnp.where(row == col, 1.0, 0.0)
    r2 = _iota((LANES, LANES), 0)
    c2 = _iota((LANES, LANES), 1)
    bd_mask = (r2 // CHUNK) == (c2 // CHUNK)
    eye2 = jnp.where(r2 == c2, 1.0, 0.0)

    zero_bf = jnp.zeros((CHUNK, LANES), BF16)
    split = lambda x: (x.astype(BF16),)

    def bd(xs):
        return tuple(jnp.concatenate([jnp.where(left, y, zero_bf), jnp.where(left, zero_bf, y)], axis=0)
                     for y in xs)

    def cat(xs, ys, axis):
        return tuple(jnp.concatenate([x, y], axis=axis) for x, y in zip(xs, ys))

    def mm(a_list, b_list):
        return [_dgs(a, bd(b)) for a, b in zip(a_list, b_list)]

    units = [(dn, c, p) for c in range(NCS) for dn in range(2) for p in range(N_PAIR)]
    each = lambda f, *lists: [f(*args) for args in zip(*lists)]

    def load(ref):
        return [ref[dn, c * CHUNK:(c + 1) * CHUNK, p * LANES:(p + 1) * LANES] for dn, c, p in units]

    rt = load(rt_scr)
    v = [v_refs[dn][c * CHUNK:(c + 1) * CHUNK, p * LANES:(p + 1) * LANES] for dn, c, p in units]
    kt_s, rt_s, kdt_s, at_s, v_s = (each(split, x) for x in (load(kt_scr), rt, load(kdt_scr), load(at_scr), v))
    gam = [jnp.exp(cum_scr[dn, (c + 1) * CHUNK - 1:(c + 1) * CHUNK, p * LANES:(p + 1) * LANES] if dn == 0 else
                   cum_scr[dn, c * CHUNK:c * CHUNK + 1, p * LANES:(p + 1) * LANES]) for dn, c, p in units]
    gram = each(lambda k, r, a, kd: _dgs(cat(k, r, 0), cat(bd(a), bd(kd), 0), NT), kt_s, rt_s, at_s, kdt_s)
    dirs = [u[0] for u in units]
    la = [jnp.where(strict[dn], g[0:CHUNK, 0:LANES], 0.0) for dn, g in zip(dirs, gram)]
    lk_s = [split(jnp.where(strict[dn], g[0:CHUNK, LANES:], 0.0)) for dn, g in zip(dirs, gram)]
    ra_s = [split(jnp.where(incl[dn], g[CHUNK:, 0:LANES], 0.0)) for dn, g in zip(dirs, gram)]
    rk_s = [split(jnp.where(incl[dn], g[CHUNK:, LANES:], 0.0)) for dn, g in zip(dirs, gram)]
    next_update()
    lrv = mm(each(lambda lk, rk: cat(lk, rk, 0), lk_s, rk_s), v_s)
    next_update()
    b = 8
    l8 = [jnp.where((row // b) == (col // b), x, 0.0) for x in la]
    l8_s = each(split, l8)
    l8_2 = mm(l8_s, l8_s)
    next_update()
    l8_2s = each(split, l8_2)
    l8_4 = mm(l8_2s, l8_2s)
    p1 = mm([split(eye - x) for x in l8], [split(eye + y) for y in l8_2])
    next_update()
    t = mm(each(split, p1), [split(eye + y) for y in l8_4])
    while b < CHUNK:
        next_update()
        offd = ((row // (2 * b)) == (col // (2 * b))) & ((row // b) != (col // b))
        t_s = each(split, t)
        x = mm(t_s, [split(jnp.where(offd, y, 0.0)) for y in la])
        t = each(lambda tt, z: tt - z, t, mm(each(split, x), t_s))
        b *= 2
    assert not pending, "more chunk updates than stages to place them between"
    tx = each(lambda tt, k, y: _dgs(split(tt), cat(bd(k), bd(split(y[0:CHUNK])), 1)), t, kt_s, lrv)
    khat_s = [split(y[:, 0:LANES]) for y in tx]
    w1_s = [split(y[:, LANES:]) for y in tx]
    rx = each(lambda r, k, w: _dgs(r, cat(bd(k), bd(w), 1)), ra_s, khat_s, w1_s)
    mk = each(lambda k, a: _dgs(k, a, TN), khat_s, at_s)
    nk = each(lambda vv, w, kd, a: _dgs(cat(vv, w, 0), cat(kd, tuple(-y for y in a), 0), TN),
              v_s, w1_s, kdt_s, at_s)
    for i, (dn, c, p) in enumerate(units):
        q_scr[par, dn, c, p] = (rt[i] - rx[i][:, 0:LANES]).astype(BF16)
        oo_scr[par, dn, c, p] = lrv[i][CHUNK:] - rx[i][:, LANES:]
        m_scr[par, dn, c, p] = ((eye2 - jnp.where(bd_mask, mk[i], 0.0)) * gam[i]).astype(BF16)
        n_scr[par, dn, c, p] = jnp.where(bd_mask, nk[i], 0.0) * gam[i]

    for dn in range(2):
        @pl.when((j > 0) & is_ctx[dn])
        def _():
            for n in range(SEQS_S - 1):
                sf_refs[dn][seq_slot(dn, n)] = sdone_scr[dn, seq_slot(dn, n)]
            sf_refs[dn][seq_slot(dn, SEQS_S - 1)] = s_scr[dn]


def _rwkv_scan(l, r, kap, v, lw, ah, kd, s0_lat, s_fin):
    def stage_blk(dn, j):
        jj = jnp.minimum(j, NBLK_S - 1)
        return jj if dn == 0 else NBLK_S - 1 - jj

    def update_blk(dn, j):
        jj = jnp.maximum(j - 1, 0)
        return jj if dn == 0 else NBLK_S - 1 - jj

    pair = (N_PAIR, LANES, LANES)
    tok = lambda dn: pl.BlockSpec((TS, W_RWKV), lambda j: (stage_blk(dn, j), 0))
    tok2 = lambda dn: pl.BlockSpec((None, TS, W_RWKV), lambda j: (dn, stage_blk(dn, j), 0))
    s0_spec = lambda dn: pl.BlockSpec(
        (None, None, None) + pair,
        lambda j: (jnp.maximum(update_blk(dn, j) - NCB_S, 0) // LAT_BLKS_S, l, dn, 0, 0, 0))
    sf_spec = lambda dn: pl.BlockSpec(
        (SEQS_S, None) + pair, lambda j: (jnp.minimum(update_blk(dn, j), NCB_S - 1), l, 0, 0, 0))
    o_spec = lambda dn: pl.BlockSpec((TS, W_RWKV), lambda j: (update_blk(dn, j), 0))
    dir_scr = lambda *shape, dtype=F32: pltpu.VMEM((2,) + shape, dtype)
    stage_scr = (2, NCS, N_PAIR)
    return pl.pallas_call(
        _scan_kernel,
        grid=(NBLK_S + 1,),
        in_specs=[tok(0), tok(0), tok(0), tok(1), tok(1), tok(1),
                  tok2(0), tok2(0), tok2(0), tok2(1), tok2(1), tok2(1),
                  s0_spec(0), s0_spec(1), ANY_SPEC, ANY_SPEC],
        out_specs=[o_spec(0), o_spec(1), sf_spec(0), sf_spec(1)],
        out_shape=[jax.ShapeDtypeStruct((N_TOK, W_RWKV), F32)] * 2
        + [jax.ShapeDtypeStruct((BATCH, DEPTH) + pair, F32)] * 2,
        scratch_shapes=[dir_scr(*pair), dir_scr(SEQS_S, *pair)] + [dir_scr(TS, W_RWKV)] * 5
        + [dir_scr(*stage_scr, LANES, LANES, dtype=BF16), dir_scr(*stage_scr, LANES, LANES),
           dir_scr(*stage_scr, CHUNK, LANES, dtype=BF16), dir_scr(*stage_scr, CHUNK, LANES)],
        input_output_aliases={14: 2, 15: 3},
        compiler_params=_cparams(1),
        name="rwkv_scan",
    )(r, kap, v, r, kap, v, lw, ah, kd, lw, ah, kd, s0_lat, s0_lat, s_fin[0], s_fin[1])


def _attend(groups):
    lhs = []
    for q_cols, _, _, _ in groups:
        left = _iota(q_cols[0].shape, 1) < HEAD_DIM
        parts = []
        for qc in q_cols:
            parts += [jnp.where(left, qc, 0.0), jnp.where(left, 0.0, qc)]
        lhs.append(jnp.concatenate(parts, axis=0).astype(BF16))
    s = [_dg(x, g[1], NT) for x, g in zip(lhs, groups)]
    s = [x if g[3] is None else x + g[3] for x, g in zip(s, groups)]
    p = [jnp.exp(x - jnp.max(x, axis=-1, keepdims=True)) for x in s]
    inv = [1.0 / jnp.sum(x, axis=-1, keepdims=True) for x in p]
    o = [_dg(x.astype(BF16), g[2]) * y for x, y, g in zip(p, inv, groups)]
    outs = []
    for x, (q_cols, _, _, _) in zip(o, groups):
        rows = q_cols[0].shape[0]
        left = _iota(q_cols[0].shape, 1) < HEAD_DIM
        outs.append([jnp.where(left, x[2 * j * rows:(2 * j + 1) * rows], x[(2 * j + 1) * rows:(2 * j + 2) * rows])
                     for j in range(len(q_cols))])
    return outs


def _cols(x):
    return [x[:, c * LANES:(c + 1) * LANES] for c in range(x.shape[1] // LANES)]


def _gqa_groups(q, k, v):
    left = _iota(k.shape, 1) < HEAD_DIM
    k_sw = pltpu.roll(k, HEAD_DIM, axis=1)
    v_sw = pltpu.roll(v, HEAD_DIM, axis=1)
    q_cols = _cols(q)
    groups = []
    for g in range(H_GQA_KV):
        k2 = jnp.where(left, k, k_sw) if g == 0 else jnp.where(left, k_sw, k)
        v2 = jnp.where(left, v, v_sw) if g == 0 else jnp.where(left, v_sw, v)
        groups.append((q_cols[2 * g:2 * g + 2], k2.astype(BF16), v2.astype(BF16), None))
    return groups


CTX_STEP_SEQS = 4


def _ctx_attn_kernel(naq_ref, nak_ref, nav_ref, gq_ref, gk_ref, gv_ref, ona_ref, og_ref):
    groups = []
    n_na = W_NA // LANES
    for n in range(CTX_STEP_SEQS):
        rs = slice(n * SEQ, (n + 1) * SEQ)
        k_cols = _cols(nak_ref[rs, :].astype(BF16))
        v_cols = _cols(nav_ref[rs, :].astype(BF16))
        groups += [([qc], kc, vc, None) for qc, kc, vc in zip(_cols(naq_ref[rs, :]), k_cols, v_cols)]
        groups += _gqa_groups(gq_ref[rs, :], gk_ref[rs, :], gv_ref[rs, :])
    outs = _attend(groups)
    per_seq = len(groups) // CTX_STEP_SEQS
    for n in range(CTX_STEP_SEQS):
        rs = slice(n * SEQ, (n + 1) * SEQ)
        o = outs[n * per_seq:(n + 1) * per_seq]
        ona_ref[rs, :] = jnp.concatenate([x[0] for x in o[:n_na]], axis=1)
        og_ref[rs, :] = jnp.concatenate([c for x in o[n_na:] for c in x], axis=1)


def _ctx_attention(naq, nak, nav, gq, gk, gv):
    spec = lambda w: pl.BlockSpec((CTX_STEP_SEQS * SEQ, w), lambda b: (b, 0))
    return pl.pallas_call(
        _ctx_attn_kernel,
        grid=(BATCH // CTX_STEP_SEQS,),
        in_specs=[spec(W_NA), spec(W_NA), spec(W_NA), spec(W_GQA), spec(W_GQA_KV), spec(W_GQA_KV)],
        out_specs=[spec(W_NA), spec(W_GQA)],
        out_shape=[jax.ShapeDtypeStruct((N_CTX, W_NA), F32), jax.ShapeDtypeStruct((N_CTX, W_GQA), F32)],
        compiler_params=_cparams(1),
        name="ctx_attention",
    )(naq, nak, nav, gq, gk, gv)


NA_STEP_ROWS = 8


def _lat_na_kernel(q_ref, k_ref, v_ref, kc_ref, vc_ref, tb_ref, o_ref):
    q = q_ref[...]
    groups = []
    for rr in range(NA_STEP_ROWS):
        r = pl.program_id(1) * NA_STEP_ROWS + rr
        r0 = jnp.clip(r - NA_ROWS // 2, 0, GRID_ROWS - NA_ROWS)
        band = pl.ds(pl.multiple_of(r0 * GRID_W, GRID_W), N_BAND)
        dr0 = r0 - r + NA_ROWS - 1
        for c, qc in enumerate(_cols(q[rr * GRID_W:(rr + 1) * GRID_W])):
            cols = slice(c * LANES, (c + 1) * LANES)
            k2 = jnp.concatenate([k_ref[band, cols], kc_ref[:, cols]], axis=0).astype(BF16)
            v2 = jnp.concatenate([v_ref[band, cols], vc_ref[:, cols]], axis=0).astype(BF16)
            bias = jnp.concatenate(
                [jnp.concatenate([tb_ref[2 * c + half, dr0 + jj] for jj in range(0, NA_ROWS, 2)], axis=1)
                 for half in range(2)], axis=0)
            bias = jnp.concatenate([bias, jnp.zeros((2 * GRID_W, PAST_LEN), F32)], axis=1)
            groups.append(([qc], k2, v2, bias))
    outs = [o[0] for o in _attend(groups)]
    n_col = W_NA // LANES
    o_ref[...] = jnp.concatenate(
        [jnp.concatenate(outs[rr * n_col:(rr + 1) * n_col], axis=1) for rr in range(NA_STEP_ROWS)], axis=0)


def _lat_na(l, naq, nak, nav, kc, vc, tb):
    rows = NA_STEP_ROWS * GRID_W
    steps = GRID_ROWS // NA_STEP_ROWS
    seq_blk0 = N_CTX // DEC_SEQ
    seq = pl.BlockSpec((DEC_SEQ, W_NA), lambda b, r: (seq_blk0 + b, 0))
    cache = pl.BlockSpec((None, None, PAST_LEN, W_NA), lambda b, r: (b, l, 0, 0))
    return pl.pallas_call(
        _lat_na_kernel,
        grid=(DEC_BATCH, steps),
        in_specs=[pl.BlockSpec((rows, W_NA), lambda b, r: (N_CTX // rows + b * steps + r, 0)),
                  seq, seq, cache, cache, _layer_spec(l, (H_NA, N_DR - 1, GRID_W, LANES))],
        out_specs=pl.BlockSpec((rows, W_NA), lambda b, r: (b * steps + r, 0)),
        out_shape=jax.ShapeDtypeStruct((N_LAT, W_NA), F32),
        compiler_params=_cparams(2),
        name="latent_na",
    )(naq, nak, nav, kc, vc, tb)


def _lat_gqa_kernel(q_ref, k_ref, v_ref, kc_ref, vc_ref, o_ref):
    k = jnp.concatenate([kc_ref[...], k_ref[...]], axis=0)
    v = jnp.concatenate([vc_ref[...], v_ref[...]], axis=0)
    o_ref[...] = jnp.concatenate([c for o in _attend(_gqa_groups(q_ref[...], k, v)) for c in o], axis=1)


GQA_ROWS = 256


def _lat_gqa(l, gq, gk, gv, kc, vc):
    seq_blk0 = N_CTX // DEC_SEQ
    steps = DEC_SEQ // GQA_ROWS
    seq = pl.BlockSpec((DEC_SEQ, W_GQA_KV), lambda b, i: (seq_blk0 + b, 0))
    cache = pl.BlockSpec((None, None, PAST_LEN, W_GQA_KV), lambda b, i: (b, l, 0, 0))
    return pl.pallas_call(
        _lat_gqa_kernel,
        grid=(DEC_BATCH, steps),
        in_specs=[pl.BlockSpec((GQA_ROWS, W_GQA), lambda b, i: (N_CTX // GQA_ROWS + b * steps + i, 0)),
                  seq, seq, cache, cache],
        out_specs=pl.BlockSpec((GQA_ROWS, W_GQA), lambda b, i: (b * steps + i, 0)),
        out_shape=jax.ShapeDtypeStruct((N_LAT, W_GQA), F32),
        compiler_params=_cparams(2),
        name="latent_gqa",
    )(gq, gk, gv, kc, vc)


def _mix_ffn_kernel(xc_ref, xl_ref, mod_ref, of_ref, ob_ref, g_ref, bonus_ref, lnxw_ref, lnxb_ref,
                    onac_ref, onal_ref, ogc_ref, ogl_ref,
                    wout_ref, ln1w_ref, ln1b_ref, wfi_ref, wfo_ref, ln2w_ref, ln2b_ref, yc_ref, yl_ref):
    i = pl.program_id(0)
    row = _mod_row(i)
    mod = lambda n: mod_ref[pl.ds(row, 1), n * D_MODEL:(n + 1) * D_MODEL]
    each = lambda f, *lists: [f(*args) for args in zip(*lists)]

    def rwkv_out(rs):
        o = of_ref[rs, :] + ob_ref[rs, :]
        mu = _seg64_sum(o) * (1.0 / HEAD_DIM)
        oc = o - mu
        var = _seg64_sum(oc * oc) * (1.0 / HEAD_DIM)
        o_rwkv = (oc * lax.rsqrt(var + GN_EPS) * lnxw_ref[...] + lnxb_ref[...] + bonus_ref[rs, :]) * g_ref[rs, :]
        return jnp.concatenate([o_rwkv, _pick(i, onac_ref, onal_ref, rs), _pick(i, ogc_ref, ogl_ref, rs)],
                               axis=1).astype(BF16)

    mix = [_dg(rwkv_out(rs), wout_ref[...]) for rs in SUB_ROWS]
    x1 = [_layer_norm(DEEPNORM_ALPHA * _pick(i, xc_ref, xl_ref, rs) + mod(2) * m, ln1w_ref[...], ln1b_ref[...])
          for rs, m in zip(SUB_ROWS, mix)]
    x_in = [(x * (1.0 + mod(4)) + mod(3)).astype(BF16) for x in x1]
    ffn = [jnp.zeros_like(x) for x in x1]
    for lo in range(0, D_FF, FF_CHUNK):
        gate = [_dg(x, wfi_ref[:, lo:lo + FF_CHUNK]) for x in x_in]
        up = [_dg(x, wfi_ref[:, D_FF + lo:D_FF + lo + FF_CHUNK]) for x in x_in]
        act = each(lambda gt, u: (gt * _sigmoid(gt) * u).astype(BF16), gate, up)
        ffn = each(lambda f, a: f + _dg(a, wfo_ref[lo:lo + FF_CHUNK, :]), ffn, act)
    y = each(lambda x, f: _layer_norm(DEEPNORM_ALPHA * x + mod(5) * f, ln2w_ref[...], ln2b_ref[...]), x1, ffn)

    @pl.when(i < NCB_D)
    def _():
        for rs, yy in zip(SUB_ROWS, y):
            yc_ref[rs, :] = yy

    @pl.when(i >= NCB_D)
    def _():
        for rs, yy in zip(SUB_ROWS, y):
            yl_ref[rs, :] = yy


def _mix_ffn(l, x_ctx, x_lat, mod_all, o_fwd, o_bwd, g, bonus, lnx_w, lnx_b, o_na_ctx, o_na_lat, o_g_ctx, o_g_lat,
             w_out_bf, ln1_w, ln1_b, w_ffn_in_bf, w_ffn_out_bf, ln2_w, ln2_b):
    tok = lambda w: pl.BlockSpec((TD, w), lambda i: (i, 0))
    once = lambda *tail: _layer_spec(l, tail, single=True)
    return pl.pallas_call(
        _mix_ffn_kernel,
        grid=(NBLK_D,),
        in_specs=[
            _ctx_spec(D_MODEL), _lat_spec(D_MODEL), once(MOD_ROWS, 6 * D_MODEL),
            tok(W_RWKV), tok(W_RWKV),
            tok(W_RWKV), tok(W_RWKV), once(1, W_RWKV), once(1, W_RWKV),
            _ctx_spec(W_NA), _lat_spec(W_NA), _ctx_spec(W_GQA), _lat_spec(W_GQA),
            once(D_MODEL, D_MODEL), once(1, D_MODEL), once(1, D_MODEL),
            once(D_MODEL, 2 * D_FF), once(D_FF, D_MODEL), once(1, D_MODEL), once(1, D_MODEL),
        ],
        out_specs=[_ctx_spec(D_MODEL), _lat_spec(D_MODEL)],
        out_shape=[jax.ShapeDtypeStruct((N_CTX, D_MODEL), F32), jax.ShapeDtypeStruct((N_LAT, D_MODEL), F32)],
        compiler_params=_cparams(1, VMEM_LIMIT_FFN),
        name="mix_ffn",
    )(x_ctx, x_lat, mod_all, o_fwd, o_bwd, g, bonus, lnx_w, lnx_b, o_na_ctx, o_na_lat, o_g_ctx, o_g_lat,
      w_out_bf, ln1_w, ln1_b, w_ffn_in_bf, w_ffn_out_bf, ln2_w, ln2_b)


def _rope_tables():
    t = jnp.arange(DEC_SEQ)
    inv = ROPE_BASE ** (-jnp.arange(ROPE_FREQ, dtype=F32) / ROPE_FREQ)
    ang_r = (t // GRID_W).astype(F32)[:, None] * inv
    ang_c = (t % GRID_W).astype(F32)[:, None] * inv
    cos = jnp.concatenate([jnp.cos(ang_r)] * 2 + [jnp.cos(ang_c)] * 2, axis=1)
    sin = jnp.concatenate([-jnp.sin(ang_r), jnp.sin(ang_r), -jnp.sin(ang_c), jnp.sin(ang_c)], axis=1)
    cos = jnp.concatenate([jnp.ones((TD, HEAD_DIM), F32), cos], axis=0)
    sin = jnp.concatenate([jnp.zeros((TD, HEAD_DIM), F32), sin], axis=0)
    return jnp.tile(cos, (1, LANES // HEAD_DIM)), jnp.tile(sin, (1, LANES // HEAD_DIM))


def _block_diag2(w):
    z = jnp.zeros_like(w[:, 0])
    return jnp.concatenate([jnp.concatenate([w[:, 0], z], axis=2), jnp.concatenate([z, w[:, 1]], axis=2)], axis=1)


def _hi_lo(w):
    return jnp.stack(_split(w), axis=1)


def _pair_states(s):
    lead = s.shape[:-3]
    s = s.reshape(lead + (N_PAIR, 2, HEAD_DIM, HEAD_DIM))
    z = jnp.zeros_like(s[..., 0, :, :])
    top = jnp.concatenate([s[..., 0, :, :], z], axis=-1)
    bot = jnp.concatenate([z, s[..., 1, :, :]], axis=-1)
    return jnp.concatenate([top, bot], axis=-2)


def _unpair_states(s):
    lead = s.shape[:-3]
    a = s[..., :HEAD_DIM, :HEAD_DIM]
    b = s[..., HEAD_DIM:, HEAD_DIM:]
    return jnp.stack([a, b], axis=-3).reshape(lead + (H_RWKV, HEAD_DIM, HEAD_DIM))


def kernel(x_prompt, x_sample, state_rwkv, cache_na_k, cache_na_v, cache_gqa_k, cache_gqa_v, c, c_ctx,
           w_mod, b_mod, w_in, rwkv_conv, rwkv_w0, rwkv_w2, rwkv_a0, rwkv_a2, rwkv_g2, rwkv_k_k, rwkv_k_a,
           rwkv_r_k, rwkv_lnx_w, rwkv_lnx_b, na_rpb, gqa_q_norm, gqa_k_norm, w_out, ln1_w, ln1_b,
           w_ffn_in, w_ffn_out, ln2_w, ln2_b):
    x_ctx, x_lat = x_prompt.reshape(N_CTX, D_MODEL), x_sample.reshape(N_LAT, D_MODEL)
    cc = jnp.concatenate([c_ctx[None], c, jnp.zeros((MOD_ROWS - 1 - DEC_BATCH, D_MODEL), F32)], axis=0)
    mod_all = _modulation(cc, w_mod, b_mod)
    tb_all = _bias_tables(na_rpb)
    cos_tab, sin_tab = _rope_tables()
    rows = lambda a: a.reshape(DEPTH, 1, -1)
    w_in_bf, w_out_bf = w_in.astype(BF16), w_out.astype(BF16)
    w_ffn_in_bf, w_ffn_out_bf = w_ffn_in.astype(BF16), w_ffn_out.astype(BF16)
    qn = jnp.tile(rows(gqa_q_norm), (1, 1, H_GQA))
    kn = jnp.tile(rows(gqa_k_norm), (1, 1, H_GQA_KV))
    rwkv_params = (rwkv_conv, rows(rwkv_w0), _hi_lo(_block_diag2(rwkv_w2)), rows(rwkv_a0),
                   _hi_lo(_block_diag2(rwkv_a2)), _hi_lo(rwkv_g2), rows(rwkv_k_k), rows(rwkv_k_a), rows(rwkv_r_k))
    s0_lat = _pair_states(state_rwkv)
    kc_na = cache_na_k.reshape(DEC_BATCH, DEPTH, PAST_LEN, W_NA)
    vc_na = cache_na_v.reshape(DEC_BATCH, DEPTH, PAST_LEN, W_NA)
    kc_g = cache_gqa_k.reshape(DEC_BATCH, DEPTH, PAST_LEN, W_GQA_KV)
    vc_g = cache_gqa_v.reshape(DEC_BATCH, DEPTH, PAST_LEN, W_GQA_KV)
    caches = [jnp.zeros((BATCH, DEPTH, SEQ, w), F32) for w in (W_NA, W_NA, W_GQA_KV, W_GQA_KV)]
    s_fin = [jnp.zeros((BATCH, DEPTH, N_PAIR, LANES, LANES), F32) for _ in range(2)]

    for l in range(DEPTH):
        naq, nak, nav, gq, gk, gv, r, kap, v, lw, ah, kd, g, bonus, *caches = _inproj(
            l, x_ctx, x_lat, mod_all, w_in_bf, qn, kn, cos_tab, sin_tab, rwkv_params, caches)
        o_fwd, o_bwd, *s_fin = _rwkv_scan(l, r, kap, v, lw, ah, kd, s0_lat, s_fin)
        o_na_ctx, o_g_ctx = _ctx_attention(naq, nak, nav, gq, gk, gv)
        o_na_lat = _lat_na(l, naq, nak, nav, kc_na, vc_na, tb_all)
        o_g_lat = _lat_gqa(l, gq, gk, gv, kc_g, vc_g)
        x_ctx, x_lat = _mix_ffn(
            l, x_ctx, x_lat, mod_all, o_fwd, o_bwd, g, bonus, rows(rwkv_lnx_w), rows(rwkv_lnx_b),
            o_na_ctx, o_na_lat, o_g_ctx, o_g_lat, w_out_bf, rows(ln1_w), rows(ln1_b),
            w_ffn_in_bf, w_ffn_out_bf, rows(ln2_w), rows(ln2_b))
    y_prompt = x_ctx.reshape(BATCH, SEQ, D_MODEL)
    y_sample = x_lat.reshape(DEC_BATCH, DEC_SEQ, D_MODEL)
    new_state = jnp.stack([_unpair_states(s) for s in s_fin], axis=2)
    return (y_prompt, y_sample, new_state,
            caches[0].reshape(BATCH, DEPTH, SEQ, H_NA, HEAD_DIM), caches[1].reshape(BATCH, DEPTH, SEQ, H_NA, HEAD_DIM),
            caches[2].reshape(BATCH, DEPTH, SEQ, H_GQA_KV, HEAD_DIM),
            caches[3].reshape(BATCH, DEPTH, SEQ, H_GQA_KV, HEAD_DIM))
```

```python
import jax
import jax.numpy as jnp
from jax import lax
from jax.experimental import pallas as pl
from jax.experimental.pallas import tpu as pltpu

F32 = jnp.float32
BF16 = jnp.bfloat16

D_MODEL = 1024
BATCH = 16
SEQ = 256
DEPTH = 4
DEC_BATCH = 2
DEC_SEQ = 1024
PAST_LEN = 512
GRID_W = 64
GRID_ROWS = DEC_SEQ // GRID_W
HEAD_DIM = 64
H_RWKV = 4
H_NA = 4
H_GQA = 8
H_GQA_KV = 2
W_RWKV = H_RWKV * HEAD_DIM
W_NA = H_NA * HEAD_DIM
W_GQA = H_GQA * HEAD_DIM
W_GQA_KV = H_GQA_KV * HEAD_DIM
LORA_W = 64
LORA_A = 64
LORA_G = 128
RWKV_IN = 3 * W_RWKV + 2 * LORA_W + 2 * LORA_A + LORA_G
NA_IN = 3 * W_NA
GQA_IN = W_GQA + 2 * W_GQA_KV
D_IN = RWKV_IN + NA_IN + GQA_IN
NA_ROWS = 8
NA_COLS = 16
ROPE_BASE = 10000.0
ROPE_FREQ = HEAD_DIM // 4
D_FF = ((8 * D_MODEL + 3 * 256 - 1) // (3 * 256)) * 256
DEEPNORM_ALPHA = (2 * DEPTH) ** 0.25
LN_EPS = 1e-5
RMS_EPS = 1e-6
GN_EPS = 64e-5
NEG_INF = -1e30
ATTN_SCALE = HEAD_DIM ** -0.5

LANES = 128
TM = 256
N_CTX = BATCH * SEQ
N_LAT = DEC_BATCH * DEC_SEQ
N_TOK = N_CTX + N_LAT
CHUNK = 64
TS = 512
NCS = TS // CHUNK
NBLK_S = N_TOK // TS
NCB_S = N_CTX // TS
LAT_BLKS_S = DEC_SEQ // TS
SEQS_S = TS // SEQ
SEQ_CHUNKS = SEQ // CHUNK
TD = 512
SUB_ROWS = [slice(h * SEQ, (h + 1) * SEQ) for h in range(TD // SEQ)]
IN_PIECES = SUB_ROWS
NBLK_D = N_TOK // TD
NCB_D = N_CTX // TD
LAT_BLKS_D = DEC_SEQ // TD
FF_CHUNK = D_FF
N_PAIR = W_RWKV // LANES
MOD_ROWS = 8
VMEM_LIMIT = 48 * 1024 * 1024
VMEM_LIMIT_FFN = 56 * 1024 * 1024
N_DR = 2 * NA_ROWS - 1
N_DC = 2 * NA_COLS - 1
N_BAND = NA_ROWS * GRID_W


def _cparams(n_grid, vmem_limit=VMEM_LIMIT):
    return pltpu.CompilerParams(dimension_semantics=("arbitrary",) * n_grid, vmem_limit_bytes=vmem_limit)


def _iota(shape, dim):
    return lax.broadcasted_iota(jnp.int32, shape, dim)


NN = (((1,), (0,)), ((), ()))
NT = (((1,), (1,)), ((), ()))
TN = (((0,), (0,)), ((), ()))


def _dg(a, b, dims=NN, precision=None):
    return lax.dot_general(a, b, dims, preferred_element_type=F32, precision=precision)


def _split(x):
    hi = x.astype(BF16)
    return hi, (x - hi.astype(F32)).astype(BF16)


def _dgs(a_s, b_s, dims=NN):
    if len(a_s) == 1:
        return _dg(a_s[0], b_s[0], dims)
    (ah, al), (bh, bl) = a_s, b_s
    ca, cb = dims[0][0][0], dims[0][1][0]
    return _dg(jnp.concatenate([ah, ah, al], axis=ca), jnp.concatenate([bh, bl, bh], axis=cb), dims)


def _sigmoid(x):
    return 1.0 / (1.0 + jnp.exp(-x))


def _softplus(x):
    return jnp.maximum(x, 0.0) + jnp.log(1.0 + jnp.exp(-jnp.abs(x)))


def _seg64_sum(x):
    rows, width = x.shape
    lo = _iota((rows, LANES), 1) < HEAD_DIM
    outs = []
    for c in range(width // LANES):
        blk = x[:, c * LANES:(c + 1) * LANES]
        s_lo = jnp.sum(jnp.where(lo, blk, 0.0), axis=-1, keepdims=True)
        s_hi = jnp.sum(jnp.where(lo, 0.0, blk), axis=-1, keepdims=True)
        outs.append(jnp.where(lo, s_lo, s_hi))
    return outs[0] if len(outs) == 1 else jnp.concatenate(outs, axis=1)


def _layer_norm(x, w, b):
    mu = jnp.mean(x, axis=-1, keepdims=True)
    xc = x - mu
    var = jnp.mean(xc * xc, axis=-1, keepdims=True)
    return xc * lax.rsqrt(var + LN_EPS) * w + b


def _mod_row(i):
    return jnp.where(i < NCB_D, 0, 1 + (i - NCB_D) // LAT_BLKS_D)


def _layer_spec(l, tail, single=False):
    idx = lambda *g: (l,) + (0,) * len(tail)
    if single:
        return pl.BlockSpec((None,) + tuple(tail), idx, pipeline_mode=pl.Buffered(1))
    return pl.BlockSpec((None,) + tuple(tail), idx)


ANY_SPEC = pl.BlockSpec(memory_space=pl.ANY)


def _ctx_spec(w):
    return pl.BlockSpec((TD, w), lambda i: (jnp.minimum(i, NCB_D - 1), 0))


def _lat_spec(w):
    return pl.BlockSpec((TD, w), lambda i: (jnp.maximum(i - NCB_D, 0), 0))


def _pick(i, ctx_ref, lat_ref, rows=slice(None)):
    return jnp.where(i < NCB_D, ctx_ref[rows, :], lat_ref[rows, :])


N_COND = 1 + DEC_BATCH


def _mod_kernel(c_ref, w_ref, b_ref, o_ref, sb_scr):
    @pl.when((pl.program_id(0) == 0) & (pl.program_id(1) == 0))
    def _():
        c = c_ref[...]
        s = c * _sigmoid(c)
        for r in range(N_COND):
            sb_scr[r] = jnp.broadcast_to(s[r:r + 1, :], (LANES, D_MODEL)).T

    def body(kc, accs):
        rows = pl.ds(pl.multiple_of(kc * 8, 8), 8)
        wk = w_ref[rows, :]
        return tuple(a + wk * jnp.concatenate([sb_scr[r, rows, :]] * (D_MODEL // LANES), axis=1)
                     for r, a in enumerate(accs))

    accs = lax.fori_loop(0, D_MODEL // 8, body, tuple(jnp.zeros((8, D_MODEL), F32) for _ in range(N_COND)),
                         unroll=8)
    rowid = _iota((MOD_ROWS, D_MODEL), 0)
    out = jnp.zeros((MOD_ROWS, D_MODEL), F32)
    for r, a in enumerate(accs):
        out = jnp.where(rowid == r, jnp.sum(a, axis=0, keepdims=True), out)
    o_ref[...] = out + b_ref[...]


def _modulation(cc, w_mod, b_mod):
    return pl.pallas_call(
        _mod_kernel,
        grid=(DEPTH, 6),
        in_specs=[
            pl.BlockSpec((MOD_ROWS, D_MODEL), lambda l, j: (0, 0)),
            pl.BlockSpec((None, D_MODEL, D_MODEL), lambda l, j: (l, 0, j)),
            pl.BlockSpec((None, 1, D_MODEL), lambda l, j: (l, 0, j)),
        ],
        out_specs=pl.BlockSpec((None, MOD_ROWS, D_MODEL), lambda l, j: (l, 0, j)),
        out_shape=jax.ShapeDtypeStruct((DEPTH, MOD_ROWS, 6 * D_MODEL), F32),
        scratch_shapes=[pltpu.VMEM((N_COND, D_MODEL, LANES), F32)],
        compiler_params=_cparams(2),
        name="modulation",
    )(cc, w_mod, b_mod.reshape(DEPTH, 1, 6 * D_MODEL))


def _bias_kernel(rpb_ref, o_ref):
    q = _iota((GRID_W, LANES), 0)
    x = _iota((GRID_W, LANES), 1)
    c = x % GRID_W
    right = x >= GRID_W
    dc = jnp.clip(c - q, 1 - NA_COLS, NA_COLS - 1) + NA_COLS - 1
    c0 = jnp.clip(q - NA_COLS // 2, 0, GRID_W - NA_COLS)
    in_win = (c >= c0) & (c < c0 + NA_COLS)

    def body(t, carry):
        h = t // (N_DR - 1)
        dr = t % (N_DR - 1)
        rows = [jnp.broadcast_to(rpb_ref[pl.ds(h * N_DR + dr + k, 1), :], (GRID_W, LANES)) for k in range(2)]
        lo, hi = (jnp.take_along_axis(r, dc, axis=1) for r in rows)
        o_ref[h, dr] = jnp.where(in_win, jnp.where(right, hi, lo), NEG_INF)
        return carry

    lax.fori_loop(0, H_NA * (N_DR - 1), body, 0, unroll=N_DR - 1)


RPB_ROWS = -(-H_NA * N_DR // 8) * 8


def _bias_tables(na_rpb):
    rpb = jnp.pad(na_rpb.reshape(DEPTH, H_NA * N_DR, N_DC), ((0, 0), (0, RPB_ROWS - H_NA * N_DR), (0, LANES - N_DC)))
    return pl.pallas_call(
        _bias_kernel,
        grid=(DEPTH,),
        in_specs=[pl.BlockSpec((None, RPB_ROWS, LANES), lambda l: (l, 0, 0))],
        out_specs=pl.BlockSpec((None, H_NA, N_DR - 1, GRID_W, LANES), lambda l: (l, 0, 0, 0, 0)),
        out_shape=jax.ShapeDtypeStruct((DEPTH, H_NA, N_DR - 1, GRID_W, LANES), F32),
        compiler_params=_cparams(1),
        name="na_bias_tables",
    )(rpb)


def _rope(x, cos, sin):
    k = x.shape[1] // LANES
    cosf = cos if k == 1 else jnp.concatenate([cos] * k, axis=1)
    sinf = sin if k == 1 else jnp.concatenate([sin] * k, axis=1)
    first = (_iota(x.shape, 1) % (2 * ROPE_FREQ)) < ROPE_FREQ
    partner = jnp.where(first, pltpu.roll(x, x.shape[1] - ROPE_FREQ, axis=1), pltpu.roll(x, ROPE_FREQ, axis=1))
    return x * cosf + partner * sinf


def _rwkv_features(x, prev_row, next_row, conv_ref, w0_ref, w2_ref, a0_ref, a2_ref, g2_ref, kk_ref, ka_ref, rk_ref):
    n = x.shape[0]
    rows = _iota(x.shape, 0)
    x_prev = jnp.where(rows == 0, prev_row, pltpu.roll(x, 1, axis=0))
    x_next = jnp.where(rows == n - 1, next_row, pltpu.roll(x, n - 1, axis=0))
    f = x_prev * conv_ref[0:1, :] + x * conv_ref[1:2, :] + x_next * conv_ref[2:3, :]
    o1, o2, o3 = W_RWKV, 2 * W_RWKV, 3 * W_RWKV
    o4 = o3 + 2 * LORA_W
    o5 = o4 + 2 * LORA_A
    r, k, v = f[:, :o1], f[:, o1:o2], f[:, o2:o3]
    wd, ad, gd = f[:, o3:o4], f[:, o4:o5], f[:, o5:]
    def lora(y, w_ref):
        y_bf = y.astype(BF16)
        return _dg(jnp.concatenate([y_bf, y_bf], axis=1), jnp.concatenate([w_ref[0], w_ref[1]], axis=0))
    log_w = -_softplus(-(w0_ref[...] + lora(jnp.tanh(wd), w2_ref))) - 0.5
    a = _sigmoid(a0_ref[...] + lora(ad, a2_ref))
    g = lora(_sigmoid(gd), g2_ref)
    kk = k * kk_ref[...]
    kap = kk / jnp.maximum(jnp.sqrt(_seg64_sum(kk * kk)), 1e-12)
    per_dir = []
    kd_sum = jnp.zeros_like(k)
    for d in range(2):
        a_d = a[:, d * W_RWKV:(d + 1) * W_RWKV]
        kd = k * (1.0 + (a_d - 1.0) * ka_ref[...])
        kd_sum = kd_sum + kd
        per_dir.append((-jnp.exp(log_w[:, d * W_RWKV:(d + 1) * W_RWKV]), a_d * kap, kd))
    bonus = _seg64_sum(r * kd_sum * rk_ref[...]) * v
    return r, kap, v, per_dir, g, bonus


HALO = 8


def _inproj_kernel(xc_ref, xl_ref, xp_ref, xn_ref, mod_ref, w_ref, qn_ref, kn_ref, cos_ref, sin_ref,
                   conv_ref, w0_ref, w2_ref, a0_ref, a2_ref, g2_ref, kk_ref, ka_ref, rk_ref,
                   c0_ref, c1_ref, c2_ref, c3_ref,
                   naq_ref, nak_ref, nav_ref, gq_ref, gk_ref, gv_ref,
                   r_ref, kap_ref, v_ref, lw_ref, ah_ref, kd_ref, g_ref, bonus_ref,
                   cnak_ref, cnav_ref, cgk_ref, cgv_ref):
    del c0_ref, c1_ref, c2_ref, c3_ref
    i = pl.program_id(0)
    row = _mod_row(i)
    shift1 = mod_ref[pl.ds(row, 1), 0:D_MODEL]
    scale1 = mod_ref[pl.ds(row, 1), D_MODEL:2 * D_MODEL]
    modulate = lambda x: (x * (1.0 + scale1) + shift1).astype(BF16)
    lat = i >= NCB_D
    pos = (i - NCB_D) % LAT_BLKS_D
    in_seq = jnp.where(lat, 1.0, 0.0)
    has_prev = jnp.where(lat & (pos != 0), 1.0, 0.0)
    has_next = jnp.where(lat & (pos != LAT_BLKS_D - 1), 1.0, 0.0)
    xm = [modulate(_pick(i, xc_ref, xl_ref, rs)) for rs in IN_PIECES]
    xm[0] = jnp.concatenate([xm[0], modulate(xp_ref[...]), modulate(xn_ref[...])], axis=0)
    projs = [_dg(x, w_ref[...]) for x in xm]
    n0 = IN_PIECES[0].stop
    halo = projs[0][n0:, :RWKV_IN]
    projs[0] = projs[0][:n0]
    feats = [proj[:, :RWKV_IN] for proj in projs]
    o_na = RWKV_IN
    o_g = RWKV_IN + NA_IN
    new_kv = []
    for h, (rs, proj) in enumerate(zip(IN_PIECES, projs)):
        if h == 0:
            prev_row = halo[HALO - 1:HALO] * has_prev
        else:
            prev_row = feats[h - 1][-1:] * (in_seq if rs.start % SEQ == 0 else 1.0)
        if h == len(IN_PIECES) - 1:
            next_row = halo[HALO:HALO + 1] * has_next
        else:
            next_row = feats[h + 1][0:1] * (in_seq if rs.stop % SEQ == 0 else 1.0)
        r, kap, v, per_dir, g, bonus = _rwkv_features(
            feats[h], prev_row, next_row, conv_ref, w0_ref, w2_ref, a0_ref, a2_ref, g2_ref, kk_ref, ka_ref, rk_ref)
        r_ref[rs, :] = r.astype(BF16)
        kap_ref[rs, :] = kap.astype(BF16)
        v_ref[rs, :] = v.astype(BF16)
        g_ref[rs, :] = g
        bonus_ref[rs, :] = bonus
        for d, (lw, ah, kd) in enumerate(per_dir):
            lw_ref[d, rs, :] = lw
            ah_ref[d, rs, :] = ah.astype(BF16)
            kd_ref[d, rs, :] = kd.astype(BF16)
        naq_ref[rs, :] = (proj[:, o_na:o_na + W_NA] * ATTN_SCALE).astype(BF16)
        nak = proj[:, o_na + W_NA:o_na + 2 * W_NA]
        nav = proj[:, o_na + 2 * W_NA:o_na + 3 * W_NA]
        q = proj[:, o_g:o_g + W_GQA]
        k = proj[:, o_g + W_GQA:o_g + W_GQA + W_GQA_KV]
        gv = proj[:, o_g + W_GQA + W_GQA_KV:]
        cos = cos_ref[rs, :]
        sin = sin_ref[rs, :]
        q = q * lax.rsqrt(_seg64_sum(q * q) * (1.0 / HEAD_DIM) + RMS_EPS) * qn_ref[...]
        k = k * lax.rsqrt(_seg64_sum(k * k) * (1.0 / HEAD_DIM) + RMS_EPS) * kn_ref[...]
        gk = _rope(k, cos, sin)
        gq_ref[rs, :] = (_rope(q, cos, sin) * ATTN_SCALE).astype(BF16)
        nak_ref[rs, :] = nak.astype(BF16)
        nav_ref[rs, :] = nav.astype(BF16)
        gk_ref[rs, :] = gk.astype(BF16)
        gv_ref[rs, :] = gv.astype(BF16)
        new_kv.append((nak, nav, gk, gv))

    @pl.when(i < NCB_D)
    def _():
        for rs, vals in zip(IN_PIECES, new_kv):
            for ref, val in zip((cnak_ref, cnav_ref, cgk_ref, cgv_ref), vals):
                ref[rs.start // SEQ, rs.start % SEQ:rs.start % SEQ + rs.stop - rs.start, :] = val


def _inproj(l, x_ctx, x_lat, mod_all, w_in_bf, qn, kn, cos_tab, sin_tab, rwkv_params, caches):
    tab_idx = lambda i: (jnp.where(i < NCB_D, 0, 1 + (i - NCB_D) % LAT_BLKS_D), 0)
    lat_blk = lambda i: jnp.maximum(i - NCB_D, 0)
    halo_blocks = TD // HALO
    widths = (W_NA, W_NA, W_NA, W_GQA, W_GQA_KV, W_GQA_KV, W_RWKV, W_RWKV, W_RWKV)
    cache_w = (W_NA, W_NA, W_GQA_KV, W_GQA_KV)
    tok = lambda w: pl.BlockSpec((TD, w), lambda i: (i, 0))
    tok2 = pl.BlockSpec((2, TD, W_RWKV), lambda i: (0, i, 0))
    cache_spec = lambda w: pl.BlockSpec((TD // SEQ, None, SEQ, w), lambda i: (jnp.minimum(i, NCB_D - 1), l, 0, 0))
    rwkv_tails = ((3, RWKV_IN), (1, 2 * W_RWKV), (2, 2 * LORA_W, 2 * W_RWKV), (1, 2 * W_RWKV),
                  (2, 2 * LORA_A, 2 * W_RWKV), (2, LORA_G, W_RWKV), (1, W_RWKV), (1, W_RWKV), (1, W_RWKV))
    in_specs = [
        _ctx_spec(D_MODEL), _lat_spec(D_MODEL),
        pl.BlockSpec((HALO, D_MODEL), lambda i: (jnp.maximum(lat_blk(i) * halo_blocks - 1, 0), 0)),
        pl.BlockSpec((HALO, D_MODEL), lambda i: (jnp.minimum((lat_blk(i) + 1) * halo_blocks, N_LAT // HALO - 1), 0)),
        _layer_spec(l, (MOD_ROWS, 6 * D_MODEL)),
        _layer_spec(l, (D_MODEL, D_IN)),
        _layer_spec(l, (1, W_GQA)),
        _layer_spec(l, (1, W_GQA_KV)),
        pl.BlockSpec((TD, LANES), tab_idx),
        pl.BlockSpec((TD, LANES), tab_idx),
    ] + [_layer_spec(l, t) for t in rwkv_tails]
    n_in = len(in_specs)
    out_specs = [tok(w) for w in widths] + [tok2] * 3 + [tok(W_RWKV)] * 2
    out_shape = ([jax.ShapeDtypeStruct((N_TOK, w), BF16) for w in widths]
                 + [jax.ShapeDtypeStruct((2, N_TOK, W_RWKV), dt) for dt in (F32, BF16, BF16)]
                 + [jax.ShapeDtypeStruct((N_TOK, W_RWKV), F32)] * 2)
    return pl.pallas_call(
        _inproj_kernel,
        grid=(NBLK_D,),
        in_specs=in_specs + [ANY_SPEC] * 4,
        out_specs=out_specs + [cache_spec(w) for w in cache_w],
        out_shape=out_shape + [jax.ShapeDtypeStruct((BATCH, DEPTH, SEQ, w), F32) for w in cache_w],
        input_output_aliases={n_in + j: len(out_specs) + j for j in range(4)},
        compiler_params=_cparams(1),
        name="inproj",
    )(x_ctx, x_lat, x_lat, x_lat, mod_all, w_in_bf, qn, kn, cos_tab, sin_tab, *rwkv_params, *caches)


def _scan_kernel(r0_ref, kap0_ref, v0_ref, r1_ref, kap1_ref, v1_ref,
                 lw0_ref, ah0_ref, kd0_ref, lw1_ref, ah1_ref, kd1_ref, s00_ref, s01_ref, sf0_in_ref, sf1_in_ref,
                 o0_ref, o1_ref, sf0_ref, sf1_ref,
                 s_scr, sdone_scr, rt_scr, kt_scr, kdt_scr, at_scr, cum_scr, m_scr, n_scr, q_scr, oo_scr):
    del sf0_in_ref, sf1_in_ref
    j = pl.program_id(0)
    par = j % 2
    jb = jnp.maximum(j - 1, 0)
    upd_blk = (jb, NBLK_S - 1 - jb)
    is_ctx = tuple(b < NCB_S for b in upd_blk)
    first_pos = (0, LAT_BLKS_S - 1)
    r_refs, kap_refs, v_refs = (r0_ref, r1_ref), (kap0_ref, kap1_ref), (v0_ref, v1_ref)
    lw_refs, ah_refs, kd_refs = (lw0_ref, lw1_ref), (ah0_ref, ah1_ref), (kd0_ref, kd1_ref)
    s0_refs, o_refs, sf_refs = (s00_ref, s01_ref), (o0_ref, o1_ref), (sf0_ref, sf1_ref)

    @pl.when(j == 0)
    def _():
        for ref in (s_scr, sdone_scr, m_scr, n_scr, q_scr, oo_scr):
            ref[...] = jnp.zeros_like(ref)

    for dn in range(2):
        blk = upd_blk[dn]

        @pl.when((j > 0) & is_ctx[dn])
        def _():
            s_scr[dn] = jnp.zeros(s_scr.shape[1:], F32)

        @pl.when((j > 0) & (blk >= NCB_S) & ((blk - NCB_S) % LAT_BLKS_S == first_pos[dn]))
        def _():
            s_scr[dn] = s0_refs[dn][...]

    def seq_slot(dn, n):
        return n if dn == 0 else SEQS_S - 1 - n

    def update(c):
        for dn in range(2):
            ce = c if dn == 0 else NCS - 1 - c
            rows = slice(ce * CHUNK, (ce + 1) * CHUNK)
            for p in range(N_PAIR):
                s = s_scr[dn, p]
                if c > 0 and c % SEQ_CHUNKS == 0:
                    sdone_scr[dn, seq_slot(dn, c // SEQ_CHUNKS - 1), p] = s
                    s = jnp.where(is_ctx[dn], 0.0, s)
                s2 = jnp.concatenate(_split(s), axis=1)
                q = q_scr[1 - par, dn, ce, p]
                m = m_scr[1 - par, dn, ce, p]
                o_refs[dn][rows, p * LANES:(p + 1) * LANES] = (
                    _dg(jnp.concatenate([q, q], axis=1), s2, NT) + oo_scr[1 - par, dn, ce, p])
                s_scr[dn, p] = _dg(s2, jnp.concatenate([m, m], axis=0)) + n_scr[1 - par, dn, ce, p]

    pending = list(range(NCS))

    def next_update():
        if pending:
            update(pending.pop(0))

    next_update()
    rr = _iota((TM, TM), 0)
    cc = _iota((TM, TM), 1)
    same_chunk = (rr // CHUNK) == (cc // CHUNK)
    for dn in range(2):
        order = (rr >= cc) if dn == 0 else (rr <= cc)
        tri = jnp.where(same_chunk & order, 1.0, 0.0).astype(BF16)
        for h in range(TS // TM):
            rs = slice(h * TM, (h + 1) * TM)
            lw = lw_refs[dn][rs, :]
            lw_hi = lw.astype(BF16)
            lw_r = lw - lw_hi.astype(F32)
            lw_mid = lw_r.astype(BF16)
            lw_lo = (lw_r - lw_mid.astype(F32)).astype(BF16)
            cum = _dg(jnp.concatenate([tri, tri, tri], axis=1), jnp.concatenate([lw_hi, lw_mid, lw_lo], axis=0))
            e_neg = jnp.exp(-cum)
            rt_scr[dn, rs, :] = r_refs[dn][rs, :] * jnp.exp(cum)
            kt_scr[dn, rs, :] = kap_refs[dn][rs, :] * jnp.exp(cum - lw)
            kdt_scr[dn, rs, :] = kd_refs[dn][rs, :] * e_neg
            at_scr[dn, rs, :] = ah_refs[dn][rs, :] * e_neg
            cum_scr[dn, rs, :] = cum
        if dn == 0:
            next_update()

    row = _iota((CHUNK, LANES), 0)
    col = _iota((CHUNK, LANES), 1) % CHUNK
    left = _iota((CHUNK, LANES), 1) < CHUNK
    incl = (row >= col, row <= col)
    strict = (row > col, row < col)
    eye = jnp.where(row == col, 1.0, 0.0)
    r2 = _iota((LANES, LANES), 0)
    c2 = _iota((LANES, LANES), 1)
    bd_mask = (r2 // CHUNK) == (c2 // CHUNK)
    eye2 = jnp.where(r2 == c2, 1.0, 0.0)

    zero_bf = jnp.zeros((CHUNK, LANES), BF16)
    split = lambda x: (x.astype(BF16),)

    def bd(xs):
        return tuple(jnp.concatenate([jnp.where(left, y, zero_bf), jnp.where(left, zero_bf, y)], axis=0)
                     for y in xs)

    def cat(xs, ys, axis):
        return tuple(jnp.concatenate([x, y], axis=axis) for x, y in zip(xs, ys))

    def mm(a_list, b_list):
        return [_dgs(a, bd(b)) for a, b in zip(a_list, b_list)]

    units = [(dn, c, p) for c in range(NCS) for dn in range(2) for p in range(N_PAIR)]
    each = lambda f, *lists: [f(*args) for args in zip(*lists)]

    def load(ref):
        return [ref[dn, c * CHUNK:(c + 1) * CHUNK, p * LANES:(p + 1) * LANES] for dn, c, p in units]

    rt = load(rt_scr)
    v = [v_refs[dn][c * CHUNK:(c + 1) * CHUNK, p * LANES:(p + 1) * LANES] for dn, c, p in units]
    kt_s, rt_s, kdt_s, at_s, v_s = (each(split, x) for x in (load(kt_scr), rt, load(kdt_scr), load(at_scr), v))
    gam = [jnp.exp(cum_scr[dn, (c + 1) * CHUNK - 1:(c + 1) * CHUNK, p * LANES:(p + 1) * LANES] if dn == 0 else
                   cum_scr[dn, c * CHUNK:c * CHUNK + 1, p * LANES:(p + 1) * LANES]) for dn, c, p in units]
    gram = each(lambda k, r, a, kd: _dgs(cat(k, r, 0), cat(bd(a), bd(kd), 0), NT), kt_s, rt_s, at_s, kdt_s)
    dirs = [u[0] for u in units]
    la = [jnp.where(strict[dn], g[0:CHUNK, 0:LANES], 0.0) for dn, g in zip(dirs, gram)]
    lk_s = [split(jnp.where(strict[dn], g[0:CHUNK, LANES:], 0.0)) for dn, g in zip(dirs, gram)]
    ra_s = [split(jnp.where(incl[dn], g[CHUNK:, 0:LANES], 0.0)) for dn, g in zip(dirs, gram)]
    rk_s = [split(jnp.where(incl[dn], g[CHUNK:, LANES:], 0.0)) for dn, g in zip(dirs, gram)]
    next_update()
    lrv = mm(each(lambda lk, rk: cat(lk, rk, 0), lk_s, rk_s), v_s)
    next_update()
    b = 8
    l8 = [jnp.where((row // b) == (col // b), x, 0.0) for x in la]
    l8_s = each(split, l8)
    l8_2 = mm(l8_s, l8_s)
    next_update()
    l8_2s = each(split, l8_2)
    l8_4 = mm(l8_2s, l8_2s)
    p1 = mm([split(eye - x) for x in l8], [split(eye + y) for y in l8_2])
    next_update()
    t = mm(each(split, p1), [split(eye + y) for y in l8_4])
    while b < CHUNK:
        next_update()
        offd = ((row // (2 * b)) == (col // (2 * b))) & ((row // b) != (col // b))
        t_s = each(split, t)
        x = mm(t_s, [split(jnp.where(offd, y, 0.0)) for y in la])
        t = each(lambda tt, z: tt - z, t, mm(each(split, x), t_s))
        b *= 2
    assert not pending, "more chunk updates than stages to place them between"
    tx = each(lambda tt, k, y: _dgs(split(tt), cat(bd(k), bd(split(y[0:CHUNK])), 1)), t, kt_s, lrv)
    khat_s = [split(y[:, 0:LANES]) for y in tx]
    w1_s = [split(y[:, LANES:]) for y in tx]
    rx = each(lambda r, k, w: _dgs(r, cat(bd(k), bd(w), 1)), ra_s, khat_s, w1_s)
    mk = each(lambda k, a: _dgs(k, a, TN), khat_s, at_s)
    nk = each(lambda vv, w, kd, a: _dgs(cat(vv, w, 0), cat(kd, tuple(-y for y in a), 0), TN),
              v_s, w1_s, kdt_s, at_s)
    for i, (dn, c, p) in enumerate(units):
        q_scr[par, dn, c, p] = (rt[i] - rx[i][:, 0:LANES]).astype(BF16)
        oo_scr[par, dn, c, p] = lrv[i][CHUNK:] - rx[i][:, LANES:]
        m_scr[par, dn, c, p] = ((eye2 - jnp.where(bd_mask, mk[i], 0.0)) * gam[i]).astype(BF16)
        n_scr[par, dn, c, p] = jnp.where(bd_mask, nk[i], 0.0) * gam[i]

    for dn in range(2):
        @pl.when((j > 0) & is_ctx[dn])
        def _():
            for n in range(SEQS_S - 1):
                sf_refs[dn][seq_slot(dn, n)] = sdone_scr[dn, seq_slot(dn, n)]
            sf_refs[dn][seq_slot(dn, SEQS_S - 1)] = s_scr[dn]


def _rwkv_scan(l, r, kap, v, lw, ah, kd, s0_lat, s_fin):
    def stage_blk(dn, j):
        jj = jnp.minimum(j, NBLK_S - 1)
        return jj if dn == 0 else NBLK_S - 1 - jj

    def update_blk(dn, j):
        jj = jnp.maximum(j - 1, 0)
        return jj if dn == 0 else NBLK_S - 1 - jj

    pair = (N_PAIR, LANES, LANES)
    tok = lambda dn: pl.BlockSpec((TS, W_RWKV), lambda j: (stage_blk(dn, j), 0))
    tok2 = lambda dn: pl.BlockSpec((None, TS, W_RWKV), lambda j: (dn, stage_blk(dn, j), 0))
    s0_spec = lambda dn: pl.BlockSpec(
        (None, None, None) + pair,
        lambda j: (jnp.maximum(update_blk(dn, j) - NCB_S, 0) // LAT_BLKS_S, l, dn, 0, 0, 0))
    sf_spec = lambda dn: pl.BlockSpec(
        (SEQS_S, None) + pair, lambda j: (jnp.minimum(update_blk(dn, j), NCB_S - 1), l, 0, 0, 0))
    o_spec = lambda dn: pl.BlockSpec((TS, W_RWKV), lambda j: (update_blk(dn, j), 0))
    dir_scr = lambda *shape, dtype=F32: pltpu.VMEM((2,) + shape, dtype)
    stage_scr = (2, NCS, N_PAIR)
    return pl.pallas_call(
        _scan_kernel,
        grid=(NBLK_S + 1,),
        in_specs=[tok(0), tok(0), tok(0), tok(1), tok(1), tok(1),
                  tok2(0), tok2(0), tok2(0), tok2(1), tok2(1), tok2(1),
                  s0_spec(0), s0_spec(1), ANY_SPEC, ANY_SPEC],
        out_specs=[o_spec(0), o_spec(1), sf_spec(0), sf_spec(1)],
        out_shape=[jax.ShapeDtypeStruct((N_TOK, W_RWKV), F32)] * 2
        + [jax.ShapeDtypeStruct((BATCH, DEPTH) + pair, F32)] * 2,
        scratch_shapes=[dir_scr(*pair), dir_scr(SEQS_S, *pair)] + [dir_scr(TS, W_RWKV)] * 5
        + [dir_scr(*stage_scr, LANES, LANES, dtype=BF16), dir_scr(*stage_scr, LANES, LANES),
           dir_scr(*stage_scr, CHUNK, LANES, dtype=BF16), dir_scr(*stage_scr, CHUNK, LANES)],
        input_output_aliases={14: 2, 15: 3},
        compiler_params=_cparams(1),
        name="rwkv_scan",
    )(r, kap, v, r, kap, v, lw, ah, kd, lw, ah, kd, s0_lat, s0_lat, s_fin[0], s_fin[1])


def _attend(groups):
    lhs = []
    for q_cols, _, _, _ in groups:
        left = _iota(q_cols[0].shape, 1) < HEAD_DIM
        parts = []
        for qc in q_cols:
            parts += [jnp.where(left, qc, 0.0), jnp.where(left, 0.0, qc)]
        lhs.append(jnp.concatenate(parts, axis=0).astype(BF16))
    s = [_dg(x, g[1], NT) for x, g in zip(lhs, groups)]
    s = [x if g[3] is None else x + g[3] for x, g in zip(s, groups)]
    p = [jnp.exp(x - jnp.max(x, axis=-1, keepdims=True)) for x in s]
    inv = [1.0 / jnp.sum(x, axis=-1, keepdims=True) for x in p]
    o = [_dg(x.astype(BF16), g[2]) * y for x, y, g in zip(p, inv, groups)]
    outs = []
    for x, (q_cols, _, _, _) in zip(o, groups):
        rows = q_cols[0].shape[0]
        left = _iota(q_cols[0].shape, 1) < HEAD_DIM
        outs.append([jnp.where(left, x[2 * j * rows:(2 * j + 1) * rows], x[(2 * j + 1) * rows:(2 * j + 2) * rows])
                     for j in range(len(q_cols))])
    return outs


def _cols(x):
    return [x[:, c * LANES:(c + 1) * LANES] for c in range(x.shape[1] // LANES)]


def _gqa_groups(q, k, v):
    k, v = k.astype(F32), v.astype(F32)
    left = _iota(k.shape, 1) < HEAD_DIM
    k_sw = pltpu.roll(k, HEAD_DIM, axis=1)
    v_sw = pltpu.roll(v, HEAD_DIM, axis=1)
    q_cols = _cols(q)
    groups = []
    for g in range(H_GQA_KV):
        k2 = jnp.where(left, k, k_sw) if g == 0 else jnp.where(left, k_sw, k)
        v2 = jnp.where(left, v, v_sw) if g == 0 else jnp.where(left, v_sw, v)
        groups.append((q_cols[2 * g:2 * g + 2], k2.astype(BF16), v2.astype(BF16), None))
    return groups


CTX_STEP_SEQS = 4


def _ctx_attn_kernel(naq_ref, nak_ref, nav_ref, gq_ref, gk_ref, gv_ref, ona_ref, og_ref):
    groups = []
    n_na = W_NA // LANES
    for n in range(CTX_STEP_SEQS):
        rs = slice(n * SEQ, (n + 1) * SEQ)
        k_cols = _cols(nak_ref[rs, :].astype(BF16))
        v_cols = _cols(nav_ref[rs, :].astype(BF16))
        groups += [([qc], kc, vc, None) for qc, kc, vc in zip(_cols(naq_ref[rs, :]), k_cols, v_cols)]
        groups += _gqa_groups(gq_ref[rs, :], gk_ref[rs, :], gv_ref[rs, :])
    outs = _attend(groups)
    per_seq = len(groups) // CTX_STEP_SEQS
    for n in range(CTX_STEP_SEQS):
        rs = slice(n * SEQ, (n + 1) * SEQ)
        o = outs[n * per_seq:(n + 1) * per_seq]
        ona_ref[rs, :] = jnp.concatenate([x[0] for x in o[:n_na]], axis=1)
        og_ref[rs, :] = jnp.concatenate([c for x in o[n_na:] for c in x], axis=1)


def _ctx_attention(naq, nak, nav, gq, gk, gv):
    spec = lambda w: pl.BlockSpec((CTX_STEP_SEQS * SEQ, w), lambda b: (b, 0))
    return pl.pallas_call(
        _ctx_attn_kernel,
        grid=(BATCH // CTX_STEP_SEQS,),
        in_specs=[spec(W_NA), spec(W_NA), spec(W_NA), spec(W_GQA), spec(W_GQA_KV), spec(W_GQA_KV)],
        out_specs=[spec(W_NA), spec(W_GQA)],
        out_shape=[jax.ShapeDtypeStruct((N_CTX, W_NA), F32), jax.ShapeDtypeStruct((N_CTX, W_GQA), F32)],
        compiler_params=_cparams(1),
        name="ctx_attention",
    )(naq, nak, nav, gq, gk, gv)


NA_STEP_ROWS = 8


def _lat_na_kernel(q_ref, k_ref, v_ref, kc_ref, vc_ref, tb_ref, o_ref):
    q = q_ref[...]
    groups = []
    for rr in range(NA_STEP_ROWS):
        r = pl.program_id(1) * NA_STEP_ROWS + rr
        r0 = jnp.clip(r - NA_ROWS // 2, 0, GRID_ROWS - NA_ROWS)
        band = pl.ds(pl.multiple_of(r0 * GRID_W, GRID_W), N_BAND)
        dr0 = r0 - r + NA_ROWS - 1
        for c, qc in enumerate(_cols(q[rr * GRID_W:(rr + 1) * GRID_W])):
            cols = slice(c * LANES, (c + 1) * LANES)
            k2 = jnp.concatenate([k_ref[band, cols], kc_ref[:, cols].astype(BF16)], axis=0)
            v2 = jnp.concatenate([v_ref[band, cols], vc_ref[:, cols].astype(BF16)], axis=0)
            bias = jnp.concatenate(
                [jnp.concatenate([tb_ref[2 * c + half, dr0 + jj] for jj in range(0, NA_ROWS, 2)], axis=1)
                 for half in range(2)], axis=0)
            bias = jnp.concatenate([bias, jnp.zeros((2 * GRID_W, PAST_LEN), F32)], axis=1)
            groups.append(([qc], k2, v2, bias))
    outs = [o[0] for o in _attend(groups)]
    n_col = W_NA // LANES
    o_ref[...] = jnp.concatenate(
        [jnp.concatenate(outs[rr * n_col:(rr + 1) * n_col], axis=1) for rr in range(NA_STEP_ROWS)], axis=0)


def _lat_na(l, naq, nak, nav, kc, vc, tb):
    rows = NA_STEP_ROWS * GRID_W
    steps = GRID_ROWS // NA_STEP_ROWS
    seq_blk0 = N_CTX // DEC_SEQ
    seq = pl.BlockSpec((DEC_SEQ, W_NA), lambda b, r: (seq_blk0 + b, 0))
    cache = pl.BlockSpec((None, None, PAST_LEN, W_NA), lambda b, r: (b, l, 0, 0))
    return pl.pallas_call(
        _lat_na_kernel,
        grid=(DEC_BATCH, steps),
        in_specs=[pl.BlockSpec((rows, W_NA), lambda b, r: (N_CTX // rows + b * steps + r, 0)),
                  seq, seq, cache, cache, _layer_spec(l, (H_NA, N_DR - 1, GRID_W, LANES))],
        out_specs=pl.BlockSpec((rows, W_NA), lambda b, r: (b * steps + r, 0)),
        out_shape=jax.ShapeDtypeStruct((N_LAT, W_NA), F32),
        compiler_params=_cparams(2),
        name="latent_na",
    )(naq, nak, nav, kc, vc, tb)


def _lat_gqa_kernel(q_ref, k_ref, v_ref, kc_ref, vc_ref, o_ref):
    k = jnp.concatenate([kc_ref[...], k_ref[...].astype(F32)], axis=0)
    v = jnp.concatenate([vc_ref[...], v_ref[...].astype(F32)], axis=0)
    o_ref[...] = jnp.concatenate([c for o in _attend(_gqa_groups(q_ref[...], k, v)) for c in o], axis=1)


GQA_ROWS = 256


def _lat_gqa(l, gq, gk, gv, kc, vc):
    seq_blk0 = N_CTX // DEC_SEQ
    steps = DEC_SEQ // GQA_ROWS
    seq = pl.BlockSpec((DEC_SEQ, W_GQA_KV), lambda b, i: (seq_blk0 + b, 0))
    cache = pl.BlockSpec((None, None, PAST_LEN, W_GQA_KV), lambda b, i: (b, l, 0, 0))
    return pl.pallas_call(
        _lat_gqa_kernel,
        grid=(DEC_BATCH, steps),
        in_specs=[pl.BlockSpec((GQA_ROWS, W_GQA), lambda b, i: (N_CTX // GQA_ROWS + b * steps + i, 0)),
                  seq, seq, cache, cache],
        out_specs=pl.BlockSpec((GQA_ROWS, W_GQA), lambda b, i: (b * steps + i, 0)),
        out_shape=jax.ShapeDtypeStruct((N_LAT, W_GQA), F32),
        compiler_params=_cparams(2),
        name="latent_gqa",
    )(gq, gk, gv, kc, vc)


def _mix_ffn_kernel(xc_ref, xl_ref, mod_ref, of_ref, ob_ref, g_ref, bonus_ref, lnxw_ref, lnxb_ref,
                    onac_ref, onal_ref, ogc_ref, ogl_ref,
                    wout_ref, ln1w_ref, ln1b_ref, wfi_ref, wfo_ref, ln2w_ref, ln2b_ref, yc_ref, yl_ref):
    i = pl.program_id(0)
    row = _mod_row(i)
    mod = lambda n: mod_ref[pl.ds(row, 1), n * D_MODEL:(n + 1) * D_MODEL]
    each = lambda f, *lists: [f(*args) for args in zip(*lists)]

    def rwkv_out(rs):
        o = of_ref[rs, :] + ob_ref[rs, :]
        mu = _seg64_sum(o) * (1.0 / HEAD_DIM)
        oc = o - mu
        var = _seg64_sum(oc * oc) * (1.0 / HEAD_DIM)
        o_rwkv = (oc * lax.rsqrt(var + GN_EPS) * lnxw_ref[...] + lnxb_ref[...] + bonus_ref[rs, :]) * g_ref[rs, :]
        return jnp.concatenate([o_rwkv, _pick(i, onac_ref, onal_ref, rs), _pick(i, ogc_ref, ogl_ref, rs)],
                               axis=1).astype(BF16)

    mix = [_dg(rwkv_out(rs), wout_ref[...]) for rs in SUB_ROWS]
    x1 = [_layer_norm(DEEPNORM_ALPHA * _pick(i, xc_ref, xl_ref, rs) + mod(2) * m, ln1w_ref[...], ln1b_ref[...])
          for rs, m in zip(SUB_ROWS, mix)]
    x_in = [(x * (1.0 + mod(4)) + mod(3)).astype(BF16) for x in x1]
    ffn = [jnp.zeros_like(x) for x in x1]
    for lo in range(0, D_FF, FF_CHUNK):
        gate = [_dg(x, wfi_ref[:, lo:lo + FF_CHUNK]) for x in x_in]
        up = [_dg(x, wfi_ref[:, D_FF + lo:D_FF + lo + FF_CHUNK]) for x in x_in]
        act = each(lambda gt, u: (gt * _sigmoid(gt) * u).astype(BF16), gate, up)
        ffn = each(lambda f, a: f + _dg(a, wfo_ref[lo:lo + FF_CHUNK, :]), ffn, act)
    y = each(lambda x, f: _layer_norm(DEEPNORM_ALPHA * x + mod(5) * f, ln2w_ref[...], ln2b_ref[...]), x1, ffn)

    @pl.when(i < NCB_D)
    def _():
        for rs, yy in zip(SUB_ROWS, y):
            yc_ref[rs, :] = yy

    @pl.when(i >= NCB_D)
    def _():
        for rs, yy in zip(SUB_ROWS, y):
            yl_ref[rs, :] = yy


def _mix_ffn(l, x_ctx, x_lat, mod_all, o_fwd, o_bwd, g, bonus, lnx_w, lnx_b, o_na_ctx, o_na_lat, o_g_ctx, o_g_lat,
             w_out_bf, ln1_w, ln1_b, w_ffn_in_bf, w_ffn_out_bf, ln2_w, ln2_b):
    tok = lambda w: pl.BlockSpec((TD, w), lambda i: (i, 0))
    once = lambda *tail: _layer_spec(l, tail, single=True)
    return pl.pallas_call(
        _mix_ffn_kernel,
        grid=(NBLK_D,),
        in_specs=[
            _ctx_spec(D_MODEL), _lat_spec(D_MODEL), once(MOD_ROWS, 6 * D_MODEL),
            tok(W_RWKV), tok(W_RWKV),
            tok(W_RWKV), tok(W_RWKV), once(1, W_RWKV), once(1, W_RWKV),
            _ctx_spec(W_NA), _lat_spec(W_NA), _ctx_spec(W_GQA), _lat_spec(W_GQA),
            once(D_MODEL, D_MODEL), once(1, D_MODEL), once(1, D_MODEL),
            once(D_MODEL, 2 * D_FF), once(D_FF, D_MODEL), once(1, D_MODEL), once(1, D_MODEL),
        ],
        out_specs=[_ctx_spec(D_MODEL), _lat_spec(D_MODEL)],
        out_shape=[jax.ShapeDtypeStruct((N_CTX, D_MODEL), F32), jax.ShapeDtypeStruct((N_LAT, D_MODEL), F32)],
        compiler_params=_cparams(1, VMEM_LIMIT_FFN),
        name="mix_ffn",
    )(x_ctx, x_lat, mod_all, o_fwd, o_bwd, g, bonus, lnx_w, lnx_b, o_na_ctx, o_na_lat, o_g_ctx, o_g_lat,
      w_out_bf, ln1_w, ln1_b, w_ffn_in_bf, w_ffn_out_bf, ln2_w, ln2_b)


def _rope_tables():
    t = jnp.arange(DEC_SEQ)
    inv = ROPE_BASE ** (-jnp.arange(ROPE_FREQ, dtype=F32) / ROPE_FREQ)
    ang_r = (t // GRID_W).astype(F32)[:, None] * inv
    ang_c = (t % GRID_W).astype(F32)[:, None] * inv
    cos = jnp.concatenate([jnp.cos(ang_r)] * 2 + [jnp.cos(ang_c)] * 2, axis=1)
    sin = jnp.concatenate([-jnp.sin(ang_r), jnp.sin(ang_r), -jnp.sin(ang_c), jnp.sin(ang_c)], axis=1)
    cos = jnp.concatenate([jnp.ones((TD, HEAD_DIM), F32), cos], axis=0)
    sin = jnp.concatenate([jnp.zeros((TD, HEAD_DIM), F32), sin], axis=0)
    return jnp.tile(cos, (1, LANES // HEAD_DIM)), jnp.tile(sin, (1, LANES // HEAD_DIM))


def _block_diag2(w):
    z = jnp.zeros_like(w[:, 0])
    return jnp.concatenate([jnp.concatenate([w[:, 0], z], axis=2), jnp.concatenate([z, w[:, 1]], axis=2)], axis=1)


def _hi_lo(w):
    return jnp.stack(_split(w), axis=1)


def _pair_states(s):
    lead = s.shape[:-3]
    s = s.reshape(lead + (N_PAIR, 2, HEAD_DIM, HEAD_DIM))
    z = jnp.zeros_like(s[..., 0, :, :])
    top = jnp.concatenate([s[..., 0, :, :], z], axis=-1)
    bot = jnp.concatenate([z, s[..., 1, :, :]], axis=-1)
    return jnp.concatenate([top, bot], axis=-2)


def _unpair_states(s):
    lead = s.shape[:-3]
    a = s[..., :HEAD_DIM, :HEAD_DIM]
    b = s[..., HEAD_DIM:, HEAD_DIM:]
    return jnp.stack([a, b], axis=-3).reshape(lead + (H_RWKV, HEAD_DIM, HEAD_DIM))


def kernel(x_prompt, x_sample, state_rwkv, cache_na_k, cache_na_v, cache_gqa_k, cache_gqa_v, c, c_ctx,
           w_mod, b_mod, w_in, rwkv_conv, rwkv_w0, rwkv_w2, rwkv_a0, rwkv_a2, rwkv_g2, rwkv_k_k, rwkv_k_a,
           rwkv_r_k, rwkv_lnx_w, rwkv_lnx_b, na_rpb, gqa_q_norm, gqa_k_norm, w_out, ln1_w, ln1_b,
           w_ffn_in, w_ffn_out, ln2_w, ln2_b):
    x_ctx, x_lat = x_prompt.reshape(N_CTX, D_MODEL), x_sample.reshape(N_LAT, D_MODEL)
    cc = jnp.concatenate([c_ctx[None], c, jnp.zeros((MOD_ROWS - 1 - DEC_BATCH, D_MODEL), F32)], axis=0)
    mod_all = _modulation(cc, w_mod, b_mod)
    tb_all = _bias_tables(na_rpb)
    cos_tab, sin_tab = _rope_tables()
    rows = lambda a: a.reshape(DEPTH, 1, -1)
    w_in_bf, w_out_bf = w_in.astype(BF16), w_out.astype(BF16)
    w_ffn_in_bf, w_ffn_out_bf = w_ffn_in.astype(BF16), w_ffn_out.astype(BF16)
    qn = jnp.tile(rows(gqa_q_norm), (1, 1, H_GQA))
    kn = jnp.tile(rows(gqa_k_norm), (1, 1, H_GQA_KV))
    rwkv_params = (rwkv_conv, rows(rwkv_w0), _hi_lo(_block_diag2(rwkv_w2)), rows(rwkv_a0),
                   _hi_lo(_block_diag2(rwkv_a2)), _hi_lo(rwkv_g2), rows(rwkv_k_k), rows(rwkv_k_a), rows(rwkv_r_k))
    s0_lat = _pair_states(state_rwkv)
    kc_na = cache_na_k.reshape(DEC_BATCH, DEPTH, PAST_LEN, W_NA)
    vc_na = cache_na_v.reshape(DEC_BATCH, DEPTH, PAST_LEN, W_NA)
    kc_g = cache_gqa_k.reshape(DEC_BATCH, DEPTH, PAST_LEN, W_GQA_KV)
    vc_g = cache_gqa_v.reshape(DEC_BATCH, DEPTH, PAST_LEN, W_GQA_KV)
    caches = [jnp.zeros((BATCH, DEPTH, SEQ, w), F32) for w in (W_NA, W_NA, W_GQA_KV, W_GQA_KV)]
    s_fin = [jnp.zeros((BATCH, DEPTH, N_PAIR, LANES, LANES), F32) for _ in range(2)]

    for l in range(DEPTH):
        naq, nak, nav, gq, gk, gv, r, kap, v, lw, ah, kd, g, bonus, *caches = _inproj(
            l, x_ctx, x_lat, mod_all, w_in_bf, qn, kn, cos_tab, sin_tab, rwkv_params, caches)
        o_fwd, o_bwd, *s_fin = _rwkv_scan(l, r, kap, v, lw, ah, kd, s0_lat, s_fin)
        o_na_ctx, o_g_ctx = _ctx_attention(naq, nak, nav, gq, gk, gv)
        o_na_lat = _lat_na(l, naq, nak, nav, kc_na, vc_na, tb_all)
        o_g_lat = _lat_gqa(l, gq, gk, gv, kc_g, vc_g)
        x_ctx, x_lat = _mix_ffn(
            l, x_ctx, x_lat, mod_all, o_fwd, o_bwd, g, bonus, rows(rwkv_lnx_w), rows(rwkv_lnx_b),
            o_na_ctx, o_na_lat, o_g_ctx, o_g_lat, w_out_bf, rows(ln1_w), rows(ln1_b),
            w_ffn_in_bf, w_ffn_out_bf, rows(ln2_w), rows(ln2_b))
    y_prompt = x_ctx.reshape(BATCH, SEQ, D_MODEL)
    y_sample = x_lat.reshape(DEC_BATCH, DEC_SEQ, D_MODEL)
    new_state = jnp.stack([_unpair_states(s) for s in s_fin], axis=2)
    return (y_prompt, y_sample, new_state,
            caches[0].reshape(BATCH, DEPTH, SEQ, H_NA, HEAD_DIM), caches[1].reshape(BATCH, DEPTH, SEQ, H_NA, HEAD_DIM),
            caches[2].reshape(BATCH, DEPTH, SEQ, H_GQA_KV, HEAD_DIM),
            caches[3].reshape(BATCH, DEPTH, SEQ, H_GQA_KV, HEAD_DIM))
```

```python
import jax
import jax.numpy as jnp
from jax import lax
from jax.experimental import pallas as pl
from jax.experimental.pallas import tpu as pltpu

F32 = jnp.float32
BF16 = jnp.bfloat16

D_MODEL = 1024
BATCH = 16
SEQ = 256
DEPTH = 4
DEC_BATCH = 2
DEC_SEQ = 1024
PAST_LEN = 512
GRID_W = 64
GRID_ROWS = DEC_SEQ // GRID_W
HEAD_DIM = 64
H_RWKV = 4
H_NA = 4
H_GQA = 8
H_GQA_KV = 2
W_RWKV = H_RWKV * HEAD_DIM
W_NA = H_NA * HEAD_DIM
W_GQA = H_GQA * HEAD_DIM
W_GQA_KV = H_GQA_KV * HEAD_DIM
LORA_W = 64
LORA_A = 64
LORA_G = 128
RWKV_IN = 3 * W_RWKV + 2 * LORA_W + 2 * LORA_A + LORA_G
NA_IN = 3 * W_NA
GQA_IN = W_GQA + 2 * W_GQA_KV
D_IN = RWKV_IN + NA_IN + GQA_IN
NA_ROWS = 8
NA_COLS = 16
ROPE_BASE = 10000.0
ROPE_FREQ = HEAD_DIM // 4
D_FF = ((8 * D_MODEL + 3 * 256 - 1) // (3 * 256)) * 256
DEEPNORM_ALPHA = (2 * DEPTH) ** 0.25
LN_EPS = 1e-5
RMS_EPS = 1e-6
GN_EPS = 64e-5
NEG_INF = -1e30
ATTN_SCALE = HEAD_DIM ** -0.5

LANES = 128
TM = 256
N_CTX = BATCH * SEQ
N_LAT = DEC_BATCH * DEC_SEQ
N_TOK = N_CTX + N_LAT
CHUNK = 64
TS = 512
NCS = TS // CHUNK
NBLK_S = N_TOK // TS
NCB_S = N_CTX // TS
LAT_BLKS_S = DEC_SEQ // TS
SEQS_S = TS // SEQ
SEQ_CHUNKS = SEQ // CHUNK
TD = 512
SUB_ROWS = [slice(h * SEQ, (h + 1) * SEQ) for h in range(TD // SEQ)]
IN_PIECES = SUB_ROWS
NBLK_D = N_TOK // TD
NCB_D = N_CTX // TD
LAT_BLKS_D = DEC_SEQ // TD
FF_CHUNK = D_FF
N_PAIR = W_RWKV // LANES
MOD_ROWS = 8
VMEM_LIMIT = 48 * 1024 * 1024
VMEM_LIMIT_FFN = 56 * 1024 * 1024
N_DR = 2 * NA_ROWS - 1
N_DC = 2 * NA_COLS - 1
N_BAND = NA_ROWS * GRID_W


def _cparams(n_grid, vmem_limit=VMEM_LIMIT):
    return pltpu.CompilerParams(dimension_semantics=("arbitrary",) * n_grid, vmem_limit_bytes=vmem_limit)


def _iota(shape, dim):
    return lax.broadcasted_iota(jnp.int32, shape, dim)


NN = (((1,), (0,)), ((), ()))
NT = (((1,), (1,)), ((), ()))
TN = (((0,), (0,)), ((), ()))


def _dg(a, b, dims=NN, precision=None):
    return lax.dot_general(a, b, dims, preferred_element_type=F32, precision=precision)


def _split(x):
    hi = x.astype(BF16)
    return hi, (x - hi.astype(F32)).astype(BF16)


def _dgs(a_s, b_s, dims=NN):
    if len(a_s) == 1:
        return _dg(a_s[0], b_s[0], dims)
    (ah, al), (bh, bl) = a_s, b_s
    ca, cb = dims[0][0][0], dims[0][1][0]
    return _dg(jnp.concatenate([ah, ah, al], axis=ca), jnp.concatenate([bh, bl, bh], axis=cb), dims)


def _sigmoid(x):
    return 1.0 / (1.0 + jnp.exp(-x))


def _softplus(x):
    return jnp.maximum(x, 0.0) + jnp.log(1.0 + jnp.exp(-jnp.abs(x)))


def _seg64_sum(x):
    rows, width = x.shape
    lo = _iota((rows, LANES), 1) < HEAD_DIM
    outs = []
    for c in range(width // LANES):
        blk = x[:, c * LANES:(c + 1) * LANES]
        s_lo = jnp.sum(jnp.where(lo, blk, 0.0), axis=-1, keepdims=True)
        s_hi = jnp.sum(jnp.where(lo, 0.0, blk), axis=-1, keepdims=True)
        outs.append(jnp.where(lo, s_lo, s_hi))
    return outs[0] if len(outs) == 1 else jnp.concatenate(outs, axis=1)


def _layer_norm(x, w, b):
    mu = jnp.mean(x, axis=-1, keepdims=True)
    xc = x - mu
    var = jnp.mean(xc * xc, axis=-1, keepdims=True)
    return xc * lax.rsqrt(var + LN_EPS) * w + b


def _mod_row(i):
    return jnp.where(i < NCB_D, 0, 1 + (i - NCB_D) // LAT_BLKS_D)


def _layer_spec(l, tail, single=False):
    idx = lambda *g: (l,) + (0,) * len(tail)
    if single:
        return pl.BlockSpec((None,) + tuple(tail), idx, pipeline_mode=pl.Buffered(1))
    return pl.BlockSpec((None,) + tuple(tail), idx)


ANY_SPEC = pl.BlockSpec(memory_space=pl.ANY)


def _ctx_spec(w):
    return pl.BlockSpec((TD, w), lambda i: (jnp.minimum(i, NCB_D - 1), 0))


def _lat_spec(w):
    return pl.BlockSpec((TD, w), lambda i: (jnp.maximum(i - NCB_D, 0), 0))


def _pick(i, ctx_ref, lat_ref, rows=slice(None)):
    return jnp.where(i < NCB_D, ctx_ref[rows, :], lat_ref[rows, :])


N_COND = 1 + DEC_BATCH


def _mod_kernel(c_ref, w_ref, b_ref, o_ref, sb_scr):
    @pl.when((pl.program_id(0) == 0) & (pl.program_id(1) == 0))
    def _():
        c = c_ref[...]
        s = c * _sigmoid(c)
        for r in range(N_COND):
            sb_scr[r] = jnp.broadcast_to(s[r:r + 1, :], (LANES, D_MODEL)).T

    def body(kc, accs):
        rows = pl.ds(pl.multiple_of(kc * 8, 8), 8)
        wk = w_ref[rows, :]
        return tuple(a + wk * jnp.concatenate([sb_scr[r, rows, :]] * (D_MODEL // LANES), axis=1)
                     for r, a in enumerate(accs))

    accs = lax.fori_loop(0, D_MODEL // 8, body, tuple(jnp.zeros((8, D_MODEL), F32) for _ in range(N_COND)),
                         unroll=8)
    rowid = _iota((MOD_ROWS, D_MODEL), 0)
    out = jnp.zeros((MOD_ROWS, D_MODEL), F32)
    for r, a in enumerate(accs):
        out = jnp.where(rowid == r, jnp.sum(a, axis=0, keepdims=True), out)
    o_ref[...] = out + b_ref[...]


def _modulation(cc, w_mod, b_mod):
    return pl.pallas_call(
        _mod_kernel,
        grid=(DEPTH, 6),
        in_specs=[
            pl.BlockSpec((MOD_ROWS, D_MODEL), lambda l, j: (0, 0)),
            pl.BlockSpec((None, D_MODEL, D_MODEL), lambda l, j: (l, 0, j)),
            pl.BlockSpec((None, 1, D_MODEL), lambda l, j: (l, 0, j)),
        ],
        out_specs=pl.BlockSpec((None, MOD_ROWS, D_MODEL), lambda l, j: (l, 0, j)),
        out_shape=jax.ShapeDtypeStruct((DEPTH, MOD_ROWS, 6 * D_MODEL), F32),
        scratch_shapes=[pltpu.VMEM((N_COND, D_MODEL, LANES), F32)],
        compiler_params=_cparams(2),
        name="modulation",
    )(cc, w_mod, b_mod.reshape(DEPTH, 1, 6 * D_MODEL))


def _bias_kernel(rpb_ref, o_ref):
    q = _iota((GRID_W, LANES), 0)
    x = _iota((GRID_W, LANES), 1)
    c = x % GRID_W
    right = x >= GRID_W
    dc = jnp.clip(c - q, 1 - NA_COLS, NA_COLS - 1) + NA_COLS - 1
    c0 = jnp.clip(q - NA_COLS // 2, 0, GRID_W - NA_COLS)
    in_win = (c >= c0) & (c < c0 + NA_COLS)

    def body(t, carry):
        h = t // (N_DR - 1)
        dr = t % (N_DR - 1)
        rows = [jnp.broadcast_to(rpb_ref[pl.ds(h * N_DR + dr + k, 1), :], (GRID_W, LANES)) for k in range(2)]
        lo, hi = (jnp.take_along_axis(r, dc, axis=1) for r in rows)
        o_ref[h, dr] = jnp.where(in_win, jnp.where(right, hi, lo), NEG_INF)
        return carry

    lax.fori_loop(0, H_NA * (N_DR - 1), body, 0, unroll=N_DR - 1)


RPB_ROWS = -(-H_NA * N_DR // 8) * 8


def _bias_tables(na_rpb):
    rpb = jnp.pad(na_rpb.reshape(DEPTH, H_NA * N_DR, N_DC), ((0, 0), (0, RPB_ROWS - H_NA * N_DR), (0, LANES - N_DC)))
    return pl.pallas_call(
        _bias_kernel,
        grid=(DEPTH,),
        in_specs=[pl.BlockSpec((None, RPB_ROWS, LANES), lambda l: (l, 0, 0))],
        out_specs=pl.BlockSpec((None, H_NA, N_DR - 1, GRID_W, LANES), lambda l: (l, 0, 0, 0, 0)),
        out_shape=jax.ShapeDtypeStruct((DEPTH, H_NA, N_DR - 1, GRID_W, LANES), F32),
        compiler_params=_cparams(1),
        name="na_bias_tables",
    )(rpb)


def _rope(x, cos, sin):
    k = x.shape[1] // LANES
    cosf = cos if k == 1 else jnp.concatenate([cos] * k, axis=1)
    sinf = sin if k == 1 else jnp.concatenate([sin] * k, axis=1)
    first = (_iota(x.shape, 1) % (2 * ROPE_FREQ)) < ROPE_FREQ
    partner = jnp.where(first, pltpu.roll(x, x.shape[1] - ROPE_FREQ, axis=1), pltpu.roll(x, ROPE_FREQ, axis=1))
    return x * cosf + partner * sinf


def _rwkv_features(x, prev_row, next_row, conv_ref, w0_ref, w2_ref, a0_ref, a2_ref, g2_ref, kk_ref, ka_ref, rk_ref):
    n = x.shape[0]
    rows = _iota(x.shape, 0)
    x_prev = jnp.where(rows == 0, prev_row, pltpu.roll(x, 1, axis=0))
    x_next = jnp.where(rows == n - 1, next_row, pltpu.roll(x, n - 1, axis=0))
    f = x_prev * conv_ref[0:1, :] + x * conv_ref[1:2, :] + x_next * conv_ref[2:3, :]
    o1, o2, o3 = W_RWKV, 2 * W_RWKV, 3 * W_RWKV
    o4 = o3 + 2 * LORA_W
    o5 = o4 + 2 * LORA_A
    r, k, v = f[:, :o1], f[:, o1:o2], f[:, o2:o3]
    wd, ad, gd = f[:, o3:o4], f[:, o4:o5], f[:, o5:]
    def lora(y, w_ref):
        y_bf = y.astype(BF16)
        return _dg(jnp.concatenate([y_bf, y_bf], axis=1), jnp.concatenate([w_ref[0], w_ref[1]], axis=0))
    log_w = -_softplus(-(w0_ref[...] + lora(jnp.tanh(wd), w2_ref))) - 0.5
    a = _sigmoid(a0_ref[...] + lora(ad, a2_ref))
    g = lora(_sigmoid(gd), g2_ref)
    kk = k * kk_ref[...]
    kap = kk / jnp.maximum(jnp.sqrt(_seg64_sum(kk * kk)), 1e-12)
    per_dir = []
    kd_sum = jnp.zeros_like(k)
    for d in range(2):
        a_d = a[:, d * W_RWKV:(d + 1) * W_RWKV]
        kd = k * (1.0 + (a_d - 1.0) * ka_ref[...])
        kd_sum = kd_sum + kd
        per_dir.append((-jnp.exp(log_w[:, d * W_RWKV:(d + 1) * W_RWKV]), a_d * kap, kd))
    bonus = _seg64_sum(r * kd_sum * rk_ref[...]) * v
    return r, kap, v, per_dir, g, bonus


HALO = 8


def _inproj_kernel(xc_ref, xl_ref, xp_ref, xn_ref, mod_ref, w_ref, qn_ref, kn_ref, cos_ref, sin_ref,
                   conv_ref, w0_ref, w2_ref, a0_ref, a2_ref, g2_ref, kk_ref, ka_ref, rk_ref,
                   c0_ref, c1_ref, c2_ref, c3_ref,
                   naq_ref, nak_ref, nav_ref, gq_ref, gk_ref, gv_ref,
                   r_ref, kap_ref, v_ref, lw_ref, ah_ref, kd_ref, g_ref, bonus_ref,
                   cnak_ref, cnav_ref, cgk_ref, cgv_ref):
    del c0_ref, c1_ref, c2_ref, c3_ref
    i = pl.program_id(0)
    row = _mod_row(i)
    shift1 = mod_ref[pl.ds(row, 1), 0:D_MODEL]
    scale1 = mod_ref[pl.ds(row, 1), D_MODEL:2 * D_MODEL]
    modulate = lambda x: (x * (1.0 + scale1) + shift1).astype(BF16)
    lat = i >= NCB_D
    pos = (i - NCB_D) % LAT_BLKS_D
    in_seq = jnp.where(lat, 1.0, 0.0)
    has_prev = jnp.where(lat & (pos != 0), 1.0, 0.0)
    has_next = jnp.where(lat & (pos != LAT_BLKS_D - 1), 1.0, 0.0)
    xm = [modulate(_pick(i, xc_ref, xl_ref, rs)) for rs in IN_PIECES]
    xm[0] = jnp.concatenate([xm[0], modulate(xp_ref[...]), modulate(xn_ref[...])], axis=0)
    projs = [_dg(x, w_ref[...]) for x in xm]
    n0 = IN_PIECES[0].stop
    halo = projs[0][n0:, :RWKV_IN]
    projs[0] = projs[0][:n0]
    feats = [proj[:, :RWKV_IN] for proj in projs]
    o_na = RWKV_IN
    o_g = RWKV_IN + NA_IN
    new_kv = []
    for h, (rs, proj) in enumerate(zip(IN_PIECES, projs)):
        if h == 0:
            prev_row = halo[HALO - 1:HALO] * has_prev
        else:
            prev_row = feats[h - 1][-1:] * (in_seq if rs.start % SEQ == 0 else 1.0)
        if h == len(IN_PIECES) - 1:
            next_row = halo[HALO:HALO + 1] * has_next
        else:
            next_row = feats[h + 1][0:1] * (in_seq if rs.stop % SEQ == 0 else 1.0)
        r, kap, v, per_dir, g, bonus = _rwkv_features(
            feats[h], prev_row, next_row, conv_ref, w0_ref, w2_ref, a0_ref, a2_ref, g2_ref, kk_ref, ka_ref, rk_ref)
        r_ref[rs, :] = r.astype(BF16)
        kap_ref[rs, :] = kap.astype(BF16)
        v_ref[rs, :] = v.astype(BF16)
        g_ref[rs, :] = g
        bonus_ref[rs, :] = bonus
        for d, (lw, ah, kd) in enumerate(per_dir):
            lw_ref[d, rs, :] = lw
            ah_ref[d, rs, :] = ah.astype(BF16)
            kd_ref[d, rs, :] = kd.astype(BF16)
        naq_ref[rs, :] = (proj[:, o_na:o_na + W_NA] * ATTN_SCALE).astype(BF16)
        nak = proj[:, o_na + W_NA:o_na + 2 * W_NA]
        nav = proj[:, o_na + 2 * W_NA:o_na + 3 * W_NA]
        q = proj[:, o_g:o_g + W_GQA]
        k = proj[:, o_g + W_GQA:o_g + W_GQA + W_GQA_KV]
        gv = proj[:, o_g + W_GQA + W_GQA_KV:]
        cos = cos_ref[rs, :]
        sin = sin_ref[rs, :]
        q = q * lax.rsqrt(_seg64_sum(q * q) * (1.0 / HEAD_DIM) + RMS_EPS) * qn_ref[...]
        k = k * lax.rsqrt(_seg64_sum(k * k) * (1.0 / HEAD_DIM) + RMS_EPS) * kn_ref[...]
        gk = _rope(k, cos, sin)
        gq_ref[rs, :] = (_rope(q, cos, sin) * ATTN_SCALE).astype(BF16)
        nak_ref[rs, :] = nak.astype(BF16)
        nav_ref[rs, :] = nav.astype(BF16)
        gk_ref[rs, :] = gk.astype(BF16)
        gv_ref[rs, :] = gv.astype(BF16)
        new_kv.append((nak, nav, gk, gv))

    @pl.when(i < NCB_D)
    def _():
        for rs, vals in zip(IN_PIECES, new_kv):
            for ref, val in zip((cnak_ref, cnav_ref, cgk_ref, cgv_ref), vals):
                ref[rs.start // SEQ, rs.start % SEQ:rs.start % SEQ + rs.stop - rs.start, :] = val


def _inproj(l, x_ctx, x_lat, mod_all, w_in_bf, qn, kn, cos_tab, sin_tab, rwkv_params, caches):
    tab_idx = lambda i: (jnp.where(i < NCB_D, 0, 1 + (i - NCB_D) % LAT_BLKS_D), 0)
    lat_blk = lambda i: jnp.maximum(i - NCB_D, 0)
    halo_blocks = TD // HALO
    widths = (W_NA, W_NA, W_NA, W_GQA, W_GQA_KV, W_GQA_KV, W_RWKV, W_RWKV, W_RWKV)
    cache_w = (W_NA, W_NA, W_GQA_KV, W_GQA_KV)
    tok = lambda w: pl.BlockSpec((TD, w), lambda i: (i, 0))
    tok2 = pl.BlockSpec((2, TD, W_RWKV), lambda i: (0, i, 0))
    cache_spec = lambda w: pl.BlockSpec((TD // SEQ, None, SEQ, w), lambda i: (jnp.minimum(i, NCB_D - 1), l, 0, 0))
    rwkv_tails = ((3, RWKV_IN), (1, 2 * W_RWKV), (2, 2 * LORA_W, 2 * W_RWKV), (1, 2 * W_RWKV),
                  (2, 2 * LORA_A, 2 * W_RWKV), (2, LORA_G, W_RWKV), (1, W_RWKV), (1, W_RWKV), (1, W_RWKV))
    in_specs = [
        _ctx_spec(D_MODEL), _lat_spec(D_MODEL),
        pl.BlockSpec((HALO, D_MODEL), lambda i: (jnp.maximum(lat_blk(i) * halo_blocks - 1, 0), 0)),
        pl.BlockSpec((HALO, D_MODEL), lambda i: (jnp.minimum((lat_blk(i) + 1) * halo_blocks, N_LAT // HALO - 1), 0)),
        _layer_spec(l, (MOD_ROWS, 6 * D_MODEL)),
        _layer_spec(l, (D_MODEL, D_IN)),
        _layer_spec(l, (1, W_GQA)),
        _layer_spec(l, (1, W_GQA_KV)),
        pl.BlockSpec((TD, LANES), tab_idx),
        pl.BlockSpec((TD, LANES), tab_idx),
    ] + [_layer_spec(l, t) for t in rwkv_tails]
    n_in = len(in_specs)
    out_specs = [tok(w) for w in widths] + [tok2] * 3 + [tok(W_RWKV)] * 2
    out_shape = ([jax.ShapeDtypeStruct((N_TOK, w), BF16) for w in widths]
                 + [jax.ShapeDtypeStruct((2, N_TOK, W_RWKV), dt) for dt in (F32, BF16, BF16)]
                 + [jax.ShapeDtypeStruct((N_TOK, W_RWKV), F32)] * 2)
    return pl.pallas_call(
        _inproj_kernel,
        grid=(NBLK_D,),
        in_specs=in_specs + [ANY_SPEC] * 4,
        out_specs=out_specs + [cache_spec(w) for w in cache_w],
        out_shape=out_shape + [jax.ShapeDtypeStruct((BATCH, DEPTH, SEQ, w), F32) for w in cache_w],
        input_output_aliases={n_in + j: len(out_specs) + j for j in range(4)},
        compiler_params=_cparams(1),
        name="inproj",
    )(x_ctx, x_lat, x_lat, x_lat, mod_all, w_in_bf, qn, kn, cos_tab, sin_tab, *rwkv_params, *caches)


def _scan_kernel(r0_ref, kap0_ref, v0_ref, r1_ref, kap1_ref, v1_ref,
                 lw0_ref, ah0_ref, kd0_ref, lw1_ref, ah1_ref, kd1_ref, s00_ref, s01_ref, sf0_in_ref, sf1_in_ref,
                 o0_ref, o1_ref, sf0_ref, sf1_ref,
                 s_scr, sdone_scr, rt_scr, kt_scr, kdt_scr, at_scr, cum_scr, m_scr, n_scr, q_scr, oo_scr):
    del sf0_in_ref, sf1_in_ref
    j = pl.program_id(0)
    par = j % 2
    jb = jnp.maximum(j - 1, 0)
    upd_blk = (jb, NBLK_S - 1 - jb)
    is_ctx = tuple(b < NCB_S for b in upd_blk)
    first_pos = (0, LAT_BLKS_S - 1)
    r_refs, kap_refs, v_refs = (r0_ref, r1_ref), (kap0_ref, kap1_ref), (v0_ref, v1_ref)
    lw_refs, ah_refs, kd_refs = (lw0_ref, lw1_ref), (ah0_ref, ah1_ref), (kd0_ref, kd1_ref)
    s0_refs, o_refs, sf_refs = (s00_ref, s01_ref), (o0_ref, o1_ref), (sf0_ref, sf1_ref)

    @pl.when(j == 0)
    def _():
        for ref in (s_scr, sdone_scr, m_scr, n_scr, q_scr, oo_scr):
            ref[...] = jnp.zeros_like(ref)

    for dn in range(2):
        blk = upd_blk[dn]

        @pl.when((j > 0) & is_ctx[dn])
        def _():
            s_scr[dn] = jnp.zeros(s_scr.shape[1:], F32)

        @pl.when((j > 0) & (blk >= NCB_S) & ((blk - NCB_S) % LAT_BLKS_S == first_pos[dn]))
        def _():
            s_scr[dn] = s0_refs[dn][...]

    def seq_slot(dn, n):
        return n if dn == 0 else SEQS_S - 1 - n

    def update(c):
        for dn in range(2):
            ce = c if dn == 0 else NCS - 1 - c
            rows = slice(ce * CHUNK, (ce + 1) * CHUNK)
            for p in range(N_PAIR):
                s = s_scr[dn, p]
                if c > 0 and c % SEQ_CHUNKS == 0:
                    sdone_scr[dn, seq_slot(dn, c // SEQ_CHUNKS - 1), p] = s
                    s = jnp.where(is_ctx[dn], 0.0, s)
                s2 = jnp.concatenate(_split(s), axis=1)
                q = q_scr[1 - par, dn, ce, p]
                m = m_scr[1 - par, dn, ce, p]
                o_refs[dn][rows, p * LANES:(p + 1) * LANES] = (
                    _dg(jnp.concatenate([q, q], axis=1), s2, NT) + oo_scr[1 - par, dn, ce, p])
                s_scr[dn, p] = _dg(s2, jnp.concatenate([m, m], axis=0)) + n_scr[1 - par, dn, ce, p]

    pending = list(range(NCS))

    def next_update():
        if pending:
            update(pending.pop(0))

    next_update()
    rr = _iota((TM, TM), 0)
    cc = _iota((TM, TM), 1)
    same_chunk = (rr // CHUNK) == (cc // CHUNK)
    for dn in range(2):
        order = (rr >= cc) if dn == 0 else (rr <= cc)
        tri = jnp.where(same_chunk & order, 1.0, 0.0).astype(BF16)
        for h in range(TS // TM):
            rs = slice(h * TM, (h + 1) * TM)
            lw = lw_refs[dn][rs, :]
            lw_hi = lw.astype(BF16)
            lw_r = lw - lw_hi.astype(F32)
            lw_mid = lw_r.astype(BF16)
            lw_lo = (lw_r - lw_mid.astype(F32)).astype(BF16)
            cum = _dg(jnp.concatenate([tri, tri, tri], axis=1), jnp.concatenate([lw_hi, lw_mid, lw_lo], axis=0))
            e_neg = jnp.exp(-cum)
            rt_scr[dn, rs, :] = r_refs[dn][rs, :] * jnp.exp(cum)
            kt_scr[dn, rs, :] = kap_refs[dn][rs, :] * jnp.exp(cum - lw)
            kdt_scr[dn, rs, :] = kd_refs[dn][rs, :] * e_neg
            at_scr[dn, rs, :] = ah_refs[dn][rs, :] * e_neg
            cum_scr[dn, rs, :] = cum
        if dn == 0:
            next_update()

    row = _iota((CHUNK, LANES), 0)
    col = _iota((CHUNK, LANES), 1) % CHUNK
    left = _iota((CHUNK, LANES), 1) < CHUNK
    incl = (row >= col, row <= col)
    strict = (row > col, row < col)
    eye = jnp.where(row == col, 1.0, 0.0)
    r2 = _iota((LANES, LANES), 0)
    c2 = _iota((LANES, LANES), 1)
    bd_mask = (r2 // CHUNK) == (c2 // CHUNK)
    eye2 = jnp.where(r2 == c2, 1.0, 0.0)

    zero_bf = jnp.zeros((CHUNK, LANES), BF16)
    split = lambda x: (x.astype(BF16),)

    def bd(xs):
        return tuple(jnp.concatenate([jnp.where(left, y, zero_bf), jnp.where(left, zero_bf, y)], axis=0)
                     for y in xs)

    def cat(xs, ys, axis):
        return tuple(jnp.concatenate([x, y], axis=axis) for x, y in zip(xs, ys))

    def mm(a_list, b_list):
        return [_dgs(a, bd(b)) for a, b in zip(a_list, b_list)]

    units = [(dn, c, p) for c in range(NCS) for dn in range(2) for p in range(N_PAIR)]
    each = lambda f, *lists: [f(*args) for args in zip(*lists)]

    def load(ref):
        return [ref[dn, c * CHUNK:(c + 1) * CHUNK, p * LANES:(p + 1) * LANES] for dn, c, p in units]

    rt = load(rt_scr)
    v = [v_refs[dn][c * CHUNK:(c + 1) * CHUNK, p * LANES:(p + 1) * LANES] for dn, c, p in units]
    kt_s, rt_s, kdt_s, at_s, v_s = (each(split, x) for x in (load(kt_scr), rt, load(kdt_scr), load(at_scr), v))
    gam = [jnp.exp(cum_scr[dn, (c + 1) * CHUNK - 1:(c + 1) * CHUNK, p * LANES:(p + 1) * LANES] if dn == 0 else
                   cum_scr[dn, c * CHUNK:c * CHUNK + 1, p * LANES:(p + 1) * LANES]) for dn, c, p in units]
    gram = each(lambda k, r, a, kd: _dgs(cat(k, r, 0), cat(bd(a), bd(kd), 0), NT), kt_s, rt_s, at_s, kdt_s)
    dirs = [u[0] for u in units]
    la = [jnp.where(strict[dn], g[0:CHUNK, 0:LANES], 0.0) for dn, g in zip(dirs, gram)]
    lk_s = [split(jnp.where(strict[dn], g[0:CHUNK, LANES:], 0.0)) for dn, g in zip(dirs, gram)]
    ra_s = [split(jnp.where(incl[dn], g[CHUNK:, 0:LANES], 0.0)) for dn, g in zip(dirs, gram)]
    rk_s = [split(jnp.where(incl[dn], g[CHUNK:, LANES:], 0.0)) for dn, g in zip(dirs, gram)]
    next_update()
    lrv = mm(each(lambda lk, rk: cat(lk, rk, 0), lk_s, rk_s), v_s)
    next_update()
    b = 8
    l8 = [jnp.where((row // b) == (col // b), x, 0.0) for x in la]
    l8_s = each(split, l8)
    l8_2 = mm(l8_s, l8_s)
    next_update()
    l8_2s = each(split, l8_2)
    l8_4 = mm(l8_2s, l8_2s)
    p1 = mm([split(eye - x) for x in l8], [split(eye + y) for y in l8_2])
    next_update()
    t = mm(each(split, p1), [split(eye + y) for y in l8_4])
    while b < CHUNK:
        next_update()
        offd = ((row // (2 * b)) == (col // (2 * b))) & ((row // b) != (col // b))
        t_s = each(split, t)
        x = mm(t_s, [split(jnp.where(offd, y, 0.0)) for y in la])
        t = each(lambda tt, z: tt - z, t, mm(each(split, x), t_s))
        b *= 2
    assert not pending, "more chunk updates than stages to place them between"
    tx = each(lambda tt, k, y: _dgs(split(tt), cat(bd(k), bd(split(y[0:CHUNK])), 1)), t, kt_s, lrv)
    khat_s = [split(y[:, 0:LANES]) for y in tx]
    w1_s = [split(y[:, LANES:]) for y in tx]
    rx = each(lambda r, k, w: _dgs(r, cat(bd(k), bd(w), 1)), ra_s, khat_s, w1_s)
    mk = each(lambda k, a: _dgs(k, a, TN), khat_s, at_s)
    nk = each(lambda vv, w, kd, a: _dgs(cat(vv, w, 0), cat(kd, tuple(-y for y in a), 0), TN),
              v_s, w1_s, kdt_s, at_s)
    for i, (dn, c, p) in enumerate(units):
        q_scr[par, dn, c, p] = (rt[i] - rx[i][:, 0:LANES]).astype(BF16)
        oo_scr[par, dn, c, p] = lrv[i][CHUNK:] - rx[i][:, LANES:]
        m_scr[par, dn, c, p] = ((eye2 - jnp.where(bd_mask, mk[i], 0.0)) * gam[i]).astype(BF16)
        n_scr[par, dn, c, p] = jnp.where(bd_mask, nk[i], 0.0) * gam[i]

    for dn in range(2):
        @pl.when((j > 0) & is_ctx[dn])
        def _():
            for n in range(SEQS_S - 1):
                sf_refs[dn][seq_slot(dn, n)] = sdone_scr[dn, seq_slot(dn, n)]
            sf_refs[dn][seq_slot(dn, SEQS_S - 1)] = s_scr[dn]


def _rwkv_scan(l, r, kap, v, lw, ah, kd, s0_lat, s_fin):
    def stage_blk(dn, j):
        jj = jnp.minimum(j, NBLK_S - 1)
        return jj if dn == 0 else NBLK_S - 1 - jj

    def update_blk(dn, j):
        jj = jnp.maximum(j - 1, 0)
        return jj if dn == 0 else NBLK_S - 1 - jj

    pair = (N_PAIR, LANES, LANES)
    tok = lambda dn: pl.BlockSpec((TS, W_RWKV), lambda j: (stage_blk(dn, j), 0))
    tok2 = lambda dn: pl.BlockSpec((None, TS, W_RWKV), lambda j: (dn, stage_blk(dn, j), 0))
    s0_spec = lambda dn: pl.BlockSpec(
        (None, None, None) + pair,
        lambda j: (jnp.maximum(update_blk(dn, j) - NCB_S, 0) // LAT_BLKS_S, l, dn, 0, 0, 0))
    sf_spec = lambda dn: pl.BlockSpec(
        (SEQS_S, None) + pair, lambda j: (jnp.minimum(update_blk(dn, j), NCB_S - 1), l, 0, 0, 0))
    o_spec = lambda dn: pl.BlockSpec((TS, W_RWKV), lambda j: (update_blk(dn, j), 0))
    dir_scr = lambda *shape, dtype=F32: pltpu.VMEM((2,) + shape, dtype)
    stage_scr = (2, NCS, N_PAIR)
    return pl.pallas_call(
        _scan_kernel,
        grid=(NBLK_S + 1,),
        in_specs=[tok(0), tok(0), tok(0), tok(1), tok(1), tok(1),
                  tok2(0), tok2(0), tok2(0), tok2(1), tok2(1), tok2(1),
                  s0_spec(0), s0_spec(1), ANY_SPEC, ANY_SPEC],
        out_specs=[o_spec(0), o_spec(1), sf_spec(0), sf_spec(1)],
        out_shape=[jax.ShapeDtypeStruct((N_TOK, W_RWKV), F32)] * 2
        + [jax.ShapeDtypeStruct((BATCH, DEPTH) + pair, F32)] * 2,
        scratch_shapes=[dir_scr(*pair), dir_scr(SEQS_S, *pair)] + [dir_scr(TS, W_RWKV)] * 5
        + [dir_scr(*stage_scr, LANES, LANES, dtype=BF16), dir_scr(*stage_scr, LANES, LANES),
           dir_scr(*stage_scr, CHUNK, LANES, dtype=BF16), dir_scr(*stage_scr, CHUNK, LANES)],
        input_output_aliases={14: 2, 15: 3},
        compiler_params=_cparams(1),
        name="rwkv_scan",
    )(r, kap, v, r, kap, v, lw, ah, kd, lw, ah, kd, s0_lat, s0_lat, s_fin[0], s_fin[1])


def _attend(groups):
    lhs = []
    for q_cols, _, _, _ in groups:
        left = _iota(q_cols[0].shape, 1) < HEAD_DIM
        parts = []
        for qc in q_cols:
            parts += [jnp.where(left, qc, 0.0), jnp.where(left, 0.0, qc)]
        lhs.append(jnp.concatenate(parts, axis=0).astype(BF16))
    s = [_dg(x, g[1], NT) for x, g in zip(lhs, groups)]
    s = [x if g[3] is None else x + g[3] for x, g in zip(s, groups)]
    p = [jnp.exp(x - jnp.max(x, axis=-1, keepdims=True)) for x in s]
    inv = [1.0 / jnp.sum(x, axis=-1, keepdims=True) for x in p]
    o = [_dg(x.astype(BF16), g[2]) * y for x, y, g in zip(p, inv, groups)]
    outs = []
    for x, (q_cols, _, _, _) in zip(o, groups):
        rows = q_cols[0].shape[0]
        left = _iota(q_cols[0].shape, 1) < HEAD_DIM
        outs.append([jnp.where(left, x[2 * j * rows:(2 * j + 1) * rows], x[(2 * j + 1) * rows:(2 * j + 2) * rows])
                     for j in range(len(q_cols))])
    return outs


def _cols(x):
    return [x[:, c * LANES:(c + 1) * LANES] for c in range(x.shape[1] // LANES)]


def _gqa_groups(q, k, v):
    k, v = k.astype(F32), v.astype(F32)
    left = _iota(k.shape, 1) < HEAD_DIM
    k_sw = pltpu.roll(k, HEAD_DIM, axis=1)
    v_sw = pltpu.roll(v, HEAD_DIM, axis=1)
    q_cols = _cols(q)
    groups = []
    for g in range(H_GQA_KV):
        k2 = jnp.where(left, k, k_sw) if g == 0 else jnp.where(left, k_sw, k)
        v2 = jnp.where(left, v, v_sw) if g == 0 else jnp.where(left, v_sw, v)
        groups.append((q_cols[2 * g:2 * g + 2], k2.astype(BF16), v2.astype(BF16), None))
    return groups


CTX_STEP_SEQS = 4


def _ctx_attn_kernel(naq_ref, nak_ref, nav_ref, gq_ref, gk_ref, gv_ref, ona_ref, og_ref):
    groups = []
    n_na = W_NA // LANES
    for n in range(CTX_STEP_SEQS):
        rs = slice(n * SEQ, (n + 1) * SEQ)
        k_cols = _cols(nak_ref[rs, :].astype(BF16))
        v_cols = _cols(nav_ref[rs, :].astype(BF16))
        groups += [([qc], kc, vc, None) for qc, kc, vc in zip(_cols(naq_ref[rs, :]), k_cols, v_cols)]
        groups += _gqa_groups(gq_ref[rs, :], gk_ref[rs, :], gv_ref[rs, :])
    outs = _attend(groups)
    per_seq = len(groups) // CTX_STEP_SEQS
    for n in range(CTX_STEP_SEQS):
        rs = slice(n * SEQ, (n + 1) * SEQ)
        o = outs[n * per_seq:(n + 1) * per_seq]
        ona_ref[rs, :] = jnp.concatenate([x[0] for x in o[:n_na]], axis=1)
        og_ref[rs, :] = jnp.concatenate([c for x in o[n_na:] for c in x], axis=1)


NA_STEP_ROWS = 8


def _lat_na_kernel(r_step, q_ref, k_ref, v_ref, kc_ref, vc_ref, tb_ref, o_ref):
    q = q_ref[...]
    groups = []
    for rr in range(NA_STEP_ROWS):
        r = r_step * NA_STEP_ROWS + rr
        r0 = jnp.clip(r - NA_ROWS // 2, 0, GRID_ROWS - NA_ROWS)
        band = pl.ds(pl.multiple_of(r0 * GRID_W, GRID_W), N_BAND)
        dr0 = r0 - r + NA_ROWS - 1
        for c, qc in enumerate(_cols(q[rr * GRID_W:(rr + 1) * GRID_W])):
            cols = slice(c * LANES, (c + 1) * LANES)
            k2 = jnp.concatenate([k_ref[band, cols], kc_ref[:, cols].astype(BF16)], axis=0)
            v2 = jnp.concatenate([v_ref[band, cols], vc_ref[:, cols].astype(BF16)], axis=0)
            bias = jnp.concatenate(
                [jnp.concatenate([tb_ref[2 * c + half, dr0 + jj] for jj in range(0, NA_ROWS, 2)], axis=1)
                 for half in range(2)], axis=0)
            bias = jnp.concatenate([bias, jnp.zeros((2 * GRID_W, PAST_LEN), F32)], axis=1)
            groups.append(([qc], k2, v2, bias))
    outs = [o[0] for o in _attend(groups)]
    n_col = W_NA // LANES
    o_ref[...] = jnp.concatenate(
        [jnp.concatenate(outs[rr * n_col:(rr + 1) * n_col], axis=1) for rr in range(NA_STEP_ROWS)], axis=0)


def _lat_gqa_kernel(q_ref, k_ref, v_ref, kc_ref, vc_ref, o_ref):
    k = jnp.concatenate([kc_ref[...], k_ref[...].astype(F32)], axis=0)
    v = jnp.concatenate([vc_ref[...], v_ref[...].astype(F32)], axis=0)
    o_ref[...] = jnp.concatenate([c for o in _attend(_gqa_groups(q_ref[...], k, v)) for c in o], axis=1)


GQA_ROWS = 256
CTX_STEPS = BATCH // CTX_STEP_SEQS
NA_STEPS = GRID_ROWS // NA_STEP_ROWS
GQA_STEPS = DEC_SEQ // GQA_ROWS
S_NA0 = CTX_STEPS
S_GQA0 = S_NA0 + DEC_BATCH * NA_STEPS
S_END = S_GQA0 + DEC_BATCH * GQA_STEPS


def _attention_kernel(c_naq, c_nak, c_nav, c_gq, c_gk, c_gv, n_q, n_k, n_v, n_kc, n_vc, tb_ref,
                      g_q, g_k, g_v, g_kc, g_vc, ona_ctx, og_ctx, ona_lat, og_lat):
    s = pl.program_id(0)

    @pl.when(s < S_NA0)
    def _():
        _ctx_attn_kernel(c_naq, c_nak, c_nav, c_gq, c_gk, c_gv, ona_ctx, og_ctx)

    @pl.when((s >= S_NA0) & (s < S_GQA0))
    def _():
        _lat_na_kernel((s - S_NA0) % NA_STEPS, n_q, n_k, n_v, n_kc, n_vc, tb_ref, ona_lat)

    @pl.when(s >= S_GQA0)
    def _():
        _lat_gqa_kernel(g_q, g_k, g_v, g_kc, g_vc, og_lat)


def _attention(l, naq, nak, nav, gq, gk, gv, kc_na, vc_na, kc_g, vc_g, tb):
    seq_blk0 = N_CTX // DEC_SEQ
    na_i = lambda s: jnp.clip(s - S_NA0, 0, DEC_BATCH * NA_STEPS - 1)
    gqa_i = lambda s: jnp.clip(s - S_GQA0, 0, DEC_BATCH * GQA_STEPS - 1)
    ctx = lambda w: pl.BlockSpec((CTX_STEP_SEQS * SEQ, w), lambda s: (jnp.minimum(s, CTX_STEPS - 1), 0))
    na_rows = NA_STEP_ROWS * GRID_W
    na_seq = pl.BlockSpec((DEC_SEQ, W_NA), lambda s: (seq_blk0 + na_i(s) // NA_STEPS, 0))
    na_cache = pl.BlockSpec((None, None, PAST_LEN, W_NA), lambda s: (na_i(s) // NA_STEPS, l, 0, 0))
    gqa_seq = pl.BlockSpec((DEC_SEQ, W_GQA_KV), lambda s: (seq_blk0 + gqa_i(s) // GQA_STEPS, 0))
    gqa_cache = pl.BlockSpec((None, None, PAST_LEN, W_GQA_KV), lambda s: (gqa_i(s) // GQA_STEPS, l, 0, 0))
    return pl.pallas_call(
        _attention_kernel,
        grid=(S_END,),
        in_specs=[ctx(W_NA), ctx(W_NA), ctx(W_NA), ctx(W_GQA), ctx(W_GQA_KV), ctx(W_GQA_KV),
                  pl.BlockSpec((na_rows, W_NA), lambda s: (N_CTX // na_rows + na_i(s), 0)),
                  na_seq, na_seq, na_cache, na_cache, _layer_spec(l, (H_NA, N_DR - 1, GRID_W, LANES)),
                  pl.BlockSpec((GQA_ROWS, W_GQA), lambda s: (N_CTX // GQA_ROWS + gqa_i(s), 0)),
                  gqa_seq, gqa_seq, gqa_cache, gqa_cache],
        out_specs=[ctx(W_NA), ctx(W_GQA),
                   pl.BlockSpec((na_rows, W_NA), lambda s: (na_i(s), 0)),
                   pl.BlockSpec((GQA_ROWS, W_GQA), lambda s: (gqa_i(s), 0))],
        out_shape=[jax.ShapeDtypeStruct((N_CTX, W_NA), F32), jax.ShapeDtypeStruct((N_CTX, W_GQA), F32),
                   jax.ShapeDtypeStruct((N_LAT, W_NA), F32), jax.ShapeDtypeStruct((N_LAT, W_GQA), F32)],
        compiler_params=_cparams(1),
        name="attention",
    )(naq, nak, nav, gq, gk, gv, naq, nak, nav, kc_na, vc_na, tb, gq, gk, gv, kc_g, vc_g)


def _mix_ffn_kernel(xc_ref, xl_ref, mod_ref, of_ref, ob_ref, g_ref, bonus_ref, lnxw_ref, lnxb_ref,
                    onac_ref, onal_ref, ogc_ref, ogl_ref,
                    wout_ref, ln1w_ref, ln1b_ref, wfi_ref, wfo_ref, ln2w_ref, ln2b_ref, yc_ref, yl_ref):
    i = pl.program_id(0)
    row = _mod_row(i)
    mod = lambda n: mod_ref[pl.ds(row, 1), n * D_MODEL:(n + 1) * D_MODEL]
    each = lambda f, *lists: [f(*args) for args in zip(*lists)]

    def rwkv_out(rs):
        o = of_ref[rs, :] + ob_ref[rs, :]
        mu = _seg64_sum(o) * (1.0 / HEAD_DIM)
        oc = o - mu
        var = _seg64_sum(oc * oc) * (1.0 / HEAD_DIM)
        o_rwkv = (oc * lax.rsqrt(var + GN_EPS) * lnxw_ref[...] + lnxb_ref[...] + bonus_ref[rs, :]) * g_ref[rs, :]
        return jnp.concatenate([o_rwkv, _pick(i, onac_ref, onal_ref, rs), _pick(i, ogc_ref, ogl_ref, rs)],
                               axis=1).astype(BF16)

    mix = [_dg(rwkv_out(rs), wout_ref[...]) for rs in SUB_ROWS]
    x1 = [_layer_norm(DEEPNORM_ALPHA * _pick(i, xc_ref, xl_ref, rs) + mod(2) * m, ln1w_ref[...], ln1b_ref[...])
          for rs, m in zip(SUB_ROWS, mix)]
    x_in = [(x * (1.0 + mod(4)) + mod(3)).astype(BF16) for x in x1]
    ffn = [jnp.zeros_like(x) for x in x1]
    for lo in range(0, D_FF, FF_CHUNK):
        gate = [_dg(x, wfi_ref[:, lo:lo + FF_CHUNK]) for x in x_in]
        up = [_dg(x, wfi_ref[:, D_FF + lo:D_FF + lo + FF_CHUNK]) for x in x_in]
        act = each(lambda gt, u: (gt * _sigmoid(gt) * u).astype(BF16), gate, up)
        ffn = each(lambda f, a: f + _dg(a, wfo_ref[lo:lo + FF_CHUNK, :]), ffn, act)
    y = each(lambda x, f: _layer_norm(DEEPNORM_ALPHA * x + mod(5) * f, ln2w_ref[...], ln2b_ref[...]), x1, ffn)

    @pl.when(i < NCB_D)
    def _():
        for rs, yy in zip(SUB_ROWS, y):
            yc_ref[rs, :] = yy

    @pl.when(i >= NCB_D)
    def _():
        for rs, yy in zip(SUB_ROWS, y):
            yl_ref[rs, :] = yy


def _mix_ffn(l, x_ctx, x_lat, mod_all, o_fwd, o_bwd, g, bonus, lnx_w, lnx_b, o_na_ctx, o_na_lat, o_g_ctx, o_g_lat,
             w_out_bf, ln1_w, ln1_b, w_ffn_in_bf, w_ffn_out_bf, ln2_w, ln2_b):
    tok = lambda w: pl.BlockSpec((TD, w), lambda i: (i, 0))
    once = lambda *tail: _layer_spec(l, tail, single=True)
    return pl.pallas_call(
        _mix_ffn_kernel,
        grid=(NBLK_D,),
        in_specs=[
            _ctx_spec(D_MODEL), _lat_spec(D_MODEL), once(MOD_ROWS, 6 * D_MODEL),
            tok(W_RWKV), tok(W_RWKV),
            tok(W_RWKV), tok(W_RWKV), once(1, W_RWKV), once(1, W_RWKV),
            _ctx_spec(W_NA), _lat_spec(W_NA), _ctx_spec(W_GQA), _lat_spec(W_GQA),
            once(D_MODEL, D_MODEL), once(1, D_MODEL), once(1, D_MODEL),
            once(D_MODEL, 2 * D_FF), once(D_FF, D_MODEL), once(1, D_MODEL), once(1, D_MODEL),
        ],
        out_specs=[_ctx_spec(D_MODEL), _lat_spec(D_MODEL)],
        out_shape=[jax.ShapeDtypeStruct((N_CTX, D_MODEL), F32), jax.ShapeDtypeStruct((N_LAT, D_MODEL), F32)],
        compiler_params=_cparams(1, VMEM_LIMIT_FFN),
        name="mix_ffn",
    )(x_ctx, x_lat, mod_all, o_fwd, o_bwd, g, bonus, lnx_w, lnx_b, o_na_ctx, o_na_lat, o_g_ctx, o_g_lat,
      w_out_bf, ln1_w, ln1_b, w_ffn_in_bf, w_ffn_out_bf, ln2_w, ln2_b)


def _rope_tables():
    t = jnp.arange(DEC_SEQ)
    inv = ROPE_BASE ** (-jnp.arange(ROPE_FREQ, dtype=F32) / ROPE_FREQ)
    ang_r = (t // GRID_W).astype(F32)[:, None] * inv
    ang_c = (t % GRID_W).astype(F32)[:, None] * inv
    cos = jnp.concatenate([jnp.cos(ang_r)] * 2 + [jnp.cos(ang_c)] * 2, axis=1)
    sin = jnp.concatenate([-jnp.sin(ang_r), jnp.sin(ang_r), -jnp.sin(ang_c), jnp.sin(ang_c)], axis=1)
    cos = jnp.concatenate([jnp.ones((TD, HEAD_DIM), F32), cos], axis=0)
    sin = jnp.concatenate([jnp.zeros((TD, HEAD_DIM), F32), sin], axis=0)
    return jnp.tile(cos, (1, LANES // HEAD_DIM)), jnp.tile(sin, (1, LANES // HEAD_DIM))


def _block_diag2(w):
    z = jnp.zeros_like(w[:, 0])
    return jnp.concatenate([jnp.concatenate([w[:, 0], z], axis=2), jnp.concatenate([z, w[:, 1]], axis=2)], axis=1)


def _hi_lo(w):
    return jnp.stack(_split(w), axis=1)


def _pair_states(s):
    lead = s.shape[:-3]
    s = s.reshape(lead + (N_PAIR, 2, HEAD_DIM, HEAD_DIM))
    z = jnp.zeros_like(s[..., 0, :, :])
    top = jnp.concatenate([s[..., 0, :, :], z], axis=-1)
    bot = jnp.concatenate([z, s[..., 1, :, :]], axis=-1)
    return jnp.concatenate([top, bot], axis=-2)


def _unpair_states(s):
    lead = s.shape[:-3]
    a = s[..., :HEAD_DIM, :HEAD_DIM]
    b = s[..., HEAD_DIM:, HEAD_DIM:]
    return jnp.stack([a, b], axis=-3).reshape(lead + (H_RWKV, HEAD_DIM, HEAD_DIM))


def kernel(x_prompt, x_sample, state_rwkv, cache_na_k, cache_na_v, cache_gqa_k, cache_gqa_v, c, c_ctx,
           w_mod, b_mod, w_in, rwkv_conv, rwkv_w0, rwkv_w2, rwkv_a0, rwkv_a2, rwkv_g2, rwkv_k_k, rwkv_k_a,
           rwkv_r_k, rwkv_lnx_w, rwkv_lnx_b, na_rpb, gqa_q_norm, gqa_k_norm, w_out, ln1_w, ln1_b,
           w_ffn_in, w_ffn_out, ln2_w, ln2_b):
    x_ctx, x_lat = x_prompt.reshape(N_CTX, D_MODEL), x_sample.reshape(N_LAT, D_MODEL)
    cc = jnp.concatenate([c_ctx[None], c, jnp.zeros((MOD_ROWS - 1 - DEC_BATCH, D_MODEL), F32)], axis=0)
    mod_all = _modulation(cc, w_mod, b_mod)
    tb_all = _bias_tables(na_rpb)
    cos_tab, sin_tab = _rope_tables()
    rows = lambda a: a.reshape(DEPTH, 1, -1)
    w_in_bf, w_out_bf = w_in.astype(BF16), w_out.astype(BF16)
    w_ffn_in_bf, w_ffn_out_bf = w_ffn_in.astype(BF16), w_ffn_out.astype(BF16)
    qn = jnp.tile(rows(gqa_q_norm), (1, 1, H_GQA))
    kn = jnp.tile(rows(gqa_k_norm), (1, 1, H_GQA_KV))
    rwkv_params = (rwkv_conv, rows(rwkv_w0), _hi_lo(_block_diag2(rwkv_w2)), rows(rwkv_a0),
                   _hi_lo(_block_diag2(rwkv_a2)), _hi_lo(rwkv_g2), rows(rwkv_k_k), rows(rwkv_k_a), rows(rwkv_r_k))
    s0_lat = _pair_states(state_rwkv)
    kc_na = cache_na_k.reshape(DEC_BATCH, DEPTH, PAST_LEN, W_NA)
    vc_na = cache_na_v.reshape(DEC_BATCH, DEPTH, PAST_LEN, W_NA)
    kc_g = cache_gqa_k.reshape(DEC_BATCH, DEPTH, PAST_LEN, W_GQA_KV)
    vc_g = cache_gqa_v.reshape(DEC_BATCH, DEPTH, PAST_LEN, W_GQA_KV)
    caches = [jnp.zeros((BATCH, DEPTH, SEQ, w), F32) for w in (W_NA, W_NA, W_GQA_KV, W_GQA_KV)]
    s_fin = [jnp.zeros((BATCH, DEPTH, N_PAIR, LANES, LANES), F32) for _ in range(2)]

    for l in range(DEPTH):
        naq, nak, nav, gq, gk, gv, r, kap, v, lw, ah, kd, g, bonus, *caches = _inproj(
            l, x_ctx, x_lat, mod_all, w_in_bf, qn, kn, cos_tab, sin_tab, rwkv_params, caches)
        o_fwd, o_bwd, *s_fin = _rwkv_scan(l, r, kap, v, lw, ah, kd, s0_lat, s_fin)
        o_na_ctx, o_g_ctx, o_na_lat, o_g_lat = _attention(
            l, naq, nak, nav, gq, gk, gv, kc_na, vc_na, kc_g, vc_g, tb_all)
        x_ctx, x_lat = _mix_ffn(
            l, x_ctx, x_lat, mod_all, o_fwd, o_bwd, g, bonus, rows(rwkv_lnx_w), rows(rwkv_lnx_b),
            o_na_ctx, o_na_lat, o_g_ctx, o_g_lat, w_out_bf, rows(ln1_w), rows(ln1_b),
            w_ffn_in_bf, w_ffn_out_bf, rows(ln2_w), rows(ln2_b))
    y_prompt = x_ctx.reshape(BATCH, SEQ, D_MODEL)
    y_sample = x_lat.reshape(DEC_BATCH, DEC_SEQ, D_MODEL)
    new_state = jnp.stack([_unpair_states(s) for s in s_fin], axis=2)
    return (y_prompt, y_sample, new_state,
            caches[0].reshape(BATCH, DEPTH, SEQ, H_NA, HEAD_DIM), caches[1].reshape(BATCH, DEPTH, SEQ, H_NA, HEAD_DIM),
            caches[2].reshape(BATCH, DEPTH, SEQ, H_GQA_KV, HEAD_DIM),
            caches[3].reshape(BATCH, DEPTH, SEQ, H_GQA_KV, HEAD_DIM))
```

```python
import jax
import jax.numpy as jnp
from jax import lax
from jax.experimental import pallas as pl
from jax.experimental.pallas import tpu as pltpu

F32 = jnp.float32
BF16 = jnp.bfloat16

D_MODEL = 1024
BATCH = 16
SEQ = 256
DEPTH = 4
DEC_BATCH = 2
DEC_SEQ = 1024
PAST_LEN = 512
GRID_W = 64
GRID_ROWS = DEC_SEQ // GRID_W
HEAD_DIM = 64
H_RWKV = 4
H_NA = 4
H_GQA = 8
H_GQA_KV = 2
W_RWKV = H_RWKV * HEAD_DIM
W_NA = H_NA * HEAD_DIM
W_GQA = H_GQA * HEAD_DIM
W_GQA_KV = H_GQA_KV * HEAD_DIM
LORA_W = 64
LORA_A = 64
LORA_G = 128
RWKV_IN = 3 * W_RWKV + 2 * LORA_W + 2 * LORA_A + LORA_G
NA_IN = 3 * W_NA
GQA_IN = W_GQA + 2 * W_GQA_KV
D_IN = RWKV_IN + NA_IN + GQA_IN
NA_ROWS = 8
NA_COLS = 16
ROPE_BASE = 10000.0
ROPE_FREQ = HEAD_DIM // 4
D_FF = ((8 * D_MODEL + 3 * 256 - 1) // (3 * 256)) * 256
DEEPNORM_ALPHA = (2 * DEPTH) ** 0.25
LN_EPS = 1e-5
RMS_EPS = 1e-6
GN_EPS = 64e-5
NEG_INF = -1e30
ATTN_SCALE = HEAD_DIM ** -0.5

LANES = 128
TM = 256
N_CTX = BATCH * SEQ
N_LAT = DEC_BATCH * DEC_SEQ
N_TOK = N_CTX + N_LAT
CHUNK = 64
TS = 512
NCS = TS // CHUNK
NBLK_S = N_TOK // TS
NCB_S = N_CTX // TS
LAT_BLKS_S = DEC_SEQ // TS
SEQS_S = TS // SEQ
SEQ_CHUNKS = SEQ // CHUNK
TD = 512
SUB_ROWS = [slice(h * SEQ, (h + 1) * SEQ) for h in range(TD // SEQ)]
IN_PIECES = SUB_ROWS
NBLK_D = N_TOK // TD
NCB_D = N_CTX // TD
LAT_BLKS_D = DEC_SEQ // TD
FF_CHUNK = D_FF
N_PAIR = W_RWKV // LANES
MOD_ROWS = 8
VMEM_LIMIT = 48 * 1024 * 1024
VMEM_LIMIT_FFN = 56 * 1024 * 1024
N_DR = 2 * NA_ROWS - 1
N_DC = 2 * NA_COLS - 1
N_BAND = NA_ROWS * GRID_W


def _cparams(n_grid, vmem_limit=VMEM_LIMIT):
    return pltpu.CompilerParams(dimension_semantics=("arbitrary",) * n_grid, vmem_limit_bytes=vmem_limit)


def _iota(shape, dim):
    return lax.broadcasted_iota(jnp.int32, shape, dim)


NN = (((1,), (0,)), ((), ()))
NT = (((1,), (1,)), ((), ()))
TN = (((0,), (0,)), ((), ()))


def _dg(a, b, dims=NN, precision=None):
    return lax.dot_general(a, b, dims, preferred_element_type=F32, precision=precision)


def _split(x):
    hi = x.astype(BF16)
    return hi, (x - hi.astype(F32)).astype(BF16)


def _dgs(a_s, b_s, dims=NN):
    if len(a_s) == 1:
        return _dg(a_s[0], b_s[0], dims)
    (ah, al), (bh, bl) = a_s, b_s
    ca, cb = dims[0][0][0], dims[0][1][0]
    return _dg(jnp.concatenate([ah, ah, al], axis=ca), jnp.concatenate([bh, bl, bh], axis=cb), dims)


def _sigmoid(x):
    return 1.0 / (1.0 + jnp.exp(-x))


def _softplus(x):
    return jnp.maximum(x, 0.0) + jnp.log(1.0 + jnp.exp(-jnp.abs(x)))


def _seg64_sum(x):
    rows, width = x.shape
    lo = _iota((rows, LANES), 1) < HEAD_DIM
    outs = []
    for c in range(width // LANES):
        blk = x[:, c * LANES:(c + 1) * LANES]
        s_lo = jnp.sum(jnp.where(lo, blk, 0.0), axis=-1, keepdims=True)
        s_hi = jnp.sum(jnp.where(lo, 0.0, blk), axis=-1, keepdims=True)
        outs.append(jnp.where(lo, s_lo, s_hi))
    return outs[0] if len(outs) == 1 else jnp.concatenate(outs, axis=1)


def _layer_norm(x, w, b):
    mu = jnp.mean(x, axis=-1, keepdims=True)
    xc = x - mu
    var = jnp.mean(xc * xc, axis=-1, keepdims=True)
    return xc * lax.rsqrt(var + LN_EPS) * w + b


def _mod_row(i):
    return jnp.where(i < NCB_D, 0, 1 + (i - NCB_D) // LAT_BLKS_D)


def _layer_spec(l, tail, single=False):
    idx = lambda *g: (l,) + (0,) * len(tail)
    if single:
        return pl.BlockSpec((None,) + tuple(tail), idx, pipeline_mode=pl.Buffered(1))
    return pl.BlockSpec((None,) + tuple(tail), idx)


ANY_SPEC = pl.BlockSpec(memory_space=pl.ANY)


def _ctx_spec(w):
    return pl.BlockSpec((TD, w), lambda i: (jnp.minimum(i, NCB_D - 1), 0))


def _lat_spec(w):
    return pl.BlockSpec((TD, w), lambda i: (jnp.maximum(i - NCB_D, 0), 0))


def _pick(i, ctx_ref, lat_ref, rows=slice(None)):
    return jnp.where(i < NCB_D, ctx_ref[rows, :], lat_ref[rows, :])


N_COND = 1 + DEC_BATCH


def _mod_kernel(c_ref, w_ref, b_ref, o_ref, sb_scr):
    @pl.when((pl.program_id(0) == 0) & (pl.program_id(1) == 0))
    def _():
        c = c_ref[...]
        s = c * _sigmoid(c)
        for r in range(N_COND):
            sb_scr[r] = jnp.broadcast_to(s[r:r + 1, :], (LANES, D_MODEL)).T

    def body(kc, accs):
        rows = pl.ds(pl.multiple_of(kc * 8, 8), 8)
        wk = w_ref[rows, :]
        return tuple(a + wk * jnp.concatenate([sb_scr[r, rows, :]] * (D_MODEL // LANES), axis=1)
                     for r, a in enumerate(accs))

    accs = lax.fori_loop(0, D_MODEL // 8, body, tuple(jnp.zeros((8, D_MODEL), F32) for _ in range(N_COND)),
                         unroll=8)
    rowid = _iota((MOD_ROWS, D_MODEL), 0)
    out = jnp.zeros((MOD_ROWS, D_MODEL), F32)
    for r, a in enumerate(accs):
        out = jnp.where(rowid == r, jnp.sum(a, axis=0, keepdims=True), out)
    o_ref[...] = out + b_ref[...]


def _modulation(cc, w_mod, b_mod):
    return pl.pallas_call(
        _mod_kernel,
        grid=(DEPTH, 6),
        in_specs=[
            pl.BlockSpec((MOD_ROWS, D_MODEL), lambda l, j: (0, 0)),
            pl.BlockSpec((None, D_MODEL, D_MODEL), lambda l, j: (l, 0, j)),
            pl.BlockSpec((None, 1, D_MODEL), lambda l, j: (l, 0, j)),
        ],
        out_specs=pl.BlockSpec((None, MOD_ROWS, D_MODEL), lambda l, j: (l, 0, j)),
        out_shape=jax.ShapeDtypeStruct((DEPTH, MOD_ROWS, 6 * D_MODEL), F32),
        scratch_shapes=[pltpu.VMEM((N_COND, D_MODEL, LANES), F32)],
        compiler_params=_cparams(2),
        name="modulation",
    )(cc, w_mod, b_mod.reshape(DEPTH, 1, 6 * D_MODEL))


def _bias_kernel(rpb_ref, o_ref):
    q = _iota((GRID_W, LANES), 0)
    x = _iota((GRID_W, LANES), 1)
    c = x % GRID_W
    right = x >= GRID_W
    dc = jnp.clip(c - q, 1 - NA_COLS, NA_COLS - 1) + NA_COLS - 1
    c0 = jnp.clip(q - NA_COLS // 2, 0, GRID_W - NA_COLS)
    in_win = (c >= c0) & (c < c0 + NA_COLS)

    def body(t, carry):
        h = t // (N_DR - 1)
        dr = t % (N_DR - 1)
        rows = [jnp.broadcast_to(rpb_ref[pl.ds(h * N_DR + dr + k, 1), :], (GRID_W, LANES)) for k in range(2)]
        lo, hi = (jnp.take_along_axis(r, dc, axis=1) for r in rows)
        o_ref[h, dr] = jnp.where(in_win, jnp.where(right, hi, lo), NEG_INF)
        return carry

    lax.fori_loop(0, H_NA * (N_DR - 1), body, 0, unroll=N_DR - 1)


RPB_ROWS = -(-H_NA * N_DR // 8) * 8


def _bias_tables(na_rpb):
    rpb = jnp.pad(na_rpb.reshape(DEPTH, H_NA * N_DR, N_DC), ((0, 0), (0, RPB_ROWS - H_NA * N_DR), (0, LANES - N_DC)))
    return pl.pallas_call(
        _bias_kernel,
        grid=(DEPTH,),
        in_specs=[pl.BlockSpec((None, RPB_ROWS, LANES), lambda l: (l, 0, 0))],
        out_specs=pl.BlockSpec((None, H_NA, N_DR - 1, GRID_W, LANES), lambda l: (l, 0, 0, 0, 0)),
        out_shape=jax.ShapeDtypeStruct((DEPTH, H_NA, N_DR - 1, GRID_W, LANES), F32),
        compiler_params=_cparams(1),
        name="na_bias_tables",
    )(rpb)


def _rope(x, cos, sin):
    k = x.shape[1] // LANES
    cosf = cos if k == 1 else jnp.concatenate([cos] * k, axis=1)
    sinf = sin if k == 1 else jnp.concatenate([sin] * k, axis=1)
    first = (_iota(x.shape, 1) % (2 * ROPE_FREQ)) < ROPE_FREQ
    partner = jnp.where(first, pltpu.roll(x, x.shape[1] - ROPE_FREQ, axis=1), pltpu.roll(x, ROPE_FREQ, axis=1))
    return x * cosf + partner * sinf


def _rwkv_features(x, prev_row, next_row, conv_ref, w0_ref, w2_ref, a0_ref, a2_ref, g2_ref, kk_ref, ka_ref, rk_ref):
    n = x.shape[0]
    rows = _iota(x.shape, 0)
    x_prev = jnp.where(rows == 0, prev_row, pltpu.roll(x, 1, axis=0))
    x_next = jnp.where(rows == n - 1, next_row, pltpu.roll(x, n - 1, axis=0))
    f = x_prev * conv_ref[0:1, :] + x * conv_ref[1:2, :] + x_next * conv_ref[2:3, :]
    o1, o2, o3 = W_RWKV, 2 * W_RWKV, 3 * W_RWKV
    o4 = o3 + 2 * LORA_W
    o5 = o4 + 2 * LORA_A
    r, k, v = f[:, :o1], f[:, o1:o2], f[:, o2:o3]
    wd, ad, gd = f[:, o3:o4], f[:, o4:o5], f[:, o5:]
    def lora(y, w_ref):
        y_bf = y.astype(BF16)
        return _dg(jnp.concatenate([y_bf, y_bf], axis=1), jnp.concatenate([w_ref[0], w_ref[1]], axis=0))
    log_w = -_softplus(-(w0_ref[...] + lora(jnp.tanh(wd), w2_ref))) - 0.5
    a = _sigmoid(a0_ref[...] + lora(ad, a2_ref))
    g = lora(_sigmoid(gd), g2_ref)
    kk = k * kk_ref[...]
    kap = kk / jnp.maximum(jnp.sqrt(_seg64_sum(kk * kk)), 1e-12)
    per_dir = []
    kd_sum = jnp.zeros_like(k)
    for d in range(2):
        a_d = a[:, d * W_RWKV:(d + 1) * W_RWKV]
        kd = k * (1.0 + (a_d - 1.0) * ka_ref[...])
        kd_sum = kd_sum + kd
        per_dir.append((-jnp.exp(log_w[:, d * W_RWKV:(d + 1) * W_RWKV]), a_d * kap, kd))
    bonus = _seg64_sum(r * kd_sum * rk_ref[...]) * v
    return r, kap, v, per_dir, g, bonus


HALO = 8


def _inproj_kernel(xc_ref, xl_ref, xp_ref, xn_ref, mod_ref, w_ref, qn_ref, kn_ref, cos_ref, sin_ref,
                   conv_ref, w0_ref, w2_ref, a0_ref, a2_ref, g2_ref, kk_ref, ka_ref, rk_ref,
                   c0_ref, c1_ref, c2_ref, c3_ref,
                   naq_ref, nak_ref, nav_ref, gq_ref, gk_ref, gv_ref,
                   r_ref, kap_ref, v_ref, lw_ref, ah_ref, kd_ref, g_ref, bonus_ref,
                   cnak_ref, cnav_ref, cgk_ref, cgv_ref):
    del c0_ref, c1_ref, c2_ref, c3_ref
    i = pl.program_id(0)
    row = _mod_row(i)
    shift1 = mod_ref[pl.ds(row, 1), 0:D_MODEL]
    scale1 = mod_ref[pl.ds(row, 1), D_MODEL:2 * D_MODEL]
    modulate = lambda x: (x * (1.0 + scale1) + shift1).astype(BF16)
    lat = i >= NCB_D
    pos = (i - NCB_D) % LAT_BLKS_D
    in_seq = jnp.where(lat, 1.0, 0.0)
    has_prev = jnp.where(lat & (pos != 0), 1.0, 0.0)
    has_next = jnp.where(lat & (pos != LAT_BLKS_D - 1), 1.0, 0.0)
    xm = [modulate(_pick(i, xc_ref, xl_ref, rs)) for rs in IN_PIECES]
    xm[0] = jnp.concatenate([xm[0], modulate(xp_ref[...]), modulate(xn_ref[...])], axis=0)
    projs = [_dg(x, w_ref[...]) for x in xm]
    n0 = IN_PIECES[0].stop
    halo = projs[0][n0:, :RWKV_IN]
    projs[0] = projs[0][:n0]
    feats = [proj[:, :RWKV_IN] for proj in projs]
    o_na = RWKV_IN
    o_g = RWKV_IN + NA_IN
    new_kv = []
    for h, (rs, proj) in enumerate(zip(IN_PIECES, projs)):
        if h == 0:
            prev_row = halo[HALO - 1:HALO] * has_prev
        else:
            prev_row = feats[h - 1][-1:] * (in_seq if rs.start % SEQ == 0 else 1.0)
        if h == len(IN_PIECES) - 1:
            next_row = halo[HALO:HALO + 1] * has_next
        else:
            next_row = feats[h + 1][0:1] * (in_seq if rs.stop % SEQ == 0 else 1.0)
        r, kap, v, per_dir, g, bonus = _rwkv_features(
            feats[h], prev_row, next_row, conv_ref, w0_ref, w2_ref, a0_ref, a2_ref, g2_ref, kk_ref, ka_ref, rk_ref)
        r_ref[rs, :] = r.astype(BF16)
        kap_ref[rs, :] = kap.astype(BF16)
        v_ref[rs, :] = v.astype(BF16)
        g_ref[rs, :] = g
        bonus_ref[rs, :] = bonus
        for d, (lw, ah, kd) in enumerate(per_dir):
            lw_ref[d, rs, :] = lw
            ah_ref[d, rs, :] = ah.astype(BF16)
            kd_ref[d, rs, :] = kd.astype(BF16)
        naq_ref[rs, :] = (proj[:, o_na:o_na + W_NA] * ATTN_SCALE).astype(BF16)
        nak = proj[:, o_na + W_NA:o_na + 2 * W_NA]
        nav = proj[:, o_na + 2 * W_NA:o_na + 3 * W_NA]
        q = proj[:, o_g:o_g + W_GQA]
        k = proj[:, o_g + W_GQA:o_g + W_GQA + W_GQA_KV]
        gv = proj[:, o_g + W_GQA + W_GQA_KV:]
        cos = cos_ref[rs, :]
        sin = sin_ref[rs, :]
        q = q * lax.rsqrt(_seg64_sum(q * q) * (1.0 / HEAD_DIM) + RMS_EPS) * qn_ref[...]
        k = k * lax.rsqrt(_seg64_sum(k * k) * (1.0 / HEAD_DIM) + RMS_EPS) * kn_ref[...]
        gk = _rope(k, cos, sin)
        gq_ref[rs, :] = (_rope(q, cos, sin) * ATTN_SCALE).astype(BF16)
        nak_ref[rs, :] = nak.astype(BF16)
        nav_ref[rs, :] = nav.astype(BF16)
        gk_ref[rs, :] = gk.astype(BF16)
        gv_ref[rs, :] = gv.astype(BF16)
        new_kv.append((nak, nav, gk, gv))

    @pl.when(i < NCB_D)
    def _():
        for rs, vals in zip(IN_PIECES, new_kv):
            for ref, val in zip((cnak_ref, cnav_ref, cgk_ref, cgv_ref), vals):
                ref[rs.start // SEQ, rs.start % SEQ:rs.start % SEQ + rs.stop - rs.start, :] = val


def _inproj(l, x_ctx, x_lat, mod_all, w_in_bf, qn, kn, cos_tab, sin_tab, rwkv_params, caches):
    tab_idx = lambda i: (jnp.where(i < NCB_D, 0, 1 + (i - NCB_D) % LAT_BLKS_D), 0)
    lat_blk = lambda i: jnp.maximum(i - NCB_D, 0)
    halo_blocks = TD // HALO
    widths = (W_NA, W_NA, W_NA, W_GQA, W_GQA_KV, W_GQA_KV, W_RWKV, W_RWKV, W_RWKV)
    cache_w = (W_NA, W_NA, W_GQA_KV, W_GQA_KV)
    tok = lambda w: pl.BlockSpec((TD, w), lambda i: (i, 0))
    tok2 = pl.BlockSpec((2, TD, W_RWKV), lambda i: (0, i, 0))
    cache_spec = lambda w: pl.BlockSpec((TD // SEQ, None, SEQ, w), lambda i: (jnp.minimum(i, NCB_D - 1), l, 0, 0))
    rwkv_tails = ((3, RWKV_IN), (1, 2 * W_RWKV), (2, 2 * LORA_W, 2 * W_RWKV), (1, 2 * W_RWKV),
                  (2, 2 * LORA_A, 2 * W_RWKV), (2, LORA_G, W_RWKV), (1, W_RWKV), (1, W_RWKV), (1, W_RWKV))
    in_specs = [
        _ctx_spec(D_MODEL), _lat_spec(D_MODEL),
        pl.BlockSpec((HALO, D_MODEL), lambda i: (jnp.maximum(lat_blk(i) * halo_blocks - 1, 0), 0)),
        pl.BlockSpec((HALO, D_MODEL), lambda i: (jnp.minimum((lat_blk(i) + 1) * halo_blocks, N_LAT // HALO - 1), 0)),
        _layer_spec(l, (MOD_ROWS, 6 * D_MODEL)),
        _layer_spec(l, (D_MODEL, D_IN)),
        _layer_spec(l, (1, W_GQA)),
        _layer_spec(l, (1, W_GQA_KV)),
        pl.BlockSpec((TD, LANES), tab_idx),
        pl.BlockSpec((TD, LANES), tab_idx),
    ] + [_layer_spec(l, t) for t in rwkv_tails]
    n_in = len(in_specs)
    out_specs = [tok(w) for w in widths] + [tok2] * 3 + [tok(W_RWKV)] * 2
    out_shape = ([jax.ShapeDtypeStruct((N_TOK, w), BF16) for w in widths]
                 + [jax.ShapeDtypeStruct((2, N_TOK, W_RWKV), dt) for dt in (F32, BF16, BF16)]
                 + [jax.ShapeDtypeStruct((N_TOK, W_RWKV), F32)] * 2)
    return pl.pallas_call(
        _inproj_kernel,
        grid=(NBLK_D,),
        in_specs=in_specs + [ANY_SPEC] * 4,
        out_specs=out_specs + [cache_spec(w) for w in cache_w],
        out_shape=out_shape + [jax.ShapeDtypeStruct((BATCH, DEPTH, SEQ, w), F32) for w in cache_w],
        input_output_aliases={n_in + j: len(out_specs) + j for j in range(4)},
        compiler_params=_cparams(1),
        name="inproj",
    )(x_ctx, x_lat, x_lat, x_lat, mod_all, w_in_bf, qn, kn, cos_tab, sin_tab, *rwkv_params, *caches)


def _scan_kernel(r0_ref, kap0_ref, v0_ref, r1_ref, kap1_ref, v1_ref,
                 lw0_ref, ah0_ref, kd0_ref, lw1_ref, ah1_ref, kd1_ref, s00_ref, s01_ref, sf0_in_ref, sf1_in_ref,
                 o0_ref, o1_ref, sf0_ref, sf1_ref,
                 s_scr, sdone_scr, rt_scr, kt_scr, kdt_scr, at_scr, cum_scr, m_scr, n_scr, q_scr, oo_scr):
    del sf0_in_ref, sf1_in_ref
    j = pl.program_id(0)
    par = j % 2
    jb = jnp.maximum(j - 1, 0)
    upd_blk = (jb, NBLK_S - 1 - jb)
    is_ctx = tuple(b < NCB_S for b in upd_blk)
    first_pos = (0, LAT_BLKS_S - 1)
    r_refs, kap_refs, v_refs = (r0_ref, r1_ref), (kap0_ref, kap1_ref), (v0_ref, v1_ref)
    lw_refs, ah_refs, kd_refs = (lw0_ref, lw1_ref), (ah0_ref, ah1_ref), (kd0_ref, kd1_ref)
    s0_refs, o_refs, sf_refs = (s00_ref, s01_ref), (o0_ref, o1_ref), (sf0_ref, sf1_ref)

    @pl.when(j == 0)
    def _():
        for ref in (s_scr, sdone_scr, m_scr, n_scr, q_scr, oo_scr):
            ref[...] = jnp.zeros_like(ref)

    for dn in range(2):
        blk = upd_blk[dn]

        @pl.when((j > 0) & is_ctx[dn])
        def _():
            s_scr[dn] = jnp.zeros(s_scr.shape[1:], F32)

        @pl.when((j > 0) & (blk >= NCB_S) & ((blk - NCB_S) % LAT_BLKS_S == first_pos[dn]))
        def _():
            s_scr[dn] = s0_refs[dn][...]

    def seq_slot(dn, n):
        return n if dn == 0 else SEQS_S - 1 - n

    def update(c):
        for dn in range(2):
            ce = c if dn == 0 else NCS - 1 - c
            rows = slice(ce * CHUNK, (ce + 1) * CHUNK)
            for p in range(N_PAIR):
                s = s_scr[dn, p]
                if c > 0 and c % SEQ_CHUNKS == 0:
                    sdone_scr[dn, seq_slot(dn, c // SEQ_CHUNKS - 1), p] = s
                    s = jnp.where(is_ctx[dn], 0.0, s)
                s2 = jnp.concatenate(_split(s), axis=1)
                q = q_scr[1 - par, dn, ce, p]
                m = m_scr[1 - par, dn, ce, p]
                o_refs[dn][rows, p * LANES:(p + 1) * LANES] = (
                    _dg(jnp.concatenate([q, q], axis=1), s2, NT) + oo_scr[1 - par, dn, ce, p])
                s_scr[dn, p] = _dg(s2, jnp.concatenate([m, m], axis=0)) + n_scr[1 - par, dn, ce, p]

    pending = list(range(NCS))

    def next_update():
        if pending:
            update(pending.pop(0))

    next_update()
    rr = _iota((TM, TM), 0)
    cc = _iota((TM, TM), 1)
    same_chunk = (rr // CHUNK) == (cc // CHUNK)
    for dn in range(2):
        order = (rr >= cc) if dn == 0 else (rr <= cc)
        tri = jnp.where(same_chunk & order, 1.0, 0.0).astype(BF16)
        for h in range(TS // TM):
            rs = slice(h * TM, (h + 1) * TM)
            lw = lw_refs[dn][rs, :]
            lw_hi = lw.astype(BF16)
            lw_r = lw - lw_hi.astype(F32)
            lw_mid = lw_r.astype(BF16)
            lw_lo = (lw_r - lw_mid.astype(F32)).astype(BF16)
            cum = _dg(jnp.concatenate([tri, tri, tri], axis=1), jnp.concatenate([lw_hi, lw_mid, lw_lo], axis=0))
            e_neg = jnp.exp(-cum)
            rt_scr[dn, rs, :] = r_refs[dn][rs, :] * jnp.exp(cum)
            kt_scr[dn, rs, :] = kap_refs[dn][rs, :] * jnp.exp(cum - lw)
            kdt_scr[dn, rs, :] = kd_refs[dn][rs, :] * e_neg
            at_scr[dn, rs, :] = ah_refs[dn][rs, :] * e_neg
            cum_scr[dn, rs, :] = cum
        if dn == 0:
            next_update()

    row = _iota((CHUNK, LANES), 0)
    col = _iota((CHUNK, LANES), 1) % CHUNK
    left = _iota((CHUNK, LANES), 1) < CHUNK
    incl = (row >= col, row <= col)
    strict = (row > col, row < col)
    eye = jnp.where(row == col, 1.0, 0.0)
    r2 = _iota((LANES, LANES), 0)
    c2 = _iota((LANES, LANES), 1)
    bd_mask = (r2 // CHUNK) == (c2 // CHUNK)
    eye2 = jnp.where(r2 == c2, 1.0, 0.0)

    zero_bf = jnp.zeros((CHUNK, LANES), BF16)
    split = lambda x: (x.astype(BF16),)

    def bd(xs):
        return tuple(jnp.concatenate([jnp.where(left, y, zero_bf), jnp.where(left, zero_bf, y)], axis=0)
                     for y in xs)

    def cat(xs, ys, axis):
        return tuple(jnp.concatenate([x, y], axis=axis) for x, y in zip(xs, ys))

    def mm(a_list, b_list):
        return [_dgs(a, bd(b)) for a, b in zip(a_list, b_list)]

    units = [(dn, c, p) for c in range(NCS) for dn in range(2) for p in range(N_PAIR)]
    each = lambda f, *lists: [f(*args) for args in zip(*lists)]

    def load(ref):
        return [ref[dn, c * CHUNK:(c + 1) * CHUNK, p * LANES:(p + 1) * LANES] for dn, c, p in units]

    rt = load(rt_scr)
    v = [v_refs[dn][c * CHUNK:(c + 1) * CHUNK, p * LANES:(p + 1) * LANES] for dn, c, p in units]
    kt_s, rt_s, kdt_s, at_s, v_s = (each(split, x) for x in (load(kt_scr), rt, load(kdt_scr), load(at_scr), v))
    gam = [jnp.exp(cum_scr[dn, (c + 1) * CHUNK - 1:(c + 1) * CHUNK, p * LANES:(p + 1) * LANES] if dn == 0 else
                   cum_scr[dn, c * CHUNK:c * CHUNK + 1, p * LANES:(p + 1) * LANES]) for dn, c, p in units]
    gram = each(lambda k, r, a, kd: _dgs(cat(k, r, 0), cat(bd(a), bd(kd), 0), NT), kt_s, rt_s, at_s, kdt_s)
    dirs = [u[0] for u in units]
    la = [jnp.where(strict[dn], g[0:CHUNK, 0:LANES], 0.0) for dn, g in zip(dirs, gram)]
    lk_s = [split(jnp.where(strict[dn], g[0:CHUNK, LANES:], 0.0)) for dn, g in zip(dirs, gram)]
    ra_s = [split(jnp.where(incl[dn], g[CHUNK:, 0:LANES], 0.0)) for dn, g in zip(dirs, gram)]
    rk_s = [split(jnp.where(incl[dn], g[CHUNK:, LANES:], 0.0)) for dn, g in zip(dirs, gram)]
    next_update()
    lrv = mm(each(lambda lk, rk: cat(lk, rk, 0), lk_s, rk_s), v_s)
    next_update()
    b = 8
    l8 = [jnp.where((row // b) == (col // b), x, 0.0) for x in la]
    l8_s = each(split, l8)
    l8_2 = mm(l8_s, l8_s)
    next_update()
    l8_2s = each(split, l8_2)
    l8_4 = mm(l8_2s, l8_2s)
    p1 = mm([split(eye - x) for x in l8], [split(eye + y) for y in l8_2])
    next_update()
    t = mm(each(split, p1), [split(eye + y) for y in l8_4])
    while b < CHUNK:
        next_update()
        offd = ((row // (2 * b)) == (col // (2 * b))) & ((row // b) != (col // b))
        t_s = each(split, t)
        x = mm(t_s, [split(jnp.where(offd, y, 0.0)) for y in la])
        t = each(lambda tt, z: tt - z, t, mm(each(split, x), t_s))
        b *= 2
    assert not pending, "more chunk updates than stages to place them between"
    tx = each(lambda tt, k, y: _dgs(split(tt), cat(bd(k), bd(split(y[0:CHUNK])), 1)), t, kt_s, lrv)
    khat_s = [split(y[:, 0:LANES]) for y in tx]
    w1_s = [split(y[:, LANES:]) for y in tx]
    rx = each(lambda r, k, w: _dgs(r, cat(bd(k), bd(w), 1)), ra_s, khat_s, w1_s)
    mk = each(lambda k, a: _dgs(k, a, TN), khat_s, at_s)
    nk = each(lambda vv, w, kd, a: _dgs(cat(vv, w, 0), cat(kd, tuple(-y for y in a), 0), TN),
              v_s, w1_s, kdt_s, at_s)
    for i, (dn, c, p) in enumerate(units):
        q_scr[par, dn, c, p] = (rt[i] - rx[i][:, 0:LANES]).astype(BF16)
        oo_scr[par, dn, c, p] = lrv[i][CHUNK:] - rx[i][:, LANES:]
        m_scr[par, dn, c, p] = ((eye2 - jnp.where(bd_mask, mk[i], 0.0)) * gam[i]).astype(BF16)
        n_scr[par, dn, c, p] = jnp.where(bd_mask, nk[i], 0.0) * gam[i]

    for dn in range(2):
        @pl.when((j > 0) & is_ctx[dn])
        def _():
            for n in range(SEQS_S - 1):
                sf_refs[dn][seq_slot(dn, n)] = sdone_scr[dn, seq_slot(dn, n)]
            sf_refs[dn][seq_slot(dn, SEQS_S - 1)] = s_scr[dn]


def _rwkv_scan(l, r, kap, v, lw, ah, kd, s0_lat, s_fin):
    def stage_blk(dn, j):
        jj = jnp.minimum(j, NBLK_S - 1)
        return jj if dn == 0 else NBLK_S - 1 - jj

    def update_blk(dn, j):
        jj = jnp.maximum(j - 1, 0)
        return jj if dn == 0 else NBLK_S - 1 - jj

    pair = (N_PAIR, LANES, LANES)
    tok = lambda dn: pl.BlockSpec((TS, W_RWKV), lambda j: (stage_blk(dn, j), 0))
    tok2 = lambda dn: pl.BlockSpec((None, TS, W_RWKV), lambda j: (dn, stage_blk(dn, j), 0))
    s0_spec = lambda dn: pl.BlockSpec(
        (None, None, None) + pair,
        lambda j: (jnp.maximum(update_blk(dn, j) - NCB_S, 0) // LAT_BLKS_S, l, dn, 0, 0, 0))
    sf_spec = lambda dn: pl.BlockSpec(
        (SEQS_S, None) + pair, lambda j: (jnp.minimum(update_blk(dn, j), NCB_S - 1), l, 0, 0, 0))
    o_spec = lambda dn: pl.BlockSpec((TS, W_RWKV), lambda j: (update_blk(dn, j), 0))
    dir_scr = lambda *shape, dtype=F32: pltpu.VMEM((2,) + shape, dtype)
    stage_scr = (2, NCS, N_PAIR)
    return pl.pallas_call(
        _scan_kernel,
        grid=(NBLK_S + 1,),
        in_specs=[tok(0), tok(0), tok(0), tok(1), tok(1), tok(1),
                  tok2(0), tok2(0), tok2(0), tok2(1), tok2(1), tok2(1),
                  s0_spec(0), s0_spec(1), ANY_SPEC, ANY_SPEC],
        out_specs=[o_spec(0), o_spec(1), sf_spec(0), sf_spec(1)],
        out_shape=[jax.ShapeDtypeStruct((N_TOK, W_RWKV), F32)] * 2
        + [jax.ShapeDtypeStruct((BATCH, DEPTH) + pair, F32)] * 2,
        scratch_shapes=[dir_scr(*pair), dir_scr(SEQS_S, *pair)] + [dir_scr(TS, W_RWKV)] * 5
        + [dir_scr(*stage_scr, LANES, LANES, dtype=BF16), dir_scr(*stage_scr, LANES, LANES),
           dir_scr(*stage_scr, CHUNK, LANES, dtype=BF16), dir_scr(*stage_scr, CHUNK, LANES)],
        input_output_aliases={14: 2, 15: 3},
        compiler_params=_cparams(1),
        name="rwkv_scan",
    )(r, kap, v, r, kap, v, lw, ah, kd, lw, ah, kd, s0_lat, s0_lat, s_fin[0], s_fin[1])


def _attend(groups):
    lhs = []
    for q_cols, _, _, _ in groups:
        left = _iota(q_cols[0].shape, 1) < HEAD_DIM
        parts = []
        for qc in q_cols:
            parts += [jnp.where(left, qc, 0.0), jnp.where(left, 0.0, qc)]
        lhs.append(jnp.concatenate(parts, axis=0).astype(BF16))
    s = [_dg(x, g[1], NT) for x, g in zip(lhs, groups)]
    s = [x if g[3] is None else x + g[3] for x, g in zip(s, groups)]
    p = [jnp.exp(x - jnp.max(x, axis=-1, keepdims=True)) for x in s]
    inv = [1.0 / jnp.sum(x, axis=-1, keepdims=True) for x in p]
    o = [_dg(x.astype(BF16), g[2]) * y for x, y, g in zip(p, inv, groups)]
    outs = []
    for x, (q_cols, _, _, _) in zip(o, groups):
        rows = q_cols[0].shape[0]
        left = _iota(q_cols[0].shape, 1) < HEAD_DIM
        outs.append([jnp.where(left, x[2 * j * rows:(2 * j + 1) * rows],
                               x[(2 * j + 1) * rows:(2 * j + 2) * rows]).astype(BF16) for j in range(len(q_cols))])
    return outs


def _cols(x):
    return [x[:, c * LANES:(c + 1) * LANES] for c in range(x.shape[1] // LANES)]


def _gqa_groups(q, k, v):
    k, v = k.astype(F32), v.astype(F32)
    left = _iota(k.shape, 1) < HEAD_DIM
    k_sw = pltpu.roll(k, HEAD_DIM, axis=1)
    v_sw = pltpu.roll(v, HEAD_DIM, axis=1)
    q_cols = _cols(q)
    groups = []
    for g in range(H_GQA_KV):
        k2 = jnp.where(left, k, k_sw) if g == 0 else jnp.where(left, k_sw, k)
        v2 = jnp.where(left, v, v_sw) if g == 0 else jnp.where(left, v_sw, v)
        groups.append((q_cols[2 * g:2 * g + 2], k2.astype(BF16), v2.astype(BF16), None))
    return groups


CTX_STEP_SEQS = 4


def _ctx_attn_kernel(naq_ref, nak_ref, nav_ref, gq_ref, gk_ref, gv_ref, ona_ref, og_ref):
    groups = []
    n_na = W_NA // LANES
    for n in range(CTX_STEP_SEQS):
        rs = slice(n * SEQ, (n + 1) * SEQ)
        k_cols = _cols(nak_ref[rs, :].astype(BF16))
        v_cols = _cols(nav_ref[rs, :].astype(BF16))
        groups += [([qc], kc, vc, None) for qc, kc, vc in zip(_cols(naq_ref[rs, :]), k_cols, v_cols)]
        groups += _gqa_groups(gq_ref[rs, :], gk_ref[rs, :], gv_ref[rs, :])
    outs = _attend(groups)
    per_seq = len(groups) // CTX_STEP_SEQS
    for n in range(CTX_STEP_SEQS):
        rs = slice(n * SEQ, (n + 1) * SEQ)
        o = outs[n * per_seq:(n + 1) * per_seq]
        ona_ref[rs, :] = jnp.concatenate([x[0] for x in o[:n_na]], axis=1)
        og_ref[rs, :] = jnp.concatenate([c for x in o[n_na:] for c in x], axis=1)


def _ctx_attention(naq, nak, nav, gq, gk, gv):
    spec = lambda w: pl.BlockSpec((CTX_STEP_SEQS * SEQ, w), lambda b: (b, 0))
    return pl.pallas_call(
        _ctx_attn_kernel,
        grid=(BATCH // CTX_STEP_SEQS,),
        in_specs=[spec(W_NA), spec(W_NA), spec(W_NA), spec(W_GQA), spec(W_GQA_KV), spec(W_GQA_KV)],
        out_specs=[spec(W_NA), spec(W_GQA)],
        out_shape=[jax.ShapeDtypeStruct((N_CTX, W_NA), BF16), jax.ShapeDtypeStruct((N_CTX, W_GQA), BF16)],
        compiler_params=_cparams(1),
        name="ctx_attention",
    )(naq, nak, nav, gq, gk, gv)


NA_STEP_ROWS = 8


def _lat_na_kernel(q_ref, k_ref, v_ref, kc_ref, vc_ref, tb_ref, o_ref):
    q = q_ref[...]
    groups = []
    for rr in range(NA_STEP_ROWS):
        r = pl.program_id(1) * NA_STEP_ROWS + rr
        r0 = jnp.clip(r - NA_ROWS // 2, 0, GRID_ROWS - NA_ROWS)
        band = pl.ds(pl.multiple_of(r0 * GRID_W, GRID_W), N_BAND)
        dr0 = r0 - r + NA_ROWS - 1
        for c, qc in enumerate(_cols(q[rr * GRID_W:(rr + 1) * GRID_W])):
            cols = slice(c * LANES, (c + 1) * LANES)
            k2 = jnp.concatenate([k_ref[band, cols], kc_ref[:, cols].astype(BF16)], axis=0)
            v2 = jnp.concatenate([v_ref[band, cols], vc_ref[:, cols].astype(BF16)], axis=0)
            bias = jnp.concatenate(
                [jnp.concatenate([tb_ref[2 * c + half, dr0 + jj] for jj in range(0, NA_ROWS, 2)], axis=1)
                 for half in range(2)], axis=0)
            bias = jnp.concatenate([bias, jnp.zeros((2 * GRID_W, PAST_LEN), F32)], axis=1)
            groups.append(([qc], k2, v2, bias))
    outs = [o[0] for o in _attend(groups)]
    n_col = W_NA // LANES
    o_ref[...] = jnp.concatenate(
        [jnp.concatenate(outs[rr * n_col:(rr + 1) * n_col], axis=1) for rr in range(NA_STEP_ROWS)], axis=0)


def _lat_na(l, naq, nak, nav, kc, vc, tb):
    rows = NA_STEP_ROWS * GRID_W
    steps = GRID_ROWS // NA_STEP_ROWS
    seq_blk0 = N_CTX // DEC_SEQ
    seq = pl.BlockSpec((DEC_SEQ, W_NA), lambda b, r: (seq_blk0 + b, 0))
    cache = pl.BlockSpec((None, None, PAST_LEN, W_NA), lambda b, r: (b, l, 0, 0))
    return pl.pallas_call(
        _lat_na_kernel,
        grid=(DEC_BATCH, steps),
        in_specs=[pl.BlockSpec((rows, W_NA), lambda b, r: (N_CTX // rows + b * steps + r, 0)),
                  seq, seq, cache, cache, _layer_spec(l, (H_NA, N_DR - 1, GRID_W, LANES))],
        out_specs=pl.BlockSpec((rows, W_NA), lambda b, r: (b * steps + r, 0)),
        out_shape=jax.ShapeDtypeStruct((N_LAT, W_NA), BF16),
        compiler_params=_cparams(2),
        name="latent_na",
    )(naq, nak, nav, kc, vc, tb)


def _lat_gqa_kernel(q_ref, k_ref, v_ref, kc_ref, vc_ref, o_ref):
    k = jnp.concatenate([kc_ref[...], k_ref[...].astype(F32)], axis=0)
    v = jnp.concatenate([vc_ref[...], v_ref[...].astype(F32)], axis=0)
    o_ref[...] = jnp.concatenate([c for o in _attend(_gqa_groups(q_ref[...], k, v)) for c in o], axis=1)


GQA_ROWS = 256


def _lat_gqa(l, gq, gk, gv, kc, vc):
    seq_blk0 = N_CTX // DEC_SEQ
    steps = DEC_SEQ // GQA_ROWS
    seq = pl.BlockSpec((DEC_SEQ, W_GQA_KV), lambda b, i: (seq_blk0 + b, 0))
    cache = pl.BlockSpec((None, None, PAST_LEN, W_GQA_KV), lambda b, i: (b, l, 0, 0))
    return pl.pallas_call(
        _lat_gqa_kernel,
        grid=(DEC_BATCH, steps),
        in_specs=[pl.BlockSpec((GQA_ROWS, W_GQA), lambda b, i: (N_CTX // GQA_ROWS + b * steps + i, 0)),
                  seq, seq, cache, cache],
        out_specs=pl.BlockSpec((GQA_ROWS, W_GQA), lambda b, i: (b * steps + i, 0)),
        out_shape=jax.ShapeDtypeStruct((N_LAT, W_GQA), BF16),
        compiler_params=_cparams(2),
        name="latent_gqa",
    )(gq, gk, gv, kc, vc)


def _mix_ffn_kernel(xc_ref, xl_ref, mod_ref, of_ref, ob_ref, g_ref, bonus_ref, lnxw_ref, lnxb_ref,
                    onac_ref, onal_ref, ogc_ref, ogl_ref,
                    wout_ref, ln1w_ref, ln1b_ref, wfi_ref, wfo_ref, ln2w_ref, ln2b_ref, yc_ref, yl_ref):
    i = pl.program_id(0)
    row = _mod_row(i)
    mod = lambda n: mod_ref[pl.ds(row, 1), n * D_MODEL:(n + 1) * D_MODEL]
    each = lambda f, *lists: [f(*args) for args in zip(*lists)]

    def rwkv_out(rs):
        o = of_ref[rs, :] + ob_ref[rs, :]
        mu = _seg64_sum(o) * (1.0 / HEAD_DIM)
        oc = o - mu
        var = _seg64_sum(oc * oc) * (1.0 / HEAD_DIM)
        o_rwkv = (oc * lax.rsqrt(var + GN_EPS) * lnxw_ref[...] + lnxb_ref[...] + bonus_ref[rs, :]) * g_ref[rs, :]
        return jnp.concatenate([o_rwkv.astype(BF16), _pick(i, onac_ref, onal_ref, rs),
                                _pick(i, ogc_ref, ogl_ref, rs)], axis=1)

    mix = [_dg(rwkv_out(rs), wout_ref[...]) for rs in SUB_ROWS]
    x1 = [_layer_norm(DEEPNORM_ALPHA * _pick(i, xc_ref, xl_ref, rs) + mod(2) * m, ln1w_ref[...], ln1b_ref[...])
          for rs, m in zip(SUB_ROWS, mix)]
    x_in = [(x * (1.0 + mod(4)) + mod(3)).astype(BF16) for x in x1]
    ffn = [jnp.zeros_like(x) for x in x1]
    for lo in range(0, D_FF, FF_CHUNK):
        gate = [_dg(x, wfi_ref[:, lo:lo + FF_CHUNK]) for x in x_in]
        up = [_dg(x, wfi_ref[:, D_FF + lo:D_FF + lo + FF_CHUNK]) for x in x_in]
        act = each(lambda gt, u: (gt * _sigmoid(gt) * u).astype(BF16), gate, up)
        ffn = each(lambda f, a: f + _dg(a, wfo_ref[lo:lo + FF_CHUNK, :]), ffn, act)
    y = each(lambda x, f: _layer_norm(DEEPNORM_ALPHA * x + mod(5) * f, ln2w_ref[...], ln2b_ref[...]), x1, ffn)

    @pl.when(i < NCB_D)
    def _():
        for rs, yy in zip(SUB_ROWS, y):
            yc_ref[rs, :] = yy

    @pl.when(i >= NCB_D)
    def _():
        for rs, yy in zip(SUB_ROWS, y):
            yl_ref[rs, :] = yy


def _mix_ffn(l, x_ctx, x_lat, mod_all, o_fwd, o_bwd, g, bonus, lnx_w, lnx_b, o_na_ctx, o_na_lat, o_g_ctx, o_g_lat,
             w_out_bf, ln1_w, ln1_b, w_ffn_in_bf, w_ffn_out_bf, ln2_w, ln2_b):
    tok = lambda w: pl.BlockSpec((TD, w), lambda i: (i, 0))
    once = lambda *tail: _layer_spec(l, tail, single=True)
    return pl.pallas_call(
        _mix_ffn_kernel,
        grid=(NBLK_D,),
        in_specs=[
            _ctx_spec(D_MODEL), _lat_spec(D_MODEL), once(MOD_ROWS, 6 * D_MODEL),
            tok(W_RWKV), tok(W_RWKV),
            tok(W_RWKV), tok(W_RWKV), once(1, W_RWKV), once(1, W_RWKV),
            _ctx_spec(W_NA), _lat_spec(W_NA), _ctx_spec(W_GQA), _lat_spec(W_GQA),
            once(D_MODEL, D_MODEL), once(1, D_MODEL), once(1, D_MODEL),
            once(D_MODEL, 2 * D_FF), once(D_FF, D_MODEL), once(1, D_MODEL), once(1, D_MODEL),
        ],
        out_specs=[_ctx_spec(D_MODEL), _lat_spec(D_MODEL)],
        out_shape=[jax.ShapeDtypeStruct((N_CTX, D_MODEL), F32), jax.ShapeDtypeStruct((N_LAT, D_MODEL), F32)],
        compiler_params=_cparams(1, VMEM_LIMIT_FFN),
        name="mix_ffn",
    )(x_ctx, x_lat, mod_all, o_fwd, o_bwd, g, bonus, lnx_w, lnx_b, o_na_ctx, o_na_lat, o_g_ctx, o_g_lat,
      w_out_bf, ln1_w, ln1_b, w_ffn_in_bf, w_ffn_out_bf, ln2_w, ln2_b)


def _rope_tables():
    t = jnp.arange(DEC_SEQ)
    inv = ROPE_BASE ** (-jnp.arange(ROPE_FREQ, dtype=F32) / ROPE_FREQ)
    ang_r = (t // GRID_W).astype(F32)[:, None] * inv
    ang_c = (t % GRID_W).astype(F32)[:, None] * inv
    cos = jnp.concatenate([jnp.cos(ang_r)] * 2 + [jnp.cos(ang_c)] * 2, axis=1)
    sin = jnp.concatenate([-jnp.sin(ang_r), jnp.sin(ang_r), -jnp.sin(ang_c), jnp.sin(ang_c)], axis=1)
    cos = jnp.concatenate([jnp.ones((TD, HEAD_DIM), F32), cos], axis=0)
    sin = jnp.concatenate([jnp.zeros((TD, HEAD_DIM), F32), sin], axis=0)
    return jnp.tile(cos, (1, LANES // HEAD_DIM)), jnp.tile(sin, (1, LANES // HEAD_DIM))


def _block_diag2(w):
    z = jnp.zeros_like(w[:, 0])
    return jnp.concatenate([jnp.concatenate([w[:, 0], z], axis=2), jnp.concatenate([z, w[:, 1]], axis=2)], axis=1)


def _hi_lo(w):
    return jnp.stack(_split(w), axis=1)


def _pair_states(s):
    lead = s.shape[:-3]
    s = s.reshape(lead + (N_PAIR, 2, HEAD_DIM, HEAD_DIM))
    z = jnp.zeros_like(s[..., 0, :, :])
    top = jnp.concatenate([s[..., 0, :, :], z], axis=-1)
    bot = jnp.concatenate([z, s[..., 1, :, :]], axis=-1)
    return jnp.concatenate([top, bot], axis=-2)


def _unpair_states(s):
    lead = s.shape[:-3]
    a = s[..., :HEAD_DIM, :HEAD_DIM]
    b = s[..., HEAD_DIM:, HEAD_DIM:]
    return jnp.stack([a, b], axis=-3).reshape(lead + (H_RWKV, HEAD_DIM, HEAD_DIM))


def kernel(x_prompt, x_sample, state_rwkv, cache_na_k, cache_na_v, cache_gqa_k, cache_gqa_v, c, c_ctx,
           w_mod, b_mod, w_in, rwkv_conv, rwkv_w0, rwkv_w2, rwkv_a0, rwkv_a2, rwkv_g2, rwkv_k_k, rwkv_k_a,
           rwkv_r_k, rwkv_lnx_w, rwkv_lnx_b, na_rpb, gqa_q_norm, gqa_k_norm, w_out, ln1_w, ln1_b,
           w_ffn_in, w_ffn_out, ln2_w, ln2_b):
    x_ctx, x_lat = x_prompt.reshape(N_CTX, D_MODEL), x_sample.reshape(N_LAT, D_MODEL)
    cc = jnp.concatenate([c_ctx[None], c, jnp.zeros((MOD_ROWS - 1 - DEC_BATCH, D_MODEL), F32)], axis=0)
    mod_all = _modulation(cc, w_mod, b_mod)
    tb_all = _bias_tables(na_rpb)
    cos_tab, sin_tab = _rope_tables()
    rows = lambda a: a.reshape(DEPTH, 1, -1)
    w_in_bf, w_out_bf = w_in.astype(BF16), w_out.astype(BF16)
    w_ffn_in_bf, w_ffn_out_bf = w_ffn_in.astype(BF16), w_ffn_out.astype(BF16)
    qn = jnp.tile(rows(gqa_q_norm), (1, 1, H_GQA))
    kn = jnp.tile(rows(gqa_k_norm), (1, 1, H_GQA_KV))
    rwkv_params = (rwkv_conv, rows(rwkv_w0), _hi_lo(_block_diag2(rwkv_w2)), rows(rwkv_a0),
                   _hi_lo(_block_diag2(rwkv_a2)), _hi_lo(rwkv_g2), rows(rwkv_k_k), rows(rwkv_k_a), rows(rwkv_r_k))
    s0_lat = _pair_states(state_rwkv)
    kc_na = cache_na_k.reshape(DEC_BATCH, DEPTH, PAST_LEN, W_NA)
    vc_na = cache_na_v.reshape(DEC_BATCH, DEPTH, PAST_LEN, W_NA)
    kc_g = cache_gqa_k.reshape(DEC_BATCH, DEPTH, PAST_LEN, W_GQA_KV)
    vc_g = cache_gqa_v.reshape(DEC_BATCH, DEPTH, PAST_LEN, W_GQA_KV)
    caches = [jnp.zeros((BATCH, DEPTH, SEQ, w), F32) for w in (W_NA, W_NA, W_GQA_KV, W_GQA_KV)]
    s_fin = [jnp.zeros((BATCH, DEPTH, N_PAIR, LANES, LANES), F32) for _ in range(2)]

    for l in range(DEPTH):
        naq, nak, nav, gq, gk, gv, r, kap, v, lw, ah, kd, g, bonus, *caches = _inproj(
            l, x_ctx, x_lat, mod_all, w_in_bf, qn, kn, cos_tab, sin_tab, rwkv_params, caches)
        o_fwd, o_bwd, *s_fin = _rwkv_scan(l, r, kap, v, lw, ah, kd, s0_lat, s_fin)
        o_na_ctx, o_g_ctx = _ctx_attention(naq, nak, nav, gq, gk, gv)
        o_na_lat = _lat_na(l, naq, nak, nav, kc_na, vc_na, tb_all)
        o_g_lat = _lat_gqa(l, gq, gk, gv, kc_g, vc_g)
        x_ctx, x_lat = _mix_ffn(
            l, x_ctx, x_lat, mod_all, o_fwd, o_bwd, g, bonus, rows(rwkv_lnx_w), rows(rwkv_lnx_b),
            o_na_ctx, o_na_lat, o_g_ctx, o_g_lat, w_out_bf, rows(ln1_w), rows(ln1_b),
            w_ffn_in_bf, w_ffn_out_bf, rows(ln2_w), rows(ln2_b))
    y_prompt = x_ctx.reshape(BATCH, SEQ, D_MODEL)
    y_sample = x_lat.reshape(DEC_BATCH, DEC_SEQ, D_MODEL)
    new_state = jnp.stack([_unpair_states(s) for s in s_fin], axis=2)
    return (y_prompt, y_sample, new_state,
            caches[0].reshape(BATCH, DEPTH, SEQ, H_NA, HEAD_DIM), caches[1].reshape(BATCH, DEPTH, SEQ, H_NA, HEAD_DIM),
            caches[2].reshape(BATCH, DEPTH, SEQ, H_GQA_KV, HEAD_DIM),
            caches[3].reshape(BATCH, DEPTH, SEQ, H_GQA_KV, HEAD_DIM))
```

```python
import jax
import jax.numpy as jnp
from jax import lax
from jax.experimental import pallas as pl
from jax.experimental.pallas import tpu as pltpu

F32 = jnp.float32
BF16 = jnp.bfloat16

D_MODEL = 1024
BATCH = 16
SEQ = 256
DEPTH = 4
DEC_BATCH = 2
DEC_SEQ = 1024
PAST_LEN = 512
GRID_W = 64
GRID_ROWS = DEC_SEQ // GRID_W
HEAD_DIM = 64
H_RWKV = 4
H_NA = 4
H_GQA = 8
H_GQA_KV = 2
W_RWKV = H_RWKV * HEAD_DIM
W_NA = H_NA * HEAD_DIM
W_GQA = H_GQA * HEAD_DIM
W_GQA_KV = H_GQA_KV * HEAD_DIM
LORA_W = 64
LORA_A = 64
LORA_G = 128
RWKV_IN = 3 * W_RWKV + 2 * LORA_W + 2 * LORA_A + LORA_G
NA_IN = 3 * W_NA
GQA_IN = W_GQA + 2 * W_GQA_KV
D_IN = RWKV_IN + NA_IN + GQA_IN
NA_ROWS = 8
NA_COLS = 16
ROPE_BASE = 10000.0
ROPE_FREQ = HEAD_DIM // 4
D_FF = ((8 * D_MODEL + 3 * 256 - 1) // (3 * 256)) * 256
DEEPNORM_ALPHA = (2 * DEPTH) ** 0.25
LN_EPS = 1e-5
RMS_EPS = 1e-6
GN_EPS = 64e-5
NEG_INF = -1e30
ATTN_SCALE = HEAD_DIM ** -0.5

LANES = 128
TM = 256
N_CTX = BATCH * SEQ
N_LAT = DEC_BATCH * DEC_SEQ
N_TOK = N_CTX + N_LAT
CHUNK = 64
TS = 512
NCS = TS // CHUNK
NBLK_S = N_TOK // TS
NCB_S = N_CTX // TS
LAT_BLKS_S = DEC_SEQ // TS
SEQS_S = TS // SEQ
SEQ_CHUNKS = SEQ // CHUNK
TD = 512
SUB_ROWS = [slice(h * SEQ, (h + 1) * SEQ) for h in range(TD // SEQ)]
IN_PIECES = SUB_ROWS
NBLK_D = N_TOK // TD
NCB_D = N_CTX // TD
LAT_BLKS_D = DEC_SEQ // TD
FF_CHUNK = D_FF
N_PAIR = W_RWKV // LANES
MOD_ROWS = 8
VMEM_LIMIT = 48 * 1024 * 1024
VMEM_LIMIT_FFN = 56 * 1024 * 1024
N_DR = 2 * NA_ROWS - 1
N_DC = 2 * NA_COLS - 1
N_BAND = NA_ROWS * GRID_W


def _cparams(n_grid, vmem_limit=VMEM_LIMIT):
    return pltpu.CompilerParams(dimension_semantics=("arbitrary",) * n_grid, vmem_limit_bytes=vmem_limit)


def _iota(shape, dim):
    return lax.broadcasted_iota(jnp.int32, shape, dim)


NN = (((1,), (0,)), ((), ()))
NT = (((1,), (1,)), ((), ()))
TN = (((0,), (0,)), ((), ()))


def _dg(a, b, dims=NN, precision=None):
    return lax.dot_general(a, b, dims, preferred_element_type=F32, precision=precision)


def _split(x):
    hi = x.astype(BF16)
    return hi, (x - hi.astype(F32)).astype(BF16)


def _dgs(a_s, b_s, dims=NN):
    if len(a_s) == 1:
        return _dg(a_s[0], b_s[0], dims)
    (ah, al), (bh, bl) = a_s, b_s
    ca, cb = dims[0][0][0], dims[0][1][0]
    return _dg(jnp.concatenate([ah, ah, al], axis=ca), jnp.concatenate([bh, bl, bh], axis=cb), dims)


def _sigmoid(x):
    return 1.0 / (1.0 + jnp.exp(-x))


def _softplus(x):
    return jnp.maximum(x, 0.0) + jnp.log(1.0 + jnp.exp(-jnp.abs(x)))


def _seg64_sum(x):
    rows, width = x.shape
    lo = _iota((rows, LANES), 1) < HEAD_DIM
    outs = []
    for c in range(width // LANES):
        blk = x[:, c * LANES:(c + 1) * LANES]
        s_lo = jnp.sum(jnp.where(lo, blk, 0.0), axis=-1, keepdims=True)
        s_hi = jnp.sum(jnp.where(lo, 0.0, blk), axis=-1, keepdims=True)
        outs.append(jnp.where(lo, s_lo, s_hi))
    return outs[0] if len(outs) == 1 else jnp.concatenate(outs, axis=1)


def _layer_norm(x, w, b):
    mu = jnp.mean(x, axis=-1, keepdims=True)
    xc = x - mu
    var = jnp.mean(xc * xc, axis=-1, keepdims=True)
    return xc * lax.rsqrt(var + LN_EPS) * w + b


def _mod_row(i):
    return jnp.where(i < NCB_D, 0, 1 + (i - NCB_D) // LAT_BLKS_D)


def _layer_spec(l, tail, single=False):
    idx = lambda *g: (l,) + (0,) * len(tail)
    if single:
        return pl.BlockSpec((None,) + tuple(tail), idx, pipeline_mode=pl.Buffered(1))
    return pl.BlockSpec((None,) + tuple(tail), idx)


ANY_SPEC = pl.BlockSpec(memory_space=pl.ANY)


def _ctx_spec(w):
    return pl.BlockSpec((TD, w), lambda i: (jnp.minimum(i, NCB_D - 1), 0))


def _lat_spec(w):
    return pl.BlockSpec((TD, w), lambda i: (jnp.maximum(i - NCB_D, 0), 0))


def _pick(i, ctx_ref, lat_ref, rows=slice(None)):
    return jnp.where(i < NCB_D, ctx_ref[rows, :], lat_ref[rows, :])


N_COND = 1 + DEC_BATCH


def _mod_kernel(c_ref, w_ref, b_ref, o_ref, sb_scr):
    @pl.when((pl.program_id(0) == 0) & (pl.program_id(1) == 0))
    def _():
        c = c_ref[...]
        s = c * _sigmoid(c)
        for r in range(N_COND):
            sb_scr[r] = jnp.broadcast_to(s[r:r + 1, :], (LANES, D_MODEL)).T

    def body(kc, accs):
        rows = pl.ds(pl.multiple_of(kc * 8, 8), 8)
        wk = w_ref[rows, :]
        return tuple(a + wk * jnp.concatenate([sb_scr[r, rows, :]] * (D_MODEL // LANES), axis=1)
                     for r, a in enumerate(accs))

    accs = lax.fori_loop(0, D_MODEL // 8, body, tuple(jnp.zeros((8, D_MODEL), F32) for _ in range(N_COND)),
                         unroll=8)
    rowid = _iota((MOD_ROWS, D_MODEL), 0)
    out = jnp.zeros((MOD_ROWS, D_MODEL), F32)
    for r, a in enumerate(accs):
        out = jnp.where(rowid == r, jnp.sum(a, axis=0, keepdims=True), out)
    o_ref[...] = out + b_ref[...]


def _modulation(cc, w_mod, b_mod):
    return pl.pallas_call(
        _mod_kernel,
        grid=(DEPTH, 6),
        in_specs=[
            pl.BlockSpec((MOD_ROWS, D_MODEL), lambda l, j: (0, 0)),
            pl.BlockSpec((None, D_MODEL, D_MODEL), lambda l, j: (l, 0, j)),
            pl.BlockSpec((None, 1, D_MODEL), lambda l, j: (l, 0, j)),
        ],
        out_specs=pl.BlockSpec((None, MOD_ROWS, D_MODEL), lambda l, j: (l, 0, j)),
        out_shape=jax.ShapeDtypeStruct((DEPTH, MOD_ROWS, 6 * D_MODEL), F32),
        scratch_shapes=[pltpu.VMEM((N_COND, D_MODEL, LANES), F32)],
        compiler_params=_cparams(2),
        name="modulation",
    )(cc, w_mod, b_mod.reshape(DEPTH, 1, 6 * D_MODEL))


def _bias_kernel(rpb_ref, o_ref):
    q = _iota((GRID_W, LANES), 0)
    x = _iota((GRID_W, LANES), 1)
    c = x % GRID_W
    right = x >= GRID_W
    dc = jnp.clip(c - q, 1 - NA_COLS, NA_COLS - 1) + NA_COLS - 1
    c0 = jnp.clip(q - NA_COLS // 2, 0, GRID_W - NA_COLS)
    in_win = (c >= c0) & (c < c0 + NA_COLS)

    def body(t, carry):
        h = t // (N_DR - 1)
        dr = t % (N_DR - 1)
        rows = [jnp.broadcast_to(rpb_ref[pl.ds(h * N_DR + dr + k, 1), :], (GRID_W, LANES)) for k in range(2)]
        lo, hi = (jnp.take_along_axis(r, dc, axis=1) for r in rows)
        o_ref[h, dr] = jnp.where(in_win, jnp.where(right, hi, lo), NEG_INF)
        return carry

    lax.fori_loop(0, H_NA * (N_DR - 1), body, 0, unroll=N_DR - 1)


RPB_ROWS = -(-H_NA * N_DR // 8) * 8


def _bias_tables(na_rpb):
    rpb = jnp.pad(na_rpb.reshape(DEPTH, H_NA * N_DR, N_DC), ((0, 0), (0, RPB_ROWS - H_NA * N_DR), (0, LANES - N_DC)))
    return pl.pallas_call(
        _bias_kernel,
        grid=(DEPTH,),
        in_specs=[pl.BlockSpec((None, RPB_ROWS, LANES), lambda l: (l, 0, 0))],
        out_specs=pl.BlockSpec((None, H_NA, N_DR - 1, GRID_W, LANES), lambda l: (l, 0, 0, 0, 0)),
        out_shape=jax.ShapeDtypeStruct((DEPTH, H_NA, N_DR - 1, GRID_W, LANES), F32),
        compiler_params=_cparams(1),
        name="na_bias_tables",
    )(rpb)


def _rope(x, cos, sin):
    k = x.shape[1] // LANES
    cosf = cos if k == 1 else jnp.concatenate([cos] * k, axis=1)
    sinf = sin if k == 1 else jnp.concatenate([sin] * k, axis=1)
    first = (_iota(x.shape, 1) % (2 * ROPE_FREQ)) < ROPE_FREQ
    partner = jnp.where(first, pltpu.roll(x, x.shape[1] - ROPE_FREQ, axis=1), pltpu.roll(x, ROPE_FREQ, axis=1))
    return x * cosf + partner * sinf


def _rwkv_features(x, prev_row, next_row, conv_ref, w0_ref, w2_ref, a0_ref, a2_ref, g2_ref, kk_ref, ka_ref, rk_ref):
    n = x.shape[0]
    rows = _iota(x.shape, 0)
    x_prev = jnp.where(rows == 0, prev_row, pltpu.roll(x, 1, axis=0))
    x_next = jnp.where(rows == n - 1, next_row, pltpu.roll(x, n - 1, axis=0))
    f = x_prev * conv_ref[0:1, :] + x * conv_ref[1:2, :] + x_next * conv_ref[2:3, :]
    o1, o2, o3 = W_RWKV, 2 * W_RWKV, 3 * W_RWKV
    o4 = o3 + 2 * LORA_W
    o5 = o4 + 2 * LORA_A
    r, k, v = f[:, :o1], f[:, o1:o2], f[:, o2:o3]
    wd, ad, gd = f[:, o3:o4], f[:, o4:o5], f[:, o5:]
    def lora(y, w_ref):
        y_bf = y.astype(BF16)
        return _dg(jnp.concatenate([y_bf, y_bf], axis=1), jnp.concatenate([w_ref[0], w_ref[1]], axis=0))
    log_w = -_softplus(-(w0_ref[...] + lora(jnp.tanh(wd), w2_ref))) - 0.5
    a = _sigmoid(a0_ref[...] + lora(ad, a2_ref))
    g = lora(_sigmoid(gd), g2_ref)
    kk = k * kk_ref[...]
    kap = kk / jnp.maximum(jnp.sqrt(_seg64_sum(kk * kk)), 1e-12)
    per_dir = []
    kd_sum = jnp.zeros_like(k)
    for d in range(2):
        a_d = a[:, d * W_RWKV:(d + 1) * W_RWKV]
        kd = k * (1.0 + (a_d - 1.0) * ka_ref[...])
        kd_sum = kd_sum + kd
        per_dir.append((-jnp.exp(log_w[:, d * W_RWKV:(d + 1) * W_RWKV]), a_d * kap, kd))
    bonus = _seg64_sum(r * kd_sum * rk_ref[...]) * v
    return r, kap, v, per_dir, g, bonus


HALO = 8


def _inproj_kernel_pair(xc_ref, xl_ref, *rest):
    _inproj_body(lambda i, rs: _pick(i, xc_ref, xl_ref, rs), *rest)


def _inproj_kernel_one(x_ref, *rest):
    _inproj_body(lambda i, rs: x_ref[rs, :], *rest)


def _inproj_body(read_x, xp_ref, xn_ref, mod_ref, w_ref, qn_ref, kn_ref, cos_ref, sin_ref,
                 conv_ref, w0_ref, w2_ref, a0_ref, a2_ref, g2_ref, kk_ref, ka_ref, rk_ref,
                 c0_ref, c1_ref, c2_ref, c3_ref,
                 naq_ref, nak_ref, nav_ref, gq_ref, gk_ref, gv_ref,
                 r_ref, kap_ref, v_ref, lw_ref, ah_ref, kd_ref, g_ref, bonus_ref,
                 cnak_ref, cnav_ref, cgk_ref, cgv_ref):
    del c0_ref, c1_ref, c2_ref, c3_ref
    i = pl.program_id(0)
    row = _mod_row(i)
    shift1 = mod_ref[pl.ds(row, 1), 0:D_MODEL]
    scale1 = mod_ref[pl.ds(row, 1), D_MODEL:2 * D_MODEL]
    modulate = lambda x: (x * (1.0 + scale1) + shift1).astype(BF16)
    lat = i >= NCB_D
    pos = (i - NCB_D) % LAT_BLKS_D
    in_seq = jnp.where(lat, 1.0, 0.0)
    has_prev = jnp.where(lat & (pos != 0), 1.0, 0.0)
    has_next = jnp.where(lat & (pos != LAT_BLKS_D - 1), 1.0, 0.0)
    xm = [modulate(read_x(i, rs)) for rs in IN_PIECES]
    xm[0] = jnp.concatenate([xm[0], modulate(xp_ref[...]), modulate(xn_ref[...])], axis=0)
    projs = [_dg(x, w_ref[...]) for x in xm]
    n0 = IN_PIECES[0].stop
    halo = projs[0][n0:, :RWKV_IN]
    projs[0] = projs[0][:n0]
    feats = [proj[:, :RWKV_IN] for proj in projs]
    o_na = RWKV_IN
    o_g = RWKV_IN + NA_IN
    new_kv = []
    for h, (rs, proj) in enumerate(zip(IN_PIECES, projs)):
        if h == 0:
            prev_row = halo[HALO - 1:HALO] * has_prev
        else:
            prev_row = feats[h - 1][-1:] * (in_seq if rs.start % SEQ == 0 else 1.0)
        if h == len(IN_PIECES) - 1:
            next_row = halo[HALO:HALO + 1] * has_next
        else:
            next_row = feats[h + 1][0:1] * (in_seq if rs.stop % SEQ == 0 else 1.0)
        r, kap, v, per_dir, g, bonus = _rwkv_features(
            feats[h], prev_row, next_row, conv_ref, w0_ref, w2_ref, a0_ref, a2_ref, g2_ref, kk_ref, ka_ref, rk_ref)
        r_ref[rs, :] = r.astype(BF16)
        kap_ref[rs, :] = kap.astype(BF16)
        v_ref[rs, :] = v.astype(BF16)
        g_ref[rs, :] = g
        bonus_ref[rs, :] = bonus
        for d, (lw, ah, kd) in enumerate(per_dir):
            lw_ref[d, rs, :] = lw
            ah_ref[d, rs, :] = ah.astype(BF16)
            kd_ref[d, rs, :] = kd.astype(BF16)
        naq_ref[rs, :] = (proj[:, o_na:o_na + W_NA] * ATTN_SCALE).astype(BF16)
        nak = proj[:, o_na + W_NA:o_na + 2 * W_NA]
        nav = proj[:, o_na + 2 * W_NA:o_na + 3 * W_NA]
        q = proj[:, o_g:o_g + W_GQA]
        k = proj[:, o_g + W_GQA:o_g + W_GQA + W_GQA_KV]
        gv = proj[:, o_g + W_GQA + W_GQA_KV:]
        cos = cos_ref[rs, :]
        sin = sin_ref[rs, :]
        q = q * lax.rsqrt(_seg64_sum(q * q) * (1.0 / HEAD_DIM) + RMS_EPS) * qn_ref[...]
        k = k * lax.rsqrt(_seg64_sum(k * k) * (1.0 / HEAD_DIM) + RMS_EPS) * kn_ref[...]
        gk = _rope(k, cos, sin)
        gq_ref[rs, :] = (_rope(q, cos, sin) * ATTN_SCALE).astype(BF16)
        nak_ref[rs, :] = nak.astype(BF16)
        nav_ref[rs, :] = nav.astype(BF16)
        gk_ref[rs, :] = gk.astype(BF16)
        gv_ref[rs, :] = gv.astype(BF16)
        new_kv.append((nak, nav, gk, gv))

    @pl.when(i < NCB_D)
    def _():
        for rs, vals in zip(IN_PIECES, new_kv):
            for ref, val in zip((cnak_ref, cnav_ref, cgk_ref, cgv_ref), vals):
                ref[rs.start // SEQ, rs.start % SEQ:rs.start % SEQ + rs.stop - rs.start, :] = val


def _inproj(l, x, mod_all, w_in_bf, qn, kn, cos_tab, sin_tab, rwkv_params, caches):
    tab_idx = lambda i: (jnp.where(i < NCB_D, 0, 1 + (i - NCB_D) % LAT_BLKS_D), 0)
    lat_blk = lambda i: jnp.maximum(i - NCB_D, 0)
    halo_blocks = TD // HALO
    widths = (W_NA, W_NA, W_NA, W_GQA, W_GQA_KV, W_GQA_KV, W_RWKV, W_RWKV, W_RWKV)
    cache_w = (W_NA, W_NA, W_GQA_KV, W_GQA_KV)
    tok = lambda w: pl.BlockSpec((TD, w), lambda i: (i, 0))
    tok2 = pl.BlockSpec((2, TD, W_RWKV), lambda i: (0, i, 0))
    cache_spec = lambda w: pl.BlockSpec((TD // SEQ, None, SEQ, w), lambda i: (jnp.minimum(i, NCB_D - 1), l, 0, 0))
    rwkv_tails = ((3, RWKV_IN), (1, 2 * W_RWKV), (2, 2 * LORA_W, 2 * W_RWKV), (1, 2 * W_RWKV),
                  (2, 2 * LORA_A, 2 * W_RWKV), (2, LORA_G, W_RWKV), (1, W_RWKV), (1, W_RWKV), (1, W_RWKV))
    pair = isinstance(x, tuple)
    x_lat, lat0 = (x[1], 0) if pair else (x, N_CTX // HALO)
    x_specs = [_ctx_spec(D_MODEL), _lat_spec(D_MODEL)] if pair else [tok(D_MODEL)]
    in_specs = x_specs + [
        pl.BlockSpec((HALO, D_MODEL), lambda i: (lat0 + jnp.maximum(lat_blk(i) * halo_blocks - 1, 0), 0)),
        pl.BlockSpec((HALO, D_MODEL),
                     lambda i: (lat0 + jnp.minimum((lat_blk(i) + 1) * halo_blocks, N_LAT // HALO - 1), 0)),
        _layer_spec(l, (MOD_ROWS, 6 * D_MODEL)),
        _layer_spec(l, (D_MODEL, D_IN)),
        _layer_spec(l, (1, W_GQA)),
        _layer_spec(l, (1, W_GQA_KV)),
        pl.BlockSpec((TD, LANES), tab_idx),
        pl.BlockSpec((TD, LANES), tab_idx),
    ] + [_layer_spec(l, t) for t in rwkv_tails]
    n_in = len(in_specs)
    out_specs = [tok(w) for w in widths] + [tok2] * 3 + [tok(W_RWKV)] * 2
    out_shape = ([jax.ShapeDtypeStruct((N_TOK, w), BF16) for w in widths]
                 + [jax.ShapeDtypeStruct((2, N_TOK, W_RWKV), dt) for dt in (F32, BF16, BF16)]
                 + [jax.ShapeDtypeStruct((N_TOK, W_RWKV), F32)] * 2)
    return pl.pallas_call(
        _inproj_kernel_pair if pair else _inproj_kernel_one,
        grid=(NBLK_D,),
        in_specs=in_specs + [ANY_SPEC] * 4,
        out_specs=out_specs + [cache_spec(w) for w in cache_w],
        out_shape=out_shape + [jax.ShapeDtypeStruct((BATCH, DEPTH, SEQ, w), F32) for w in cache_w],
        input_output_aliases={n_in + j: len(out_specs) + j for j in range(4)},
        compiler_params=_cparams(1),
        name="inproj",
    )(*(x if pair else (x,)), x_lat, x_lat, mod_all, w_in_bf, qn, kn, cos_tab, sin_tab, *rwkv_params, *caches)


def _scan_kernel(r0_ref, kap0_ref, v0_ref, r1_ref, kap1_ref, v1_ref,
                 lw0_ref, ah0_ref, kd0_ref, lw1_ref, ah1_ref, kd1_ref, s00_ref, s01_ref, sf0_in_ref, sf1_in_ref,
                 o0_ref, o1_ref, sf0_ref, sf1_ref,
                 s_scr, sdone_scr, rt_scr, kt_scr, kdt_scr, at_scr, cum_scr, m_scr, n_scr, q_scr, oo_scr):
    del sf0_in_ref, sf1_in_ref
    j = pl.program_id(0)
    par = j % 2
    jb = jnp.maximum(j - 1, 0)
    upd_blk = (jb, NBLK_S - 1 - jb)
    is_ctx = tuple(b < NCB_S for b in upd_blk)
    first_pos = (0, LAT_BLKS_S - 1)
    r_refs, kap_refs, v_refs = (r0_ref, r1_ref), (kap0_ref, kap1_ref), (v0_ref, v1_ref)
    lw_refs, ah_refs, kd_refs = (lw0_ref, lw1_ref), (ah0_ref, ah1_ref), (kd0_ref, kd1_ref)
    s0_refs, o_refs, sf_refs = (s00_ref, s01_ref), (o0_ref, o1_ref), (sf0_ref, sf1_ref)

    @pl.when(j == 0)
    def _():
        for ref in (s_scr, sdone_scr, m_scr, n_scr, q_scr, oo_scr):
            ref[...] = jnp.zeros_like(ref)

    for dn in range(2):
        blk = upd_blk[dn]

        @pl.when((j > 0) & is_ctx[dn])
        def _():
            s_scr[dn] = jnp.zeros(s_scr.shape[1:], F32)

        @pl.when((j > 0) & (blk >= NCB_S) & ((blk - NCB_S) % LAT_BLKS_S == first_pos[dn]))
        def _():
            s_scr[dn] = s0_refs[dn][...]

    def seq_slot(dn, n):
        return n if dn == 0 else SEQS_S - 1 - n

    def update(c):
        for dn in range(2):
            ce = c if dn == 0 else NCS - 1 - c
            rows = slice(ce * CHUNK, (ce + 1) * CHUNK)
            for p in range(N_PAIR):
                s = s_scr[dn, p]
                if c > 0 and c % SEQ_CHUNKS == 0:
                    sdone_scr[dn, seq_slot(dn, c // SEQ_CHUNKS - 1), p] = s
                    s = jnp.where(is_ctx[dn], 0.0, s)
                s2 = jnp.concatenate(_split(s), axis=1)
                q = q_scr[1 - par, dn, ce, p]
                m = m_scr[1 - par, dn, ce, p]
                o_refs[dn][rows, p * LANES:(p + 1) * LANES] = (
                    _dg(jnp.concatenate([q, q], axis=1), s2, NT) + oo_scr[1 - par, dn, ce, p])
                s_scr[dn, p] = _dg(s2, jnp.concatenate([m, m], axis=0)) + n_scr[1 - par, dn, ce, p]

    pending = list(range(NCS))

    def next_update():
        if pending:
            update(pending.pop(0))

    next_update()
    rr = _iota((TM, TM), 0)
    cc = _iota((TM, TM), 1)
    same_chunk = (rr // CHUNK) == (cc // CHUNK)
    for dn in range(2):
        order = (rr >= cc) if dn == 0 else (rr <= cc)
        tri = jnp.where(same_chunk & order, 1.0, 0.0).astype(BF16)
        for h in range(TS // TM):
            rs = slice(h * TM, (h + 1) * TM)
            lw = lw_refs[dn][rs, :]
            lw_hi = lw.astype(BF16)
            lw_r = lw - lw_hi.astype(F32)
            lw_mid = lw_r.astype(BF16)
            lw_lo = (lw_r - lw_mid.astype(F32)).astype(BF16)
            cum = _dg(jnp.concatenate([tri, tri, tri], axis=1), jnp.concatenate([lw_hi, lw_mid, lw_lo], axis=0))
            e_neg = jnp.exp(-cum)
            rt_scr[dn, rs, :] = r_refs[dn][rs, :] * jnp.exp(cum)
            kt_scr[dn, rs, :] = kap_refs[dn][rs, :] * jnp.exp(cum - lw)
            kdt_scr[dn, rs, :] = kd_refs[dn][rs, :] * e_neg
            at_scr[dn, rs, :] = ah_refs[dn][rs, :] * e_neg
            cum_scr[dn, rs, :] = cum
        if dn == 0:
            next_update()

    row = _iota((CHUNK, LANES), 0)
    col = _iota((CHUNK, LANES), 1) % CHUNK
    left = _iota((CHUNK, LANES), 1) < CHUNK
    incl = (row >= col, row <= col)
    strict = (row > col, row < col)
    eye = jnp.where(row == col, 1.0, 0.0)
    r2 = _iota((LANES, LANES), 0)
    c2 = _iota((LANES, LANES), 1)
    bd_mask = (r2 // CHUNK) == (c2 // CHUNK)
    eye2 = jnp.where(r2 == c2, 1.0, 0.0)

    zero_bf = jnp.zeros((CHUNK, LANES), BF16)
    split = lambda x: (x.astype(BF16),)

    def bd(xs):
        return tuple(jnp.concatenate([jnp.where(left, y, zero_bf), jnp.where(left, zero_bf, y)], axis=0)
                     for y in xs)

    def cat(xs, ys, axis):
        return tuple(jnp.concatenate([x, y], axis=axis) for x, y in zip(xs, ys))

    def mm(a_list, b_list):
        return [_dgs(a, bd(b)) for a, b in zip(a_list, b_list)]

    units = [(dn, c, p) for c in range(NCS) for dn in range(2) for p in range(N_PAIR)]
    each = lambda f, *lists: [f(*args) for args in zip(*lists)]

    def load(ref):
        return [ref[dn, c * CHUNK:(c + 1) * CHUNK, p * LANES:(p + 1) * LANES] for dn, c, p in units]

    rt = load(rt_scr)
    v = [v_refs[dn][c * CHUNK:(c + 1) * CHUNK, p * LANES:(p + 1) * LANES] for dn, c, p in units]
    kt_s, rt_s, kdt_s, at_s, v_s = (each(split, x) for x in (load(kt_scr), rt, load(kdt_scr), load(at_scr), v))
    gam = [jnp.exp(cum_scr[dn, (c + 1) * CHUNK - 1:(c + 1) * CHUNK, p * LANES:(p + 1) * LANES] if dn == 0 else
                   cum_scr[dn, c * CHUNK:c * CHUNK + 1, p * LANES:(p + 1) * LANES]) for dn, c, p in units]
    gram = each(lambda k, r, a, kd: _dgs(cat(k, r, 0), cat(bd(a), bd(kd), 0), NT), kt_s, rt_s, at_s, kdt_s)
    dirs = [u[0] for u in units]
    la = [jnp.where(strict[dn], g[0:CHUNK, 0:LANES], 0.0) for dn, g in zip(dirs, gram)]
    lk_s = [split(jnp.where(strict[dn], g[0:CHUNK, LANES:], 0.0)) for dn, g in zip(dirs, gram)]
    ra_s = [split(jnp.where(incl[dn], g[CHUNK:, 0:LANES], 0.0)) for dn, g in zip(dirs, gram)]
    rk_s = [split(jnp.where(incl[dn], g[CHUNK:, LANES:], 0.0)) for dn, g in zip(dirs, gram)]
    next_update()
    lrv = mm(each(lambda lk, rk: cat(lk, rk, 0), lk_s, rk_s), v_s)
    next_update()
    b = 8
    l8 = [jnp.where((row // b) == (col // b), x, 0.0) for x in la]
    l8_s = each(split, l8)
    l8_2 = mm(l8_s, l8_s)
    next_update()
    l8_2s = each(split, l8_2)
    l8_4 = mm(l8_2s, l8_2s)
    p1 = mm([split(eye - x) for x in l8], [split(eye + y) for y in l8_2])
    next_update()
    t = mm(each(split, p1), [split(eye + y) for y in l8_4])
    while b < CHUNK:
        next_update()
        offd = ((row // (2 * b)) == (col // (2 * b))) & ((row // b) != (col // b))
        t_s = each(split, t)
        x = mm(t_s, [split(jnp.where(offd, y, 0.0)) for y in la])
        t = each(lambda tt, z: tt - z, t, mm(each(split, x), t_s))
        b *= 2
    assert not pending, "more chunk updates than stages to place them between"
    tx = each(lambda tt, k, y: _dgs(split(tt), cat(bd(k), bd(split(y[0:CHUNK])), 1)), t, kt_s, lrv)
    khat_s = [split(y[:, 0:LANES]) for y in tx]
    w1_s = [split(y[:, LANES:]) for y in tx]
    rx = each(lambda r, k, w: _dgs(r, cat(bd(k), bd(w), 1)), ra_s, khat_s, w1_s)
    mk = each(lambda k, a: _dgs(k, a, TN), khat_s, at_s)
    nk = each(lambda vv, w, kd, a: _dgs(cat(vv, w, 0), cat(kd, tuple(-y for y in a), 0), TN),
              v_s, w1_s, kdt_s, at_s)
    for i, (dn, c, p) in enumerate(units):
        q_scr[par, dn, c, p] = (rt[i] - rx[i][:, 0:LANES]).astype(BF16)
        oo_scr[par, dn, c, p] = lrv[i][CHUNK:] - rx[i][:, LANES:]
        m_scr[par, dn, c, p] = ((eye2 - jnp.where(bd_mask, mk[i], 0.0)) * gam[i]).astype(BF16)
        n_scr[par, dn, c, p] = jnp.where(bd_mask, nk[i], 0.0) * gam[i]

    for dn in range(2):
        @pl.when((j > 0) & is_ctx[dn])
        def _():
            for n in range(SEQS_S - 1):
                sf_refs[dn][seq_slot(dn, n)] = sdone_scr[dn, seq_slot(dn, n)]
            sf_refs[dn][seq_slot(dn, SEQS_S - 1)] = s_scr[dn]


def _rwkv_scan(l, r, kap, v, lw, ah, kd, s0_lat, s_fin):
    def stage_blk(dn, j):
        jj = jnp.minimum(j, NBLK_S - 1)
        return jj if dn == 0 else NBLK_S - 1 - jj

    def update_blk(dn, j):
        jj = jnp.maximum(j - 1, 0)
        return jj if dn == 0 else NBLK_S - 1 - jj

    pair = (N_PAIR, LANES, LANES)
    tok = lambda dn: pl.BlockSpec((TS, W_RWKV), lambda j: (stage_blk(dn, j), 0))
    tok2 = lambda dn: pl.BlockSpec((None, TS, W_RWKV), lambda j: (dn, stage_blk(dn, j), 0))
    s0_spec = lambda dn: pl.BlockSpec(
        (None, None, None) + pair,
        lambda j: (jnp.maximum(update_blk(dn, j) - NCB_S, 0) // LAT_BLKS_S, l, dn, 0, 0, 0))
    sf_spec = lambda dn: pl.BlockSpec(
        (SEQS_S, None) + pair, lambda j: (jnp.minimum(update_blk(dn, j), NCB_S - 1), l, 0, 0, 0))
    o_spec = lambda dn: pl.BlockSpec((TS, W_RWKV), lambda j: (update_blk(dn, j), 0))
    dir_scr = lambda *shape, dtype=F32: pltpu.VMEM((2,) + shape, dtype)
    stage_scr = (2, NCS, N_PAIR)
    return pl.pallas_call(
        _scan_kernel,
        grid=(NBLK_S + 1,),
        in_specs=[tok(0), tok(0), tok(0), tok(1), tok(1), tok(1),
                  tok2(0), tok2(0), tok2(0), tok2(1), tok2(1), tok2(1),
                  s0_spec(0), s0_spec(1), ANY_SPEC, ANY_SPEC],
        out_specs=[o_spec(0), o_spec(1), sf_spec(0), sf_spec(1)],
        out_shape=[jax.ShapeDtypeStruct((N_TOK, W_RWKV), F32)] * 2
        + [jax.ShapeDtypeStruct((BATCH, DEPTH) + pair, F32)] * 2,
        scratch_shapes=[dir_scr(*pair), dir_scr(SEQS_S, *pair)] + [dir_scr(TS, W_RWKV)] * 5
        + [dir_scr(*stage_scr, LANES, LANES, dtype=BF16), dir_scr(*stage_scr, LANES, LANES),
           dir_scr(*stage_scr, CHUNK, LANES, dtype=BF16), dir_scr(*stage_scr, CHUNK, LANES)],
        input_output_aliases={14: 2, 15: 3},
        compiler_params=_cparams(1),
        name="rwkv_scan",
    )(r, kap, v, r, kap, v, lw, ah, kd, lw, ah, kd, s0_lat, s0_lat, s_fin[0], s_fin[1])


def _attend(groups):
    lhs = []
    for q_cols, _, _, _ in groups:
        left = _iota(q_cols[0].shape, 1) < HEAD_DIM
        parts = []
        for qc in q_cols:
            parts += [jnp.where(left, qc, 0.0), jnp.where(left, 0.0, qc)]
        lhs.append(jnp.concatenate(parts, axis=0).astype(BF16))
    s = [_dg(x, g[1], NT) for x, g in zip(lhs, groups)]
    s = [x if g[3] is None else x + g[3] for x, g in zip(s, groups)]
    p = [jnp.exp(x - jnp.max(x, axis=-1, keepdims=True)) for x in s]
    inv = [1.0 / jnp.sum(x, axis=-1, keepdims=True) for x in p]
    o = [_dg(x.astype(BF16), g[2]) * y for x, y, g in zip(p, inv, groups)]
    outs = []
    for x, (q_cols, _, _, _) in zip(o, groups):
        rows = q_cols[0].shape[0]
        left = _iota(q_cols[0].shape, 1) < HEAD_DIM
        outs.append([jnp.where(left, x[2 * j * rows:(2 * j + 1) * rows],
                               x[(2 * j + 1) * rows:(2 * j + 2) * rows]).astype(BF16) for j in range(len(q_cols))])
    return outs


def _cols(x):
    return [x[:, c * LANES:(c + 1) * LANES] for c in range(x.shape[1] // LANES)]


def _gqa_groups(q, k, v):
    k, v = k.astype(F32), v.astype(F32)
    left = _iota(k.shape, 1) < HEAD_DIM
    k_sw = pltpu.roll(k, HEAD_DIM, axis=1)
    v_sw = pltpu.roll(v, HEAD_DIM, axis=1)
    q_cols = _cols(q)
    groups = []
    for g in range(H_GQA_KV):
        k2 = jnp.where(left, k, k_sw) if g == 0 else jnp.where(left, k_sw, k)
        v2 = jnp.where(left, v, v_sw) if g == 0 else jnp.where(left, v_sw, v)
        groups.append((q_cols[2 * g:2 * g + 2], k2.astype(BF16), v2.astype(BF16), None))
    return groups


CTX_STEP_SEQS = 4


def _ctx_attn_kernel(naq_ref, nak_ref, nav_ref, gq_ref, gk_ref, gv_ref, ona_ref, og_ref):
    groups = []
    n_na = W_NA // LANES
    for n in range(CTX_STEP_SEQS):
        rs = slice(n * SEQ, (n + 1) * SEQ)
        k_cols = _cols(nak_ref[rs, :].astype(BF16))
        v_cols = _cols(nav_ref[rs, :].astype(BF16))
        groups += [([qc], kc, vc, None) for qc, kc, vc in zip(_cols(naq_ref[rs, :]), k_cols, v_cols)]
        groups += _gqa_groups(gq_ref[rs, :], gk_ref[rs, :], gv_ref[rs, :])
    outs = _attend(groups)
    per_seq = len(groups) // CTX_STEP_SEQS
    for n in range(CTX_STEP_SEQS):
        rs = slice(n * SEQ, (n + 1) * SEQ)
        o = outs[n * per_seq:(n + 1) * per_seq]
        ona_ref[rs, :] = jnp.concatenate([x[0] for x in o[:n_na]], axis=1)
        og_ref[rs, :] = jnp.concatenate([c for x in o[n_na:] for c in x], axis=1)


def _ctx_attention(naq, nak, nav, gq, gk, gv):
    spec = lambda w: pl.BlockSpec((CTX_STEP_SEQS * SEQ, w), lambda b: (b, 0))
    return pl.pallas_call(
        _ctx_attn_kernel,
        grid=(BATCH // CTX_STEP_SEQS,),
        in_specs=[spec(W_NA), spec(W_NA), spec(W_NA), spec(W_GQA), spec(W_GQA_KV), spec(W_GQA_KV)],
        out_specs=[spec(W_NA), spec(W_GQA)],
        out_shape=[jax.ShapeDtypeStruct((N_CTX, W_NA), BF16), jax.ShapeDtypeStruct((N_CTX, W_GQA), BF16)],
        compiler_params=_cparams(1),
        name="ctx_attention",
    )(naq, nak, nav, gq, gk, gv)


NA_STEP_ROWS = 8


def _lat_na_kernel(q_ref, k_ref, v_ref, kc_ref, vc_ref, tb_ref, o_ref):
    q = q_ref[...]
    groups = []
    for rr in range(NA_STEP_ROWS):
        r = pl.program_id(1) * NA_STEP_ROWS + rr
        r0 = jnp.clip(r - NA_ROWS // 2, 0, GRID_ROWS - NA_ROWS)
        band = pl.ds(pl.multiple_of(r0 * GRID_W, GRID_W), N_BAND)
        dr0 = r0 - r + NA_ROWS - 1
        for c, qc in enumerate(_cols(q[rr * GRID_W:(rr + 1) * GRID_W])):
            cols = slice(c * LANES, (c + 1) * LANES)
            k2 = jnp.concatenate([k_ref[band, cols], kc_ref[:, cols].astype(BF16)], axis=0)
            v2 = jnp.concatenate([v_ref[band, cols], vc_ref[:, cols].astype(BF16)], axis=0)
            bias = jnp.concatenate(
                [jnp.concatenate([tb_ref[2 * c + half, dr0 + jj] for jj in range(0, NA_ROWS, 2)], axis=1)
                 for half in range(2)], axis=0)
            bias = jnp.concatenate([bias, jnp.zeros((2 * GRID_W, PAST_LEN), F32)], axis=1)
            groups.append(([qc], k2, v2, bias))
    outs = [o[0] for o in _attend(groups)]
    n_col = W_NA // LANES
    o_ref[...] = jnp.concatenate(
        [jnp.concatenate(outs[rr * n_col:(rr + 1) * n_col], axis=1) for rr in range(NA_STEP_ROWS)], axis=0)


def _lat_na(l, naq, nak, nav, kc, vc, tb):
    rows = NA_STEP_ROWS * GRID_W
    steps = GRID_ROWS // NA_STEP_ROWS
    seq_blk0 = N_CTX // DEC_SEQ
    seq = pl.BlockSpec((DEC_SEQ, W_NA), lambda b, r: (seq_blk0 + b, 0))
    cache = pl.BlockSpec((None, None, PAST_LEN, W_NA), lambda b, r: (b, l, 0, 0))
    return pl.pallas_call(
        _lat_na_kernel,
        grid=(DEC_BATCH, steps),
        in_specs=[pl.BlockSpec((rows, W_NA), lambda b, r: (N_CTX // rows + b * steps + r, 0)),
                  seq, seq, cache, cache, _layer_spec(l, (H_NA, N_DR - 1, GRID_W, LANES))],
        out_specs=pl.BlockSpec((rows, W_NA), lambda b, r: (b * steps + r, 0)),
        out_shape=jax.ShapeDtypeStruct((N_LAT, W_NA), BF16),
        compiler_params=_cparams(2),
        name="latent_na",
    )(naq, nak, nav, kc, vc, tb)


def _lat_gqa_kernel(q_ref, k_ref, v_ref, kc_ref, vc_ref, o_ref):
    k = jnp.concatenate([kc_ref[...], k_ref[...].astype(F32)], axis=0)
    v = jnp.concatenate([vc_ref[...], v_ref[...].astype(F32)], axis=0)
    o_ref[...] = jnp.concatenate([c for o in _attend(_gqa_groups(q_ref[...], k, v)) for c in o], axis=1)


GQA_ROWS = 256


def _lat_gqa(l, gq, gk, gv, kc, vc):
    seq_blk0 = N_CTX // DEC_SEQ
    steps = DEC_SEQ // GQA_ROWS
    seq = pl.BlockSpec((DEC_SEQ, W_GQA_KV), lambda b, i: (seq_blk0 + b, 0))
    cache = pl.BlockSpec((None, None, PAST_LEN, W_GQA_KV), lambda b, i: (b, l, 0, 0))
    return pl.pallas_call(
        _lat_gqa_kernel,
        grid=(DEC_BATCH, steps),
        in_specs=[pl.BlockSpec((GQA_ROWS, W_GQA), lambda b, i: (N_CTX // GQA_ROWS + b * steps + i, 0)),
                  seq, seq, cache, cache],
        out_specs=pl.BlockSpec((GQA_ROWS, W_GQA), lambda b, i: (b * steps + i, 0)),
        out_shape=jax.ShapeDtypeStruct((N_LAT, W_GQA), BF16),
        compiler_params=_cparams(2),
        name="latent_gqa",
    )(gq, gk, gv, kc, vc)


def _make_mix_ffn_kernel(pair_in, pair_out):
    def kernel_fn(*refs):
        n_in, n_out = (2 if pair_in else 1), (2 if pair_out else 1)
        _mix_ffn_body(refs[:n_in], refs[len(refs) - n_out:], *refs[n_in:len(refs) - n_out])
    return kernel_fn


def _mix_ffn_body(x_refs, y_refs, mod_ref, of_ref, ob_ref, g_ref, bonus_ref, lnxw_ref, lnxb_ref,
                  onac_ref, onal_ref, ogc_ref, ogl_ref,
                  wout_ref, ln1w_ref, ln1b_ref, wfi_ref, wfo_ref, ln2w_ref, ln2b_ref):
    i = pl.program_id(0)
    read_x = (lambda rs: _pick(i, x_refs[0], x_refs[1], rs)) if len(x_refs) == 2 else (lambda rs: x_refs[0][rs, :])
    row = _mod_row(i)
    mod = lambda n: mod_ref[pl.ds(row, 1), n * D_MODEL:(n + 1) * D_MODEL]
    each = lambda f, *lists: [f(*args) for args in zip(*lists)]

    def rwkv_out(rs):
        o = of_ref[rs, :] + ob_ref[rs, :]
        mu = _seg64_sum(o) * (1.0 / HEAD_DIM)
        oc = o - mu
        var = _seg64_sum(oc * oc) * (1.0 / HEAD_DIM)
        o_rwkv = (oc * lax.rsqrt(var + GN_EPS) * lnxw_ref[...] + lnxb_ref[...] + bonus_ref[rs, :]) * g_ref[rs, :]
        return jnp.concatenate([o_rwkv.astype(BF16), _pick(i, onac_ref, onal_ref, rs),
                                _pick(i, ogc_ref, ogl_ref, rs)], axis=1)

    mix = [_dg(rwkv_out(rs), wout_ref[...]) for rs in SUB_ROWS]
    x1 = [_layer_norm(DEEPNORM_ALPHA * read_x(rs) + mod(2) * m, ln1w_ref[...], ln1b_ref[...])
          for rs, m in zip(SUB_ROWS, mix)]
    x_in = [(x * (1.0 + mod(4)) + mod(3)).astype(BF16) for x in x1]
    ffn = [jnp.zeros_like(x) for x in x1]
    for lo in range(0, D_FF, FF_CHUNK):
        gate = [_dg(x, wfi_ref[:, lo:lo + FF_CHUNK]) for x in x_in]
        up = [_dg(x, wfi_ref[:, D_FF + lo:D_FF + lo + FF_CHUNK]) for x in x_in]
        act = each(lambda gt, u: (gt * _sigmoid(gt) * u).astype(BF16), gate, up)
        ffn = each(lambda f, a: f + _dg(a, wfo_ref[lo:lo + FF_CHUNK, :]), ffn, act)
    y = each(lambda x, f: _layer_norm(DEEPNORM_ALPHA * x + mod(5) * f, ln2w_ref[...], ln2b_ref[...]), x1, ffn)

    if len(y_refs) == 1:
        for rs, yy in zip(SUB_ROWS, y):
            y_refs[0][rs, :] = yy
        return
    yc_ref, yl_ref = y_refs

    @pl.when(i < NCB_D)
    def _():
        for rs, yy in zip(SUB_ROWS, y):
            yc_ref[rs, :] = yy

    @pl.when(i >= NCB_D)
    def _():
        for rs, yy in zip(SUB_ROWS, y):
            yl_ref[rs, :] = yy


def _mix_ffn(l, x, pair_out, mod_all, o_fwd, o_bwd, g, bonus, lnx_w, lnx_b, o_na_ctx, o_na_lat, o_g_ctx, o_g_lat,
             w_out_bf, ln1_w, ln1_b, w_ffn_in_bf, w_ffn_out_bf, ln2_w, ln2_b):
    tok = lambda w: pl.BlockSpec((TD, w), lambda i: (i, 0))
    once = lambda *tail: _layer_spec(l, tail, single=True)
    pair_in = isinstance(x, tuple)
    x_specs = [_ctx_spec(D_MODEL), _lat_spec(D_MODEL)] if pair_in else [tok(D_MODEL)]
    if pair_out:
        out_specs = [_ctx_spec(D_MODEL), _lat_spec(D_MODEL)]
        out_shape = [jax.ShapeDtypeStruct((N_CTX, D_MODEL), F32), jax.ShapeDtypeStruct((N_LAT, D_MODEL), F32)]
    else:
        out_specs, out_shape = tok(D_MODEL), jax.ShapeDtypeStruct((N_TOK, D_MODEL), F32)
    return pl.pallas_call(
        _make_mix_ffn_kernel(pair_in, pair_out),
        grid=(NBLK_D,),
        in_specs=x_specs + [
            once(MOD_ROWS, 6 * D_MODEL),
            tok(W_RWKV), tok(W_RWKV),
            tok(W_RWKV), tok(W_RWKV), once(1, W_RWKV), once(1, W_RWKV),
            _ctx_spec(W_NA), _lat_spec(W_NA), _ctx_spec(W_GQA), _lat_spec(W_GQA),
            once(D_MODEL, D_MODEL), once(1, D_MODEL), once(1, D_MODEL),
            once(D_MODEL, 2 * D_FF), once(D_FF, D_MODEL), once(1, D_MODEL), once(1, D_MODEL),
        ],
        out_specs=out_specs,
        out_shape=out_shape,
        compiler_params=_cparams(1, VMEM_LIMIT_FFN),
        name="mix_ffn",
    )(*(x if pair_in else (x,)), mod_all, o_fwd, o_bwd, g, bonus, lnx_w, lnx_b, o_na_ctx, o_na_lat, o_g_ctx, o_g_lat,
      w_out_bf, ln1_w, ln1_b, w_ffn_in_bf, w_ffn_out_bf, ln2_w, ln2_b)


def _rope_tables():
    t = jnp.arange(DEC_SEQ)
    inv = ROPE_BASE ** (-jnp.arange(ROPE_FREQ, dtype=F32) / ROPE_FREQ)
    ang_r = (t // GRID_W).astype(F32)[:, None] * inv
    ang_c = (t % GRID_W).astype(F32)[:, None] * inv
    cos = jnp.concatenate([jnp.cos(ang_r)] * 2 + [jnp.cos(ang_c)] * 2, axis=1)
    sin = jnp.concatenate([-jnp.sin(ang_r), jnp.sin(ang_r), -jnp.sin(ang_c), jnp.sin(ang_c)], axis=1)
    cos = jnp.concatenate([jnp.ones((TD, HEAD_DIM), F32), cos], axis=0)
    sin = jnp.concatenate([jnp.zeros((TD, HEAD_DIM), F32), sin], axis=0)
    return jnp.tile(cos, (1, LANES // HEAD_DIM)), jnp.tile(sin, (1, LANES // HEAD_DIM))


def _block_diag2(w):
    z = jnp.zeros_like(w[:, 0])
    return jnp.concatenate([jnp.concatenate([w[:, 0], z], axis=2), jnp.concatenate([z, w[:, 1]], axis=2)], axis=1)


def _hi_lo(w):
    return jnp.stack(_split(w), axis=1)


def _pair_states(s):
    lead = s.shape[:-3]
    s = s.reshape(lead + (N_PAIR, 2, HEAD_DIM, HEAD_DIM))
    z = jnp.zeros_like(s[..., 0, :, :])
    top = jnp.concatenate([s[..., 0, :, :], z], axis=-1)
    bot = jnp.concatenate([z, s[..., 1, :, :]], axis=-1)
    return jnp.concatenate([top, bot], axis=-2)


def _unpair_states(s):
    lead = s.shape[:-3]
    a = s[..., :HEAD_DIM, :HEAD_DIM]
    b = s[..., HEAD_DIM:, HEAD_DIM:]
    return jnp.stack([a, b], axis=-3).reshape(lead + (H_RWKV, HEAD_DIM, HEAD_DIM))


def kernel(x_prompt, x_sample, state_rwkv, cache_na_k, cache_na_v, cache_gqa_k, cache_gqa_v, c, c_ctx,
           w_mod, b_mod, w_in, rwkv_conv, rwkv_w0, rwkv_w2, rwkv_a0, rwkv_a2, rwkv_g2, rwkv_k_k, rwkv_k_a,
           rwkv_r_k, rwkv_lnx_w, rwkv_lnx_b, na_rpb, gqa_q_norm, gqa_k_norm, w_out, ln1_w, ln1_b,
           w_ffn_in, w_ffn_out, ln2_w, ln2_b):
    x = (x_prompt.reshape(N_CTX, D_MODEL), x_sample.reshape(N_LAT, D_MODEL))
    cc = jnp.concatenate([c_ctx[None], c, jnp.zeros((MOD_ROWS - 1 - DEC_BATCH, D_MODEL), F32)], axis=0)
    mod_all = _modulation(cc, w_mod, b_mod)
    tb_all = _bias_tables(na_rpb)
    cos_tab, sin_tab = _rope_tables()
    rows = lambda a: a.reshape(DEPTH, 1, -1)
    w_in_bf, w_out_bf = w_in.astype(BF16), w_out.astype(BF16)
    w_ffn_in_bf, w_ffn_out_bf = w_ffn_in.astype(BF16), w_ffn_out.astype(BF16)
    qn = jnp.tile(rows(gqa_q_norm), (1, 1, H_GQA))
    kn = jnp.tile(rows(gqa_k_norm), (1, 1, H_GQA_KV))
    rwkv_params = (rwkv_conv, rows(rwkv_w0), _hi_lo(_block_diag2(rwkv_w2)), rows(rwkv_a0),
                   _hi_lo(_block_diag2(rwkv_a2)), _hi_lo(rwkv_g2), rows(rwkv_k_k), rows(rwkv_k_a), rows(rwkv_r_k))
    s0_lat = _pair_states(state_rwkv)
    kc_na = cache_na_k.reshape(DEC_BATCH, DEPTH, PAST_LEN, W_NA)
    vc_na = cache_na_v.reshape(DEC_BATCH, DEPTH, PAST_LEN, W_NA)
    kc_g = cache_gqa_k.reshape(DEC_BATCH, DEPTH, PAST_LEN, W_GQA_KV)
    vc_g = cache_gqa_v.reshape(DEC_BATCH, DEPTH, PAST_LEN, W_GQA_KV)
    caches = [jnp.zeros((BATCH, DEPTH, SEQ, w), F32) for w in (W_NA, W_NA, W_GQA_KV, W_GQA_KV)]
    s_fin = [jnp.zeros((BATCH, DEPTH, N_PAIR, LANES, LANES), F32) for _ in range(2)]

    for l in range(DEPTH):
        naq, nak, nav, gq, gk, gv, r, kap, v, lw, ah, kd, g, bonus, *caches = _inproj(
            l, x, mod_all, w_in_bf, qn, kn, cos_tab, sin_tab, rwkv_params, caches)
        o_fwd, o_bwd, *s_fin = _rwkv_scan(l, r, kap, v, lw, ah, kd, s0_lat, s_fin)
        o_na_ctx, o_g_ctx = _ctx_attention(naq, nak, nav, gq, gk, gv)
        o_na_lat = _lat_na(l, naq, nak, nav, kc_na, vc_na, tb_all)
        o_g_lat = _lat_gqa(l, gq, gk, gv, kc_g, vc_g)
        x = _mix_ffn(
            l, x, l == DEPTH - 1, mod_all, o_fwd, o_bwd, g, bonus, rows(rwkv_lnx_w), rows(rwkv_lnx_b),
            o_na_ctx, o_na_lat, o_g_ctx, o_g_lat, w_out_bf, rows(ln1_w), rows(ln1_b),
            w_ffn_in_bf, w_ffn_out_bf, rows(ln2_w), rows(ln2_b))
        x = tuple(x) if l == DEPTH - 1 else x
    y_prompt = x[0].reshape(BATCH, SEQ, D_MODEL)
    y_sample = x[1].reshape(DEC_BATCH, DEC_SEQ, D_MODEL)
    new_state = jnp.stack([_unpair_states(s) for s in s_fin], axis=2)
    return (y_prompt, y_sample, new_state,
            caches[0].reshape(BATCH, DEPTH, SEQ, H_NA, HEAD_DIM), caches[1].reshape(BATCH, DEPTH, SEQ, H_NA, HEAD_DIM),
            caches[2].reshape(BATCH, DEPTH, SEQ, H_GQA_KV, HEAD_DIM),
            caches[3].reshape(BATCH, DEPTH, SEQ, H_GQA_KV, HEAD_DIM))
```

```python
import jax
import jax.numpy as jnp
from jax import lax
from jax.experimental import pallas as pl
from jax.experimental.pallas import tpu as pltpu

F32 = jnp.float32
BF16 = jnp.bfloat16

D_MODEL = 1024
BATCH = 16
SEQ = 256
DEPTH = 4
DEC_BATCH = 2
DEC_SEQ = 1024
PAST_LEN = 512
GRID_W = 64
GRID_ROWS = DEC_SEQ // GRID_W
HEAD_DIM = 64
H_RWKV = 4
H_NA = 4
H_GQA = 8
H_GQA_KV = 2
W_RWKV = H_RWKV * HEAD_DIM
W_NA = H_NA * HEAD_DIM
W_GQA = H_GQA * HEAD_DIM
W_GQA_KV = H_GQA_KV * HEAD_DIM
LORA_W = 64
LORA_A = 64
LORA_G = 128
RWKV_IN = 3 * W_RWKV + 2 * LORA_W + 2 * LORA_A + LORA_G
NA_IN = 3 * W_NA
GQA_IN = W_GQA + 2 * W_GQA_KV
D_IN = RWKV_IN + NA_IN + GQA_IN
NA_ROWS = 8
NA_COLS = 16
ROPE_BASE = 10000.0
ROPE_FREQ = HEAD_DIM // 4
D_FF = ((8 * D_MODEL + 3 * 256 - 1) // (3 * 256)) * 256
DEEPNORM_ALPHA = (2 * DEPTH) ** 0.25
LN_EPS = 1e-5
RMS_EPS = 1e-6
GN_EPS = 64e-5
NEG_INF = -1e30
ATTN_SCALE = HEAD_DIM ** -0.5

LANES = 128
TM = 256
N_CTX = BATCH * SEQ
N_LAT = DEC_BATCH * DEC_SEQ
N_TOK = N_CTX + N_LAT
CHUNK = 64
TS = 512
NCS = TS // CHUNK
NBLK_S = N_TOK // TS
NCB_S = N_CTX // TS
LAT_BLKS_S = DEC_SEQ // TS
SEQS_S = TS // SEQ
SEQ_CHUNKS = SEQ // CHUNK
TD = 512
SUB_ROWS = [slice(h * SEQ, (h + 1) * SEQ) for h in range(TD // SEQ)]
IN_PIECES = SUB_ROWS
NBLK_D = N_TOK // TD
NCB_D = N_CTX // TD
LAT_BLKS_D = DEC_SEQ // TD
FF_CHUNK = D_FF
N_PAIR = W_RWKV // LANES
MOD_ROWS = 8
VMEM_LIMIT = 48 * 1024 * 1024
VMEM_LIMIT_FFN = 56 * 1024 * 1024
N_DR = 2 * NA_ROWS - 1
N_DC = 2 * NA_COLS - 1
N_BAND = NA_ROWS * GRID_W


def _cparams(n_grid, vmem_limit=VMEM_LIMIT):
    return pltpu.CompilerParams(dimension_semantics=("arbitrary",) * n_grid, vmem_limit_bytes=vmem_limit)


def _iota(shape, dim):
    return lax.broadcasted_iota(jnp.int32, shape, dim)


NN = (((1,), (0,)), ((), ()))
NT = (((1,), (1,)), ((), ()))
TN = (((0,), (0,)), ((), ()))


def _dg(a, b, dims=NN, precision=None):
    return lax.dot_general(a, b, dims, preferred_element_type=F32, precision=precision)


def _split(x):
    hi = x.astype(BF16)
    return hi, (x - hi.astype(F32)).astype(BF16)


def _dgs(a_s, b_s, dims=NN):
    if len(a_s) == 1:
        return _dg(a_s[0], b_s[0], dims)
    (ah, al), (bh, bl) = a_s, b_s
    ca, cb = dims[0][0][0], dims[0][1][0]
    return _dg(jnp.concatenate([ah, ah, al], axis=ca), jnp.concatenate([bh, bl, bh], axis=cb), dims)


def _sigmoid(x):
    return 1.0 / (1.0 + jnp.exp(-x))


def _softplus(x):
    return jnp.maximum(x, 0.0) + jnp.log(1.0 + jnp.exp(-jnp.abs(x)))


def _seg64_sum(x):
    rows, width = x.shape
    lo = _iota((rows, LANES), 1) < HEAD_DIM
    outs = []
    for c in range(width // LANES):
        blk = x[:, c * LANES:(c + 1) * LANES]
        s_lo = jnp.sum(jnp.where(lo, blk, 0.0), axis=-1, keepdims=True)
        s_hi = jnp.sum(jnp.where(lo, 0.0, blk), axis=-1, keepdims=True)
        outs.append(jnp.where(lo, s_lo, s_hi))
    return outs[0] if len(outs) == 1 else jnp.concatenate(outs, axis=1)


def _layer_norm(x, w, b):
    mu = jnp.mean(x, axis=-1, keepdims=True)
    xc = x - mu
    var = jnp.mean(xc * xc, axis=-1, keepdims=True)
    return xc * lax.rsqrt(var + LN_EPS) * w + b


def _mod_row(i):
    return jnp.where(i < NCB_D, 0, 1 + (i - NCB_D) // LAT_BLKS_D)


def _layer_spec(l, tail, single=False):
    idx = lambda *g: (l,) + (0,) * len(tail)
    if single:
        return pl.BlockSpec((None,) + tuple(tail), idx, pipeline_mode=pl.Buffered(1))
    return pl.BlockSpec((None,) + tuple(tail), idx)


ANY_SPEC = pl.BlockSpec(memory_space=pl.ANY)


def _ctx_spec(w):
    return pl.BlockSpec((TD, w), lambda i: (jnp.minimum(i, NCB_D - 1), 0))


def _lat_spec(w):
    return pl.BlockSpec((TD, w), lambda i: (jnp.maximum(i - NCB_D, 0), 0))


def _pick(i, ctx_ref, lat_ref, rows=slice(None)):
    return jnp.where(i < NCB_D, ctx_ref[rows, :], lat_ref[rows, :])


N_COND = 1 + DEC_BATCH


def _mod_kernel(c_ref, w_ref, b_ref, o_ref, sb_scr):
    @pl.when((pl.program_id(0) == 0) & (pl.program_id(1) == 0))
    def _():
        c = c_ref[...]
        s = c * _sigmoid(c)
        for r in range(N_COND):
            sb_scr[r] = jnp.broadcast_to(s[r:r + 1, :], (LANES, D_MODEL)).T

    def body(kc, accs):
        rows = pl.ds(pl.multiple_of(kc * 8, 8), 8)
        wk = w_ref[rows, :]
        return tuple(a + wk * jnp.concatenate([sb_scr[r, rows, :]] * (D_MODEL // LANES), axis=1)
                     for r, a in enumerate(accs))

    accs = lax.fori_loop(0, D_MODEL // 8, body, tuple(jnp.zeros((8, D_MODEL), F32) for _ in range(N_COND)),
                         unroll=8)
    rowid = _iota((MOD_ROWS, D_MODEL), 0)
    out = jnp.zeros((MOD_ROWS, D_MODEL), F32)
    for r, a in enumerate(accs):
        out = jnp.where(rowid == r, jnp.sum(a, axis=0, keepdims=True), out)
    o_ref[...] = out + b_ref[...]


def _modulation(cc, w_mod, b_mod):
    return pl.pallas_call(
        _mod_kernel,
        grid=(DEPTH, 6),
        in_specs=[
            pl.BlockSpec((MOD_ROWS, D_MODEL), lambda l, j: (0, 0)),
            pl.BlockSpec((None, D_MODEL, D_MODEL), lambda l, j: (l, 0, j)),
            pl.BlockSpec((None, 1, D_MODEL), lambda l, j: (l, 0, j)),
        ],
        out_specs=pl.BlockSpec((None, MOD_ROWS, D_MODEL), lambda l, j: (l, 0, j)),
        out_shape=jax.ShapeDtypeStruct((DEPTH, MOD_ROWS, 6 * D_MODEL), F32),
        scratch_shapes=[pltpu.VMEM((N_COND, D_MODEL, LANES), F32)],
        compiler_params=_cparams(2),
        name="modulation",
    )(cc, w_mod, b_mod.reshape(DEPTH, 1, 6 * D_MODEL))


def _bias_kernel(rpb_ref, o_ref):
    q = _iota((GRID_W, LANES), 0)
    x = _iota((GRID_W, LANES), 1)
    c = x % GRID_W
    right = x >= GRID_W
    dc = jnp.clip(c - q, 1 - NA_COLS, NA_COLS - 1) + NA_COLS - 1
    c0 = jnp.clip(q - NA_COLS // 2, 0, GRID_W - NA_COLS)
    in_win = (c >= c0) & (c < c0 + NA_COLS)

    def body(t, carry):
        h = t // (N_DR - 1)
        dr = t % (N_DR - 1)
        rows = [jnp.broadcast_to(rpb_ref[pl.ds(h * N_DR + dr + k, 1), :], (GRID_W, LANES)) for k in range(2)]
        lo, hi = (jnp.take_along_axis(r, dc, axis=1) for r in rows)
        o_ref[h, dr] = jnp.where(in_win, jnp.where(right, hi, lo), NEG_INF)
        return carry

    lax.fori_loop(0, H_NA * (N_DR - 1), body, 0, unroll=N_DR - 1)


RPB_ROWS = -(-H_NA * N_DR // 8) * 8


def _bias_tables(na_rpb):
    rpb = jnp.pad(na_rpb.reshape(DEPTH, H_NA * N_DR, N_DC), ((0, 0), (0, RPB_ROWS - H_NA * N_DR), (0, LANES - N_DC)))
    return pl.pallas_call(
        _bias_kernel,
        grid=(DEPTH,),
        in_specs=[pl.BlockSpec((None, RPB_ROWS, LANES), lambda l: (l, 0, 0))],
        out_specs=pl.BlockSpec((None, H_NA, N_DR - 1, GRID_W, LANES), lambda l: (l, 0, 0, 0, 0)),
        out_shape=jax.ShapeDtypeStruct((DEPTH, H_NA, N_DR - 1, GRID_W, LANES), F32),
        compiler_params=_cparams(1),
        name="na_bias_tables",
    )(rpb)


def _rope(x, cos, sin):
    k = x.shape[1] // LANES
    cosf = cos if k == 1 else jnp.concatenate([cos] * k, axis=1)
    sinf = sin if k == 1 else jnp.concatenate([sin] * k, axis=1)
    first = (_iota(x.shape, 1) % (2 * ROPE_FREQ)) < ROPE_FREQ
    partner = jnp.where(first, pltpu.roll(x, x.shape[1] - ROPE_FREQ, axis=1), pltpu.roll(x, ROPE_FREQ, axis=1))
    return x * cosf + partner * sinf


def _rwkv_features(x, prev_row, next_row, conv_ref, w0_ref, w2_ref, a0_ref, a2_ref, g2_ref, kk_ref, ka_ref, rk_ref):
    n = x.shape[0]
    rows = _iota(x.shape, 0)
    x_prev = jnp.where(rows == 0, prev_row, pltpu.roll(x, 1, axis=0))
    x_next = jnp.where(rows == n - 1, next_row, pltpu.roll(x, n - 1, axis=0))
    f = x_prev * conv_ref[0:1, :] + x * conv_ref[1:2, :] + x_next * conv_ref[2:3, :]
    o1, o2, o3 = W_RWKV, 2 * W_RWKV, 3 * W_RWKV
    o4 = o3 + 2 * LORA_W
    o5 = o4 + 2 * LORA_A
    r, k, v = f[:, :o1], f[:, o1:o2], f[:, o2:o3]
    wd, ad, gd = f[:, o3:o4], f[:, o4:o5], f[:, o5:]
    def lora(y, w_ref):
        y_bf = y.astype(BF16)
        return _dg(jnp.concatenate([y_bf, y_bf], axis=1), jnp.concatenate([w_ref[0], w_ref[1]], axis=0))
    log_w = -_softplus(-(w0_ref[...] + lora(jnp.tanh(wd), w2_ref))) - 0.5
    a = _sigmoid(a0_ref[...] + lora(ad, a2_ref))
    g = lora(_sigmoid(gd), g2_ref)
    kk = k * kk_ref[...]
    kap = kk / jnp.maximum(jnp.sqrt(_seg64_sum(kk * kk)), 1e-12)
    per_dir = []
    kd_sum = jnp.zeros_like(k)
    for d in range(2):
        a_d = a[:, d * W_RWKV:(d + 1) * W_RWKV]
        kd = k * (1.0 + (a_d - 1.0) * ka_ref[...])
        kd_sum = kd_sum + kd
        per_dir.append((-jnp.exp(log_w[:, d * W_RWKV:(d + 1) * W_RWKV]), a_d * kap, kd))
    bonus = _seg64_sum(r * kd_sum * rk_ref[...]) * v
    return r, kap, v, per_dir, g, bonus


HALO = 8


def _inproj_kernel_pair(xc_ref, xl_ref, *rest):
    n_par = 17
    _inproj_body(lambda i, rs: _pick(i, xc_ref, xl_ref, rs), *rest[:n_par], None, None, None, None, *rest[n_par:])


def _inproj_kernel_one(x_ref, *rest):
    _inproj_body(lambda i, rs: x_ref[rs, :], *rest)


def _inproj_body(read_x, xp_ref, xn_ref, mod_ref, w_ref, qn_ref, kn_ref, cos_ref, sin_ref,
                 conv_ref, w0_ref, w2_ref, a0_ref, a2_ref, g2_ref, kk_ref, ka_ref, rk_ref,
                 c0_ref, c1_ref, c2_ref, c3_ref,
                 naq_ref, nak_ref, nav_ref, gq_ref, gk_ref, gv_ref,
                 r_ref, kap_ref, v_ref, lw_ref, ah_ref, kd_ref, g_ref, bonus_ref,
                 cnak_ref, cnav_ref, cgk_ref, cgv_ref):
    del c0_ref, c1_ref, c2_ref, c3_ref
    i = pl.program_id(0)
    row = _mod_row(i)
    shift1 = mod_ref[pl.ds(row, 1), 0:D_MODEL]
    scale1 = mod_ref[pl.ds(row, 1), D_MODEL:2 * D_MODEL]
    modulate = lambda x: (x * (1.0 + scale1) + shift1).astype(BF16)
    lat = i >= NCB_D
    pos = (i - NCB_D) % LAT_BLKS_D
    in_seq = jnp.where(lat, 1.0, 0.0)
    has_prev = jnp.where(lat & (pos != 0), 1.0, 0.0)
    has_next = jnp.where(lat & (pos != LAT_BLKS_D - 1), 1.0, 0.0)
    xm = [modulate(read_x(i, rs)) for rs in IN_PIECES]
    xm[0] = jnp.concatenate([xm[0], modulate(xp_ref[...]), modulate(xn_ref[...])], axis=0)
    projs = [_dg(x, w_ref[...]) for x in xm]
    n0 = IN_PIECES[0].stop
    halo = projs[0][n0:, :RWKV_IN]
    projs[0] = projs[0][:n0]
    feats = [proj[:, :RWKV_IN] for proj in projs]
    o_na = RWKV_IN
    o_g = RWKV_IN + NA_IN
    new_kv = []
    for h, (rs, proj) in enumerate(zip(IN_PIECES, projs)):
        if h == 0:
            prev_row = halo[HALO - 1:HALO] * has_prev
        else:
            prev_row = feats[h - 1][-1:] * (in_seq if rs.start % SEQ == 0 else 1.0)
        if h == len(IN_PIECES) - 1:
            next_row = halo[HALO:HALO + 1] * has_next
        else:
            next_row = feats[h + 1][0:1] * (in_seq if rs.stop % SEQ == 0 else 1.0)
        r, kap, v, per_dir, g, bonus = _rwkv_features(
            feats[h], prev_row, next_row, conv_ref, w0_ref, w2_ref, a0_ref, a2_ref, g2_ref, kk_ref, ka_ref, rk_ref)
        r_ref[rs, :] = r.astype(BF16)
        kap_ref[rs, :] = kap.astype(BF16)
        v_ref[rs, :] = v.astype(BF16)
        g_ref[rs, :] = g
        bonus_ref[rs, :] = bonus
        for d, (lw, ah, kd) in enumerate(per_dir):
            lw_ref[d, rs, :] = lw
            ah_ref[d, rs, :] = ah.astype(BF16)
            kd_ref[d, rs, :] = kd.astype(BF16)
        naq_ref[rs, :] = (proj[:, o_na:o_na + W_NA] * ATTN_SCALE).astype(BF16)
        nak = proj[:, o_na + W_NA:o_na + 2 * W_NA]
        nav = proj[:, o_na + 2 * W_NA:o_na + 3 * W_NA]
        q = proj[:, o_g:o_g + W_GQA]
        k = proj[:, o_g + W_GQA:o_g + W_GQA + W_GQA_KV]
        gv = proj[:, o_g + W_GQA + W_GQA_KV:]
        cos = cos_ref[rs, :]
        sin = sin_ref[rs, :]
        q = q * lax.rsqrt(_seg64_sum(q * q) * (1.0 / HEAD_DIM) + RMS_EPS) * qn_ref[...]
        k = k * lax.rsqrt(_seg64_sum(k * k) * (1.0 / HEAD_DIM) + RMS_EPS) * kn_ref[...]
        gk = _rope(k, cos, sin)
        gq_ref[rs, :] = (_rope(q, cos, sin) * ATTN_SCALE).astype(BF16)
        nak_ref[rs, :] = nak.astype(BF16)
        nav_ref[rs, :] = nav.astype(BF16)
        gk_ref[rs, :] = gk.astype(BF16)
        gv_ref[rs, :] = gv.astype(BF16)
        new_kv.append((nak, nav, gk, gv))

    @pl.when(i < NCB_D)
    def _():
        for rs, vals in zip(IN_PIECES, new_kv):
            for ref, val in zip((cnak_ref, cnav_ref, cgk_ref, cgv_ref), vals):
                ref[rs.start // SEQ, rs.start % SEQ:rs.start % SEQ + rs.stop - rs.start, :] = val


def _inproj(l, x, mod_all, w_in_bf, qn, kn, cos_tab, sin_tab, rwkv_params, caches):
    tab_idx = lambda i: (jnp.where(i < NCB_D, 0, 1 + (i - NCB_D) % LAT_BLKS_D), 0)
    lat_blk = lambda i: jnp.maximum(i - NCB_D, 0)
    halo_blocks = TD // HALO
    widths = (W_NA, W_NA, W_NA, W_GQA, W_GQA_KV, W_GQA_KV, W_RWKV, W_RWKV, W_RWKV)
    cache_w = (W_NA, W_NA, W_GQA_KV, W_GQA_KV)
    tok = lambda w: pl.BlockSpec((TD, w), lambda i: (i, 0))
    tok2 = pl.BlockSpec((2, TD, W_RWKV), lambda i: (0, i, 0))
    cache_spec = lambda w: pl.BlockSpec((TD // SEQ, None, SEQ, w), lambda i: (jnp.minimum(i, NCB_D - 1), l, 0, 0))
    rwkv_tails = ((3, RWKV_IN), (1, 2 * W_RWKV), (2, 2 * LORA_W, 2 * W_RWKV), (1, 2 * W_RWKV),
                  (2, 2 * LORA_A, 2 * W_RWKV), (2, LORA_G, W_RWKV), (1, W_RWKV), (1, W_RWKV), (1, W_RWKV))
    pair = isinstance(x, tuple)
    x_lat, lat0 = (x[1], 0) if pair else (x, N_CTX // HALO)
    x_specs = [_ctx_spec(D_MODEL), _lat_spec(D_MODEL)] if pair else [tok(D_MODEL)]
    in_specs = x_specs + [
        pl.BlockSpec((HALO, D_MODEL), lambda i: (lat0 + jnp.maximum(lat_blk(i) * halo_blocks - 1, 0), 0)),
        pl.BlockSpec((HALO, D_MODEL),
                     lambda i: (lat0 + jnp.minimum((lat_blk(i) + 1) * halo_blocks, N_LAT // HALO - 1), 0)),
        _layer_spec(l, (MOD_ROWS, 6 * D_MODEL)),
        _layer_spec(l, (D_MODEL, D_IN)),
        _layer_spec(l, (1, W_GQA)),
        _layer_spec(l, (1, W_GQA_KV)),
        pl.BlockSpec((TD, LANES), tab_idx),
        pl.BlockSpec((TD, LANES), tab_idx),
    ] + [_layer_spec(l, t) for t in rwkv_tails]
    n_in = len(in_specs)
    out_specs = [tok(w) for w in widths] + [tok2] * 3 + [tok(W_RWKV)] * 2
    out_shape = ([jax.ShapeDtypeStruct((N_TOK, w), BF16) for w in widths]
                 + [jax.ShapeDtypeStruct((2, N_TOK, W_RWKV), dt) for dt in (F32, BF16, BF16)]
                 + [jax.ShapeDtypeStruct((N_TOK, W_RWKV), F32)] * 2)
    return pl.pallas_call(
        _inproj_kernel_pair if pair else _inproj_kernel_one,
        grid=(NBLK_D,),
        in_specs=in_specs + ([] if pair else [ANY_SPEC] * 4),
        out_specs=out_specs + [cache_spec(w) for w in cache_w],
        out_shape=out_shape + [jax.ShapeDtypeStruct((BATCH, DEPTH, SEQ, w), F32) for w in cache_w],
        input_output_aliases={} if pair else {n_in + j: len(out_specs) + j for j in range(4)},
        compiler_params=_cparams(1),
        name="inproj",
    )(*(x if pair else (x,)), x_lat, x_lat, mod_all, w_in_bf, qn, kn, cos_tab, sin_tab, *rwkv_params,
      *(() if pair else caches))


def _scan_kernel(r0_ref, kap0_ref, v0_ref, r1_ref, kap1_ref, v1_ref,
                 lw0_ref, ah0_ref, kd0_ref, lw1_ref, ah1_ref, kd1_ref, s00_ref, s01_ref, sf0_in_ref, sf1_in_ref,
                 o0_ref, o1_ref, sf0_ref, sf1_ref,
                 s_scr, sdone_scr, rt_scr, kt_scr, kdt_scr, at_scr, cum_scr, m_scr, n_scr, q_scr, oo_scr):
    del sf0_in_ref, sf1_in_ref
    j = pl.program_id(0)
    par = j % 2
    jb = jnp.maximum(j - 1, 0)
    upd_blk = (jb, NBLK_S - 1 - jb)
    is_ctx = tuple(b < NCB_S for b in upd_blk)
    first_pos = (0, LAT_BLKS_S - 1)
    r_refs, kap_refs, v_refs = (r0_ref, r1_ref), (kap0_ref, kap1_ref), (v0_ref, v1_ref)
    lw_refs, ah_refs, kd_refs = (lw0_ref, lw1_ref), (ah0_ref, ah1_ref), (kd0_ref, kd1_ref)
    s0_refs, o_refs, sf_refs = (s00_ref, s01_ref), (o0_ref, o1_ref), (sf0_ref, sf1_ref)

    @pl.when(j == 0)
    def _():
        for ref in (s_scr, sdone_scr, m_scr, n_scr, q_scr, oo_scr):
            ref[...] = jnp.zeros_like(ref)

    for dn in range(2):
        blk = upd_blk[dn]

        @pl.when((j > 0) & is_ctx[dn])
        def _():
            s_scr[dn] = jnp.zeros(s_scr.shape[1:], F32)

        @pl.when((j > 0) & (blk >= NCB_S) & ((blk - NCB_S) % LAT_BLKS_S == first_pos[dn]))
        def _():
            s_scr[dn] = s0_refs[dn][...]

    def seq_slot(dn, n):
        return n if dn == 0 else SEQS_S - 1 - n

    def update(c):
        for dn in range(2):
            ce = c if dn == 0 else NCS - 1 - c
            rows = slice(ce * CHUNK, (ce + 1) * CHUNK)
            for p in range(N_PAIR):
                s = s_scr[dn, p]
                if c > 0 and c % SEQ_CHUNKS == 0:
                    sdone_scr[dn, seq_slot(dn, c // SEQ_CHUNKS - 1), p] = s
                    s = jnp.where(is_ctx[dn], 0.0, s)
                s2 = jnp.concatenate(_split(s), axis=1)
                q = q_scr[1 - par, dn, ce, p]
                m = m_scr[1 - par, dn, ce, p]
                o_refs[dn][rows, p * LANES:(p + 1) * LANES] = (
                    _dg(jnp.concatenate([q, q], axis=1), s2, NT) + oo_scr[1 - par, dn, ce, p])
                s_scr[dn, p] = _dg(s2, jnp.concatenate([m, m], axis=0)) + n_scr[1 - par, dn, ce, p]

    pending = list(range(NCS))

    def next_update():
        if pending:
            update(pending.pop(0))

    next_update()
    rr = _iota((TM, TM), 0)
    cc = _iota((TM, TM), 1)
    same_chunk = (rr // CHUNK) == (cc // CHUNK)
    for dn in range(2):
        order = (rr >= cc) if dn == 0 else (rr <= cc)
        tri = jnp.where(same_chunk & order, 1.0, 0.0).astype(BF16)
        for h in range(TS // TM):
            rs = slice(h * TM, (h + 1) * TM)
            lw = lw_refs[dn][rs, :]
            lw_hi = lw.astype(BF16)
            lw_r = lw - lw_hi.astype(F32)
            lw_mid = lw_r.astype(BF16)
            lw_lo = (lw_r - lw_mid.astype(F32)).astype(BF16)
            cum = _dg(jnp.concatenate([tri, tri, tri], axis=1), jnp.concatenate([lw_hi, lw_mid, lw_lo], axis=0))
            e_neg = jnp.exp(-cum)
            rt_scr[dn, rs, :] = r_refs[dn][rs, :] * jnp.exp(cum)
            kt_scr[dn, rs, :] = kap_refs[dn][rs, :] * jnp.exp(cum - lw)
            kdt_scr[dn, rs, :] = kd_refs[dn][rs, :] * e_neg
            at_scr[dn, rs, :] = ah_refs[dn][rs, :] * e_neg
            cum_scr[dn, rs, :] = cum
        if dn == 0:
            next_update()

    row = _iota((CHUNK, LANES), 0)
    col = _iota((CHUNK, LANES), 1) % CHUNK
    left = _iota((CHUNK, LANES), 1) < CHUNK
    incl = (row >= col, row <= col)
    strict = (row > col, row < col)
    eye = jnp.where(row == col, 1.0, 0.0)
    r2 = _iota((LANES, LANES), 0)
    c2 = _iota((LANES, LANES), 1)
    bd_mask = (r2 // CHUNK) == (c2 // CHUNK)
    eye2 = jnp.where(r2 == c2, 1.0, 0.0)

    zero_bf = jnp.zeros((CHUNK, LANES), BF16)
    split = lambda x: (x.astype(BF16),)

    def bd(xs):
        return tuple(jnp.concatenate([jnp.where(left, y, zero_bf), jnp.where(left, zero_bf, y)], axis=0)
                     for y in xs)

    def cat(xs, ys, axis):
        return tuple(jnp.concatenate([x, y], axis=axis) for x, y in zip(xs, ys))

    def mm(a_list, b_list):
        return [_dgs(a, bd(b)) for a, b in zip(a_list, b_list)]

    units = [(dn, c, p) for c in range(NCS) for dn in range(2) for p in range(N_PAIR)]
    each = lambda f, *lists: [f(*args) for args in zip(*lists)]

    def load(ref):
        return [ref[dn, c * CHUNK:(c + 1) * CHUNK, p * LANES:(p + 1) * LANES] for dn, c, p in units]

    rt = load(rt_scr)
    v = [v_refs[dn][c * CHUNK:(c + 1) * CHUNK, p * LANES:(p + 1) * LANES] for dn, c, p in units]
    kt_s, rt_s, kdt_s, at_s, v_s = (each(split, x) for x in (load(kt_scr), rt, load(kdt_scr), load(at_scr), v))
    gam = [jnp.exp(cum_scr[dn, (c + 1) * CHUNK - 1:(c + 1) * CHUNK, p * LANES:(p + 1) * LANES] if dn == 0 else
                   cum_scr[dn, c * CHUNK:c * CHUNK + 1, p * LANES:(p + 1) * LANES]) for dn, c, p in units]
    gram = each(lambda k, r, a, kd: _dgs(cat(k, r, 0), cat(bd(a), bd(kd), 0), NT), kt_s, rt_s, at_s, kdt_s)
    dirs = [u[0] for u in units]
    la = [jnp.where(strict[dn], g[0:CHUNK, 0:LANES], 0.0) for dn, g in zip(dirs, gram)]
    lk_s = [split(jnp.where(strict[dn], g[0:CHUNK, LANES:], 0.0)) for dn, g in zip(dirs, gram)]
    ra_s = [split(jnp.where(incl[dn], g[CHUNK:, 0:LANES], 0.0)) for dn, g in zip(dirs, gram)]
    rk_s = [split(jnp.where(incl[dn], g[CHUNK:, LANES:], 0.0)) for dn, g in zip(dirs, gram)]
    next_update()
    lrv = mm(each(lambda lk, rk: cat(lk, rk, 0), lk_s, rk_s), v_s)
    next_update()
    b = 8
    l8 = [jnp.where((row // b) == (col // b), x, 0.0) for x in la]
    l8_s = each(split, l8)
    l8_2 = mm(l8_s, l8_s)
    next_update()
    l8_2s = each(split, l8_2)
    l8_4 = mm(l8_2s, l8_2s)
    p1 = mm([split(eye - x) for x in l8], [split(eye + y) for y in l8_2])
    next_update()
    t = mm(each(split, p1), [split(eye + y) for y in l8_4])
    while b < CHUNK:
        next_update()
        offd = ((row // (2 * b)) == (col // (2 * b))) & ((row // b) != (col // b))
        t_s = each(split, t)
        x = mm(t_s, [split(jnp.where(offd, y, 0.0)) for y in la])
        t = each(lambda tt, z: tt - z, t, mm(each(split, x), t_s))
        b *= 2
    assert not pending, "more chunk updates than stages to place them between"
    tx = each(lambda tt, k, y: _dgs(split(tt), cat(bd(k), bd(split(y[0:CHUNK])), 1)), t, kt_s, lrv)
    khat_s = [split(y[:, 0:LANES]) for y in tx]
    w1_s = [split(y[:, LANES:]) for y in tx]
    rx = each(lambda r, k, w: _dgs(r, cat(bd(k), bd(w), 1)), ra_s, khat_s, w1_s)
    mk = each(lambda k, a: _dgs(k, a, TN), khat_s, at_s)
    nk = each(lambda vv, w, kd, a: _dgs(cat(vv, w, 0), cat(kd, tuple(-y for y in a), 0), TN),
              v_s, w1_s, kdt_s, at_s)
    for i, (dn, c, p) in enumerate(units):
        q_scr[par, dn, c, p] = (rt[i] - rx[i][:, 0:LANES]).astype(BF16)
        oo_scr[par, dn, c, p] = lrv[i][CHUNK:] - rx[i][:, LANES:]
        m_scr[par, dn, c, p] = ((eye2 - jnp.where(bd_mask, mk[i], 0.0)) * gam[i]).astype(BF16)
        n_scr[par, dn, c, p] = jnp.where(bd_mask, nk[i], 0.0) * gam[i]

    for dn in range(2):
        @pl.when((j > 0) & is_ctx[dn])
        def _():
            for n in range(SEQS_S - 1):
                sf_refs[dn][seq_slot(dn, n)] = sdone_scr[dn, seq_slot(dn, n)]
            sf_refs[dn][seq_slot(dn, SEQS_S - 1)] = s_scr[dn]


def _rwkv_scan(l, r, kap, v, lw, ah, kd, s0_lat, s_fin):
    def stage_blk(dn, j):
        jj = jnp.minimum(j, NBLK_S - 1)
        return jj if dn == 0 else NBLK_S - 1 - jj

    def update_blk(dn, j):
        jj = jnp.maximum(j - 1, 0)
        return jj if dn == 0 else NBLK_S - 1 - jj

    pair = (N_PAIR, LANES, LANES)
    tok = lambda dn: pl.BlockSpec((TS, W_RWKV), lambda j: (stage_blk(dn, j), 0))
    tok2 = lambda dn: pl.BlockSpec((None, TS, W_RWKV), lambda j: (dn, stage_blk(dn, j), 0))
    s0_spec = lambda dn: pl.BlockSpec(
        (None, None, None) + pair,
        lambda j: (jnp.maximum(update_blk(dn, j) - NCB_S, 0) // LAT_BLKS_S, l, dn, 0, 0, 0))
    sf_spec = lambda dn: pl.BlockSpec(
        (SEQS_S, None) + pair, lambda j: (jnp.minimum(update_blk(dn, j), NCB_S - 1), l, 0, 0, 0))
    o_spec = lambda dn: pl.BlockSpec((TS, W_RWKV), lambda j: (update_blk(dn, j), 0))
    dir_scr = lambda *shape, dtype=F32: pltpu.VMEM((2,) + shape, dtype)
    stage_scr = (2, NCS, N_PAIR)
    return pl.pallas_call(
        _scan_kernel,
        grid=(NBLK_S + 1,),
        in_specs=[tok(0), tok(0), tok(0), tok(1), tok(1), tok(1),
                  tok2(0), tok2(0), tok2(0), tok2(1), tok2(1), tok2(1),
                  s0_spec(0), s0_spec(1), ANY_SPEC, ANY_SPEC],
        out_specs=[o_spec(0), o_spec(1), sf_spec(0), sf_spec(1)],
        out_shape=[jax.ShapeDtypeStruct((N_TOK, W_RWKV), F32)] * 2
        + [jax.ShapeDtypeStruct((BATCH, DEPTH) + pair, F32)] * 2,
        scratch_shapes=[dir_scr(*pair), dir_scr(SEQS_S, *pair)] + [dir_scr(TS, W_RWKV)] * 5
        + [dir_scr(*stage_scr, LANES, LANES, dtype=BF16), dir_scr(*stage_scr, LANES, LANES),
           dir_scr(*stage_scr, CHUNK, LANES, dtype=BF16), dir_scr(*stage_scr, CHUNK, LANES)],
        input_output_aliases={14: 2, 15: 3},
        compiler_params=_cparams(1),
        name="rwkv_scan",
    )(r, kap, v, r, kap, v, lw, ah, kd, lw, ah, kd, s0_lat, s0_lat, s_fin[0], s_fin[1])


def _attend(groups):
    lhs = []
    for q_cols, _, _, _ in groups:
        left = _iota(q_cols[0].shape, 1) < HEAD_DIM
        parts = []
        for qc in q_cols:
            parts += [jnp.where(left, qc, 0.0), jnp.where(left, 0.0, qc)]
        lhs.append(jnp.concatenate(parts, axis=0).astype(BF16))
    s = [_dg(x, g[1], NT) for x, g in zip(lhs, groups)]
    s = [x if g[3] is None else x + g[3] for x, g in zip(s, groups)]
    p = [jnp.exp(x - jnp.max(x, axis=-1, keepdims=True)) for x in s]
    inv = [1.0 / jnp.sum(x, axis=-1, keepdims=True) for x in p]
    o = [_dg(x.astype(BF16), g[2]) * y for x, y, g in zip(p, inv, groups)]
    outs = []
    for x, (q_cols, _, _, _) in zip(o, groups):
        rows = q_cols[0].shape[0]
        left = _iota(q_cols[0].shape, 1) < HEAD_DIM
        outs.append([jnp.where(left, x[2 * j * rows:(2 * j + 1) * rows],
                               x[(2 * j + 1) * rows:(2 * j + 2) * rows]).astype(BF16) for j in range(len(q_cols))])
    return outs


def _cols(x):
    return [x[:, c * LANES:(c + 1) * LANES] for c in range(x.shape[1] // LANES)]


def _gqa_groups(q, k, v):
    k, v = k.astype(F32), v.astype(F32)
    left = _iota(k.shape, 1) < HEAD_DIM
    k_sw = pltpu.roll(k, HEAD_DIM, axis=1)
    v_sw = pltpu.roll(v, HEAD_DIM, axis=1)
    q_cols = _cols(q)
    groups = []
    for g in range(H_GQA_KV):
        k2 = jnp.where(left, k, k_sw) if g == 0 else jnp.where(left, k_sw, k)
        v2 = jnp.where(left, v, v_sw) if g == 0 else jnp.where(left, v_sw, v)
        groups.append((q_cols[2 * g:2 * g + 2], k2.astype(BF16), v2.astype(BF16), None))
    return groups


CTX_STEP_SEQS = 4


def _ctx_attn_kernel(naq_ref, nak_ref, nav_ref, gq_ref, gk_ref, gv_ref, ona_ref, og_ref):
    groups = []
    n_na = W_NA // LANES
    for n in range(CTX_STEP_SEQS):
        rs = slice(n * SEQ, (n + 1) * SEQ)
        k_cols = _cols(nak_ref[rs, :].astype(BF16))
        v_cols = _cols(nav_ref[rs, :].astype(BF16))
        groups += [([qc], kc, vc, None) for qc, kc, vc in zip(_cols(naq_ref[rs, :]), k_cols, v_cols)]
        groups += _gqa_groups(gq_ref[rs, :], gk_ref[rs, :], gv_ref[rs, :])
    outs = _attend(groups)
    per_seq = len(groups) // CTX_STEP_SEQS
    for n in range(CTX_STEP_SEQS):
        rs = slice(n * SEQ, (n + 1) * SEQ)
        o = outs[n * per_seq:(n + 1) * per_seq]
        ona_ref[rs, :] = jnp.concatenate([x[0] for x in o[:n_na]], axis=1)
        og_ref[rs, :] = jnp.concatenate([c for x in o[n_na:] for c in x], axis=1)


def _ctx_attention(naq, nak, nav, gq, gk, gv):
    spec = lambda w: pl.BlockSpec((CTX_STEP_SEQS * SEQ, w), lambda b: (b, 0))
    return pl.pallas_call(
        _ctx_attn_kernel,
        grid=(BATCH // CTX_STEP_SEQS,),
        in_specs=[spec(W_NA), spec(W_NA), spec(W_NA), spec(W_GQA), spec(W_GQA_KV), spec(W_GQA_KV)],
        out_specs=[spec(W_NA), spec(W_GQA)],
        out_shape=[jax.ShapeDtypeStruct((N_CTX, W_NA), BF16), jax.ShapeDtypeStruct((N_CTX, W_GQA), BF16)],
        compiler_params=_cparams(1),
        name="ctx_attention",
    )(naq, nak, nav, gq, gk, gv)


NA_STEP_ROWS = 8


def _lat_na_kernel(q_ref, k_ref, v_ref, kc_ref, vc_ref, tb_ref, o_ref):
    q = q_ref[...]
    groups = []
    for rr in range(NA_STEP_ROWS):
        r = pl.program_id(1) * NA_STEP_ROWS + rr
        r0 = jnp.clip(r - NA_ROWS // 2, 0, GRID_ROWS - NA_ROWS)
        band = pl.ds(pl.multiple_of(r0 * GRID_W, GRID_W), N_BAND)
        dr0 = r0 - r + NA_ROWS - 1
        for c, qc in enumerate(_cols(q[rr * GRID_W:(rr + 1) * GRID_W])):
            cols = slice(c * LANES, (c + 1) * LANES)
            k2 = jnp.concatenate([k_ref[band, cols], kc_ref[:, cols].astype(BF16)], axis=0)
            v2 = jnp.concatenate([v_ref[band, cols], vc_ref[:, cols].astype(BF16)], axis=0)
            bias = jnp.concatenate(
                [jnp.concatenate([tb_ref[2 * c + half, dr0 + jj] for jj in range(0, NA_ROWS, 2)], axis=1)
                 for half in range(2)], axis=0)
            bias = jnp.concatenate([bias, jnp.zeros((2 * GRID_W, PAST_LEN), F32)], axis=1)
            groups.append(([qc], k2, v2, bias))
    outs = [o[0] for o in _attend(groups)]
    n_col = W_NA // LANES
    o_ref[...] = jnp.concatenate(
        [jnp.concatenate(outs[rr * n_col:(rr + 1) * n_col], axis=1) for rr in range(NA_STEP_ROWS)], axis=0)


def _lat_na(l, naq, nak, nav, kc, vc, tb):
    rows = NA_STEP_ROWS * GRID_W
    steps = GRID_ROWS // NA_STEP_ROWS
    seq_blk0 = N_CTX // DEC_SEQ
    seq = pl.BlockSpec((DEC_SEQ, W_NA), lambda b, r: (seq_blk0 + b, 0))
    cache = pl.BlockSpec((None, None, PAST_LEN, W_NA), lambda b, r: (b, l, 0, 0))
    return pl.pallas_call(
        _lat_na_kernel,
        grid=(DEC_BATCH, steps),
        in_specs=[pl.BlockSpec((rows, W_NA), lambda b, r: (N_CTX // rows + b * steps + r, 0)),
                  seq, seq, cache, cache, _layer_spec(l, (H_NA, N_DR - 1, GRID_W, LANES))],
        out_specs=pl.BlockSpec((rows, W_NA), lambda b, r: (b * steps + r, 0)),
        out_shape=jax.ShapeDtypeStruct((N_LAT, W_NA), BF16),
        compiler_params=_cparams(2),
        name="latent_na",
    )(naq, nak, nav, kc, vc, tb)


def _lat_gqa_kernel(q_ref, k_ref, v_ref, kc_ref, vc_ref, o_ref):
    k = jnp.concatenate([kc_ref[...], k_ref[...].astype(F32)], axis=0)
    v = jnp.concatenate([vc_ref[...], v_ref[...].astype(F32)], axis=0)
    o_ref[...] = jnp.concatenate([c for o in _attend(_gqa_groups(q_ref[...], k, v)) for c in o], axis=1)


GQA_ROWS = 256


def _lat_gqa(l, gq, gk, gv, kc, vc):
    seq_blk0 = N_CTX // DEC_SEQ
    steps = DEC_SEQ // GQA_ROWS
    seq = pl.BlockSpec((DEC_SEQ, W_GQA_KV), lambda b, i: (seq_blk0 + b, 0))
    cache = pl.BlockSpec((None, None, PAST_LEN, W_GQA_KV), lambda b, i: (b, l, 0, 0))
    return pl.pallas_call(
        _lat_gqa_kernel,
        grid=(DEC_BATCH, steps),
        in_specs=[pl.BlockSpec((GQA_ROWS, W_GQA), lambda b, i: (N_CTX // GQA_ROWS + b * steps + i, 0)),
                  seq, seq, cache, cache],
        out_specs=pl.BlockSpec((GQA_ROWS, W_GQA), lambda b, i: (b * steps + i, 0)),
        out_shape=jax.ShapeDtypeStruct((N_LAT, W_GQA), BF16),
        compiler_params=_cparams(2),
        name="latent_gqa",
    )(gq, gk, gv, kc, vc)


def _make_mix_ffn_kernel(pair_in, pair_out):
    def kernel_fn(*refs):
        n_in, n_out = (2 if pair_in else 1), (2 if pair_out else 1)
        _mix_ffn_body(refs[:n_in], refs[len(refs) - n_out:], *refs[n_in:len(refs) - n_out])
    return kernel_fn


def _mix_ffn_body(x_refs, y_refs, mod_ref, of_ref, ob_ref, g_ref, bonus_ref, lnxw_ref, lnxb_ref,
                  onac_ref, onal_ref, ogc_ref, ogl_ref,
                  wout_ref, ln1w_ref, ln1b_ref, wfi_ref, wfo_ref, ln2w_ref, ln2b_ref):
    i = pl.program_id(0)
    read_x = (lambda rs: _pick(i, x_refs[0], x_refs[1], rs)) if len(x_refs) == 2 else (lambda rs: x_refs[0][rs, :])
    row = _mod_row(i)
    mod = lambda n: mod_ref[pl.ds(row, 1), n * D_MODEL:(n + 1) * D_MODEL]
    each = lambda f, *lists: [f(*args) for args in zip(*lists)]

    def rwkv_out(rs):
        o = of_ref[rs, :] + ob_ref[rs, :]
        mu = _seg64_sum(o) * (1.0 / HEAD_DIM)
        oc = o - mu
        var = _seg64_sum(oc * oc) * (1.0 / HEAD_DIM)
        o_rwkv = (oc * lax.rsqrt(var + GN_EPS) * lnxw_ref[...] + lnxb_ref[...] + bonus_ref[rs, :]) * g_ref[rs, :]
        return jnp.concatenate([o_rwkv.astype(BF16), _pick(i, onac_ref, onal_ref, rs),
                                _pick(i, ogc_ref, ogl_ref, rs)], axis=1)

    mix = [_dg(rwkv_out(rs), wout_ref[...]) for rs in SUB_ROWS]
    x1 = [_layer_norm(DEEPNORM_ALPHA * read_x(rs) + mod(2) * m, ln1w_ref[...], ln1b_ref[...])
          for rs, m in zip(SUB_ROWS, mix)]
    x_in = [(x * (1.0 + mod(4)) + mod(3)).astype(BF16) for x in x1]
    ffn = [jnp.zeros_like(x) for x in x1]
    for lo in range(0, D_FF, FF_CHUNK):
        gate = [_dg(x, wfi_ref[:, lo:lo + FF_CHUNK]) for x in x_in]
        up = [_dg(x, wfi_ref[:, D_FF + lo:D_FF + lo + FF_CHUNK]) for x in x_in]
        act = each(lambda gt, u: (gt * _sigmoid(gt) * u).astype(BF16), gate, up)
        ffn = each(lambda f, a: f + _dg(a, wfo_ref[lo:lo + FF_CHUNK, :]), ffn, act)
    y = each(lambda x, f: _layer_norm(DEEPNORM_ALPHA * x + mod(5) * f, ln2w_ref[...], ln2b_ref[...]), x1, ffn)

    if len(y_refs) == 1:
        for rs, yy in zip(SUB_ROWS, y):
            y_refs[0][rs, :] = yy
        return
    yc_ref, yl_ref = y_refs

    @pl.when(i < NCB_D)
    def _():
        for rs, yy in zip(SUB_ROWS, y):
            yc_ref[rs, :] = yy

    @pl.when(i >= NCB_D)
    def _():
        for rs, yy in zip(SUB_ROWS, y):
            yl_ref[rs, :] = yy


def _mix_ffn(l, x, pair_out, mod_all, o_fwd, o_bwd, g, bonus, lnx_w, lnx_b, o_na_ctx, o_na_lat, o_g_ctx, o_g_lat,
             w_out_bf, ln1_w, ln1_b, w_ffn_in_bf, w_ffn_out_bf, ln2_w, ln2_b):
    tok = lambda w: pl.BlockSpec((TD, w), lambda i: (i, 0))
    once = lambda *tail: _layer_spec(l, tail, single=True)
    pair_in = isinstance(x, tuple)
    x_specs = [_ctx_spec(D_MODEL), _lat_spec(D_MODEL)] if pair_in else [tok(D_MODEL)]
    if pair_out:
        out_specs = [_ctx_spec(D_MODEL), _lat_spec(D_MODEL)]
        out_shape = [jax.ShapeDtypeStruct((N_CTX, D_MODEL), F32), jax.ShapeDtypeStruct((N_LAT, D_MODEL), F32)]
    else:
        out_specs, out_shape = tok(D_MODEL), jax.ShapeDtypeStruct((N_TOK, D_MODEL), F32)
    return pl.pallas_call(
        _make_mix_ffn_kernel(pair_in, pair_out),
        grid=(NBLK_D,),
        in_specs=x_specs + [
            once(MOD_ROWS, 6 * D_MODEL),
            tok(W_RWKV), tok(W_RWKV),
            tok(W_RWKV), tok(W_RWKV), once(1, W_RWKV), once(1, W_RWKV),
            _ctx_spec(W_NA), _lat_spec(W_NA), _ctx_spec(W_GQA), _lat_spec(W_GQA),
            once(D_MODEL, D_MODEL), once(1, D_MODEL), once(1, D_MODEL),
            once(D_MODEL, 2 * D_FF), once(D_FF, D_MODEL), once(1, D_MODEL), once(1, D_MODEL),
        ],
        out_specs=out_specs,
        out_shape=out_shape,
        compiler_params=_cparams(1, VMEM_LIMIT_FFN),
        name="mix_ffn",
    )(*(x if pair_in else (x,)), mod_all, o_fwd, o_bwd, g, bonus, lnx_w, lnx_b, o_na_ctx, o_na_lat, o_g_ctx, o_g_lat,
      w_out_bf, ln1_w, ln1_b, w_ffn_in_bf, w_ffn_out_bf, ln2_w, ln2_b)


def _rope_tables():
    t = jnp.arange(DEC_SEQ)
    inv = ROPE_BASE ** (-jnp.arange(ROPE_FREQ, dtype=F32) / ROPE_FREQ)
    ang_r = (t // GRID_W).astype(F32)[:, None] * inv
    ang_c = (t % GRID_W).astype(F32)[:, None] * inv
    cos = jnp.concatenate([jnp.cos(ang_r)] * 2 + [jnp.cos(ang_c)] * 2, axis=1)
    sin = jnp.concatenate([-jnp.sin(ang_r), jnp.sin(ang_r), -jnp.sin(ang_c), jnp.sin(ang_c)], axis=1)
    cos = jnp.concatenate([jnp.ones((TD, HEAD_DIM), F32), cos], axis=0)
    sin = jnp.concatenate([jnp.zeros((TD, HEAD_DIM), F32), sin], axis=0)
    return jnp.tile(cos, (1, LANES // HEAD_DIM)), jnp.tile(sin, (1, LANES // HEAD_DIM))


def _block_diag2(w):
    z = jnp.zeros_like(w[:, 0])
    return jnp.concatenate([jnp.concatenate([w[:, 0], z], axis=2), jnp.concatenate([z, w[:, 1]], axis=2)], axis=1)


def _hi_lo(w):
    return jnp.stack(_split(w), axis=1)


def _pair_states(s):
    lead = s.shape[:-3]
    s = s.reshape(lead + (N_PAIR, 2, HEAD_DIM, HEAD_DIM))
    z = jnp.zeros_like(s[..., 0, :, :])
    top = jnp.concatenate([s[..., 0, :, :], z], axis=-1)
    bot = jnp.concatenate([z, s[..., 1, :, :]], axis=-1)
    return jnp.concatenate([top, bot], axis=-2)


def _unpair_states(s):
    lead = s.shape[:-3]
    a = s[..., :HEAD_DIM, :HEAD_DIM]
    b = s[..., HEAD_DIM:, HEAD_DIM:]
    return jnp.stack([a, b], axis=-3).reshape(lead + (H_RWKV, HEAD_DIM, HEAD_DIM))


def kernel(x_prompt, x_sample, state_rwkv, cache_na_k, cache_na_v, cache_gqa_k, cache_gqa_v, c, c_ctx,
           w_mod, b_mod, w_in, rwkv_conv, rwkv_w0, rwkv_w2, rwkv_a0, rwkv_a2, rwkv_g2, rwkv_k_k, rwkv_k_a,
           rwkv_r_k, rwkv_lnx_w, rwkv_lnx_b, na_rpb, gqa_q_norm, gqa_k_norm, w_out, ln1_w, ln1_b,
           w_ffn_in, w_ffn_out, ln2_w, ln2_b):
    x = (x_prompt.reshape(N_CTX, D_MODEL), x_sample.reshape(N_LAT, D_MODEL))
    cc = jnp.concatenate([c_ctx[None], c, jnp.zeros((MOD_ROWS - 1 - DEC_BATCH, D_MODEL), F32)], axis=0)
    mod_all = _modulation(cc, w_mod, b_mod)
    tb_all = _bias_tables(na_rpb)
    cos_tab, sin_tab = _rope_tables()
    rows = lambda a: a.reshape(DEPTH, 1, -1)
    w_in_bf, w_out_bf = w_in.astype(BF16), w_out.astype(BF16)
    w_ffn_in_bf, w_ffn_out_bf = w_ffn_in.astype(BF16), w_ffn_out.astype(BF16)
    qn = jnp.tile(rows(gqa_q_norm), (1, 1, H_GQA))
    kn = jnp.tile(rows(gqa_k_norm), (1, 1, H_GQA_KV))
    rwkv_params = (rwkv_conv, rows(rwkv_w0), _hi_lo(_block_diag2(rwkv_w2)), rows(rwkv_a0),
                   _hi_lo(_block_diag2(rwkv_a2)), _hi_lo(rwkv_g2), rows(rwkv_k_k), rows(rwkv_k_a), rows(rwkv_r_k))
    s0_lat = _pair_states(state_rwkv)
    kc_na = cache_na_k.reshape(DEC_BATCH, DEPTH, PAST_LEN, W_NA)
    vc_na = cache_na_v.reshape(DEC_BATCH, DEPTH, PAST_LEN, W_NA)
    kc_g = cache_gqa_k.reshape(DEC_BATCH, DEPTH, PAST_LEN, W_GQA_KV)
    vc_g = cache_gqa_v.reshape(DEC_BATCH, DEPTH, PAST_LEN, W_GQA_KV)
    caches = ()
    s_fin = [jnp.zeros((BATCH, DEPTH, N_PAIR, LANES, LANES), F32) for _ in range(2)]

    for l in range(DEPTH):
        naq, nak, nav, gq, gk, gv, r, kap, v, lw, ah, kd, g, bonus, *caches = _inproj(
            l, x, mod_all, w_in_bf, qn, kn, cos_tab, sin_tab, rwkv_params, caches)
        o_fwd, o_bwd, *s_fin = _rwkv_scan(l, r, kap, v, lw, ah, kd, s0_lat, s_fin)
        o_na_ctx, o_g_ctx = _ctx_attention(naq, nak, nav, gq, gk, gv)
        o_na_lat = _lat_na(l, naq, nak, nav, kc_na, vc_na, tb_all)
        o_g_lat = _lat_gqa(l, gq, gk, gv, kc_g, vc_g)
        x = _mix_ffn(
            l, x, l == DEPTH - 1, mod_all, o_fwd, o_bwd, g, bonus, rows(rwkv_lnx_w), rows(rwkv_lnx_b),
            o_na_ctx, o_na_lat, o_g_ctx, o_g_lat, w_out_bf, rows(ln1_w), rows(ln1_b),
            w_ffn_in_bf, w_ffn_out_bf, rows(ln2_w), rows(ln2_b))
        x = tuple(x) if l == DEPTH - 1 else x
    y_prompt = x[0].reshape(BATCH, SEQ, D_MODEL)
    y_sample = x[1].reshape(DEC_BATCH, DEC_SEQ, D_MODEL)
    new_state = jnp.stack([_unpair_states(s) for s in s_fin], axis=2)
    return (y_prompt, y_sample, new_state,
            caches[0].reshape(BATCH, DEPTH, SEQ, H_NA, HEAD_DIM), caches[1].reshape(BATCH, DEPTH, SEQ, H_NA, HEAD_DIM),
            caches[2].reshape(BATCH, DEPTH, SEQ, H_GQA_KV, HEAD_DIM),
            caches[3].reshape(BATCH, DEPTH, SEQ, H_GQA_KV, HEAD_DIM))
```
